```python
import jax, jax.numpy as jnp
from jax import lax
import numpy as np

D_MODEL = 1024
BATCH = 8
SEQ = 4096
DEPTH = 1

GRID_W = 64
D_MIX = D_MODEL
D_ATTN = D_MIX // 2
D_FOUR = D_MIX - D_ATTN
N_HEADS_A = 8
HEAD_DIM = D_ATTN // N_HEADS_A
WIN_H = 8
WIN_W = 16
N_FOUR_GROUPS = 8
FOUR_GROUP_DIM = D_FOUR // N_FOUR_GROUPS
D_PROJ = 3 * D_ATTN + D_FOUR
N_EXPERT_GROUPS = 4
EXPERTS_PER_GROUP = 8
N_EXPERTS = N_EXPERT_GROUPS * EXPERTS_PER_GROUP
TOP_K_FINE = 2
D_EXPERT = D_MODEL // 4
RMS_EPS = 1e-6

kernel_name = "hymba_natten_fnet_hiermoe_encoder"


def rmsnorm(x, g):
    xf = x.astype(jnp.float32)
    y = xf * lax.rsqrt(jnp.mean(xf * xf, axis=-1, keepdims=True) + RMS_EPS)
    return (y * g.astype(jnp.float32)).astype(x.dtype)


def neighbourhood_attention(q, k, v, rpb):
    b, t, h, dh = q.shape
    rows = t // GRID_W
    kh = min(WIN_H, rows)
    q = q.reshape(b, rows, GRID_W, h, dh)
    k = k.reshape(b, rows, GRID_W, h, dh)
    v = v.reshape(b, rows, GRID_W, h, dh)
    cols = jnp.arange(GRID_W)
    col_start = jnp.clip(cols - WIN_W // 2, 0, GRID_W - WIN_W)
    col_idx = col_start[:, None] + jnp.arange(WIN_W)[None, :]
    dc = col_idx - cols[:, None] + (WIN_W - 1)
    scale = HEAD_DIM ** -0.5

    def row_block(r):
        rs = jnp.clip(r - kh // 2, 0, rows - kh)
        k_rows = lax.dynamic_slice_in_dim(k, rs, kh, axis=1)
        v_rows = lax.dynamic_slice_in_dim(v, rs, kh, axis=1)
        k_win = k_rows[:, :, col_idx]
        v_win = v_rows[:, :, col_idx]
        q_row = lax.dynamic_index_in_dim(q, r, axis=1, keepdims=False)
        s = jnp.einsum('bchd,bicjhd->bhcij', q_row, k_win).astype(jnp.float32) * scale
        dr = rs + jnp.arange(kh) - r + (WIN_H - 1)
        bias = rpb[:, dr[None, :, None], dc[:, None, :]]
        s = s + bias.astype(jnp.float32)[None]
        p = jax.nn.softmax(s.reshape(b, h, GRID_W, kh * WIN_W), axis=-1)
        p = p.reshape(b, h, GRID_W, kh, WIN_W).astype(v.dtype)
        return jnp.einsum('bhcij,bicjhd->bchd', p, v_win)

    o = lax.map(row_block, jnp.arange(rows))
    return jnp.moveaxis(o, 0, 1).reshape(b, t, h * dh)


def fourier_mix(u, w_four, b_four):
    b, t, _ = u.shape
    ug = u.reshape(b, t, N_FOUR_GROUPS, FOUR_GROUP_DIM).astype(jnp.float32)
    f = jnp.fft.fft2(ug, axes=(1, 3), norm="ortho").real
    y = jnp.einsum('btgc,gcd->btgd', f, w_four.astype(jnp.float32)).reshape(b, t, D_FOUR)
    return (y + b_four.astype(jnp.float32)).astype(u.dtype)


def hybrid_mixer(xn, w_in, rpb, w_four, b_four, g_attn_out, g_four_out, w_out):
    b, t, _ = xn.shape
    proj = xn @ w_in
    q, k, v, u = jnp.split(proj, [D_ATTN, 2 * D_ATTN, 3 * D_ATTN], axis=-1)
    q = q.reshape(b, t, N_HEADS_A, HEAD_DIM)
    k = k.reshape(b, t, N_HEADS_A, HEAD_DIM)
    v = v.reshape(b, t, N_HEADS_A, HEAD_DIM)
    o_a = neighbourhood_attention(q, k, v, rpb)
    o_f = fourier_mix(u, w_four, b_four)
    merged = jnp.concatenate([rmsnorm(o_a, g_attn_out), rmsnorm(o_f, g_four_out)], axis=-1)
    return merged @ w_out


def hierarchical_moe(x, w_rc, b_rc, w_rf, b_rf, w_gate, w_up, w_down):
    b, t, d = x.shape
    n = b * t
    xt = x.reshape(n, d)
    coarse = (xt @ w_rc).astype(jnp.float32) + b_rc.astype(jnp.float32)
    coarse_p = jax.nn.softmax(coarse, axis=-1)
    g_w, g_idx = lax.top_k(coarse_p, 1)
    fine = ((xt @ w_rf).astype(jnp.float32) + b_rf.astype(jnp.float32))
    fine = fine.reshape(n, N_EXPERT_GROUPS, EXPERTS_PER_GROUP)
    fine_sel = jnp.take_along_axis(fine, g_idx[:, :, None], axis=1)[:, 0]
    f_val, f_idx = lax.top_k(fine_sel, TOP_K_FINE)
    f_w = jax.nn.softmax(f_val, axis=-1) * g_w
    expert_id = g_idx * EXPERTS_PER_GROUP + f_idx
    combine = jnp.sum(jax.nn.one_hot(expert_id, N_EXPERTS, dtype=jnp.float32)
                      * f_w[..., None], axis=1)
    y = jnp.zeros((n, d), jnp.float32)
    for e in range(N_EXPERTS):
        hdn = jax.nn.silu(xt @ w_gate[e]) * (xt @ w_up[e])
        y = y + combine[:, e:e + 1] * (hdn @ w_down[e]).astype(jnp.float32)
    return y.astype(x.dtype).reshape(b, t, d)


def setup_inputs(seed: int = 0) -> dict:
    key = jax.random.key(seed)
    ks = jax.random.split(key, 18)
    f32 = jnp.float32
    nrm = lambda k, shape, s: jax.random.normal(k, shape, f32) * s
    gain = lambda k, shape: 1.0 + 0.05 * jax.random.normal(k, shape, f32)
    return {
        "x": jax.random.normal(ks[0], (BATCH, SEQ, D_MODEL), f32),
        "norm_mix": gain(ks[1], (DEPTH, D_MODEL)),
        "w_in": nrm(ks[2], (DEPTH, D_MODEL, D_PROJ), D_MODEL ** -0.5),
        "rpb": nrm(ks[3], (DEPTH, N_HEADS_A, 2 * WIN_H - 1, 2 * WIN_W - 1), 0.5),
        "w_four": nrm(ks[4], (DEPTH, N_FOUR_GROUPS, FOUR_GROUP_DIM, FOUR_GROUP_DIM), FOUR_GROUP_DIM ** -0.5),
        "b_four": nrm(ks[5], (DEPTH, D_FOUR), 0.02),
        "g_attn_out": gain(ks[6], (DEPTH, D_ATTN)),
        "g_four_out": gain(ks[7], (DEPTH, D_FOUR)),
        "w_out": nrm(ks[8], (DEPTH, D_MIX, D_MODEL), D_MIX ** -0.5),
        "norm_moe": gain(ks[9], (DEPTH, D_MODEL)),
        "w_router_coarse": nrm(ks[10], (DEPTH, D_MODEL, N_EXPERT_GROUPS), D_MODEL ** -0.5),
        "b_router_coarse": nrm(ks[11], (DEPTH, N_EXPERT_GROUPS), 0.01),
        "w_router_fine": nrm(ks[12], (DEPTH, D_MODEL, N_EXPERTS), D_MODEL ** -0.5),
        "b_router_fine": nrm(ks[13], (DEPTH, N_EXPERTS), 0.01),
        "w_gate": nrm(ks[14], (DEPTH, N_EXPERTS, D_MODEL, D_EXPERT), D_MODEL ** -0.5),
        "w_up": nrm(ks[15], (DEPTH, N_EXPERTS, D_MODEL, D_EXPERT), D_MODEL ** -0.5),
        "w_down": nrm(ks[16], (DEPTH, N_EXPERTS, D_EXPERT, D_MODEL), D_EXPERT ** -0.5),
        "norm_final": gain(ks[17], (D_MODEL,)),
    }


def reference(x, norm_mix, w_in, rpb, w_four, b_four, g_attn_out, g_four_out, w_out,
              norm_moe, w_router_coarse, b_router_coarse, w_router_fine, b_router_fine,
              w_gate, w_up, w_down, norm_final):
    h = x
    for layer in range(DEPTH):
        xn = rmsnorm(h, norm_mix[layer])
        h = h + hybrid_mixer(xn, w_in[layer], rpb[layer], w_four[layer], b_four[layer],
                             g_attn_out[layer], g_four_out[layer], w_out[layer])
        hn = rmsnorm(h, norm_moe[layer])
        h = h + hierarchical_moe(hn, w_router_coarse[layer], b_router_coarse[layer],
                                 w_router_fine[layer], b_router_fine[layer],
                                 w_gate[layer], w_up[layer], w_down[layer])
    return rmsnorm(h, norm_final)
```

```python
import functools

import numpy as np
import jax
import jax.numpy as jnp
from jax import lax
from jax.experimental import pallas as pl
from jax.experimental.pallas import tpu as pltpu

F32 = jnp.float32
BF16 = jnp.bfloat16
I32 = jnp.int32

D_MODEL = 1024
SEQ = 4096
GRID_W = 64
ROWS = SEQ // GRID_W
D_ATTN = 512
D_FOUR = 512
N_HEADS = 8
HEAD_DIM = 64
WIN_H = 8
WIN_W = 16
N_FOUR_GROUPS = 8
FOUR_GROUP_DIM = 64
D_PROJ = 3 * D_ATTN + D_FOUR
N_GROUPS = 4
EPG = 8
N_EXPERTS = N_GROUPS * EPG
D_EXPERT = 256
EPS = 1e-6
NEG = -1e30

V7X_VMEM_LIMIT = 56 * 1024 * 1024

TM_IN = 512
TM_OUT = 512
TM_MOE = 128
N_PAIRS = EPG * (EPG - 1) // 2
N_BUCKETS = N_GROUPS * N_PAIRS
BUCKET_LANES = 128

QB_ROWS = 8
QB_COLS = 16
KB_ROWS = 16
KB_COLS = 32


def _rms(x, g):
    ms = jnp.mean(x * x, axis=-1, keepdims=True)
    return x * lax.rsqrt(ms + EPS) * g


def _inproj_kernel(x_ref, g_ref, w_ref, qkv_ref, u_ref):
    xn = _rms(x_ref[...], g_ref[...]).astype(BF16)
    p = jnp.dot(xn, w_ref[...], preferred_element_type=F32)
    qkv_ref[:, :D_ATTN] = (p[:, :D_ATTN] * (HEAD_DIM ** -0.5)).astype(BF16)
    qkv_ref[:, D_ATTN:] = p[:, D_ATTN:3 * D_ATTN].astype(BF16)
    u_ref[...] = p[:, 3 * D_ATTN:]


def _inproj(x2, g, w_bf):
    n = x2.shape[0]
    return pl.pallas_call(
        _inproj_kernel,
        grid=(n // TM_IN,),
        in_specs=[
            pl.BlockSpec((TM_IN, D_MODEL), lambda i: (i, 0)),
            pl.BlockSpec((1, D_MODEL), lambda i: (0, 0)),
            pl.BlockSpec((D_MODEL, D_PROJ), lambda i: (0, 0)),
        ],
        out_specs=[
            pl.BlockSpec((TM_IN, 3 * D_ATTN), lambda i: (i, 0)),
            pl.BlockSpec((TM_IN, D_FOUR), lambda i: (i, 0)),
        ],
        out_shape=[
            jax.ShapeDtypeStruct((n, 3 * D_ATTN), BF16),
            jax.ShapeDtypeStruct((n, D_FOUR), F32),
        ],
        compiler_params=pltpu.CompilerParams(
            dimension_semantics=("parallel",), vmem_limit_bytes=V7X_VMEM_LIMIT),
        name="inproj",
    )(x2, g, w_bf)


_KCOL_START = (0, 8, 24, 32)
_KCOL_SHIFTED = (False, True, True, False)
_KCOL_OFF = (0, 0, 16, 32)
_COL_TYPE = (0, 1, 1, 2)


def _bias_index_tables():
    dr = np.zeros((9, 128, 512), np.int32)
    dc = np.zeros((9, 128, 512), np.int32)
    ok = np.zeros((9, 128, 512), bool)
    qi, qc = np.divmod(np.arange(128), QB_COLS)
    ki, kc = np.divmod(np.arange(512), KB_COLS)
    for rt, (q0, k0) in enumerate(((0, 0), (8, 4), (56, 48))):
        qrow = q0 + qi
        krow = k0 + ki
        rs = np.clip(qrow - WIN_H // 2, 0, ROWS - WIN_H)
        rok = (krow[None, :] >= rs[:, None]) & (krow[None, :] < rs[:, None] + WIN_H)
        drr = krow[None, :] - qrow[:, None] + (WIN_H - 1)
        for ct, (c0, kc0) in enumerate(((0, 0), (16, 8), (48, 32))):
            qcol = c0 + qc
            kcol = kc0 + kc
            cs = np.clip(qcol - WIN_W // 2, 0, GRID_W - WIN_W)
            cok = (kcol[None, :] >= cs[:, None]) & (kcol[None, :] < cs[:, None] + WIN_W)
            dcc = kcol[None, :] - qcol[:, None] + (WIN_W - 1)
            t = rt * 3 + ct
            ok[t] = rok & cok
            dr[t] = np.where(ok[t], drr, 0)
            dc[t] = np.where(ok[t], dcc, 0)
    return dr, dc, ok


_BIAS_DR, _BIAS_DC, _BIAS_OK = _bias_index_tables()


def _attn_kernel(q_ref, k_ref, v_ref, bias_ref, o_ref, ksh_ref, vsh_ref):
    zpad = jnp.zeros((8, 128), F32)
    ksh_ref[...] = jnp.concatenate([k_ref[...].astype(F32)[8:], zpad], axis=0).astype(BF16)
    vsh_ref[...] = jnp.concatenate([v_ref[...].astype(F32)[8:], zpad], axis=0).astype(BF16)
    lane = lax.broadcasted_iota(I32, (1, 128), 1)
    head_masks = (lane < HEAD_DIM, lane >= HEAD_DIM)

    def row_block(rb, carry):
        rb = jnp.asarray(rb, I32)
        ks = jnp.clip(QB_ROWS * rb - WIN_H // 2, 0, ROWS - KB_ROWS)
        rt = jnp.where(rb == 0, 0, jnp.where(rb == ROWS // QB_ROWS - 1, 2, 1))
        for j in range(GRID_W // QB_COLS):
            kr, vr = (ksh_ref, vsh_ref) if _KCOL_SHIFTED[j] else (k_ref, v_ref)
            q = jnp.concatenate(
                [q_ref[pl.ds(pl.multiple_of((QB_ROWS * rb + i) * GRID_W + QB_COLS * j, 16), QB_COLS), :]
                 for i in range(QB_ROWS)], axis=0)
            kstarts = [pl.multiple_of((ks + i) * GRID_W + _KCOL_OFF[j], 16) for i in range(KB_ROWS)]
            kt = jnp.concatenate([kr[pl.ds(s, KB_COLS), :] for s in kstarts], axis=0)
            vt = jnp.concatenate([vr[pl.ds(s, KB_COLS), :] for s in kstarts], axis=0)
            outs = []
            for hh in range(2):
                hm = head_masks[hh]
                qm = jnp.where(hm, q, jnp.zeros_like(q))
                s = lax.dot_general(qm, kt, (((1,), (1,)), ((), ())), preferred_element_type=F32)
                s = s + bias_ref[hh, rt * 3 + _COL_TYPE[j]]
                mx = jnp.max(s, axis=-1, keepdims=True)
                p = jnp.exp(s - mx).astype(BF16)
                vm = jnp.where(hm, vt, jnp.ones_like(vt))
                o = jnp.dot(p, vm, preferred_element_type=F32)
                outs.append(o / pltpu.roll(o, HEAD_DIM, 1))
            out = jnp.where(head_masks[0], outs[0], outs[1]).astype(BF16)
            for i in range(QB_ROWS):
                o_ref[pl.ds(pl.multiple_of((QB_ROWS * rb + i) * GRID_W + QB_COLS * j, 16), QB_COLS), :] = (
                    out[QB_COLS * i:QB_COLS * (i + 1)])
        return carry

    lax.fori_loop(0, ROWS // QB_ROWS, row_block, 0)


def _attention(qkv3, bias_tab):
    b = qkv3.shape[0]
    n_hp = N_HEADS // 2
    blk = lambda off: pl.BlockSpec((None, SEQ, 128), lambda hp, bi: (bi, 0, off + hp))
    return pl.pallas_call(
        _attn_kernel,
        grid=(n_hp, b),
        in_specs=[
            blk(0), blk(n_hp), blk(2 * n_hp),
            pl.BlockSpec((2, 9, 128, 512), lambda hp, bi: (hp, 0, 0, 0)),
        ],
        out_specs=pl.BlockSpec((None, SEQ, 128), lambda hp, bi: (bi, 0, hp)),
        out_shape=jax.ShapeDtypeStruct((b, SEQ, D_ATTN), BF16),
        scratch_shapes=[pltpu.VMEM((SEQ, 128), BF16), pltpu.VMEM((SEQ, 128), BF16)],
        compiler_params=pltpu.CompilerParams(
            dimension_semantics=("parallel", "parallel"), vmem_limit_bytes=V7X_VMEM_LIMIT),
        name="nattn",
    )(qkv3, qkv3, qkv3, bias_tab)


def _fourier_tables():
    n = 64
    k = np.arange(n)
    ang = 2.0 * np.pi * np.outer(k, k) / n
    c64, s64 = np.cos(ang), np.sin(ang)
    eye8 = np.eye(8)
    k1 = np.concatenate([np.kron(c64, eye8), np.kron(-s64, eye8)], axis=0)
    a = np.arange(8)[:, None, None, None, None]
    t2p = np.arange(64)[None, :, None, None, None]
    jo = np.arange(8)[None, None, :, None, None]
    c = np.arange(8)[None, None, None, :, None]
    j = np.arange(8)[None, None, None, None, :]
    idx = ((8 * c + j) * (8 * a + jo + 64 * t2p)) % SEQ
    th = 2.0 * np.pi * idx / SEQ
    cos_t, sin_t = np.cos(th), np.sin(th)
    delta = np.eye(8)[None, None, :, None, :, None]
    def expand(m):
        return (m[:, :, :, :, None, :] * delta).reshape(8, 512, 512)
    cc, ss = expand(cos_t), expand(sin_t)
    la = np.concatenate([np.concatenate([cc, ss], axis=2), np.concatenate([-ss, cc], axis=2)], axis=1)
    cbd = np.kron(np.eye(4), c64)
    sbd = np.kron(np.eye(4), s64)
    cs = np.concatenate([cbd, sbd], axis=0)
    return k1.astype(np.float32), la.astype(np.float32), cs.astype(np.float32)


_K1_NP, _LA_NP, _CS_NP = _fourier_tables()


def _fourier_kernel(u_ref, k1_ref, la_ref, cs_ref, wbd_ref, bf_ref, y_ref, zs_ref):
    s = pl.program_id(1)

    @pl.when(s < 8)
    def _():
        xc = u_ref[...].reshape(512, D_FOUR).astype(BF16)
        z = jnp.dot(k1_ref[...], xc, preferred_element_type=F32)
        zs_ref[s] = z.astype(BF16)

    @pl.when(s >= 8)
    def _():
        off = pl.multiple_of((s - 8) * 64, 64)
        rhs = jnp.concatenate(
            [zs_ref[c, pl.ds(part * 512 + off, 64), :] for part in range(2) for c in range(8)], axis=0)
        x = jnp.dot(la_ref[...], rhs, preferred_element_type=F32)
        xr = x[:512].astype(BF16)
        xi = x[512:].astype(BF16)
        halves = []
        for hf in range(2):
            sl = slice(256 * hf, 256 * (hf + 1))
            lhs = jnp.concatenate([xr[:, sl], xi[:, sl]], axis=1)
            f = jnp.dot(lhs, cs_ref[...], preferred_element_type=F32) * (1.0 / 512.0)
            halves.append(jnp.dot(f.astype(BF16), wbd_ref[hf], preferred_element_type=F32))
        y = jnp.concatenate(halves, axis=1) + bf_ref[...]
        y_ref[...] = y.reshape(64, 8, D_FOUR)


def _fourier(u4, wbd, bf):
    b = u4.shape[0]
    k1 = jnp.asarray(_K1_NP, BF16)
    la = jnp.asarray(_LA_NP, BF16)
    cs = jnp.asarray(_CS_NP, BF16)
    return pl.pallas_call(
        _fourier_kernel,
        grid=(b, 16),
        in_specs=[
            pl.BlockSpec((None, 64, 8, D_FOUR), lambda bi, s: (bi, 0, jnp.minimum(s, 7), 0)),
            pl.BlockSpec((1024, 512), lambda bi, s: (0, 0)),
            pl.BlockSpec((None, 1024, 1024), lambda bi, s: (jnp.maximum(s - 8, 0), 0, 0)),
            pl.BlockSpec((512, 256), lambda bi, s: (0, 0)),
            pl.BlockSpec((2, 256, 256), lambda bi, s: (0, 0, 0)),
            pl.BlockSpec((1, D_FOUR), lambda bi, s: (0, 0)),
        ],
        out_specs=pl.BlockSpec((None, 64, 8, D_FOUR), lambda bi, s: (bi, 0, jnp.maximum(s - 8, 0), 0)),
        out_shape=jax.ShapeDtypeStruct((b, 64, 64, D_FOUR), F32),
        scratch_shapes=[pltpu.VMEM((8, 1024, 512), BF16)],
        compiler_params=pltpu.CompilerParams(
            dimension_semantics=("parallel", "arbitrary"), vmem_limit_bytes=V7X_VMEM_LIMIT),
        name="fourier",
    )(u4, k1, la, cs, wbd, bf)


def _mixout_kernel(oa_ref, yf_ref, x_ref, ga_ref, gf_ref, wout_ref, gm_ref, wrt_ref, br_ref, tri_ref,
                   h_ref, bucket_ref, rank_ref, wlo_ref, whi_ref, cnt_ref, carry_ref):
    i = pl.program_id(0)

    @pl.when(i == 0)
    def _():
        carry_ref[...] = jnp.zeros_like(carry_ref)

    na = _rms(oa_ref[...].astype(F32), ga_ref[...]).astype(BF16)
    nf = _rms(yf_ref[...], gf_ref[...]).astype(BF16)
    merged = jnp.concatenate([na, nf], axis=1)
    h = x_ref[...] + jnp.dot(merged, wout_ref[...], preferred_element_type=F32)
    h_ref[...] = h
    hn = _rms(h, gm_ref[...]).astype(BF16)
    lt = lax.dot_general(wrt_ref[...], hn, (((1,), (1,)), ((), ())), preferred_element_type=F32)
    lt = lt + br_ref[...]
    c = [lt[k:k + 1] for k in range(N_GROUPS)]
    cmax = jnp.maximum(jnp.maximum(c[0], c[1]), jnp.maximum(c[2], c[3]))
    e = [jnp.exp(ck - cmax) for ck in c]
    esum = (e[0] + e[1]) + (e[2] + e[3])
    p = [ek / esum for ek in e]
    pmax = jnp.maximum(jnp.maximum(p[0], p[1]), jnp.maximum(p[2], p[3]))
    g = jnp.where(p[0] == pmax, 0, jnp.where(p[1] == pmax, 1, jnp.where(p[2] == pmax, 2, 3))).astype(I32)
    fine = jnp.where(g == 0, lt[8:16], jnp.where(g == 1, lt[16:24], jnp.where(g == 2, lt[24:32], lt[32:40])))
    rows = lax.broadcasted_iota(I32, fine.shape, 0)
    v1 = jnp.max(fine, axis=0, keepdims=True)
    i1 = jnp.min(jnp.where(fine == v1, rows, EPG), axis=0, keepdims=True)
    rest = jnp.where(rows == i1, -jnp.inf, fine)
    v2 = jnp.max(rest, axis=0, keepdims=True)
    i2 = jnp.min(jnp.where(rest == v2, rows, EPG), axis=0, keepdims=True)
    t = jnp.exp(v2 - v1)
    den = 1.0 + t
    w1 = (1.0 / den) * pmax
    w2 = (t / den) * pmax
    lo = jnp.minimum(i1, i2)
    hi = jnp.maximum(i1, i2)
    first_is_lo = i1 < i2
    wlo_ref[...] = jnp.where(first_is_lo, w1, w2)
    whi_ref[...] = jnp.where(first_is_lo, w2, w1)
    pair = lax.shift_right_logical(lo * (2 * EPG - 1 - lo), 1) + (hi - lo - 1)
    bucket = g * N_PAIRS + pair
    bucket_ref[...] = bucket
    brow = lax.broadcasted_iota(I32, (BUCKET_LANES, TM_OUT), 0)
    onehot = (brow == bucket).astype(F32)
    prefix = jnp.dot(onehot.astype(BF16), tri_ref[...], preferred_element_type=F32)
    rank = jnp.sum(onehot * (prefix + carry_ref[...]), axis=0, keepdims=True)
    rank_ref[...] = rank.astype(I32)
    carry_ref[...] = carry_ref[...] + jnp.sum(onehot, axis=1, keepdims=True)
    cnt_ref[...] = carry_ref[...]


def _mixout(oa, yf, x2, ga, gf, wout_bf, gm, wrt, br, tri):
    n = x2.shape[0]
    nt = n // TM_OUT
    full = lambda *shape: pl.BlockSpec(shape, lambda i: (0,) * len(shape))
    row3 = pl.BlockSpec((None, 1, TM_OUT), lambda i: (i, 0, 0))
    return pl.pallas_call(
        _mixout_kernel,
        grid=(nt,),
        in_specs=[
            pl.BlockSpec((TM_OUT, D_ATTN), lambda i: (i, 0)),
            pl.BlockSpec((TM_OUT, D_FOUR), lambda i: (i, 0)),
            pl.BlockSpec((TM_OUT, D_MODEL), lambda i: (i, 0)),
            full(1, D_ATTN), full(1, D_FOUR), full(D_MODEL, D_MODEL), full(1, D_MODEL),
            full(BUCKET_LANES, D_MODEL), full(BUCKET_LANES, 1), full(TM_OUT, TM_OUT),
        ],
        out_specs=[
            pl.BlockSpec((TM_OUT, D_MODEL), lambda i: (i, 0)),
            row3, row3, row3, row3,
            full(BUCKET_LANES, 1),
        ],
        out_shape=[
            jax.ShapeDtypeStruct((n, D_MODEL), F32),
            jax.ShapeDtypeStruct((nt, 1, TM_OUT), I32),
            jax.ShapeDtypeStruct((nt, 1, TM_OUT), I32),
            jax.ShapeDtypeStruct((nt, 1, TM_OUT), F32),
            jax.ShapeDtypeStruct((nt, 1, TM_OUT), F32),
            jax.ShapeDtypeStruct((BUCKET_LANES, 1), F32),
        ],
        scratch_shapes=[pltpu.VMEM((BUCKET_LANES, 1), F32)],
        compiler_params=pltpu.CompilerParams(
            dimension_semantics=("arbitrary",), vmem_limit_bytes=V7X_VMEM_LIMIT),
        name="mixout",
    )(oa, yf, x2, ga, gf, wout_bf, gm, wrt, br, tri)


def _pair_tables():
    lo, hi = [], []
    for a in range(EPG):
        for b in range(a + 1, EPG):
            lo.append(a)
            hi.append(b)
    return np.asarray(lo, np.int32), np.asarray(hi, np.int32)


_PAIR_LO, _PAIR_HI = _pair_tables()


def _moe_kernel(tg_ref, tlo_ref, thi_ref, tnv_ref, nused_ref,
                src_ref, srcn_ref, wl_ref, wh_ref, h_hbm, wg_ref, wu_ref, wd_ref, gm_ref, gfin_ref,
                out_hbm, hbuf, obuf, sem_g, sem_s):
    t = pl.program_id(0)
    n_used = nused_ref[0]
    slot = t % 2

    def gather_start(idx_ref, sl):
        for r in range(TM_MOE):
            pltpu.make_async_copy(h_hbm.at[pl.ds(idx_ref[0, 0, r], 1)], hbuf.at[sl, pl.ds(r, 1)],
                                  sem_g.at[sl]).start()

    def gather_wait(sl):
        for r in range(TM_MOE):
            pltpu.make_async_copy(h_hbm.at[pl.ds(0, 1)], hbuf.at[sl, pl.ds(r, 1)], sem_g.at[sl]).wait()

    def scatter_start(sl, nv):
        for r in range(TM_MOE):
            @pl.when(r < nv)
            def _():
                pltpu.make_async_copy(obuf.at[sl, pl.ds(r, 1)], out_hbm.at[pl.ds(src_ref[0, 0, r], 1)],
                                      sem_s.at[sl]).start()

    def scatter_wait(sl, nv):
        for r in range(TM_MOE):
            @pl.when(r < nv)
            def _():
                pltpu.make_async_copy(obuf.at[sl, pl.ds(r, 1)], out_hbm.at[pl.ds(0, 1)], sem_s.at[sl]).wait()

    @pl.when(jnp.logical_and(t == 0, n_used > 0))
    def _():
        gather_start(src_ref, 0)

    @pl.when(t + 1 < n_used)
    def _():
        gather_start(srcn_ref, 1 - slot)

    @pl.when(t < n_used)
    def _():
        gather_wait(slot)

        @pl.when(t >= 2)
        def _():
            scatter_wait(slot, tnv_ref[jnp.maximum(t - 2, 0)])

        lo = tlo_ref[t]
        hi = thi_ref[t]
        hrows = hbuf[slot]
        hn = _rms(hrows, gm_ref[...]).astype(BF16)

        def expert(e, w):
            gate = jnp.dot(hn, wg_ref[e], preferred_element_type=F32)
            up = jnp.dot(hn, wu_ref[e], preferred_element_type=F32)
            act = (gate * jax.nn.sigmoid(gate) * up).astype(BF16)
            return w * jnp.dot(act, wd_ref[e], preferred_element_type=F32)

        y = expert(lo, wl_ref[...]) + expert(hi, wh_ref[...])
        obuf[slot] = _rms(hrows + y, gfin_ref[...])
        scatter_start(slot, tnv_ref[t])

        @pl.when(t == n_used - 1)
        def _():
            @pl.when(t >= 1)
            def _():
                scatter_wait(1 - slot, tnv_ref[jnp.maximum(t - 1, 0)])

            scatter_wait(slot, tnv_ref[t])


def _moe(h, src3, wl_s, wh_s, tile_g, tile_lo, tile_hi, tile_nv, n_used, wg, wu, wd, gm, gfin):
    n = h.shape[0]
    nt = src3.shape[0]
    smem_blk = lambda fn: pl.BlockSpec((1, 1, TM_MOE), fn, memory_space=pltpu.SMEM)
    grid_spec = pltpu.PrefetchScalarGridSpec(
        num_scalar_prefetch=5,
        grid=(nt,),
        in_specs=[
            smem_blk(lambda t, *_: (t, 0, 0)),
            smem_blk(lambda t, *_: (jnp.minimum(t + 1, nt - 1), 0, 0)),
            pl.BlockSpec((TM_MOE, 1), lambda t, *_: (t, 0)),
            pl.BlockSpec((TM_MOE, 1), lambda t, *_: (t, 0)),
            pl.BlockSpec(memory_space=pl.ANY),
            pl.BlockSpec((None, EPG, D_MODEL, D_EXPERT), lambda t, tg, *_: (tg[t], 0, 0, 0)),
            pl.BlockSpec((None, EPG, D_MODEL, D_EXPERT), lambda t, tg, *_: (tg[t], 0, 0, 0)),
            pl.BlockSpec((None, EPG, D_EXPERT, D_MODEL), lambda t, tg, *_: (tg[t], 0, 0, 0)),
            pl.BlockSpec((1, D_MODEL), lambda t, *_: (0, 0)),
            pl.BlockSpec((1, D_MODEL), lambda t, *_: (0, 0)),
        ],
        out_specs=pl.BlockSpec(memory_space=pl.ANY),
        scratch_shapes=[
            pltpu.VMEM((2, TM_MOE, D_MODEL), F32),
            pltpu.VMEM((2, TM_MOE, D_MODEL), F32),
            pltpu.SemaphoreType.DMA((2,)),
            pltpu.SemaphoreType.DMA((2,)),
        ],
    )
    return pl.pallas_call(
        _moe_kernel,
        grid_spec=grid_spec,
        out_shape=jax.ShapeDtypeStruct((n, D_MODEL), F32),
        compiler_params=pltpu.CompilerParams(
            dimension_semantics=("arbitrary",), vmem_limit_bytes=V7X_VMEM_LIMIT),
        name="moe",
    )(tile_g, tile_lo, tile_hi, tile_nv, n_used, src3, src3, wl_s, wh_s, h, wg, wu, wd, gm, gfin)


def _bucket_plan(bucket, rank, counts, n):
    nt = n // TM_MOE + N_BUCKETS
    tiles_b = (counts + (TM_MOE - 1)) // TM_MOE
    tile_end = jnp.cumsum(tiles_b)
    tile_start = tile_end - tiles_b
    n_used = tile_end[-1]
    dest = tile_start[bucket] * TM_MOE + rank
    src = jnp.zeros((nt * TM_MOE,), I32).at[dest].set(jnp.arange(n, dtype=I32))
    t_idx = jnp.arange(nt, dtype=I32)
    tb = jnp.searchsorted(tile_end, t_idx, side="right").astype(I32)
    tb_last = tb[jnp.maximum(n_used - 1, 0)]
    tb = jnp.minimum(jnp.where(t_idx < n_used, tb, tb_last), N_BUCKETS - 1)
    pair = tb % N_PAIRS
    tile_g = tb // N_PAIRS
    tile_lo = jnp.asarray(_PAIR_LO)[pair]
    tile_hi = jnp.asarray(_PAIR_HI)[pair]
    nv = jnp.clip(counts[tb] - (t_idx - tile_start[tb]) * TM_MOE, 0, TM_MOE)
    tile_nv = jnp.where(t_idx < n_used, nv, 0).astype(I32)
    return src, tile_g, tile_lo, tile_hi, tile_nv, n_used.reshape(1).astype(I32)


def kernel(x, norm_mix, w_in, rpb, w_four, b_four, g_attn_out, g_four_out, w_out, norm_moe,
           w_router_coarse, b_router_coarse, w_router_fine, b_router_fine, w_gate, w_up, w_down, norm_final):
    b, seq, d = x.shape
    assert (seq, d) == (SEQ, D_MODEL) and norm_mix.shape[0] == 1
    n = b * seq
    x2 = x.reshape(n, d)

    qkv, u = _inproj(x2, norm_mix[0][None], w_in[0].astype(BF16))

    bias_tab = jnp.where(_BIAS_OK[None], rpb[0][:, _BIAS_DR, _BIAS_DC], NEG)
    oa = _attention(qkv.reshape(b, seq, 3 * D_ATTN), bias_tab)

    eye4 = jnp.eye(4, dtype=F32)
    wf = w_four[0].reshape(2, 4, FOUR_GROUP_DIM, FOUR_GROUP_DIM)
    wbd = (eye4[None, :, None, :, None] * wf[:, :, :, None, :]).reshape(2, 256, 256).astype(BF16)
    yf = _fourier(u.reshape(b, ROWS, GRID_W, D_FOUR), wbd, b_four[0][None])

    wrt = jnp.zeros((BUCKET_LANES, d), F32)
    wrt = wrt.at[0:N_GROUPS].set(w_router_coarse[0].T).at[8:8 + N_EXPERTS].set(w_router_fine[0].T)
    br = jnp.zeros((BUCKET_LANES, 1), F32)
    br = br.at[0:N_GROUPS, 0].set(b_router_coarse[0]).at[8:8 + N_EXPERTS, 0].set(b_router_fine[0])
    tri = (np.arange(TM_OUT)[:, None] < np.arange(TM_OUT)[None, :]).astype(np.float32)
    h, bucket, rank, wlo, whi, cnt = _mixout(
        oa.reshape(n, D_ATTN), yf.reshape(n, D_FOUR), x2, g_attn_out[0][None], g_four_out[0][None],
        w_out[0].astype(BF16), norm_moe[0][None], wrt.astype(BF16), br, jnp.asarray(tri, BF16))

    counts = cnt[:N_BUCKETS, 0].astype(I32)
    src, tile_g, tile_lo, tile_hi, tile_nv, n_used = _bucket_plan(bucket.reshape(n), rank.reshape(n), counts, n)
    wl_s = wlo.reshape(n)[src][:, None]
    wh_s = whi.reshape(n)[src][:, None]
    shape_e = (N_GROUPS, EPG)
    out = _moe(h, src.reshape(-1, 1, TM_MOE), wl_s, wh_s, tile_g, tile_lo, tile_hi, tile_nv, n_used,
               w_gate[0].astype(BF16).reshape(shape_e + (d, D_EXPERT)),
               w_up[0].astype(BF16).reshape(shape_e + (d, D_EXPERT)),
               w_down[0].astype(BF16).reshape(shape_e + (D_EXPERT, d)),
               norm_moe[0][None], norm_final[None])
    return out.reshape(b, seq, d)
```

```python
import functools

import numpy as np
import jax
import jax.numpy as jnp
from jax import lax
from jax.experimental import pallas as pl
from jax.experimental.pallas import tpu as pltpu

F32 = jnp.float32
BF16 = jnp.bfloat16
I32 = jnp.int32

D_MODEL = 1024
SEQ = 4096
GRID_W = 64
ROWS = SEQ // GRID_W
D_ATTN = 512
D_FOUR = 512
N_HEADS = 8
HEAD_DIM = 64
WIN_H = 8
WIN_W = 16
N_FOUR_GROUPS = 8
FOUR_GROUP_DIM = 64
D_PROJ = 3 * D_ATTN + D_FOUR
N_GROUPS = 4
EPG = 8
N_EXPERTS = N_GROUPS * EPG
D_EXPERT = 256
EPS = 1e-6
NEG = -1e30

V7X_VMEM_LIMIT = 56 * 1024 * 1024

TM_IN = 512
TM_OUT = 512
TM_MOE = 128
D_ROW = D_MODEL + 128
ROWS_PER_STEP = 1024
N_PAIRS = EPG * (EPG - 1) // 2
N_BUCKETS = N_GROUPS * N_PAIRS
BUCKET_LANES = 128

QB_ROWS = 8
QB_COLS = 16
KB_ROWS = 16
KB_COLS = 32


def _rms(x, g):
    ms = jnp.mean(x * x, axis=-1, keepdims=True)
    return x * lax.rsqrt(ms + EPS) * g


def _inproj_kernel(x_ref, g_ref, w_ref, qkv_ref, u_ref):
    xn = _rms(x_ref[...], g_ref[...]).astype(BF16)
    p = jnp.dot(xn, w_ref[...], preferred_element_type=F32)
    qkv_ref[:, :D_ATTN] = (p[:, :D_ATTN] * (HEAD_DIM ** -0.5)).astype(BF16)
    qkv_ref[:, D_ATTN:] = p[:, D_ATTN:3 * D_ATTN].astype(BF16)
    u_ref[...] = p[:, 3 * D_ATTN:]


def _inproj(x2, g, w_bf):
    n = x2.shape[0]
    return pl.pallas_call(
        _inproj_kernel,
        grid=(n // TM_IN,),
        in_specs=[
            pl.BlockSpec((TM_IN, D_MODEL), lambda i: (i, 0)),
            pl.BlockSpec((1, D_MODEL), lambda i: (0, 0)),
            pl.BlockSpec((D_MODEL, D_PROJ), lambda i: (0, 0)),
        ],
        out_specs=[
            pl.BlockSpec((TM_IN, 3 * D_ATTN), lambda i: (i, 0)),
            pl.BlockSpec((TM_IN, D_FOUR), lambda i: (i, 0)),
        ],
        out_shape=[
            jax.ShapeDtypeStruct((n, 3 * D_ATTN), BF16),
            jax.ShapeDtypeStruct((n, D_FOUR), F32),
        ],
        compiler_params=pltpu.CompilerParams(
            dimension_semantics=("parallel",), vmem_limit_bytes=V7X_VMEM_LIMIT),
        name="inproj",
    )(x2, g, w_bf)


_KCOL_START = (0, 8, 24, 32)
_KCOL_SHIFTED = (False, True, True, False)
_KCOL_OFF = (0, 0, 16, 32)
_COL_TYPE = (0, 1, 1, 2)


def _bias_index_tables():
    dr = np.zeros((9, 128, 512), np.int32)
    dc = np.zeros((9, 128, 512), np.int32)
    ok = np.zeros((9, 128, 512), bool)
    qi, qc = np.divmod(np.arange(128), QB_COLS)
    ki, kc = np.divmod(np.arange(512), KB_COLS)
    for rt, (q0, k0) in enumerate(((0, 0), (8, 4), (56, 48))):
        qrow = q0 + qi
        krow = k0 + ki
        rs = np.clip(qrow - WIN_H // 2, 0, ROWS - WIN_H)
        rok = (krow[None, :] >= rs[:, None]) & (krow[None, :] < rs[:, None] + WIN_H)
        drr = krow[None, :] - qrow[:, None] + (WIN_H - 1)
        for ct, (c0, kc0) in enumerate(((0, 0), (16, 8), (48, 32))):
            qcol = c0 + qc
            kcol = kc0 + kc
            cs = np.clip(qcol - WIN_W // 2, 0, GRID_W - WIN_W)
            cok = (kcol[None, :] >= cs[:, None]) & (kcol[None, :] < cs[:, None] + WIN_W)
            dcc = kcol[None, :] - qcol[:, None] + (WIN_W - 1)
            t = rt * 3 + ct
            ok[t] = rok & cok
            dr[t] = np.where(ok[t], drr, 0)
            dc[t] = np.where(ok[t], dcc, 0)
    return dr, dc, ok


_BIAS_DR, _BIAS_DC, _BIAS_OK = _bias_index_tables()


def _bias_selectors():
    ok = _BIAS_OK.reshape(3, 3, QB_ROWS, QB_COLS, KB_ROWS, KB_COLS)
    dr = _BIAS_DR.reshape(ok.shape)
    dc = _BIAS_DC.reshape(ok.shape)
    row_ok = ok.any(axis=(1, 3, 5))
    col_ok = ok.any(axis=(0, 2, 4))
    dr_r = dr.max(axis=(1, 3, 5))
    dc_c = dc.max(axis=(0, 2, 4))
    sr = (np.arange(2 * WIN_H - 1)[None, None, None, :] == dr_r[..., None]) & row_ok[..., None]
    sc = (np.arange(2 * WIN_W - 1)[:, None, None, None] == dc_c[None]) & col_ok[None]
    return sr.astype(np.float32), sc.astype(np.float32)


_BIAS_SR, _BIAS_SC = _bias_selectors()


def _bias_tables(rpb):
    hi = lax.Precision.HIGHEST
    t1 = jnp.einsum('hab,bcqk->hacqk', rpb, jnp.asarray(_BIAS_SC), precision=hi)
    t2 = jnp.einsum('riya,hacqk->hrciqyk', jnp.asarray(_BIAS_SR), t1, precision=hi)
    return jnp.where(_BIAS_OK[None], t2.reshape(N_HEADS, 9, 128, 512), NEG)


def _attn_kernel(q_ref, k_ref, v_ref, bias_ref, o_ref, ksh_ref, vsh_ref):
    zpad = jnp.zeros((8, 128), F32)
    ksh_ref[...] = jnp.concatenate([k_ref[...].astype(F32)[8:], zpad], axis=0).astype(BF16)
    vsh_ref[...] = jnp.concatenate([v_ref[...].astype(F32)[8:], zpad], axis=0).astype(BF16)
    lane = lax.broadcasted_iota(I32, (1, 128), 1)
    head_masks = (lane < HEAD_DIM, lane >= HEAD_DIM)

    def row_block(rb, carry):
        rb = jnp.asarray(rb, I32)
        ks = jnp.clip(QB_ROWS * rb - WIN_H // 2, 0, ROWS - KB_ROWS)
        rt = jnp.where(rb == 0, 0, jnp.where(rb == ROWS // QB_ROWS - 1, 2, 1))
        for j in range(GRID_W // QB_COLS):
            kr, vr = (ksh_ref, vsh_ref) if _KCOL_SHIFTED[j] else (k_ref, v_ref)
            q = jnp.concatenate(
                [q_ref[pl.ds(pl.multiple_of((QB_ROWS * rb + i) * GRID_W + QB_COLS * j, 16), QB_COLS), :]
                 for i in range(QB_ROWS)], axis=0)
            kstarts = [pl.multiple_of((ks + i) * GRID_W + _KCOL_OFF[j], 16) for i in range(KB_ROWS)]
            kt = jnp.concatenate([kr[pl.ds(s, KB_COLS), :] for s in kstarts], axis=0)
            vt = jnp.concatenate([vr[pl.ds(s, KB_COLS), :] for s in kstarts], axis=0)
            outs = []
            for hh in range(2):
                hm = head_masks[hh]
                qm = jnp.where(hm, q, jnp.zeros_like(q))
                s = lax.dot_general(qm, kt, (((1,), (1,)), ((), ())), preferred_element_type=F32)
                s = s + bias_ref[hh, rt * 3 + _COL_TYPE[j]]
                mx = jnp.max(s, axis=-1, keepdims=True)
                p = jnp.exp(s - mx).astype(BF16)
                vm = jnp.where(hm, vt, jnp.ones_like(vt))
                o = jnp.dot(p, vm, preferred_element_type=F32)
                outs.append(o / pltpu.roll(o, HEAD_DIM, 1))
            out = jnp.where(head_masks[0], outs[0], outs[1]).astype(BF16)
            for i in range(QB_ROWS):
                o_ref[pl.ds(pl.multiple_of((QB_ROWS * rb + i) * GRID_W + QB_COLS * j, 16), QB_COLS), :] = (
                    out[QB_COLS * i:QB_COLS * (i + 1)])
        return carry

    lax.fori_loop(0, ROWS // QB_ROWS, row_block, 0)


def _attention(qkv3, bias_tab):
    b = qkv3.shape[0]
    n_hp = N_HEADS // 2
    blk = lambda off: pl.BlockSpec((None, SEQ, 128), lambda hp, bi: (bi, 0, off + hp))
    return pl.pallas_call(
        _attn_kernel,
        grid=(n_hp, b),
        in_specs=[
            blk(0), blk(n_hp), blk(2 * n_hp),
            pl.BlockSpec((2, 9, 128, 512), lambda hp, bi: (hp, 0, 0, 0)),
        ],
        out_specs=pl.BlockSpec((None, SEQ, 128), lambda hp, bi: (bi, 0, hp)),
        out_shape=jax.ShapeDtypeStruct((b, SEQ, D_ATTN), BF16),
        scratch_shapes=[pltpu.VMEM((SEQ, 128), BF16), pltpu.VMEM((SEQ, 128), BF16)],
        compiler_params=pltpu.CompilerParams(
            dimension_semantics=("parallel", "parallel"), vmem_limit_bytes=V7X_VMEM_LIMIT),
        name="nattn",
    )(qkv3, qkv3, qkv3, bias_tab)


def _fourier_tables():
    n = 64
    k = np.arange(n)
    ang = 2.0 * np.pi * np.outer(k, k) / n
    c64, s64 = np.cos(ang), np.sin(ang)
    eye8 = np.eye(8)
    k1 = np.concatenate([np.kron(c64, eye8), np.kron(-s64, eye8)], axis=0)
    a = np.arange(8)[:, None, None, None, None]
    t2p = np.arange(64)[None, :, None, None, None]
    jo = np.arange(8)[None, None, :, None, None]
    c = np.arange(8)[None, None, None, :, None]
    j = np.arange(8)[None, None, None, None, :]
    idx = ((8 * c + j) * (8 * a + jo + 64 * t2p)) % SEQ
    th = 2.0 * np.pi * idx / SEQ
    cos_t, sin_t = np.cos(th), np.sin(th)
    delta = np.eye(8)[None, None, :, None, :, None]
    def expand(m):
        return (m[:, :, :, :, None, :] * delta).reshape(8, 512, 512)
    cc, ss = expand(cos_t), expand(sin_t)
    la = np.concatenate([np.concatenate([cc, ss], axis=2), np.concatenate([-ss, cc], axis=2)], axis=1)
    cbd = np.kron(np.eye(4), c64)
    sbd = np.kron(np.eye(4), s64)
    cs = np.concatenate([cbd, sbd], axis=0)
    return k1.astype(np.float32), la.astype(np.float32), cs.astype(np.float32)


_K1_NP, _LA_NP, _CS_NP = _fourier_tables()


def _fourier_kernel(u_ref, k1_ref, la_ref, cs_ref, wbd_ref, bf_ref, y_ref, zs_ref):
    s = pl.program_id(1)

    @pl.when(s < 8)
    def _():
        xc = u_ref[...].reshape(512, D_FOUR).astype(BF16)
        z = jnp.dot(k1_ref[...], xc, preferred_element_type=F32)
        zs_ref[s] = z.astype(BF16)

    @pl.when(s >= 8)
    def _():
        off = pl.multiple_of((s - 8) * 64, 64)
        rhs = jnp.concatenate(
            [zs_ref[c, pl.ds(part * 512 + off, 64), :] for part in range(2) for c in range(8)], axis=0)
        x = jnp.dot(la_ref[...], rhs, preferred_element_type=F32)
        xr = x[:512].astype(BF16)
        xi = x[512:].astype(BF16)
        halves = []
        for hf in range(2):
            sl = slice(256 * hf, 256 * (hf + 1))
            lhs = jnp.concatenate([xr[:, sl], xi[:, sl]], axis=1)
            f = jnp.dot(lhs, cs_ref[...], preferred_element_type=F32) * (1.0 / 512.0)
            halves.append(jnp.dot(f.astype(BF16), wbd_ref[hf], preferred_element_type=F32))
        y = jnp.concatenate(halves, axis=1) + bf_ref[...]
        y_ref[...] = y.reshape(64, 8, D_FOUR)


def _fourier(u4, wbd, bf):
    b = u4.shape[0]
    k1 = jnp.asarray(_K1_NP).astype(BF16)
    la = jnp.asarray(_LA_NP).astype(BF16)
    cs = jnp.asarray(_CS_NP).astype(BF16)
    return pl.pallas_call(
        _fourier_kernel,
        grid=(b, 16),
        in_specs=[
            pl.BlockSpec((None, 64, 8, D_FOUR), lambda bi, s: (bi, 0, jnp.minimum(s, 7), 0)),
            pl.BlockSpec((1024, 512), lambda bi, s: (0, 0)),
            pl.BlockSpec((None, 1024, 1024), lambda bi, s: (jnp.maximum(s - 8, 0), 0, 0)),
            pl.BlockSpec((512, 256), lambda bi, s: (0, 0)),
            pl.BlockSpec((2, 256, 256), lambda bi, s: (0, 0, 0)),
            pl.BlockSpec((1, D_FOUR), lambda bi, s: (0, 0)),
        ],
        out_specs=pl.BlockSpec((None, 64, 8, D_FOUR), lambda bi, s: (bi, 0, jnp.maximum(s - 8, 0), 0)),
        out_shape=jax.ShapeDtypeStruct((b, 64, 64, D_FOUR), F32),
        scratch_shapes=[pltpu.VMEM((8, 1024, 512), BF16)],
        compiler_params=pltpu.CompilerParams(
            dimension_semantics=("parallel", "arbitrary"), vmem_limit_bytes=V7X_VMEM_LIMIT),
        name="fourier",
    )(u4, k1, la, cs, wbd, bf)


def _mixout_kernel(oa_ref, yf_ref, x_ref, ga_ref, gf_ref, wout_ref, gm_ref, wrt_ref, br_ref, tri_ref,
                   h_ref, bucket_ref, rank_ref, cnt_ref, carry_ref):
    i = pl.program_id(0)

    @pl.when(i == 0)
    def _():
        carry_ref[...] = jnp.zeros_like(carry_ref)

    na = _rms(oa_ref[...].astype(F32), ga_ref[...]).astype(BF16)
    nf = _rms(yf_ref[...], gf_ref[...]).astype(BF16)
    merged = jnp.concatenate([na, nf], axis=1)
    h = x_ref[...] + jnp.dot(merged, wout_ref[...], preferred_element_type=F32)
    h_ref[:, :D_MODEL] = h
    hn = _rms(h, gm_ref[...]).astype(BF16)
    lt = lax.dot_general(wrt_ref[...], hn, (((1,), (1,)), ((), ())), preferred_element_type=F32)
    lt = lt + br_ref[...]
    c = [lt[k:k + 1] for k in range(N_GROUPS)]
    cmax = jnp.maximum(jnp.maximum(c[0], c[1]), jnp.maximum(c[2], c[3]))
    e = [jnp.exp(ck - cmax) for ck in c]
    esum = (e[0] + e[1]) + (e[2] + e[3])
    p = [ek / esum for ek in e]
    pmax = jnp.maximum(jnp.maximum(p[0], p[1]), jnp.maximum(p[2], p[3]))
    g = jnp.where(p[0] == pmax, 0, jnp.where(p[1] == pmax, 1, jnp.where(p[2] == pmax, 2, 3))).astype(I32)
    fine = jnp.where(g == 0, lt[8:16], jnp.where(g == 1, lt[16:24], jnp.where(g == 2, lt[24:32], lt[32:40])))
    rows = lax.broadcasted_iota(I32, fine.shape, 0)
    v1 = jnp.max(fine, axis=0, keepdims=True)
    i1 = jnp.min(jnp.where(fine == v1, rows, EPG), axis=0, keepdims=True)
    rest = jnp.where(rows == i1, -jnp.inf, fine)
    v2 = jnp.max(rest, axis=0, keepdims=True)
    i2 = jnp.min(jnp.where(rest == v2, rows, EPG), axis=0, keepdims=True)
    t = jnp.exp(v2 - v1)
    den = 1.0 + t
    w1 = (1.0 / den) * pmax
    w2 = (t / den) * pmax
    lo = jnp.minimum(i1, i2)
    hi = jnp.maximum(i1, i2)
    first_is_lo = i1 < i2
    wlo = jnp.where(first_is_lo, w1, w2)
    whi = jnp.where(first_is_lo, w2, w1)
    pair = lax.shift_right_logical(lo * (2 * EPG - 1 - lo), 1) + (hi - lo - 1)
    bucket = g * N_PAIRS + pair
    bucket_ref[...] = bucket
    brow = lax.broadcasted_iota(I32, (BUCKET_LANES, TM_OUT), 0)
    wcols = jnp.where(brow == 0, wlo, jnp.where(brow == 1, whi, 0.0))
    h_ref[:, D_MODEL:] = wcols.T
    onehot = (brow == bucket).astype(F32)
    prefix = jnp.dot(onehot.astype(BF16), tri_ref[...], preferred_element_type=F32)
    rank = jnp.sum(onehot * (prefix + carry_ref[...]), axis=0, keepdims=True)
    rank_ref[...] = rank.astype(I32)
    carry_ref[...] = carry_ref[...] + jnp.sum(onehot, axis=1, keepdims=True)
    cnt_ref[...] = carry_ref[...]


def _mixout(oa, yf, x2, ga, gf, wout_bf, gm, wrt, br, tri):
    n = x2.shape[0]
    nt = n // TM_OUT
    full = lambda *shape: pl.BlockSpec(shape, lambda i: (0,) * len(shape))
    row3 = pl.BlockSpec((None, 1, TM_OUT), lambda i: (i, 0, 0))
    return pl.pallas_call(
        _mixout_kernel,
        grid=(nt,),
        in_specs=[
            pl.BlockSpec((TM_OUT, D_ATTN), lambda i: (i, 0)),
            pl.BlockSpec((TM_OUT, D_FOUR), lambda i: (i, 0)),
            pl.BlockSpec((TM_OUT, D_MODEL), lambda i: (i, 0)),
            full(1, D_ATTN), full(1, D_FOUR), full(D_MODEL, D_MODEL), full(1, D_MODEL),
            full(BUCKET_LANES, D_MODEL), full(BUCKET_LANES, 1), full(TM_OUT, TM_OUT),
        ],
        out_specs=[
            pl.BlockSpec((TM_OUT, D_ROW), lambda i: (i, 0)),
            row3, row3,
            full(BUCKET_LANES, 1),
        ],
        out_shape=[
            jax.ShapeDtypeStruct((n, D_ROW), F32),
            jax.ShapeDtypeStruct((nt, 1, TM_OUT), I32),
            jax.ShapeDtypeStruct((nt, 1, TM_OUT), I32),
            jax.ShapeDtypeStruct((BUCKET_LANES, 1), F32),
        ],
        scratch_shapes=[pltpu.VMEM((BUCKET_LANES, 1), F32)],
        compiler_params=pltpu.CompilerParams(
            dimension_semantics=("arbitrary",), vmem_limit_bytes=V7X_VMEM_LIMIT),
        name="mixout",
    )(oa, yf, x2, ga, gf, wout_bf, gm, wrt, br, tri)


def _pair_tables():
    lo, hi = [], []
    for a in range(EPG):
        for b in range(a + 1, EPG):
            lo.append(a)
            hi.append(b)
    return np.asarray(lo, np.int32), np.asarray(hi, np.int32)


_PAIR_LO, _PAIR_HI = _pair_tables()


def _dispatch_kernel(tnv_ref, dest_ref, h_ref, hs_hbm, zbuf, zsem, sem):
    k = pl.program_id(0)
    n_tiles = hs_hbm.shape[0] // TM_MOE

    def zero_copy(t):
        return pltpu.make_async_copy(zbuf, hs_hbm.at[pl.ds(pl.multiple_of(t * TM_MOE, TM_MOE), TM_MOE)], zsem)

    @pl.when(k == 0)
    def _():
        zbuf[...] = jnp.zeros_like(zbuf)

        def zstart(t, c):
            @pl.when(tnv_ref[t] < TM_MOE)
            def _():
                zero_copy(t).start()
            return c

        def zwait(t, c):
            @pl.when(tnv_ref[t] < TM_MOE)
            def _():
                zero_copy(t).wait()
            return c

        lax.fori_loop(0, n_tiles, zstart, 0)
        lax.fori_loop(0, n_tiles, zwait, 0)

    def rows(r8, c):
        for u in range(8):
            r = r8 * 8 + u
            pltpu.make_async_copy(h_ref.at[pl.ds(r, 1)], hs_hbm.at[pl.ds(dest_ref[0, 0, r], 1)], sem).start()
        return c

    lax.fori_loop(0, ROWS_PER_STEP // 8, rows, 0)
    pltpu.make_async_copy(h_ref, hs_hbm.at[pl.ds(0, ROWS_PER_STEP)], sem).wait()


def _dispatch(h_ext, dest3, tile_nv, n_slots):
    n = h_ext.shape[0]
    grid_spec = pltpu.PrefetchScalarGridSpec(
        num_scalar_prefetch=1,
        grid=(n // ROWS_PER_STEP,),
        in_specs=[
            pl.BlockSpec((1, 1, ROWS_PER_STEP), lambda k, *_: (k, 0, 0), memory_space=pltpu.SMEM),
            pl.BlockSpec((ROWS_PER_STEP, D_ROW), lambda k, *_: (k, 0)),
        ],
        out_specs=pl.BlockSpec(memory_space=pl.ANY),
        scratch_shapes=[pltpu.VMEM((TM_MOE, D_ROW), F32), pltpu.SemaphoreType.DMA(()), pltpu.SemaphoreType.DMA(())],
    )
    return pl.pallas_call(
        _dispatch_kernel,
        grid_spec=grid_spec,
        out_shape=jax.ShapeDtypeStruct((n_slots, D_ROW), F32),
        compiler_params=pltpu.CompilerParams(
            dimension_semantics=("arbitrary",), vmem_limit_bytes=V7X_VMEM_LIMIT),
        name="dispatch",
    )(tile_nv, dest3, h_ext)


def _combine_kernel(dest_ref, ys_hbm, o_ref, sem):
    def rows(r8, c):
        for u in range(8):
            r = r8 * 8 + u
            pltpu.make_async_copy(ys_hbm.at[pl.ds(dest_ref[0, 0, r], 1)], o_ref.at[pl.ds(r, 1)], sem).start()
        return c

    lax.fori_loop(0, ROWS_PER_STEP // 8, rows, 0)
    pltpu.make_async_copy(ys_hbm.at[pl.ds(0, ROWS_PER_STEP)], o_ref, sem).wait()


def _combine(ys, dest3, n):
    return pl.pallas_call(
        _combine_kernel,
        grid=(n // ROWS_PER_STEP,),
        in_specs=[
            pl.BlockSpec((1, 1, ROWS_PER_STEP), lambda k: (k, 0, 0), memory_space=pltpu.SMEM),
            pl.BlockSpec(memory_space=pl.ANY),
        ],
        out_specs=pl.BlockSpec((ROWS_PER_STEP, D_MODEL), lambda k: (k, 0)),
        out_shape=jax.ShapeDtypeStruct((n, D_MODEL), F32),
        scratch_shapes=[pltpu.SemaphoreType.DMA(())],
        compiler_params=pltpu.CompilerParams(
            dimension_semantics=("arbitrary",), vmem_limit_bytes=V7X_VMEM_LIMIT),
        name="combine",
    )(dest3, ys)


def _moe_kernel(tg_ref, tlo_ref, thi_ref, nused_ref, hs_ref, wg_ref, wu_ref, wd_ref, gm_ref, gfin_ref, ys_ref):
    t = pl.program_id(0)

    @pl.when(t < nused_ref[0])
    def _():
        hrows = hs_ref[:, :D_MODEL]
        hn = _rms(hrows, gm_ref[...]).astype(BF16)

        def expert(e, w):
            gate = jnp.dot(hn, wg_ref[e], preferred_element_type=F32)
            up = jnp.dot(hn, wu_ref[e], preferred_element_type=F32)
            act = (gate * jax.nn.sigmoid(gate) * up).astype(BF16)
            return w * jnp.dot(act, wd_ref[e], preferred_element_type=F32)

        y = (expert(tlo_ref[t], hs_ref[:, D_MODEL:D_MODEL + 1])
             + expert(thi_ref[t], hs_ref[:, D_MODEL + 1:D_MODEL + 2]))
        ys_ref[...] = _rms(hrows + y, gfin_ref[...])

    @pl.when(t >= nused_ref[0])
    def _():
        ys_ref[...] = jnp.zeros_like(ys_ref)


def _moe(hs, tile_g, tile_lo, tile_hi, n_used, wg, wu, wd, gm, gfin):
    n_slots = hs.shape[0]
    nt = n_slots // TM_MOE
    by_group = lambda t, tg, *_: (tg[t], 0, 0, 0)
    grid_spec = pltpu.PrefetchScalarGridSpec(
        num_scalar_prefetch=4,
        grid=(nt,),
        in_specs=[
            pl.BlockSpec((TM_MOE, D_ROW), lambda t, tg, tlo, thi, nu: (jnp.minimum(t, jnp.maximum(nu[0] - 1, 0)), 0)),
            pl.BlockSpec((None, EPG, D_MODEL, D_EXPERT), by_group),
            pl.BlockSpec((None, EPG, D_MODEL, D_EXPERT), by_group),
            pl.BlockSpec((None, EPG, D_EXPERT, D_MODEL), by_group),
            pl.BlockSpec((1, D_MODEL), lambda t, *_: (0, 0)),
            pl.BlockSpec((1, D_MODEL), lambda t, *_: (0, 0)),
        ],
        out_specs=pl.BlockSpec((TM_MOE, D_MODEL), lambda t, *_: (t, 0)),
    )
    return pl.pallas_call(
        _moe_kernel,
        grid_spec=grid_spec,
        out_shape=jax.ShapeDtypeStruct((n_slots, D_MODEL), F32),
        compiler_params=pltpu.CompilerParams(
            dimension_semantics=("arbitrary",), vmem_limit_bytes=V7X_VMEM_LIMIT),
        name="moe",
    )(tile_g, tile_lo, tile_hi, n_used, hs, wg, wu, wd, gm, gfin)


def _bucket_plan(bucket, rank, counts, n):
    nt = n // TM_MOE + N_BUCKETS
    tiles_b = (counts + (TM_MOE - 1)) // TM_MOE
    tile_end = jnp.cumsum(tiles_b)
    tile_start = tile_end - tiles_b
    n_used = tile_end[-1]
    b_ids = jnp.arange(N_BUCKETS, dtype=I32)
    dest = rank + TM_MOE * jnp.sum(jnp.where(bucket[:, None] == b_ids[None, :], tile_start[None, :], 0), axis=1)
    t_idx = jnp.arange(nt, dtype=I32)
    tb = jnp.sum((tile_end[None, :] <= t_idx[:, None]).astype(I32), axis=1)
    tb_last = jnp.sum((tile_end <= n_used - 1).astype(I32))
    tb = jnp.minimum(jnp.where(t_idx < n_used, tb, tb_last), N_BUCKETS - 1)
    sel = tb[:, None] == b_ids[None, :]
    pick = lambda table: jnp.sum(jnp.where(sel, table[None, :], 0), axis=1).astype(I32)
    tile_g = tb // N_PAIRS
    tile_lo = pick(jnp.asarray(np.tile(_PAIR_LO, N_GROUPS)))
    tile_hi = pick(jnp.asarray(np.tile(_PAIR_HI, N_GROUPS)))
    nv = jnp.clip(pick(counts) - (t_idx - pick(tile_start)) * TM_MOE, 0, TM_MOE)
    tile_nv = jnp.where(t_idx < n_used, nv, 0).astype(I32)
    return dest.astype(I32), tile_g.astype(I32), tile_lo, tile_hi, tile_nv, n_used.reshape(1).astype(I32)


def kernel(x, norm_mix, w_in, rpb, w_four, b_four, g_attn_out, g_four_out, w_out, norm_moe,
           w_router_coarse, b_router_coarse, w_router_fine, b_router_fine, w_gate, w_up, w_down, norm_final):
    b, seq, d = x.shape
    assert (seq, d) == (SEQ, D_MODEL) and norm_mix.shape[0] == 1
    n = b * seq
    x2 = x.reshape(n, d)

    qkv, u = _inproj(x2, norm_mix[0][None], w_in[0].astype(BF16))

    oa = _attention(qkv.reshape(b, seq, 3 * D_ATTN), _bias_tables(rpb[0]))

    eye4 = jnp.eye(4, dtype=F32)
    wf = w_four[0].reshape(2, 4, FOUR_GROUP_DIM, FOUR_GROUP_DIM)
    wbd = (eye4[None, :, None, :, None] * wf[:, :, :, None, :]).reshape(2, 256, 256).astype(BF16)
    yf = _fourier(u.reshape(b, ROWS, GRID_W, D_FOUR), wbd, b_four[0][None])

    wrt = jnp.zeros((BUCKET_LANES, d), F32)
    wrt = wrt.at[0:N_GROUPS].set(w_router_coarse[0].T).at[8:8 + N_EXPERTS].set(w_router_fine[0].T)
    br = jnp.zeros((BUCKET_LANES, 1), F32)
    br = br.at[0:N_GROUPS, 0].set(b_router_coarse[0]).at[8:8 + N_EXPERTS, 0].set(b_router_fine[0])
    tri = (np.arange(TM_OUT)[:, None] < np.arange(TM_OUT)[None, :]).astype(np.float32)
    h_ext, bucket, rank, cnt = _mixout(
        oa.reshape(n, D_ATTN), yf.reshape(n, D_FOUR), x2, g_attn_out[0][None], g_four_out[0][None],
        w_out[0].astype(BF16), norm_moe[0][None], wrt.astype(BF16), br, jnp.asarray(tri, BF16))

    counts = cnt[:N_BUCKETS, 0].astype(I32)
    dest, tile_g, tile_lo, tile_hi, tile_nv, n_used = _bucket_plan(bucket.reshape(n), rank.reshape(n), counts, n)
    dest3 = dest.reshape(n // ROWS_PER_STEP, 1, ROWS_PER_STEP)
    hs = _dispatch(h_ext, dest3, tile_nv, tile_nv.shape[0] * TM_MOE)
    shape_e = (N_GROUPS, EPG)
    ys = _moe(hs, tile_g, tile_lo, tile_hi, n_used,
              w_gate[0].astype(BF16).reshape(shape_e + (d, D_EXPERT)),
              w_up[0].astype(BF16).reshape(shape_e + (d, D_EXPERT)),
              w_down[0].astype(BF16).reshape(shape_e + (D_EXPERT, d)),
              norm_moe[0][None], norm_final[None])
    return _combine(ys, dest3, n).reshape(b, seq, d)
```

```python
import functools

import numpy as np
import jax
import jax.numpy as jnp
from jax import lax
from jax.experimental import pallas as pl
from jax.experimental.pallas import tpu as pltpu

F32 = jnp.float32
BF16 = jnp.bfloat16
I32 = jnp.int32

D_MODEL = 1024
SEQ = 4096
GRID_W = 64
ROWS = SEQ // GRID_W
D_ATTN = 512
D_FOUR = 512
N_HEADS = 8
HEAD_DIM = 64
WIN_H = 8
WIN_W = 16
N_FOUR_GROUPS = 8
FOUR_GROUP_DIM = 64
D_PROJ = 3 * D_ATTN + D_FOUR
N_GROUPS = 4
EPG = 8
N_EXPERTS = N_GROUPS * EPG
D_EXPERT = 256
EPS = 1e-6
NEG = -1e30

V7X_VMEM_LIMIT = 56 * 1024 * 1024

TM_IN = 512
TM_OUT = 512
TM_MOE = 128
D_ROW = D_MODEL + 128
ROWS_PER_STEP = 1024
TILES_PER_STEP = 2
N_PAIRS = EPG * (EPG - 1) // 2
N_BUCKETS = N_GROUPS * N_PAIRS
BUCKET_LANES = 128

QB_ROWS = 8
QB_COLS = 16
KB_ROWS = 16
KB_COLS = 32
ATTN_AHEAD = 3


def _rms(x, g):
    ms = jnp.mean(x * x, axis=-1, keepdims=True)
    return x * lax.rsqrt(ms + EPS) * g


def _inproj_kernel(x_ref, g_ref, w_ref, qkv_ref, u_ref):
    xn = _rms(x_ref[...], g_ref[...]).astype(BF16)
    p = jnp.dot(xn, w_ref[...], preferred_element_type=F32)
    qkv_ref[:, :D_ATTN] = (p[:, :D_ATTN] * (HEAD_DIM ** -0.5)).astype(BF16)
    qkv_ref[:, D_ATTN:] = p[:, D_ATTN:3 * D_ATTN].astype(BF16)
    u_ref[...] = p[:, 3 * D_ATTN:]


def _inproj(x2, g, w_bf):
    n = x2.shape[0]
    return pl.pallas_call(
        _inproj_kernel,
        grid=(n // TM_IN,),
        in_specs=[
            pl.BlockSpec((TM_IN, D_MODEL), lambda i: (i, 0)),
            pl.BlockSpec((1, D_MODEL), lambda i: (0, 0)),
            pl.BlockSpec((D_MODEL, D_PROJ), lambda i: (0, 0)),
        ],
        out_specs=[
            pl.BlockSpec((TM_IN, 3 * D_ATTN), lambda i: (i, 0)),
            pl.BlockSpec((TM_IN, D_FOUR), lambda i: (i, 0)),
        ],
        out_shape=[
            jax.ShapeDtypeStruct((n, 3 * D_ATTN), BF16),
            jax.ShapeDtypeStruct((n, D_FOUR), F32),
        ],
        compiler_params=pltpu.CompilerParams(
            dimension_semantics=("parallel",), vmem_limit_bytes=V7X_VMEM_LIMIT),
        name="inproj",
    )(x2, g, w_bf)


_KCOL_START = (0, 8, 24, 32)
_KCOL_SHIFTED = (False, True, True, False)
_KCOL_OFF = (0, 0, 16, 32)
_COL_TYPE = (0, 1, 1, 2)


def _bias_index_tables():
    dr = np.zeros((9, 128, 512), np.int32)
    dc = np.zeros((9, 128, 512), np.int32)
    ok = np.zeros((9, 128, 512), bool)
    qi, qc = np.divmod(np.arange(128), QB_COLS)
    ki, kc = np.divmod(np.arange(512), KB_COLS)
    for rt, (q0, k0) in enumerate(((0, 0), (8, 4), (56, 48))):
        qrow = q0 + qi
        krow = k0 + ki
        rs = np.clip(qrow - WIN_H // 2, 0, ROWS - WIN_H)
        rok = (krow[None, :] >= rs[:, None]) & (krow[None, :] < rs[:, None] + WIN_H)
        drr = krow[None, :] - qrow[:, None] + (WIN_H - 1)
        for ct, (c0, kc0) in enumerate(((0, 0), (16, 8), (48, 32))):
            qcol = c0 + qc
            kcol = kc0 + kc
            cs = np.clip(qcol - WIN_W // 2, 0, GRID_W - WIN_W)
            cok = (kcol[None, :] >= cs[:, None]) & (kcol[None, :] < cs[:, None] + WIN_W)
            dcc = kcol[None, :] - qcol[:, None] + (WIN_W - 1)
            t = rt * 3 + ct
            ok[t] = rok & cok
            dr[t] = np.where(ok[t], drr, 0)
            dc[t] = np.where(ok[t], dcc, 0)
    return dr, dc, ok


_BIAS_DR, _BIAS_DC, _BIAS_OK = _bias_index_tables()


def _bias_selectors():
    ok = _BIAS_OK.reshape(3, 3, QB_ROWS, QB_COLS, KB_ROWS, KB_COLS)
    dr = _BIAS_DR.reshape(ok.shape)
    dc = _BIAS_DC.reshape(ok.shape)
    row_ok = ok.any(axis=(1, 3, 5))
    col_ok = ok.any(axis=(0, 2, 4))
    dr_r = dr.max(axis=(1, 3, 5))
    dc_c = dc.max(axis=(0, 2, 4))
    sr = (np.arange(2 * WIN_H - 1)[None, None, None, :] == dr_r[..., None]) & row_ok[..., None]
    sc = (np.arange(2 * WIN_W - 1)[:, None, None, None] == dc_c[None]) & col_ok[None]
    return sr.astype(np.float32), sc.astype(np.float32)


_BIAS_SR, _BIAS_SC = _bias_selectors()


def _bias_tables(rpb):
    sc = jnp.asarray(_BIAS_SC)
    sr = jnp.asarray(_BIAS_SR)
    t1 = jnp.sum(rpb[:, :, :, None, None, None] * sc[None, None], axis=2)
    sra = jnp.moveaxis(sr, -1, 0)
    t2 = jnp.sum(sra[None, :, :, None, :, None, :, None] * t1[:, :, None, :, None, :, None, :],
                 axis=1)
    return jnp.where(_BIAS_OK[None], t2.reshape(N_HEADS, 9, 128, 512), NEG)


def _attn_kernel(q_ref, k_ref, v_ref, bias_ref, o_ref, ksh_ref, vsh_ref):
    zpad = jnp.zeros((8, 128), F32)
    ksh_ref[...] = jnp.concatenate([k_ref[...].astype(F32)[8:], zpad], axis=0).astype(BF16)
    vsh_ref[...] = jnp.concatenate([v_ref[...].astype(F32)[8:], zpad], axis=0).astype(BF16)
    lane = lax.broadcasted_iota(I32, (1, 128), 1)
    head_masks = (lane < HEAD_DIM, lane >= HEAD_DIM)

    def row_block(rb, carry):
        rb = jnp.asarray(rb, I32)
        ks = jnp.clip(QB_ROWS * rb - WIN_H // 2, 0, ROWS - KB_ROWS)
        rt = jnp.where(rb == 0, 0, jnp.where(rb == ROWS // QB_ROWS - 1, 2, 1))
        for j in range(GRID_W // QB_COLS):
            kr, vr = (ksh_ref, vsh_ref) if _KCOL_SHIFTED[j] else (k_ref, v_ref)
            q = jnp.concatenate(
                [q_ref[pl.ds(pl.multiple_of((QB_ROWS * rb + i) * GRID_W + QB_COLS * j, 16), QB_COLS), :]
                 for i in range(QB_ROWS)], axis=0)
            kstarts = [pl.multiple_of((ks + i) * GRID_W + _KCOL_OFF[j], 16) for i in range(KB_ROWS)]
            kt = jnp.concatenate([kr[pl.ds(s, KB_COLS), :] for s in kstarts], axis=0)
            vt = jnp.concatenate([vr[pl.ds(s, KB_COLS), :] for s in kstarts], axis=0)
            qm = jnp.concatenate([jnp.where(hm, q, jnp.zeros_like(q)) for hm in head_masks], axis=0)
            s = lax.dot_general(qm, kt, (((1,), (1,)), ((), ())), preferred_element_type=F32)
            tab = rt * 3 + _COL_TYPE[j]
            s = s + jnp.concatenate([bias_ref[0, tab], bias_ref[1, tab]], axis=0)
            e = jnp.exp(s - jnp.max(s, axis=-1, keepdims=True))
            o = jnp.dot(e.astype(BF16), vt, preferred_element_type=F32)
            o = o / jnp.sum(e, axis=-1, keepdims=True)
            out = jnp.where(head_masks[0], o[:128], o[128:]).astype(BF16)
            for i in range(QB_ROWS):
                o_ref[pl.ds(pl.multiple_of((QB_ROWS * rb + i) * GRID_W + QB_COLS * j, 16), QB_COLS), :] = (
                    out[QB_COLS * i:QB_COLS * (i + 1)])
        return carry

    lax.fori_loop(0, ROWS // QB_ROWS, row_block, 0)


def _attention(qkv3, bias_tab):
    b = qkv3.shape[0]
    n_hp = N_HEADS // 2
    blk = lambda off: pl.BlockSpec((None, SEQ, 128), lambda hp, bi: (bi, 0, off + hp))
    return pl.pallas_call(
        _attn_kernel,
        grid=(n_hp, b),
        in_specs=[
            blk(0), blk(n_hp), blk(2 * n_hp),
            pl.BlockSpec((2, 9, 128, 512), lambda hp, bi: (hp, 0, 0, 0)),
        ],
        out_specs=pl.BlockSpec((None, SEQ, 128), lambda hp, bi: (bi, 0, hp)),
        out_shape=jax.ShapeDtypeStruct((b, SEQ, D_ATTN), BF16),
        scratch_shapes=[pltpu.VMEM((SEQ, 128), BF16), pltpu.VMEM((SEQ, 128), BF16)],
        compiler_params=pltpu.CompilerParams(
            dimension_semantics=("parallel", "parallel"), vmem_limit_bytes=V7X_VMEM_LIMIT),
        name="nattn",
    )(qkv3, qkv3, qkv3, bias_tab)


def _fourier_tables():
    n = 64
    k = np.arange(n)
    ang = 2.0 * np.pi * np.outer(k, k) / n
    c64, s64 = np.cos(ang), np.sin(ang)
    eye8 = np.eye(8)
    k1 = np.concatenate([np.kron(c64, eye8), np.kron(-s64, eye8)], axis=0)
    a = np.arange(8)[:, None, None, None, None]
    t2p = np.arange(64)[None, :, None, None, None]
    jo = np.arange(8)[None, None, :, None, None]
    c = np.arange(8)[None, None, None, :, None]
    j = np.arange(8)[None, None, None, None, :]
    idx = ((8 * c + j) * (8 * a + jo + 64 * t2p)) % SEQ
    th = 2.0 * np.pi * idx / SEQ
    cos_t, sin_t = np.cos(th), np.sin(th)
    delta = np.eye(8)[None, None, :, None, :, None]
    def expand(m):
        return (m[:, :, :, :, None, :] * delta).reshape(8, 512, 512)
    cc, ss = expand(cos_t), expand(sin_t)
    la = np.concatenate([np.concatenate([cc, ss], axis=2), np.concatenate([-ss, cc], axis=2)], axis=1)
    cbd = np.kron(np.eye(4), c64)
    sbd = np.kron(np.eye(4), s64)
    cs = np.concatenate([cbd, sbd], axis=0)
    return k1.astype(np.float32), la.astype(np.float32), cs.astype(np.float32)


_K1_NP, _LA_NP, _CS_NP = _fourier_tables()


def _fourier_kernel(u_ref, k1_ref, la_ref, cs_ref, wbd_ref, bf_ref, y_ref, zs_ref):
    s = pl.program_id(1)

    @pl.when(s < 8)
    def _():
        xc = u_ref[...].reshape(512, D_FOUR).astype(BF16)
        z = jnp.dot(k1_ref[...], xc, preferred_element_type=F32)
        zs_ref[s] = z.astype(BF16)

    @pl.when(s >= 8)
    def _():
        off = pl.multiple_of((s - 8) * 64, 64)
        rhs = jnp.concatenate(
            [zs_ref[c, pl.ds(part * 512 + off, 64), :] for part in range(2) for c in range(8)], axis=0)
        x = jnp.dot(la_ref[...], rhs, preferred_element_type=F32)
        xr = x[:512].astype(BF16)
        xi = x[512:].astype(BF16)
        halves = []
        for hf in range(2):
            sl = slice(256 * hf, 256 * (hf + 1))
            lhs = jnp.concatenate([xr[:, sl], xi[:, sl]], axis=1)
            f = jnp.dot(lhs, cs_ref[...], preferred_element_type=F32) * (1.0 / 512.0)
            halves.append(jnp.dot(f.astype(BF16), wbd_ref[hf], preferred_element_type=F32))
        y = jnp.concatenate(halves, axis=1) + bf_ref[...]
        y_ref[...] = y.reshape(64, 8, D_FOUR)


def _fourier(u4, wbd, bf):
    b = u4.shape[0]
    k1 = jnp.asarray(_K1_NP).astype(BF16)
    la = jnp.asarray(_LA_NP).astype(BF16)
    cs = jnp.asarray(_CS_NP).astype(BF16)
    return pl.pallas_call(
        _fourier_kernel,
        grid=(b, 16),
        in_specs=[
            pl.BlockSpec((None, 64, 8, D_FOUR), lambda bi, s: (bi, 0, jnp.minimum(s, 7), 0)),
            pl.BlockSpec((1024, 512), lambda bi, s: (0, 0)),
            pl.BlockSpec((None, 1024, 1024), lambda bi, s: (jnp.maximum(s - 8, 0), 0, 0)),
            pl.BlockSpec((512, 256), lambda bi, s: (0, 0)),
            pl.BlockSpec((2, 256, 256), lambda bi, s: (0, 0, 0)),
            pl.BlockSpec((1, D_FOUR), lambda bi, s: (0, 0)),
        ],
        out_specs=pl.BlockSpec((None, 64, 8, D_FOUR), lambda bi, s: (bi, 0, jnp.maximum(s - 8, 0), 0)),
        out_shape=jax.ShapeDtypeStruct((b, 64, 64, D_FOUR), F32),
        scratch_shapes=[pltpu.VMEM((8, 1024, 512), BF16)],
        compiler_params=pltpu.CompilerParams(
            dimension_semantics=("parallel", "arbitrary"), vmem_limit_bytes=V7X_VMEM_LIMIT),
        name="fourier",
    )(u4, k1, la, cs, wbd, bf)


def _mixout_kernel(oa_ref, yf_ref, x_ref, ga_ref, gf_ref, wout_ref, gm_ref, wrt_ref, br_ref, tri_ref,
                   h_ref, bucket_ref, rank_ref, cnt_ref, carry_ref):
    i = pl.program_id(0)

    @pl.when(i == 0)
    def _():
        carry_ref[...] = jnp.zeros_like(carry_ref)

    na = _rms(oa_ref[...].astype(F32), ga_ref[...]).astype(BF16)
    nf = _rms(yf_ref[...], gf_ref[...]).astype(BF16)
    merged = jnp.concatenate([na, nf], axis=1)
    h = x_ref[...] + jnp.dot(merged, wout_ref[...], preferred_element_type=F32)
    h_ref[:, :D_MODEL] = h
    hn = _rms(h, gm_ref[...]).astype(BF16)
    lt = lax.dot_general(wrt_ref[...], hn, (((1,), (1,)), ((), ())), preferred_element_type=F32)
    lt = lt + br_ref[...]
    c = [lt[k:k + 1] for k in range(N_GROUPS)]
    cmax = jnp.maximum(jnp.maximum(c[0], c[1]), jnp.maximum(c[2], c[3]))
    e = [jnp.exp(ck - cmax) for ck in c]
    esum = (e[0] + e[1]) + (e[2] + e[3])
    p = [ek / esum for ek in e]
    pmax = jnp.maximum(jnp.maximum(p[0], p[1]), jnp.maximum(p[2], p[3]))
    g = jnp.where(p[0] == pmax, 0, jnp.where(p[1] == pmax, 1, jnp.where(p[2] == pmax, 2, 3))).astype(I32)
    fine = jnp.where(g == 0, lt[8:16], jnp.where(g == 1, lt[16:24], jnp.where(g == 2, lt[24:32], lt[32:40])))
    rows = lax.broadcasted_iota(I32, fine.shape, 0)
    v1 = jnp.max(fine, axis=0, keepdims=True)
    i1 = jnp.min(jnp.where(fine == v1, rows, EPG), axis=0, keepdims=True)
    rest = jnp.where(rows == i1, -jnp.inf, fine)
    v2 = jnp.max(rest, axis=0, keepdims=True)
    i2 = jnp.min(jnp.where(rest == v2, rows, EPG), axis=0, keepdims=True)
    t = jnp.exp(v2 - v1)
    den = 1.0 + t
    w1 = (1.0 / den) * pmax
    w2 = (t / den) * pmax
    lo = jnp.minimum(i1, i2)
    hi = jnp.maximum(i1, i2)
    first_is_lo = i1 < i2
    wlo = jnp.where(first_is_lo, w1, w2)
    whi = jnp.where(first_is_lo, w2, w1)
    pair = lax.shift_right_logical(lo * (2 * EPG - 1 - lo), 1) + (hi - lo - 1)
    bucket = g * N_PAIRS + pair
    bucket_ref[...] = bucket
    brow = lax.broadcasted_iota(I32, (BUCKET_LANES, TM_OUT), 0)
    wcols = jnp.where(brow == 0, wlo, jnp.where(brow == 1, whi, 0.0))
    h_ref[:, D_MODEL:] = wcols.T
    onehot = (brow == bucket).astype(F32)
    prefix = jnp.dot(onehot.astype(BF16), tri_ref[...], preferred_element_type=F32)
    rank = jnp.sum(onehot * (prefix + carry_ref[...]), axis=0, keepdims=True)
    rank_ref[...] = rank.astype(I32)
    carry_ref[...] = carry_ref[...] + jnp.sum(onehot, axis=1, keepdims=True)
    cnt_ref[...] = carry_ref[...]


def _mixout(oa, yf, x2, ga, gf, wout_bf, gm, wrt, br, tri):
    n = x2.shape[0]
    nt = n // TM_OUT
    full = lambda *shape: pl.BlockSpec(shape, lambda i: (0,) * len(shape))
    row3 = pl.BlockSpec((None, 1, TM_OUT), lambda i: (i, 0, 0))
    return pl.pallas_call(
        _mixout_kernel,
        grid=(nt,),
        in_specs=[
            pl.BlockSpec((TM_OUT, D_ATTN), lambda i: (i, 0)),
            pl.BlockSpec((TM_OUT, D_FOUR), lambda i: (i, 0)),
            pl.BlockSpec((TM_OUT, D_MODEL), lambda i: (i, 0)),
            full(1, D_ATTN), full(1, D_FOUR), full(D_MODEL, D_MODEL), full(1, D_MODEL),
            full(BUCKET_LANES, D_MODEL), full(BUCKET_LANES, 1), full(TM_OUT, TM_OUT),
        ],
        out_specs=[
            pl.BlockSpec((TM_OUT, D_ROW), lambda i: (i, 0)),
            row3, row3,
            full(BUCKET_LANES, 1),
        ],
        out_shape=[
            jax.ShapeDtypeStruct((n, D_ROW), F32),
            jax.ShapeDtypeStruct((nt, 1, TM_OUT), I32),
            jax.ShapeDtypeStruct((nt, 1, TM_OUT), I32),
            jax.ShapeDtypeStruct((BUCKET_LANES, 1), F32),
        ],
        scratch_shapes=[pltpu.VMEM((BUCKET_LANES, 1), F32)],
        compiler_params=pltpu.CompilerParams(
            dimension_semantics=("arbitrary",), vmem_limit_bytes=V7X_VMEM_LIMIT),
        name="mixout",
    )(oa, yf, x2, ga, gf, wout_bf, gm, wrt, br, tri)


def _pair_tables():
    lo, hi = [], []
    for a in range(EPG):
        for b in range(a + 1, EPG):
            lo.append(a)
            hi.append(b)
    return np.asarray(lo, np.int32), np.asarray(hi, np.int32)


_PAIR_LO, _PAIR_HI = _pair_tables()


def _dispatch_kernel(tnv_ref, dtile_ref, dsub_ref, h_ref, hs_hbm, zbuf, zsem, sem):
    k = pl.program_id(0)
    tile8 = TM_MOE // 8
    n_tiles = hs_hbm.shape[0] // tile8

    def zero_copy(t):
        return pltpu.make_async_copy(zbuf, hs_hbm.at[pl.ds(t * tile8, tile8)], zsem)

    @pl.when(k == 0)
    def _():
        zbuf[...] = jnp.zeros_like(zbuf)

        def zstart(t, c):
            @pl.when(tnv_ref[t] < TM_MOE)
            def _():
                zero_copy(t).start()
            return c

        def zwait(t, c):
            @pl.when(tnv_ref[t] < TM_MOE)
            def _():
                zero_copy(t).wait()
            return c

        lax.fori_loop(0, n_tiles, zstart, 0)
        lax.fori_loop(0, n_tiles, zwait, 0)

    def rows(r8, c):
        for u in range(8):
            r = r8 * 8 + u
            pltpu.make_async_copy(h_ref.at[r8, pl.ds(u, 1)],
                                  hs_hbm.at[dtile_ref[0, 0, r], pl.ds(dsub_ref[0, 0, r], 1)], sem).start()
        return c

    lax.fori_loop(0, ROWS_PER_STEP // 8, rows, 0)
    pltpu.make_async_copy(h_ref, hs_hbm.at[pl.ds(0, ROWS_PER_STEP // 8)], sem).wait()


def _row_index_specs(index_map):
    spec = pl.BlockSpec((1, 1, ROWS_PER_STEP), index_map, memory_space=pltpu.SMEM)
    return [spec, spec]


def _dispatch(h_ext3, dtile3, dsub3, tile_nv, n_slots):
    n8 = h_ext3.shape[0]
    grid_spec = pltpu.PrefetchScalarGridSpec(
        num_scalar_prefetch=1,
        grid=(n8 * 8 // ROWS_PER_STEP,),
        in_specs=_row_index_specs(lambda k, *_: (k, 0, 0)) + [
            pl.BlockSpec((ROWS_PER_STEP // 8, 8, D_ROW), lambda k, *_: (k, 0, 0)),
        ],
        out_specs=pl.BlockSpec(memory_space=pl.ANY),
        scratch_shapes=[pltpu.VMEM((TM_MOE // 8, 8, D_ROW), F32),
                        pltpu.SemaphoreType.DMA(()), pltpu.SemaphoreType.DMA(())],
    )
    return pl.pallas_call(
        _dispatch_kernel,
        grid_spec=grid_spec,
        out_shape=jax.ShapeDtypeStruct((n_slots // 8, 8, D_ROW), F32),
        compiler_params=pltpu.CompilerParams(
            dimension_semantics=("arbitrary",), vmem_limit_bytes=V7X_VMEM_LIMIT),
        name="dispatch",
    )(tile_nv, dtile3, dsub3, h_ext3)


def _combine_kernel(dtile_ref, dsub_ref, ys_hbm, o_ref, sem):
    def rows(r8, c):
        for u in range(8):
            r = r8 * 8 + u
            pltpu.make_async_copy(ys_hbm.at[dtile_ref[0, 0, r], pl.ds(dsub_ref[0, 0, r], 1)],
                                  o_ref.at[r8, pl.ds(u, 1)], sem).start()
        return c

    lax.fori_loop(0, ROWS_PER_STEP // 8, rows, 0)
    pltpu.make_async_copy(ys_hbm.at[pl.ds(0, ROWS_PER_STEP // 8)], o_ref, sem).wait()


def _combine(ys3, dtile3, dsub3, n):
    return pl.pallas_call(
        _combine_kernel,
        grid=(n // ROWS_PER_STEP,),
        in_specs=_row_index_specs(lambda k: (k, 0, 0)) + [pl.BlockSpec(memory_space=pl.ANY)],
        out_specs=pl.BlockSpec((ROWS_PER_STEP // 8, 8, D_MODEL), lambda k: (k, 0, 0)),
        out_shape=jax.ShapeDtypeStruct((n // 8, 8, D_MODEL), F32),
        scratch_shapes=[pltpu.SemaphoreType.DMA(())],
        compiler_params=pltpu.CompilerParams(
            dimension_semantics=("arbitrary",), vmem_limit_bytes=V7X_VMEM_LIMIT),
        name="combine",
    )(dtile3, dsub3, ys3)


def _moe_kernel(tg_ref, tlo_ref, thi_ref, nused_ref, hs_ref, wg_ref, wu_ref, wd_ref, gm_ref, gfin_ref, ys_ref):
    step = pl.program_id(0)

    @pl.when(step * TILES_PER_STEP < nused_ref[0])
    def _():
        for k in range(TILES_PER_STEP):
            t = step * TILES_PER_STEP + k
            rows = pl.ds(k * TM_MOE, TM_MOE)
            hrows = hs_ref[rows, :D_MODEL]
            hn = _rms(hrows, gm_ref[...]).astype(BF16)

            def expert(e, w):
                gate = jnp.dot(hn, wg_ref[e], preferred_element_type=F32)
                up = jnp.dot(hn, wu_ref[e], preferred_element_type=F32)
                act = (gate * jax.nn.sigmoid(gate) * up).astype(BF16)
                return w * jnp.dot(act, wd_ref[e], preferred_element_type=F32)

            y = (expert(tlo_ref[t], hs_ref[rows, D_MODEL:D_MODEL + 1])
                 + expert(thi_ref[t], hs_ref[rows, D_MODEL + 1:D_MODEL + 2]))
            ys_ref[rows, :] = _rms(hrows + y, gfin_ref[...])

    @pl.when(step * TILES_PER_STEP >= nused_ref[0])
    def _():
        ys_ref[...] = jnp.zeros_like(ys_ref)


def _moe(hs, tile_g, tile_lo, tile_hi, n_used, wg, wu, wd, gm, gfin):
    n_slots = hs.shape[0]
    rows_step = TM_MOE * TILES_PER_STEP
    n_steps = n_slots // rows_step
    by_group = lambda s, tg, *_: (tg[s * TILES_PER_STEP], 0, 0, 0)

    def hs_index(s, tg, tlo, thi, nu):
        last_step = jnp.maximum(nu[0] - 1, 0) // TILES_PER_STEP
        return (jnp.minimum(s, last_step), 0)

    grid_spec = pltpu.PrefetchScalarGridSpec(
        num_scalar_prefetch=4,
        grid=(n_steps,),
        in_specs=[
            pl.BlockSpec((rows_step, D_ROW), hs_index),
            pl.BlockSpec((None, EPG, D_MODEL, D_EXPERT), by_group),
            pl.BlockSpec((None, EPG, D_MODEL, D_EXPERT), by_group),
            pl.BlockSpec((None, EPG, D_EXPERT, D_MODEL), by_group),
            pl.BlockSpec((1, D_MODEL), lambda s, *_: (0, 0)),
            pl.BlockSpec((1, D_MODEL), lambda s, *_: (0, 0)),
        ],
        out_specs=pl.BlockSpec((rows_step, D_MODEL), lambda s, *_: (s, 0)),
    )
    return pl.pallas_call(
        _moe_kernel,
        grid_spec=grid_spec,
        out_shape=jax.ShapeDtypeStruct((n_slots, D_MODEL), F32),
        compiler_params=pltpu.CompilerParams(
            dimension_semantics=("arbitrary",), vmem_limit_bytes=V7X_VMEM_LIMIT),
        name="moe",
    )(tile_g, tile_lo, tile_hi, n_used, hs, wg, wu, wd, gm, gfin)


def _bucket_plan(bucket, rank, counts, n):
    nt = n // TM_MOE + N_BUCKETS + N_GROUPS * (TILES_PER_STEP - 1)
    nt = -(-nt // TILES_PER_STEP) * TILES_PER_STEP
    tiles_b = (counts + (TM_MOE - 1)) // TM_MOE
    tiles_g = jnp.sum(tiles_b.reshape(N_GROUPS, N_PAIRS), axis=1)
    extra_g = (-tiles_g) % TILES_PER_STEP
    is_last = (np.arange(N_PAIRS) == N_PAIRS - 1)[None, :]
    tiles_b = (tiles_b.reshape(N_GROUPS, N_PAIRS) + jnp.where(is_last, extra_g[:, None], 0)).reshape(N_BUCKETS)
    tile_end = jnp.cumsum(tiles_b)
    tile_start = tile_end - tiles_b
    n_used = tile_end[-1]
    b_ids = jnp.arange(N_BUCKETS, dtype=I32)
    dest = rank + TM_MOE * jnp.sum(jnp.where(bucket[:, None] == b_ids[None, :], tile_start[None, :], 0), axis=1)
    t_idx = jnp.arange(nt, dtype=I32)
    tb = jnp.sum((tile_end[None, :] <= t_idx[:, None]).astype(I32), axis=1)
    tb_last = jnp.sum((tile_end <= n_used - 1).astype(I32))
    tb = jnp.minimum(jnp.where(t_idx < n_used, tb, tb_last), N_BUCKETS - 1)
    sel = tb[:, None] == b_ids[None, :]
    pick = lambda table: jnp.sum(jnp.where(sel, table[None, :], 0), axis=1).astype(I32)
    tile_g = tb // N_PAIRS
    tile_lo = pick(jnp.asarray(np.tile(_PAIR_LO, N_GROUPS)))
    tile_hi = pick(jnp.asarray(np.tile(_PAIR_HI, N_GROUPS)))
    nv = jnp.clip(pick(counts) - (t_idx - pick(tile_start)) * TM_MOE, 0, TM_MOE)
    tile_nv = jnp.where(t_idx < n_used, nv, 0).astype(I32)
    return dest.astype(I32), tile_g.astype(I32), tile_lo, tile_hi, tile_nv, n_used.reshape(1).astype(I32)


def kernel(x, norm_mix, w_in, rpb, w_four, b_four, g_attn_out, g_four_out, w_out, norm_moe,
           w_router_coarse, b_router_coarse, w_router_fine, b_router_fine, w_gate, w_up, w_down, norm_final):
    b, seq, d = x.shape
    assert (seq, d) == (SEQ, D_MODEL) and norm_mix.shape[0] == 1
    n = b * seq
    x2 = x.reshape(n, d)

    qkv, u = _inproj(x2, norm_mix[0][None], w_in[0].astype(BF16))

    oa = _attention(qkv.reshape(b, seq, 3 * D_ATTN), _bias_tables(rpb[0]))

    eye4 = jnp.eye(4, dtype=F32)
    wf = w_four[0].reshape(2, 4, FOUR_GROUP_DIM, FOUR_GROUP_DIM)
    wbd = (eye4[None, :, None, :, None] * wf[:, :, :, None, :]).reshape(2, 256, 256).astype(BF16)
    yf = _fourier(u.reshape(b, ROWS, GRID_W, D_FOUR), wbd, b_four[0][None])

    wrt = jnp.zeros((BUCKET_LANES, d), F32)
    wrt = wrt.at[0:N_GROUPS].set(w_router_coarse[0].T).at[8:8 + N_EXPERTS].set(w_router_fine[0].T)
    br = jnp.zeros((BUCKET_LANES, 1), F32)
    br = br.at[0:N_GROUPS, 0].set(b_router_coarse[0]).at[8:8 + N_EXPERTS, 0].set(b_router_fine[0])
    tri = (np.arange(TM_OUT)[:, None] < np.arange(TM_OUT)[None, :]).astype(np.float32)
    h_ext, bucket, rank, cnt = _mixout(
        oa.reshape(n, D_ATTN), yf.reshape(n, D_FOUR), x2, g_attn_out[0][None], g_four_out[0][None],
        w_out[0].astype(BF16), norm_moe[0][None], wrt.astype(BF16), br, jnp.asarray(tri, BF16))

    counts = cnt[:N_BUCKETS, 0].astype(I32)
    dest, tile_g, tile_lo, tile_hi, tile_nv, n_used = _bucket_plan(bucket.reshape(n), rank.reshape(n), counts, n)
    per_step = (n // ROWS_PER_STEP, 1, ROWS_PER_STEP)
    dtile3 = lax.shift_right_logical(dest, 3).reshape(per_step)
    dsub3 = (dest & 7).reshape(per_step)
    n_slots = tile_nv.shape[0] * TM_MOE
    hs = _dispatch(h_ext.reshape(n // 8, 8, D_ROW), dtile3, dsub3, tile_nv, n_slots)
    shape_e = (N_GROUPS, EPG)
    ys = _moe(hs.reshape(n_slots, D_ROW), tile_g, tile_lo, tile_hi, n_used,
              w_gate[0].astype(BF16).reshape(shape_e + (d, D_EXPERT)),
              w_up[0].astype(BF16).reshape(shape_e + (d, D_EXPERT)),
              w_down[0].astype(BF16).reshape(shape_e + (D_EXPERT, d)),
              norm_moe[0][None], norm_final[None])
    return _combine(ys.reshape(n_slots // 8, 8, d), dtile3, dsub3, n).reshape(b, seq, d)
```

```python
import functools

import numpy as np
import jax
import jax.numpy as jnp
from jax import lax
from jax.experimental import pallas as pl
from jax.experimental.pallas import tpu as pltpu

F32 = jnp.float32
BF16 = jnp.bfloat16
I32 = jnp.int32

D_MODEL = 1024
SEQ = 4096
GRID_W = 64
ROWS = SEQ // GRID_W
D_ATTN = 512
D_FOUR = 512
N_HEADS = 8
HEAD_DIM = 64
WIN_H = 8
WIN_W = 16
N_FOUR_GROUPS = 8
FOUR_GROUP_DIM = 64
D_PROJ = 3 * D_ATTN + D_FOUR
N_GROUPS = 4
EPG = 8
N_EXPERTS = N_GROUPS * EPG
D_EXPERT = 256
EPS = 1e-6
NEG = -1e30

V7X_VMEM_LIMIT = 56 * 1024 * 1024

TM_IN = 512
TM_OUT = 512
OUT_SUBTILES = 2
TM_MOE = 128
D_ROW = D_MODEL + 128
ROWS_PER_STEP = 1024
TILES_PER_STEP = 2
N_PAIRS = EPG * (EPG - 1) // 2
N_BUCKETS = N_GROUPS * N_PAIRS
BUCKET_LANES = 128

QB_ROWS = 8
QB_COLS = 16
KB_ROWS = 16
KB_COLS = 32
ATTN_AHEAD = 3


def _rms(x, g):
    ms = jnp.mean(x * x, axis=-1, keepdims=True)
    return x * lax.rsqrt(ms + EPS) * g


def _inproj_kernel(x_ref, g_ref, w_ref, qkv_ref, u_ref):
    xn = _rms(x_ref[...], g_ref[...]).astype(BF16)
    p = jnp.dot(xn, w_ref[...], preferred_element_type=F32)
    qkv_ref[:, :D_ATTN] = (p[:, :D_ATTN] * (HEAD_DIM ** -0.5)).astype(BF16)
    qkv_ref[:, D_ATTN:] = p[:, D_ATTN:3 * D_ATTN].astype(BF16)
    u_ref[...] = p[:, 3 * D_ATTN:]


def _inproj(x2, g, w_bf):
    n = x2.shape[0]
    return pl.pallas_call(
        _inproj_kernel,
        grid=(n // TM_IN,),
        in_specs=[
            pl.BlockSpec((TM_IN, D_MODEL), lambda i: (i, 0)),
            pl.BlockSpec((1, D_MODEL), lambda i: (0, 0)),
            pl.BlockSpec((D_MODEL, D_PROJ), lambda i: (0, 0)),
        ],
        out_specs=[
            pl.BlockSpec((TM_IN, 3 * D_ATTN), lambda i: (i, 0)),
            pl.BlockSpec((TM_IN, D_FOUR), lambda i: (i, 0)),
        ],
        out_shape=[
            jax.ShapeDtypeStruct((n, 3 * D_ATTN), BF16),
            jax.ShapeDtypeStruct((n, D_FOUR), F32),
        ],
        compiler_params=pltpu.CompilerParams(
            dimension_semantics=("parallel",), vmem_limit_bytes=V7X_VMEM_LIMIT),
        name="inproj",
    )(x2, g, w_bf)


_KCOL_START = (0, 8, 24, 32)
_KCOL_SHIFTED = (False, True, True, False)
_KCOL_OFF = (0, 0, 16, 32)
_COL_TYPE = (0, 1, 1, 2)


def _bias_index_tables():
    dr = np.zeros((9, 128, 512), np.int32)
    dc = np.zeros((9, 128, 512), np.int32)
    ok = np.zeros((9, 128, 512), bool)
    qi, qc = np.divmod(np.arange(128), QB_COLS)
    ki, kc = np.divmod(np.arange(512), KB_COLS)
    for rt, (q0, k0) in enumerate(((0, 0), (8, 4), (56, 48))):
        qrow = q0 + qi
        krow = k0 + ki
        rs = np.clip(qrow - WIN_H // 2, 0, ROWS - WIN_H)
        rok = (krow[None, :] >= rs[:, None]) & (krow[None, :] < rs[:, None] + WIN_H)
        drr = krow[None, :] - qrow[:, None] + (WIN_H - 1)
        for ct, (c0, kc0) in enumerate(((0, 0), (16, 8), (48, 32))):
            qcol = c0 + qc
            kcol = kc0 + kc
            cs = np.clip(qcol - WIN_W // 2, 0, GRID_W - WIN_W)
            cok = (kcol[None, :] >= cs[:, None]) & (kcol[None, :] < cs[:, None] + WIN_W)
            dcc = kcol[None, :] - qcol[:, None] + (WIN_W - 1)
            t = rt * 3 + ct
            ok[t] = rok & cok
            dr[t] = np.where(ok[t], drr, 0)
            dc[t] = np.where(ok[t], dcc, 0)
    return dr, dc, ok


_BIAS_DR, _BIAS_DC, _BIAS_OK = _bias_index_tables()


def _bias_selectors():
    ok = _BIAS_OK.reshape(3, 3, QB_ROWS, QB_COLS, KB_ROWS, KB_COLS)
    dr = _BIAS_DR.reshape(ok.shape)
    dc = _BIAS_DC.reshape(ok.shape)
    row_ok = ok.any(axis=(1, 3, 5))
    col_ok = ok.any(axis=(0, 2, 4))
    dr_r = dr.max(axis=(1, 3, 5))
    dc_c = dc.max(axis=(0, 2, 4))
    sc = (np.arange(2 * WIN_W - 1)[:, None, None, None] == dc_c[None]) & col_ok[None]
    return row_ok, dr_r, col_ok, sc.astype(np.float32)


_BIAS_ROW_OK, _BIAS_ROW_DR, _BIAS_COL_OK, _BIAS_SC = _bias_selectors()


def _bias_columns(rpb):
    sc = jnp.asarray(_BIAS_SC)
    t1 = jnp.sum(rpb[:, :, :, None, None, None] * sc[None, None], axis=2)
    t1 = jnp.where(_BIAS_COL_OK[None, None], t1, NEG)
    return jnp.tile(t1, (1, 1, 1, 1, KB_ROWS))


def _attn_kernel(q_ref, k_ref, v_ref, bcol_ref, o_ref, ksh_ref, vsh_ref, bias_ref):
    @pl.when(pl.program_id(1) == 0)
    def _():
        key_row = lax.broadcasted_iota(I32, (QB_COLS, KB_ROWS * KB_COLS), 1) // KB_COLS
        for hh in range(2):
            for rt in range(3):
                for ct in range(3):
                    for i in range(QB_ROWS):
                        acc = jnp.full((QB_COLS, KB_ROWS * KB_COLS), NEG, F32)
                        for y in range(KB_ROWS):
                            if _BIAS_ROW_OK[rt, i, y]:
                                acc = jnp.where(key_row == y, bcol_ref[hh, int(_BIAS_ROW_DR[rt, i, y]), ct], acc)
                        bias_ref[rt * 3 + ct, pl.ds(hh * 128 + i * QB_COLS, QB_COLS), :] = acc

    zpad = jnp.zeros((8, 128), F32)
    ksh_ref[...] = jnp.concatenate([k_ref[...].astype(F32)[8:], zpad], axis=0).astype(BF16)
    vsh_ref[...] = jnp.concatenate([v_ref[...].astype(F32)[8:], zpad], axis=0).astype(BF16)
    lane = lax.broadcasted_iota(I32, (1, 128), 1)
    head_masks = (lane < HEAD_DIM, lane >= HEAD_DIM)

    def row_block(rb, carry):
        rb = jnp.asarray(rb, I32)
        ks = jnp.clip(QB_ROWS * rb - WIN_H // 2, 0, ROWS - KB_ROWS)
        rt = jnp.where(rb == 0, 0, jnp.where(rb == ROWS // QB_ROWS - 1, 2, 1))
        for j in range(GRID_W // QB_COLS):
            kr, vr = (ksh_ref, vsh_ref) if _KCOL_SHIFTED[j] else (k_ref, v_ref)
            q = jnp.concatenate(
                [q_ref[pl.ds(pl.multiple_of((QB_ROWS * rb + i) * GRID_W + QB_COLS * j, 16), QB_COLS), :]
                 for i in range(QB_ROWS)], axis=0)
            kstarts = [pl.multiple_of((ks + i) * GRID_W + _KCOL_OFF[j], 16) for i in range(KB_ROWS)]
            kt = jnp.concatenate([kr[pl.ds(s, KB_COLS), :] for s in kstarts], axis=0)
            vt = jnp.concatenate([vr[pl.ds(s, KB_COLS), :] for s in kstarts], axis=0)
            qm = jnp.concatenate([jnp.where(hm, q, jnp.zeros_like(q)) for hm in head_masks], axis=0)
            s = lax.dot_general(qm, kt, (((1,), (1,)), ((), ())), preferred_element_type=F32)
            s = s + bias_ref[rt * 3 + _COL_TYPE[j]]
            e = jnp.exp(s - jnp.max(s, axis=-1, keepdims=True))
            o = jnp.dot(e.astype(BF16), vt, preferred_element_type=F32)
            o = o / jnp.sum(e, axis=-1, keepdims=True)
            out = jnp.where(head_masks[0], o[:128], o[128:]).astype(BF16)
            for i in range(QB_ROWS):
                o_ref[pl.ds(pl.multiple_of((QB_ROWS * rb + i) * GRID_W + QB_COLS * j, 16), QB_COLS), :] = (
                    out[QB_COLS * i:QB_COLS * (i + 1)])
        return carry

    lax.fori_loop(0, ROWS // QB_ROWS, row_block, 0)


def _attention(qkv3, bias_cols):
    b = qkv3.shape[0]
    n_hp = N_HEADS // 2
    blk = lambda off: pl.BlockSpec((None, SEQ, 128), lambda hp, bi: (bi, 0, off + hp))
    return pl.pallas_call(
        _attn_kernel,
        grid=(n_hp, b),
        in_specs=[
            blk(0), blk(n_hp), blk(2 * n_hp),
            pl.BlockSpec((2,) + bias_cols.shape[1:], lambda hp, bi: (hp, 0, 0, 0, 0)),
        ],
        out_specs=pl.BlockSpec((None, SEQ, 128), lambda hp, bi: (bi, 0, hp)),
        out_shape=jax.ShapeDtypeStruct((b, SEQ, D_ATTN), BF16),
        scratch_shapes=[pltpu.VMEM((SEQ, 128), BF16), pltpu.VMEM((SEQ, 128), BF16),
                        pltpu.VMEM((9, 2 * QB_ROWS * QB_COLS, KB_ROWS * KB_COLS), F32)],
        compiler_params=pltpu.CompilerParams(
            dimension_semantics=("arbitrary", "arbitrary"), vmem_limit_bytes=V7X_VMEM_LIMIT),
        name="nattn",
    )(qkv3, qkv3, qkv3, bias_cols)


def _fourier_tables():
    n = 64
    k = np.arange(n)
    ang = 2.0 * np.pi * np.outer(k, k) / n
    c64, s64 = np.cos(ang), np.sin(ang)
    eye8 = np.eye(8)
    k1 = np.concatenate([np.kron(c64, eye8), np.kron(-s64, eye8)], axis=0)
    a = np.arange(8)[:, None, None, None, None]
    t2p = np.arange(64)[None, :, None, None, None]
    jo = np.arange(8)[None, None, :, None, None]
    c = np.arange(8)[None, None, None, :, None]
    j = np.arange(8)[None, None, None, None, :]
    idx = ((8 * c + j) * (8 * a + jo + 64 * t2p)) % SEQ
    th = 2.0 * np.pi * idx / SEQ
    cos_t, sin_t = np.cos(th), np.sin(th)
    delta = np.eye(8)[None, None, :, None, :, None]
    def expand(m):
        return (m[:, :, :, :, None, :] * delta).reshape(8, 512, 512)
    cc, ss = expand(cos_t), expand(sin_t)
    la = np.concatenate([np.concatenate([cc, ss], axis=2), np.concatenate([-ss, cc], axis=2)], axis=1)
    cbd = np.kron(np.eye(4), c64)
    sbd = np.kron(np.eye(4), s64)
    cs = np.concatenate([cbd, sbd], axis=0)
    return k1.astype(np.float32), la.astype(np.float32), cs.astype(np.float32)


_K1_NP, _LA_NP, _CS_NP = _fourier_tables()


def _fourier_kernel(u_ref, k1_ref, la_ref, cs_ref, wbd_ref, bf_ref, y_ref, zs_ref):
    s = pl.program_id(1)

    @pl.when(s < 8)
    def _():
        xc = u_ref[...].reshape(512, D_FOUR).astype(BF16)
        z = jnp.dot(k1_ref[...], xc, preferred_element_type=F32)
        zs_ref[s] = z.astype(BF16)

    @pl.when(s >= 8)
    def _():
        off = pl.multiple_of((s - 8) * 64, 64)
        rhs = jnp.concatenate(
            [zs_ref[c, pl.ds(part * 512 + off, 64), :] for part in range(2) for c in range(8)], axis=0)
        x = jnp.dot(la_ref[...], rhs, preferred_element_type=F32)
        xr = x[:512].astype(BF16)
        xi = x[512:].astype(BF16)
        halves = []
        for hf in range(2):
            sl = slice(256 * hf, 256 * (hf + 1))
            lhs = jnp.concatenate([xr[:, sl], xi[:, sl]], axis=1)
            f = jnp.dot(lhs, cs_ref[...], preferred_element_type=F32) * (1.0 / 512.0)
            halves.append(jnp.dot(f.astype(BF16), wbd_ref[hf], preferred_element_type=F32))
        y = jnp.concatenate(halves, axis=1) + bf_ref[...]
        y_ref[...] = y.reshape(64, 8, D_FOUR)


def _fourier(u4, wbd, bf):
    b = u4.shape[0]
    k1 = jnp.asarray(_K1_NP).astype(BF16)
    la = jnp.asarray(_LA_NP).astype(BF16)
    cs = jnp.asarray(_CS_NP).astype(BF16)
    return pl.pallas_call(
        _fourier_kernel,
        grid=(b, 16),
        in_specs=[
            pl.BlockSpec((None, 64, 8, D_FOUR), lambda bi, s: (bi, 0, jnp.minimum(s, 7), 0)),
            pl.BlockSpec((1024, 512), lambda bi, s: (0, 0)),
            pl.BlockSpec((None, 1024, 1024), lambda bi, s: (jnp.maximum(s - 8, 0), 0, 0)),
            pl.BlockSpec((512, 256), lambda bi, s: (0, 0)),
            pl.BlockSpec((2, 256, 256), lambda bi, s: (0, 0, 0)),
            pl.BlockSpec((1, D_FOUR), lambda bi, s: (0, 0)),
        ],
        out_specs=pl.BlockSpec((None, 64, 8, D_FOUR), lambda bi, s: (bi, 0, jnp.maximum(s - 8, 0), 0)),
        out_shape=jax.ShapeDtypeStruct((b, 64, 64, D_FOUR), F32),
        scratch_shapes=[pltpu.VMEM((8, 1024, 512), BF16)],
        compiler_params=pltpu.CompilerParams(
            dimension_semantics=("parallel", "arbitrary"), vmem_limit_bytes=V7X_VMEM_LIMIT),
        name="fourier",
    )(u4, k1, la, cs, wbd, bf)


def _mixout_kernel(oa_ref, yf_ref, x_ref, ga_ref, gf_ref, wout_ref, gm_ref, wrt_ref, br_ref, tri_ref,
                   h_ref, bucket_ref, rank_ref, cnt_ref, carry_ref):
    i = pl.program_id(0)

    @pl.when(i == 0)
    def _():
        carry_ref[...] = jnp.zeros_like(carry_ref)

    carry = carry_ref[...]
    for k in range(OUT_SUBTILES):
        carry = _mixout_subtile(k, carry, oa_ref, yf_ref, x_ref, ga_ref, gf_ref, wout_ref, gm_ref, wrt_ref,
                                br_ref, tri_ref, h_ref, bucket_ref, rank_ref)
    carry_ref[...] = carry
    cnt_ref[...] = carry


def _mixout_subtile(k, carry, oa_ref, yf_ref, x_ref, ga_ref, gf_ref, wout_ref, gm_ref, wrt_ref, br_ref, tri_ref,
                    h_ref, bucket_ref, rank_ref):
    rows_k = pl.ds(k * TM_OUT, TM_OUT)
    na = _rms(oa_ref[rows_k, :].astype(F32), ga_ref[...]).astype(BF16)
    nf = _rms(yf_ref[rows_k, :], gf_ref[...]).astype(BF16)
    merged = jnp.concatenate([na, nf], axis=1)
    h = x_ref[rows_k, :] + jnp.dot(merged, wout_ref[...], preferred_element_type=F32)
    h_ref[rows_k, :D_MODEL] = h
    hn = _rms(h, gm_ref[...]).astype(BF16)
    lt = lax.dot_general(wrt_ref[...], hn, (((1,), (1,)), ((), ())), preferred_element_type=F32)
    lt = lt + br_ref[...]
    c = [lt[k:k + 1] for k in range(N_GROUPS)]
    cmax = jnp.maximum(jnp.maximum(c[0], c[1]), jnp.maximum(c[2], c[3]))
    e = [jnp.exp(ck - cmax) for ck in c]
    esum = (e[0] + e[1]) + (e[2] + e[3])
    p = [ek / esum for ek in e]
    pmax = jnp.maximum(jnp.maximum(p[0], p[1]), jnp.maximum(p[2], p[3]))
    g = jnp.where(p[0] == pmax, 0, jnp.where(p[1] == pmax, 1, jnp.where(p[2] == pmax, 2, 3))).astype(I32)
    fine = jnp.where(g == 0, lt[8:16], jnp.where(g == 1, lt[16:24], jnp.where(g == 2, lt[24:32], lt[32:40])))
    rows = lax.broadcasted_iota(I32, fine.shape, 0)
    v1 = jnp.max(fine, axis=0, keepdims=True)
    i1 = jnp.min(jnp.where(fine == v1, rows, EPG), axis=0, keepdims=True)
    rest = jnp.where(rows == i1, -jnp.inf, fine)
    v2 = jnp.max(rest, axis=0, keepdims=True)
    i2 = jnp.min(jnp.where(rest == v2, rows, EPG), axis=0, keepdims=True)
    t = jnp.exp(v2 - v1)
    den = 1.0 + t
    w1 = (1.0 / den) * pmax
    w2 = (t / den) * pmax
    lo = jnp.minimum(i1, i2)
    hi = jnp.maximum(i1, i2)
    first_is_lo = i1 < i2
    wlo = jnp.where(first_is_lo, w1, w2)
    whi = jnp.where(first_is_lo, w2, w1)
    pair = lax.shift_right_logical(lo * (2 * EPG - 1 - lo), 1) + (hi - lo - 1)
    bucket = g * N_PAIRS + pair
    bucket_ref[k] = bucket
    brow = lax.broadcasted_iota(I32, (BUCKET_LANES, TM_OUT), 0)
    wcols = jnp.where(brow == 0, wlo, jnp.where(brow == 1, whi, 0.0))
    h_ref[rows_k, D_MODEL:] = wcols.T
    onehot = (brow == bucket).astype(F32)
    prefix = jnp.dot(onehot.astype(BF16), tri_ref[...], preferred_element_type=F32)
    rank = jnp.sum(onehot * (prefix + carry), axis=0, keepdims=True)
    rank_ref[k] = rank.astype(I32)
    return carry + jnp.sum(onehot, axis=1, keepdims=True)


def _mixout(oa, yf, x2, ga, gf, wout_bf, gm, wrt, br, tri):
    n = x2.shape[0]
    nt = n // TM_OUT
    rows_step = TM_OUT * OUT_SUBTILES
    full = lambda *shape: pl.BlockSpec(shape, lambda i: (0,) * len(shape))
    row3 = pl.BlockSpec((OUT_SUBTILES, 1, TM_OUT), lambda i: (i, 0, 0))
    return pl.pallas_call(
        _mixout_kernel,
        grid=(n // rows_step,),
        in_specs=[
            pl.BlockSpec((rows_step, D_ATTN), lambda i: (i, 0)),
            pl.BlockSpec((rows_step, D_FOUR), lambda i: (i, 0)),
            pl.BlockSpec((rows_step, D_MODEL), lambda i: (i, 0)),
            full(1, D_ATTN), full(1, D_FOUR), full(D_MODEL, D_MODEL), full(1, D_MODEL),
            full(BUCKET_LANES, D_MODEL), full(BUCKET_LANES, 1), full(TM_OUT, TM_OUT),
        ],
        out_specs=[
            pl.BlockSpec((rows_step, D_ROW), lambda i: (i, 0)),
            row3, row3,
            full(BUCKET_LANES, 1),
        ],
        out_shape=[
            jax.ShapeDtypeStruct((n, D_ROW), F32),
            jax.ShapeDtypeStruct((nt, 1, TM_OUT), I32),
            jax.ShapeDtypeStruct((nt, 1, TM_OUT), I32),
            jax.ShapeDtypeStruct((BUCKET_LANES, 1), F32),
        ],
        scratch_shapes=[pltpu.VMEM((BUCKET_LANES, 1), F32)],
        compiler_params=pltpu.CompilerParams(
            dimension_semantics=("arbitrary",), vmem_limit_bytes=V7X_VMEM_LIMIT),
        name="mixout",
    )(oa, yf, x2, ga, gf, wout_bf, gm, wrt, br, tri)


def _pair_tables():
    lo, hi = [], []
    for a in range(EPG):
        for b in range(a + 1, EPG):
            lo.append(a)
            hi.append(b)
    return np.asarray(lo, np.int32), np.asarray(hi, np.int32)


_PAIR_LO, _PAIR_HI = _pair_tables()


def _dispatch_kernel(tnv_ref, dtile_ref, dsub_ref, h_ref, hs_hbm, zbuf, zsem, sem):
    k = pl.program_id(0)
    tile8 = TM_MOE // 8
    n_tiles = hs_hbm.shape[0] // tile8

    def zero_copy(t):
        return pltpu.make_async_copy(zbuf, hs_hbm.at[pl.ds(t * tile8, tile8)], zsem)

    @pl.when(k == 0)
    def _():
        zbuf[...] = jnp.zeros_like(zbuf)

        def zstart(t, c):
            @pl.when(tnv_ref[t] < TM_MOE)
            def _():
                zero_copy(t).start()
            return c

        def zwait(t, c):
            @pl.when(tnv_ref[t] < TM_MOE)
            def _():
                zero_copy(t).wait()
            return c

        lax.fori_loop(0, n_tiles, zstart, 0)
        lax.fori_loop(0, n_tiles, zwait, 0)

    def rows(r8, c):
        for u in range(8):
            r = r8 * 8 + u
            pltpu.make_async_copy(h_ref.at[r8, pl.ds(u, 1)],
                                  hs_hbm.at[dtile_ref[0, 0, r], pl.ds(dsub_ref[0, 0, r], 1)], sem).start()
        return c

    lax.fori_loop(0, ROWS_PER_STEP // 8, rows, 0)
    pltpu.make_async_copy(h_ref, hs_hbm.at[pl.ds(0, ROWS_PER_STEP // 8)], sem).wait()


def _row_index_specs(index_map):
    spec = pl.BlockSpec((1, 1, ROWS_PER_STEP), index_map, memory_space=pltpu.SMEM)
    return [spec, spec]


def _dispatch(h_ext3, dtile3, dsub3, tile_nv, n_slots):
    n8 = h_ext3.shape[0]
    grid_spec = pltpu.PrefetchScalarGridSpec(
        num_scalar_prefetch=1,
        grid=(n8 * 8 // ROWS_PER_STEP,),
        in_specs=_row_index_specs(lambda k, *_: (k, 0, 0)) + [
            pl.BlockSpec((ROWS_PER_STEP // 8, 8, D_ROW), lambda k, *_: (k, 0, 0)),
        ],
        out_specs=pl.BlockSpec(memory_space=pl.ANY),
        scratch_shapes=[pltpu.VMEM((TM_MOE // 8, 8, D_ROW), F32),
                        pltpu.SemaphoreType.DMA(()), pltpu.SemaphoreType.DMA(())],
    )
    return pl.pallas_call(
        _dispatch_kernel,
        grid_spec=grid_spec,
        out_shape=jax.ShapeDtypeStruct((n_slots // 8, 8, D_ROW), F32),
        compiler_params=pltpu.CompilerParams(
            dimension_semantics=("arbitrary",), vmem_limit_bytes=V7X_VMEM_LIMIT),
        name="dispatch",
    )(tile_nv, dtile3, dsub3, h_ext3)


def _combine_kernel(dtile_ref, dsub_ref, ys_hbm, o_ref, sem):
    def rows(r8, c):
        for u in range(8):
            r = r8 * 8 + u
            pltpu.make_async_copy(ys_hbm.at[dtile_ref[0, 0, r], pl.ds(dsub_ref[0, 0, r], 1)],
                                  o_ref.at[r8, pl.ds(u, 1)], sem).start()
        return c

    lax.fori_loop(0, ROWS_PER_STEP // 8, rows, 0)
    pltpu.make_async_copy(ys_hbm.at[pl.ds(0, ROWS_PER_STEP // 8)], o_ref, sem).wait()


def _combine(ys3, dtile3, dsub3, n):
    return pl.pallas_call(
        _combine_kernel,
        grid=(n // ROWS_PER_STEP,),
        in_specs=_row_index_specs(lambda k: (k, 0, 0)) + [pl.BlockSpec(memory_space=pl.ANY)],
        out_specs=pl.BlockSpec((ROWS_PER_STEP // 8, 8, D_MODEL), lambda k: (k, 0, 0)),
        out_shape=jax.ShapeDtypeStruct((n // 8, 8, D_MODEL), F32),
        scratch_shapes=[pltpu.SemaphoreType.DMA(())],
        compiler_params=pltpu.CompilerParams(
            dimension_semantics=("arbitrary",), vmem_limit_bytes=V7X_VMEM_LIMIT),
        name="combine",
    )(dtile3, dsub3, ys3)


def _moe_kernel(tg_ref, tlo_ref, thi_ref, nused_ref, hs_ref, wg_ref, wu_ref, wd_ref, gm_ref, gfin_ref, ys_ref):
    step = pl.program_id(0)

    @pl.when(step * TILES_PER_STEP < nused_ref[0])
    def _():
        for k in range(TILES_PER_STEP):
            t = step * TILES_PER_STEP + k
            rows = pl.ds(k * TM_MOE, TM_MOE)
            hrows = hs_ref[rows, :D_MODEL]
            hn = _rms(hrows, gm_ref[...]).astype(BF16)

            def expert(e, w):
                gate = jnp.dot(hn, wg_ref[e], preferred_element_type=F32)
                up = jnp.dot(hn, wu_ref[e], preferred_element_type=F32)
                act = (gate * jax.nn.sigmoid(gate) * up).astype(BF16)
                return w * jnp.dot(act, wd_ref[e], preferred_element_type=F32)

            y = (expert(tlo_ref[t], hs_ref[rows, D_MODEL:D_MODEL + 1])
                 + expert(thi_ref[t], hs_ref[rows, D_MODEL + 1:D_MODEL + 2]))
            ys_ref[rows, :] = _rms(hrows + y, gfin_ref[...])

    @pl.when(step * TILES_PER_STEP >= nused_ref[0])
    def _():
        ys_ref[...] = jnp.zeros_like(ys_ref)


def _moe(hs, tile_g, tile_lo, tile_hi, n_used, wg, wu, wd, gm, gfin):
    n_slots = hs.shape[0]
    rows_step = TM_MOE * TILES_PER_STEP
    n_steps = n_slots // rows_step
    by_group = lambda s, tg, *_: (tg[s * TILES_PER_STEP], 0, 0, 0)

    def hs_index(s, tg, tlo, thi, nu):
        last_step = jnp.maximum(nu[0] - 1, 0) // TILES_PER_STEP
        return (jnp.minimum(s, last_step), 0)

    grid_spec = pltpu.PrefetchScalarGridSpec(
        num_scalar_prefetch=4,
        grid=(n_steps,),
        in_specs=[
            pl.BlockSpec((rows_step, D_ROW), hs_index),
            pl.BlockSpec((None, EPG, D_MODEL, D_EXPERT), by_group),
            pl.BlockSpec((None, EPG, D_MODEL, D_EXPERT), by_group),
            pl.BlockSpec((None, EPG, D_EXPERT, D_MODEL), by_group),
            pl.BlockSpec((1, D_MODEL), lambda s, *_: (0, 0)),
            pl.BlockSpec((1, D_MODEL), lambda s, *_: (0, 0)),
        ],
        out_specs=pl.BlockSpec((rows_step, D_MODEL), lambda s, *_: (s, 0)),
    )
    return pl.pallas_call(
        _moe_kernel,
        grid_spec=grid_spec,
        out_shape=jax.ShapeDtypeStruct((n_slots, D_MODEL), F32),
        compiler_params=pltpu.CompilerParams(
            dimension_semantics=("arbitrary",), vmem_limit_bytes=V7X_VMEM_LIMIT),
        name="moe",
    )(tile_g, tile_lo, tile_hi, n_used, hs, wg, wu, wd, gm, gfin)


def _bucket_plan(bucket, rank, counts, n):
    nt = n // TM_MOE + N_BUCKETS + N_GROUPS * (TILES_PER_STEP - 1)
    nt = -(-nt // TILES_PER_STEP) * TILES_PER_STEP
    tiles_b = (counts + (TM_MOE - 1)) // TM_MOE
    tiles_g = jnp.sum(tiles_b.reshape(N_GROUPS, N_PAIRS), axis=1)
    extra_g = (-tiles_g) % TILES_PER_STEP
    is_last = (np.arange(N_PAIRS) == N_PAIRS - 1)[None, :]
    tiles_b = (tiles_b.reshape(N_GROUPS, N_PAIRS) + jnp.where(is_last, extra_g[:, None], 0)).reshape(N_BUCKETS)
    tile_end = jnp.cumsum(tiles_b)
    tile_start = tile_end - tiles_b
    n_used = tile_end[-1]
    b_ids = jnp.arange(N_BUCKETS, dtype=I32)
    dest = rank + TM_MOE * jnp.sum(jnp.where(bucket[:, None] == b_ids[None, :], tile_start[None, :], 0), axis=1)
    t_idx = jnp.arange(nt, dtype=I32)
    tb = jnp.sum((tile_end[None, :] <= t_idx[:, None]).astype(I32), axis=1)
    tb_last = jnp.sum((tile_end <= n_used - 1).astype(I32))
    tb = jnp.minimum(jnp.where(t_idx < n_used, tb, tb_last), N_BUCKETS - 1)
    sel = tb[:, None] == b_ids[None, :]
    pick = lambda table: jnp.sum(jnp.where(sel, table[None, :], 0), axis=1).astype(I32)
    tile_g = tb // N_PAIRS
    tile_lo = pick(jnp.asarray(np.tile(_PAIR_LO, N_GROUPS)))
    tile_hi = pick(jnp.asarray(np.tile(_PAIR_HI, N_GROUPS)))
    nv = jnp.clip(pick(counts) - (t_idx - pick(tile_start)) * TM_MOE, 0, TM_MOE)
    tile_nv = jnp.where(t_idx < n_used, nv, 0).astype(I32)
    return dest.astype(I32), tile_g.astype(I32), tile_lo, tile_hi, tile_nv, n_used.reshape(1).astype(I32)


def kernel(x, norm_mix, w_in, rpb, w_four, b_four, g_attn_out, g_four_out, w_out, norm_moe,
           w_router_coarse, b_router_coarse, w_router_fine, b_router_fine, w_gate, w_up, w_down, norm_final):
    b, seq, d = x.shape
    assert (seq, d) == (SEQ, D_MODEL) and norm_mix.shape[0] == 1
    n = b * seq
    x2 = x.reshape(n, d)

    qkv, u = _inproj(x2, norm_mix[0][None], w_in[0].astype(BF16))

    oa = _attention(qkv.reshape(b, seq, 3 * D_ATTN), _bias_columns(rpb[0]))

    eye4 = jnp.eye(4, dtype=F32)
    wf = w_four[0].reshape(2, 4, FOUR_GROUP_DIM, FOUR_GROUP_DIM)
    wbd = (eye4[None, :, None, :, None] * wf[:, :, :, None, :]).reshape(2, 256, 256).astype(BF16)
    yf = _fourier(u.reshape(b, ROWS, GRID_W, D_FOUR), wbd, b_four[0][None])

    wrt = jnp.zeros((BUCKET_LANES, d), F32)
    wrt = wrt.at[0:N_GROUPS].set(w_router_coarse[0].T).at[8:8 + N_EXPERTS].set(w_router_fine[0].T)
    br = jnp.zeros((BUCKET_LANES, 1), F32)
    br = br.at[0:N_GROUPS, 0].set(b_router_coarse[0]).at[8:8 + N_EXPERTS, 0].set(b_router_fine[0])
    tri = (np.arange(TM_OUT)[:, None] < np.arange(TM_OUT)[None, :]).astype(np.float32)
    h_ext, bucket, rank, cnt = _mixout(
        oa.reshape(n, D_ATTN), yf.reshape(n, D_FOUR), x2, g_attn_out[0][None], g_four_out[0][None],
        w_out[0].astype(BF16), norm_moe[0][None], wrt.astype(BF16), br, jnp.asarray(tri, BF16))

    counts = cnt[:N_BUCKETS, 0].astype(I32)
    dest, tile_g, tile_lo, tile_hi, tile_nv, n_used = _bucket_plan(bucket.reshape(n), rank.reshape(n), counts, n)
    per_step = (n // ROWS_PER_STEP, 1, ROWS_PER_STEP)
    dtile3 = lax.shift_right_logical(dest, 3).reshape(per_step)
    dsub3 = (dest & 7).reshape(per_step)
    n_slots = tile_nv.shape[0] * TM_MOE
    hs = _dispatch(h_ext.reshape(n // 8, 8, D_ROW), dtile3, dsub3, tile_nv, n_slots)
    shape_e = (N_GROUPS, EPG)
    ys = _moe(hs.reshape(n_slots, D_ROW), tile_g, tile_lo, tile_hi, n_used,
              w_gate[0].astype(BF16).reshape(shape_e + (d, D_EXPERT)),
              w_up[0].astype(BF16).reshape(shape_e + (d, D_EXPERT)),
              w_down[0].astype(BF16).reshape(shape_e + (D_EXPERT, d)),
              norm_moe[0][None], norm_final[None])
    return _combine(ys.reshape(n_slots // 8, 8, d), dtile3, dsub3, n).reshape(b, seq, d)
```

```python
import functools

import numpy as np
import jax
import jax.numpy as jnp
from jax import lax
from jax.experimental import pallas as pl
from jax.experimental.pallas import tpu as pltpu

F32 = jnp.float32
BF16 = jnp.bfloat16
I32 = jnp.int32

D_MODEL = 1024
SEQ = 4096
GRID_W = 64
ROWS = SEQ // GRID_W
D_ATTN = 512
D_FOUR = 512
N_HEADS = 8
HEAD_DIM = 64
WIN_H = 8
WIN_W = 16
N_FOUR_GROUPS = 8
FOUR_GROUP_DIM = 64
D_PROJ = 3 * D_ATTN + D_FOUR
N_GROUPS = 4
EPG = 8
N_EXPERTS = N_GROUPS * EPG
D_EXPERT = 256
EPS = 1e-6
NEG = -1e30

V7X_VMEM_LIMIT = 56 * 1024 * 1024

TM_IN = 512
TM_OUT = 512
OUT_SUBTILES = 2
TM_MOE = 128
ROW_TILE = D_MODEL // 128
ROWS_PER_STEP = 1024
TILES_PER_STEP = 2
N_PAIRS = EPG * (EPG - 1) // 2
N_BUCKETS = N_GROUPS * N_PAIRS
BUCKET_LANES = 128

QB_ROWS = 8
QB_COLS = 16
KB_ROWS = 16
KB_COLS = 32
ATTN_AHEAD = 3


def _rms(x, g):
    ms = jnp.mean(x * x, axis=-1, keepdims=True)
    return x * lax.rsqrt(ms + EPS) * g


def _inproj_kernel(x_ref, g_ref, w_ref, qkv_ref, u_ref):
    xn = _rms(x_ref[...], g_ref[...]).astype(BF16)
    p = jnp.dot(xn, w_ref[...], preferred_element_type=F32)
    qkv_ref[:, :D_ATTN] = (p[:, :D_ATTN] * (HEAD_DIM ** -0.5)).astype(BF16)
    qkv_ref[:, D_ATTN:] = p[:, D_ATTN:3 * D_ATTN].astype(BF16)
    u_ref[...] = p[:, 3 * D_ATTN:]


def _inproj(x2, g, w_bf):
    n = x2.shape[0]
    return pl.pallas_call(
        _inproj_kernel,
        grid=(n // TM_IN,),
        in_specs=[
            pl.BlockSpec((TM_IN, D_MODEL), lambda i: (i, 0)),
            pl.BlockSpec((1, D_MODEL), lambda i: (0, 0)),
            pl.BlockSpec((D_MODEL, D_PROJ), lambda i: (0, 0)),
        ],
        out_specs=[
            pl.BlockSpec((TM_IN, 3 * D_ATTN), lambda i: (i, 0)),
            pl.BlockSpec((TM_IN, D_FOUR), lambda i: (i, 0)),
        ],
        out_shape=[
            jax.ShapeDtypeStruct((n, 3 * D_ATTN), BF16),
            jax.ShapeDtypeStruct((n, D_FOUR), F32),
        ],
        compiler_params=pltpu.CompilerParams(
            dimension_semantics=("parallel",), vmem_limit_bytes=V7X_VMEM_LIMIT),
        name="inproj",
    )(x2, g, w_bf)


_KCOL_START = (0, 8, 24, 32)
_KCOL_SHIFTED = (False, True, True, False)
_KCOL_OFF = (0, 0, 16, 32)
_COL_TYPE = (0, 1, 1, 2)


def _bias_index_tables():
    dr = np.zeros((9, 128, 512), np.int32)
    dc = np.zeros((9, 128, 512), np.int32)
    ok = np.zeros((9, 128, 512), bool)
    qi, qc = np.divmod(np.arange(128), QB_COLS)
    ki, kc = np.divmod(np.arange(512), KB_COLS)
    for rt, (q0, k0) in enumerate(((0, 0), (8, 4), (56, 48))):
        qrow = q0 + qi
        krow = k0 + ki
        rs = np.clip(qrow - WIN_H // 2, 0, ROWS - WIN_H)
        rok = (krow[None, :] >= rs[:, None]) & (krow[None, :] < rs[:, None] + WIN_H)
        drr = krow[None, :] - qrow[:, None] + (WIN_H - 1)
        for ct, (c0, kc0) in enumerate(((0, 0), (16, 8), (48, 32))):
            qcol = c0 + qc
            kcol = kc0 + kc
            cs = np.clip(qcol - WIN_W // 2, 0, GRID_W - WIN_W)
            cok = (kcol[None, :] >= cs[:, None]) & (kcol[None, :] < cs[:, None] + WIN_W)
            dcc = kcol[None, :] - qcol[:, None] + (WIN_W - 1)
            t = rt * 3 + ct
            ok[t] = rok & cok
            dr[t] = np.where(ok[t], drr, 0)
            dc[t] = np.where(ok[t], dcc, 0)
    return dr, dc, ok


_BIAS_DR, _BIAS_DC, _BIAS_OK = _bias_index_tables()


def _bias_selectors():
    ok = _BIAS_OK.reshape(3, 3, QB_ROWS, QB_COLS, KB_ROWS, KB_COLS)
    dr = _BIAS_DR.reshape(ok.shape)
    dc = _BIAS_DC.reshape(ok.shape)
    row_ok = ok.any(axis=(1, 3, 5))
    col_ok = ok.any(axis=(0, 2, 4))
    dr_r = dr.max(axis=(1, 3, 5))
    dc_c = dc.max(axis=(0, 2, 4))
    sc = (np.arange(2 * WIN_W - 1)[:, None, None, None] == dc_c[None]) & col_ok[None]
    return row_ok, dr_r, col_ok, sc.astype(np.float32)


_BIAS_ROW_OK, _BIAS_ROW_DR, _BIAS_COL_OK, _BIAS_SC = _bias_selectors()


def _bias_columns(rpb):
    sc = jnp.asarray(np.tile(_BIAS_SC, 128 // KB_COLS))
    ok = np.tile(_BIAS_COL_OK, 128 // KB_COLS)
    t1 = jnp.sum(rpb[:, :, :, None, None, None] * sc[None, None], axis=2)
    return jnp.where(ok[None, None], t1, NEG)


def _attn_kernel(q_ref, k_ref, v_ref, bcol_ref, o_ref, ksh_ref, vsh_ref, bias_ref):
    @pl.when(pl.program_id(1) == 0)
    def _():
        key_row = lax.broadcasted_iota(I32, (QB_COLS, KB_ROWS * KB_COLS), 1) // KB_COLS
        for hh in range(2):
            for rt in range(3):
                for ct in range(3):
                    for i in range(QB_ROWS):
                        acc = jnp.full((QB_COLS, KB_ROWS * KB_COLS), NEG, F32)
                        for y in range(KB_ROWS):
                            if _BIAS_ROW_OK[rt, i, y]:
                                cols = bcol_ref[hh, int(_BIAS_ROW_DR[rt, i, y]), ct]
                                cols = jnp.concatenate([cols] * (KB_ROWS * KB_COLS // 128), axis=1)
                                acc = jnp.where(key_row == y, cols, acc)
                        bias_ref[rt * 3 + ct, pl.ds(hh * 128 + i * QB_COLS, QB_COLS), :] = acc

    zpad = jnp.zeros((8, 128), F32)
    ksh_ref[...] = jnp.concatenate([k_ref[...].astype(F32)[8:], zpad], axis=0).astype(BF16)
    vsh_ref[...] = jnp.concatenate([v_ref[...].astype(F32)[8:], zpad], axis=0).astype(BF16)
    lane = lax.broadcasted_iota(I32, (1, 128), 1)
    head_masks = (lane < HEAD_DIM, lane >= HEAD_DIM)

    def row_block(rb, carry):
        rb = jnp.asarray(rb, I32)
        ks = jnp.clip(QB_ROWS * rb - WIN_H // 2, 0, ROWS - KB_ROWS)
        rt = jnp.where(rb == 0, 0, jnp.where(rb == ROWS // QB_ROWS - 1, 2, 1))
        for j in range(GRID_W // QB_COLS):
            kr, vr = (ksh_ref, vsh_ref) if _KCOL_SHIFTED[j] else (k_ref, v_ref)
            q = jnp.concatenate(
                [q_ref[pl.ds(pl.multiple_of((QB_ROWS * rb + i) * GRID_W + QB_COLS * j, 16), QB_COLS), :]
                 for i in range(QB_ROWS)], axis=0)
            kstarts = [pl.multiple_of((ks + i) * GRID_W + _KCOL_OFF[j], 16) for i in range(KB_ROWS)]
            kt = jnp.concatenate([kr[pl.ds(s, KB_COLS), :] for s in kstarts], axis=0)
            vt = jnp.concatenate([vr[pl.ds(s, KB_COLS), :] for s in kstarts], axis=0)
            qm = jnp.concatenate([jnp.where(hm, q, jnp.zeros_like(q)) for hm in head_masks], axis=0)
            s = lax.dot_general(qm, kt, (((1,), (1,)), ((), ())), preferred_element_type=F32)
            s = s + bias_ref[rt * 3 + _COL_TYPE[j]]
            e = jnp.exp(s - jnp.max(s, axis=-1, keepdims=True))
            o = jnp.dot(e.astype(BF16), vt, preferred_element_type=F32)
            o = o / jnp.sum(e, axis=-1, keepdims=True)
            out = jnp.where(head_masks[0], o[:128], o[128:]).astype(BF16)
            for i in range(QB_ROWS):
                o_ref[pl.ds(pl.multiple_of((QB_ROWS * rb + i) * GRID_W + QB_COLS * j, 16), QB_COLS), :] = (
                    out[QB_COLS * i:QB_COLS * (i + 1)])
        return carry

    lax.fori_loop(0, ROWS // QB_ROWS, row_block, 0)


def _attention(qkv3, bias_cols):
    b = qkv3.shape[0]
    n_hp = N_HEADS // 2
    blk = lambda off: pl.BlockSpec((None, SEQ, 128), lambda hp, bi: (bi, 0, off + hp))
    return pl.pallas_call(
        _attn_kernel,
        grid=(n_hp, b),
        in_specs=[
            blk(0), blk(n_hp), blk(2 * n_hp),
            pl.BlockSpec((2,) + bias_cols.shape[1:], lambda hp, bi: (hp, 0, 0, 0, 0)),
        ],
        out_specs=pl.BlockSpec((None, SEQ, 128), lambda hp, bi: (bi, 0, hp)),
        out_shape=jax.ShapeDtypeStruct((b, SEQ, D_ATTN), BF16),
        scratch_shapes=[pltpu.VMEM((SEQ, 128), BF16), pltpu.VMEM((SEQ, 128), BF16),
                        pltpu.VMEM((9, 2 * QB_ROWS * QB_COLS, KB_ROWS * KB_COLS), F32)],
        compiler_params=pltpu.CompilerParams(
            dimension_semantics=("arbitrary", "arbitrary"), vmem_limit_bytes=V7X_VMEM_LIMIT),
        name="nattn",
    )(qkv3, qkv3, qkv3, bias_cols)


def _fourier_tables():
    n = 64
    k = np.arange(n)
    ang = 2.0 * np.pi * np.outer(k, k) / n
    c64, s64 = np.cos(ang), np.sin(ang)
    eye8 = np.eye(8)
    k1 = np.concatenate([np.kron(c64, eye8), np.kron(-s64, eye8)], axis=0)
    a = np.arange(8)[:, None, None, None, None]
    t2p = np.arange(64)[None, :, None, None, None]
    jo = np.arange(8)[None, None, :, None, None]
    c = np.arange(8)[None, None, None, :, None]
    j = np.arange(8)[None, None, None, None, :]
    idx = ((8 * c + j) * (8 * a + jo + 64 * t2p)) % SEQ
    th = 2.0 * np.pi * idx / SEQ
    cos_t, sin_t = np.cos(th), np.sin(th)
    delta = np.eye(8)[None, None, :, None, :, None]
    def expand(m):
        return (m[:, :, :, :, None, :] * delta).reshape(8, 512, 512)
    cc, ss = expand(cos_t), expand(sin_t)
    la = np.concatenate([np.concatenate([cc, ss], axis=2), np.concatenate([-ss, cc], axis=2)], axis=1)
    cbd = np.kron(np.eye(4), c64)
    sbd = np.kron(np.eye(4), s64)
    cs = np.concatenate([cbd, sbd], axis=0)
    return k1.astype(np.float32), la.astype(np.float32), cs.astype(np.float32)


_K1_NP, _LA_NP, _CS_NP = _fourier_tables()


def _fourier_kernel(u_ref, k1_ref, la_ref, cs_ref, wbd_ref, bf_ref, y_ref, zs_ref):
    s = pl.program_id(1)

    @pl.when(s < 8)
    def _():
        xc = u_ref[...].reshape(512, D_FOUR).astype(BF16)
        z = jnp.dot(k1_ref[...], xc, preferred_element_type=F32)
        zs_ref[s] = z.astype(BF16)

    @pl.when(s >= 8)
    def _():
        off = pl.multiple_of((s - 8) * 64, 64)
        rhs = jnp.concatenate(
            [zs_ref[c, pl.ds(part * 512 + off, 64), :] for part in range(2) for c in range(8)], axis=0)
        x = jnp.dot(la_ref[...], rhs, preferred_element_type=F32)
        xr = x[:512].astype(BF16)
        xi = x[512:].astype(BF16)
        halves = []
        for hf in range(2):
            sl = slice(256 * hf, 256 * (hf + 1))
            lhs = jnp.concatenate([xr[:, sl], xi[:, sl]], axis=1)
            f = jnp.dot(lhs, cs_ref[...], preferred_element_type=F32) * (1.0 / 512.0)
            halves.append(jnp.dot(f.astype(BF16), wbd_ref[hf], preferred_element_type=F32))
        y = jnp.concatenate(halves, axis=1) + bf_ref[...]
        y_ref[...] = y.reshape(64, 8, D_FOUR)


def _fourier(u4, wbd, bf):
    b = u4.shape[0]
    k1 = jnp.asarray(_K1_NP).astype(BF16)
    la = jnp.asarray(_LA_NP).astype(BF16)
    cs = jnp.asarray(_CS_NP).astype(BF16)
    return pl.pallas_call(
        _fourier_kernel,
        grid=(b, 16),
        in_specs=[
            pl.BlockSpec((None, 64, 8, D_FOUR), lambda bi, s: (bi, 0, jnp.minimum(s, 7), 0)),
            pl.BlockSpec((1024, 512), lambda bi, s: (0, 0)),
            pl.BlockSpec((None, 1024, 1024), lambda bi, s: (jnp.maximum(s - 8, 0), 0, 0)),
            pl.BlockSpec((512, 256), lambda bi, s: (0, 0)),
            pl.BlockSpec((2, 256, 256), lambda bi, s: (0, 0, 0)),
            pl.BlockSpec((1, D_FOUR), lambda bi, s: (0, 0)),
        ],
        out_specs=pl.BlockSpec((None, 64, 8, D_FOUR), lambda bi, s: (bi, 0, jnp.maximum(s - 8, 0), 0)),
        out_shape=jax.ShapeDtypeStruct((b, 64, 64, D_FOUR), F32),
        scratch_shapes=[pltpu.VMEM((8, 1024, 512), BF16)],
        compiler_params=pltpu.CompilerParams(
            dimension_semantics=("parallel", "arbitrary"), vmem_limit_bytes=V7X_VMEM_LIMIT),
        name="fourier",
    )(u4, k1, la, cs, wbd, bf)


def _mixout_kernel(oa_ref, yf_ref, x_ref, ga_ref, gf_ref, wout_ref, gm_ref, wrt_ref, br_ref, tri_ref,
                   h_ref, bucket_ref, rank_ref, cnt_ref, carry_ref):
    i = pl.program_id(0)

    @pl.when(i == 0)
    def _():
        carry_ref[...] = jnp.zeros_like(carry_ref)

    carry = carry_ref[...]
    for k in range(OUT_SUBTILES):
        carry = _mixout_subtile(k, carry, oa_ref, yf_ref, x_ref, ga_ref, gf_ref, wout_ref, gm_ref, wrt_ref,
                                br_ref, tri_ref, h_ref, bucket_ref, rank_ref)
    carry_ref[...] = carry
    cnt_ref[...] = carry


def _mixout_subtile(k, carry, oa_ref, yf_ref, x_ref, ga_ref, gf_ref, wout_ref, gm_ref, wrt_ref, br_ref, tri_ref,
                    h_ref, bucket_ref, rank_ref):
    rows_k = pl.ds(k * TM_OUT, TM_OUT)
    na = _rms(oa_ref[rows_k, :].astype(F32), ga_ref[...]).astype(BF16)
    nf = _rms(yf_ref[rows_k, :], gf_ref[...]).astype(BF16)
    merged = jnp.concatenate([na, nf], axis=1)
    h = x_ref[rows_k, :] + jnp.dot(merged, wout_ref[...], preferred_element_type=F32)
    for cb in range(ROW_TILE):
        h_ref[pl.ds(k * TM_OUT * ROW_TILE + cb, TM_OUT, stride=ROW_TILE), :] = h[:, cb * 128:(cb + 1) * 128]
    hn = _rms(h, gm_ref[...]).astype(BF16)
    lt = lax.dot_general(wrt_ref[...], hn, (((1,), (1,)), ((), ())), preferred_element_type=F32)
    lt = lt + br_ref[...]
    c = [lt[k:k + 1] for k in range(N_GROUPS)]
    cmax = jnp.maximum(jnp.maximum(c[0], c[1]), jnp.maximum(c[2], c[3]))
    e = [jnp.exp(ck - cmax) for ck in c]
    esum = (e[0] + e[1]) + (e[2] + e[3])
    p = [ek / esum for ek in e]
    pmax = jnp.maximum(jnp.maximum(p[0], p[1]), jnp.maximum(p[2], p[3]))
    g = jnp.where(p[0] == pmax, 0, jnp.where(p[1] == pmax, 1, jnp.where(p[2] == pmax, 2, 3))).astype(I32)
    fine = jnp.where(g == 0, lt[8:16], jnp.where(g == 1, lt[16:24], jnp.where(g == 2, lt[24:32], lt[32:40])))
    rows = lax.broadcasted_iota(I32, fine.shape, 0)
    v1 = jnp.max(fine, axis=0, keepdims=True)
    i1 = jnp.min(jnp.where(fine == v1, rows, EPG), axis=0, keepdims=True)
    rest = jnp.where(rows == i1, -jnp.inf, fine)
    v2 = jnp.max(rest, axis=0, keepdims=True)
    i2 = jnp.min(jnp.where(rest == v2, rows, EPG), axis=0, keepdims=True)
    lo = jnp.minimum(i1, i2)
    hi = jnp.maximum(i1, i2)
    pair = lax.shift_right_logical(lo * (2 * EPG - 1 - lo), 1) + (hi - lo - 1)
    bucket = g * N_PAIRS + pair
    bucket_ref[k] = bucket
    brow = lax.broadcasted_iota(I32, (BUCKET_LANES, TM_OUT), 0)
    onehot = (brow == bucket).astype(F32)
    prefix = jnp.dot(onehot.astype(BF16), tri_ref[...], preferred_element_type=F32)
    rank = jnp.sum(onehot * (prefix + carry), axis=0, keepdims=True)
    rank_ref[k] = rank.astype(I32)
    return carry + jnp.sum(onehot, axis=1, keepdims=True)


def _mixout(oa, yf, x2, ga, gf, wout_bf, gm, wrt, br, tri):
    n = x2.shape[0]
    nt = n // TM_OUT
    rows_step = TM_OUT * OUT_SUBTILES
    full = lambda *shape: pl.BlockSpec(shape, lambda i: (0,) * len(shape))
    row3 = pl.BlockSpec((OUT_SUBTILES, 1, TM_OUT), lambda i: (i, 0, 0))
    return pl.pallas_call(
        _mixout_kernel,
        grid=(n // rows_step,),
        in_specs=[
            pl.BlockSpec((rows_step, D_ATTN), lambda i: (i, 0)),
            pl.BlockSpec((rows_step, D_FOUR), lambda i: (i, 0)),
            pl.BlockSpec((rows_step, D_MODEL), lambda i: (i, 0)),
            full(1, D_ATTN), full(1, D_FOUR), full(D_MODEL, D_MODEL), full(1, D_MODEL),
            full(BUCKET_LANES, D_MODEL), full(BUCKET_LANES, 1), full(TM_OUT, TM_OUT),
        ],
        out_specs=[
            pl.BlockSpec((rows_step * ROW_TILE, 128), lambda i: (i, 0)),
            row3, row3,
            full(BUCKET_LANES, 1),
        ],
        out_shape=[
            jax.ShapeDtypeStruct((n * ROW_TILE, 128), F32),
            jax.ShapeDtypeStruct((nt, 1, TM_OUT), I32),
            jax.ShapeDtypeStruct((nt, 1, TM_OUT), I32),
            jax.ShapeDtypeStruct((BUCKET_LANES, 1), F32),
        ],
        scratch_shapes=[pltpu.VMEM((BUCKET_LANES, 1), F32)],
        compiler_params=pltpu.CompilerParams(
            dimension_semantics=("arbitrary",), vmem_limit_bytes=V7X_VMEM_LIMIT),
        name="mixout",
    )(oa, yf, x2, ga, gf, wout_bf, gm, wrt, br, tri)


def _pair_tables():
    lo, hi = [], []
    for a in range(EPG):
        for b in range(a + 1, EPG):
            lo.append(a)
            hi.append(b)
    return np.asarray(lo, np.int32), np.asarray(hi, np.int32)


_PAIR_LO, _PAIR_HI = _pair_tables()


def _dispatch_kernel(tnv_ref, dest_ref, h_ref, hs_hbm, zbuf, zsem, sem):
    k = pl.program_id(0)
    tile_rows = TM_MOE * ROW_TILE
    n_tiles = hs_hbm.shape[0] // tile_rows

    def zero_copy(t):
        return pltpu.make_async_copy(zbuf, hs_hbm.at[pl.ds(pl.multiple_of(t * tile_rows, tile_rows), tile_rows)],
                                     zsem)

    @pl.when(k == 0)
    def _():
        zbuf[...] = jnp.zeros_like(zbuf)

        def zstart(t, c):
            @pl.when(tnv_ref[t] < TM_MOE)
            def _():
                zero_copy(t).start()
            return c

        def zwait(t, c):
            @pl.when(tnv_ref[t] < TM_MOE)
            def _():
                zero_copy(t).wait()
            return c

        lax.fori_loop(0, n_tiles, zstart, 0)
        lax.fori_loop(0, n_tiles, zwait, 0)

    def rows(r8, c):
        for u in range(8):
            r = r8 * 8 + u
            dst = pl.multiple_of(dest_ref[0, 0, r] * ROW_TILE, ROW_TILE)
            pltpu.make_async_copy(h_ref.at[pl.ds(pl.multiple_of(r * ROW_TILE, ROW_TILE), ROW_TILE)],
                                  hs_hbm.at[pl.ds(dst, ROW_TILE)], sem).start()
        return c

    lax.fori_loop(0, ROWS_PER_STEP // 8, rows, 0)
    pltpu.make_async_copy(h_ref, hs_hbm.at[pl.ds(0, ROWS_PER_STEP * ROW_TILE)], sem).wait()


def _dispatch(h_rt, dest3, tile_nv, n_slots):
    n = h_rt.shape[0] // ROW_TILE
    grid_spec = pltpu.PrefetchScalarGridSpec(
        num_scalar_prefetch=1,
        grid=(n // ROWS_PER_STEP,),
        in_specs=[
            pl.BlockSpec((1, 1, ROWS_PER_STEP), lambda k, *_: (k, 0, 0), memory_space=pltpu.SMEM),
            pl.BlockSpec((ROWS_PER_STEP * ROW_TILE, 128), lambda k, *_: (k, 0)),
        ],
        out_specs=pl.BlockSpec(memory_space=pl.ANY),
        scratch_shapes=[pltpu.VMEM((TM_MOE * ROW_TILE, 128), F32),
                        pltpu.SemaphoreType.DMA(()), pltpu.SemaphoreType.DMA(())],
    )
    return pl.pallas_call(
        _dispatch_kernel,
        grid_spec=grid_spec,
        out_shape=jax.ShapeDtypeStruct((n_slots * ROW_TILE, 128), F32),
        compiler_params=pltpu.CompilerParams(
            dimension_semantics=("arbitrary",), vmem_limit_bytes=V7X_VMEM_LIMIT),
        name="dispatch",
    )(tile_nv, dest3, h_rt)


def _combine_kernel(dest_ref, ys_hbm, o_ref, buf, sem):
    def rows(r8, c):
        for u in range(8):
            r = r8 * 8 + u
            src = pl.multiple_of(dest_ref[0, 0, r] * ROW_TILE, ROW_TILE)
            pltpu.make_async_copy(ys_hbm.at[pl.ds(src, ROW_TILE)],
                                  buf.at[pl.ds(pl.multiple_of(r * ROW_TILE, ROW_TILE), ROW_TILE)], sem).start()
        return c

    lax.fori_loop(0, ROWS_PER_STEP // 8, rows, 0)
    pltpu.make_async_copy(ys_hbm.at[pl.ds(0, ROWS_PER_STEP * ROW_TILE)], buf, sem).wait()
    for cb in range(ROW_TILE):
        o_ref[:, cb * 128:(cb + 1) * 128] = buf[pl.ds(cb, ROWS_PER_STEP, stride=ROW_TILE), :]


def _combine(ys_rt, dest3, n):
    return pl.pallas_call(
        _combine_kernel,
        grid=(n // ROWS_PER_STEP,),
        in_specs=[
            pl.BlockSpec((1, 1, ROWS_PER_STEP), lambda k: (k, 0, 0), memory_space=pltpu.SMEM),
            pl.BlockSpec(memory_space=pl.ANY),
        ],
        out_specs=pl.BlockSpec((ROWS_PER_STEP, D_MODEL), lambda k: (k, 0)),
        out_shape=jax.ShapeDtypeStruct((n, D_MODEL), F32),
        scratch_shapes=[pltpu.VMEM((ROWS_PER_STEP * ROW_TILE, 128), F32), pltpu.SemaphoreType.DMA(())],
        compiler_params=pltpu.CompilerParams(
            dimension_semantics=("arbitrary",), vmem_limit_bytes=V7X_VMEM_LIMIT),
        name="combine",
    )(dest3, ys_rt)


def _moe_kernel(tg_ref, tlo_ref, thi_ref, nused_ref, hs_ref, wg_ref, wu_ref, wd_ref, wr_ref, br_ref,
                gm_ref, gfin_ref, ys_ref):
    step = pl.program_id(0)

    @pl.when(step * TILES_PER_STEP < nused_ref[0])
    def _():
        lane = lax.broadcasted_iota(I32, (TM_MOE, BUCKET_LANES), 1)
        for k in range(TILES_PER_STEP):
            t = step * TILES_PER_STEP + k
            base = k * TM_MOE * ROW_TILE
            hrows = jnp.concatenate(
                [hs_ref[pl.ds(base + cb, TM_MOE, stride=ROW_TILE), :] for cb in range(ROW_TILE)], axis=1)
            hn = _rms(hrows, gm_ref[...]).astype(BF16)
            logits = jnp.dot(hn, wr_ref[...], preferred_element_type=F32) + br_ref[...]
            g, lo, hi = tg_ref[t], tlo_ref[t], thi_ref[t]
            coarse = jnp.where(lane < N_GROUPS, logits, -jnp.inf)
            ec = jnp.exp(coarse - jnp.max(coarse, axis=-1, keepdims=True))
            pick = lambda col, v: jnp.sum(jnp.where(lane == col, v, 0.0), axis=-1, keepdims=True)
            g_w = pick(g, ec) / jnp.sum(ec, axis=-1, keepdims=True)
            f_lo = pick(8 + g * EPG + lo, logits)
            f_hi = pick(8 + g * EPG + hi, logits)
            f_max = jnp.maximum(f_lo, f_hi)
            e_lo = jnp.exp(f_lo - f_max)
            e_hi = jnp.exp(f_hi - f_max)
            den = e_lo + e_hi

            def expert(e, w):
                gate = jnp.dot(hn, wg_ref[e], preferred_element_type=F32)
                up = jnp.dot(hn, wu_ref[e], preferred_element_type=F32)
                act = (gate * jax.nn.sigmoid(gate) * up).astype(BF16)
                return w * jnp.dot(act, wd_ref[e], preferred_element_type=F32)

            y = expert(lo, (e_lo / den) * g_w) + expert(hi, (e_hi / den) * g_w)
            res = _rms(hrows + y, gfin_ref[...])
            for cb in range(ROW_TILE):
                ys_ref[pl.ds(base + cb, TM_MOE, stride=ROW_TILE), :] = res[:, cb * 128:(cb + 1) * 128]

    @pl.when(step * TILES_PER_STEP >= nused_ref[0])
    def _():
        ys_ref[...] = jnp.zeros_like(ys_ref)


def _moe(hs_rt, tile_g, tile_lo, tile_hi, n_used, wg, wu, wd, wr, br, gm, gfin):
    rows_step = TM_MOE * TILES_PER_STEP * ROW_TILE
    n_steps = hs_rt.shape[0] // rows_step
    by_group = lambda s, tg, *_: (tg[s * TILES_PER_STEP], 0, 0, 0)
    full2 = lambda a, c: pl.BlockSpec((a, c), lambda s, *_: (0, 0))

    def hs_index(s, tg, tlo, thi, nu):
        last_step = jnp.maximum(nu[0] - 1, 0) // TILES_PER_STEP
        return (jnp.minimum(s, last_step), 0)

    grid_spec = pltpu.PrefetchScalarGridSpec(
        num_scalar_prefetch=4,
        grid=(n_steps,),
        in_specs=[
            pl.BlockSpec((rows_step, 128), hs_index),
            pl.BlockSpec((None, EPG, D_MODEL, D_EXPERT), by_group),
            pl.BlockSpec((None, EPG, D_MODEL, D_EXPERT), by_group),
            pl.BlockSpec((None, EPG, D_EXPERT, D_MODEL), by_group),
            full2(D_MODEL, BUCKET_LANES), full2(1, BUCKET_LANES), full2(1, D_MODEL), full2(1, D_MODEL),
        ],
        out_specs=pl.BlockSpec((rows_step, 128), lambda s, *_: (s, 0)),
    )
    return pl.pallas_call(
        _moe_kernel,
        grid_spec=grid_spec,
        out_shape=jax.ShapeDtypeStruct(hs_rt.shape, F32),
        compiler_params=pltpu.CompilerParams(
            dimension_semantics=("arbitrary",), vmem_limit_bytes=V7X_VMEM_LIMIT),
        name="moe",
    )(tile_g, tile_lo, tile_hi, n_used, hs_rt, wg, wu, wd, wr, br, gm, gfin)


def _bucket_plan(bucket, rank, counts, n):
    nt = n // TM_MOE + N_BUCKETS + N_GROUPS * (TILES_PER_STEP - 1)
    nt = -(-nt // TILES_PER_STEP) * TILES_PER_STEP
    tiles_b = (counts + (TM_MOE - 1)) // TM_MOE
    tiles_g = jnp.sum(tiles_b.reshape(N_GROUPS, N_PAIRS), axis=1)
    extra_g = (-tiles_g) % TILES_PER_STEP
    is_last = (np.arange(N_PAIRS) == N_PAIRS - 1)[None, :]
    tiles_b = (tiles_b.reshape(N_GROUPS, N_PAIRS) + jnp.where(is_last, extra_g[:, None], 0)).reshape(N_BUCKETS)
    tile_end = jnp.cumsum(tiles_b)
    tile_start = tile_end - tiles_b
    n_used = tile_end[-1]
    b_ids = jnp.arange(N_BUCKETS, dtype=I32)
    dest = rank + TM_MOE * jnp.sum(jnp.where(bucket[:, None] == b_ids[None, :], tile_start[None, :], 0), axis=1)
    t_idx = jnp.arange(nt, dtype=I32)
    tb = jnp.sum((tile_end[None, :] <= t_idx[:, None]).astype(I32), axis=1)
    tb_last = jnp.sum((tile_end <= n_used - 1).astype(I32))
    tb = jnp.minimum(jnp.where(t_idx < n_used, tb, tb_last), N_BUCKETS - 1)
    sel = tb[:, None] == b_ids[None, :]
    pick = lambda table: jnp.sum(jnp.where(sel, table[None, :], 0), axis=1).astype(I32)
    tile_g = tb // N_PAIRS
    tile_lo = pick(jnp.asarray(np.tile(_PAIR_LO, N_GROUPS)))
    tile_hi = pick(jnp.asarray(np.tile(_PAIR_HI, N_GROUPS)))
    nv = jnp.clip(pick(counts) - (t_idx - pick(tile_start)) * TM_MOE, 0, TM_MOE)
    tile_nv = jnp.where(t_idx < n_used, nv, 0).astype(I32)
    return dest.astype(I32), tile_g.astype(I32), tile_lo, tile_hi, tile_nv, n_used.reshape(1).astype(I32)


def kernel(x, norm_mix, w_in, rpb, w_four, b_four, g_attn_out, g_four_out, w_out, norm_moe,
           w_router_coarse, b_router_coarse, w_router_fine, b_router_fine, w_gate, w_up, w_down, norm_final):
    b, seq, d = x.shape
    assert (seq, d) == (SEQ, D_MODEL) and norm_mix.shape[0] == 1
    n = b * seq
    x2 = x.reshape(n, d)

    qkv, u = _inproj(x2, norm_mix[0][None], w_in[0].astype(BF16))

    oa = _attention(qkv.reshape(b, seq, 3 * D_ATTN), _bias_columns(rpb[0]))

    eye4 = jnp.eye(4, dtype=F32)
    wf = w_four[0].reshape(2, 4, FOUR_GROUP_DIM, FOUR_GROUP_DIM)
    wbd = (eye4[None, :, None, :, None] * wf[:, :, :, None, :]).reshape(2, 256, 256).astype(BF16)
    yf = _fourier(u.reshape(b, ROWS, GRID_W, D_FOUR), wbd, b_four[0][None])

    wrt = jnp.zeros((BUCKET_LANES, d), F32)
    wrt = wrt.at[0:N_GROUPS].set(w_router_coarse[0].T).at[8:8 + N_EXPERTS].set(w_router_fine[0].T)
    br = jnp.zeros((BUCKET_LANES, 1), F32)
    br = br.at[0:N_GROUPS, 0].set(b_router_coarse[0]).at[8:8 + N_EXPERTS, 0].set(b_router_fine[0])
    tri = (np.arange(TM_OUT)[:, None] < np.arange(TM_OUT)[None, :]).astype(np.float32)
    wrt_bf = wrt.astype(BF16)
    h_rt, bucket, rank, cnt = _mixout(
        oa.reshape(n, D_ATTN), yf.reshape(n, D_FOUR), x2, g_attn_out[0][None], g_four_out[0][None],
        w_out[0].astype(BF16), norm_moe[0][None], wrt_bf, br, jnp.asarray(tri, BF16))

    counts = cnt[:N_BUCKETS, 0].astype(I32)
    dest, tile_g, tile_lo, tile_hi, tile_nv, n_used = _bucket_plan(bucket.reshape(n), rank.reshape(n), counts, n)
    dest3 = dest.reshape(n // ROWS_PER_STEP, 1, ROWS_PER_STEP)
    hs_rt = _dispatch(h_rt, dest3, tile_nv, tile_nv.shape[0] * TM_MOE)
    shape_e = (N_GROUPS, EPG)
    ys_rt = _moe(hs_rt, tile_g, tile_lo, tile_hi, n_used,
                 w_gate[0].astype(BF16).reshape(shape_e + (d, D_EXPERT)),
                 w_up[0].astype(BF16).reshape(shape_e + (d, D_EXPERT)),
                 w_down[0].astype(BF16).reshape(shape_e + (D_EXPERT, d)),
                 wrt_bf.T, br.T, norm_moe[0][None], norm_final[None])
    return _combine(ys_rt, dest3, n).reshape(b, seq, d)
```

```python
import functools

import numpy as np
import jax
import jax.numpy as jnp
from jax import lax
from jax.experimental import pallas as pl
from jax.experimental.pallas import tpu as pltpu

F32 = jnp.float32
BF16 = jnp.bfloat16
I32 = jnp.int32

D_MODEL = 1024
SEQ = 4096
GRID_W = 64
ROWS = SEQ // GRID_W
D_ATTN = 512
D_FOUR = 512
N_HEADS = 8
HEAD_DIM = 64
WIN_H = 8
WIN_W = 16
N_FOUR_GROUPS = 8
FOUR_GROUP_DIM = 64
D_PROJ = 3 * D_ATTN + D_FOUR
N_GROUPS = 4
EPG = 8
N_EXPERTS = N_GROUPS * EPG
D_EXPERT = 256
EPS = 1e-6
NEG = -1e30

V7X_VMEM_LIMIT = 56 * 1024 * 1024

TM_IN = 512
TM_OUT = 512
OUT_SUBTILES = 2
TM_MOE = 128
ROW_TILE = D_MODEL // 128
ROWS_PER_STEP = 2048
TILES_PER_STEP = 2
N_PAIRS = EPG * (EPG - 1) // 2
N_BUCKETS = N_GROUPS * N_PAIRS
BUCKET_LANES = 128

QB_ROWS = 8
QB_COLS = 16
KB_ROWS = 16
KB_COLS = 32
GRID_PITCH = 72


def _rms(x, g):
    ms = jnp.mean(x * x, axis=-1, keepdims=True)
    return x * lax.rsqrt(ms + EPS) * g


def _inproj_kernel(x_ref, g_ref, w_ref, qkv_ref, u_ref):
    xn = _rms(x_ref[...], g_ref[...]).astype(BF16)
    p = jnp.dot(xn, w_ref[...], preferred_element_type=F32)
    qkv_ref[:, :D_ATTN] = (p[:, :D_ATTN] * (HEAD_DIM ** -0.5)).astype(BF16)
    qkv_ref[:, D_ATTN:] = p[:, D_ATTN:3 * D_ATTN].astype(BF16)
    for cb in range(D_FOUR // 128):
        lanes = slice(3 * D_ATTN + cb * 128, 3 * D_ATTN + (cb + 1) * 128)
        for r in range(TM_IN // GRID_W):
            u_ref[cb, r * GRID_PITCH:r * GRID_PITCH + GRID_W, :] = p[r * GRID_W:(r + 1) * GRID_W, lanes]
            u_ref[cb, r * GRID_PITCH + GRID_W:(r + 1) * GRID_PITCH, :] = jnp.zeros((GRID_PITCH - GRID_W, 128), F32)


def _inproj(x2, g, w_bf):
    n = x2.shape[0]
    rows_step = TM_IN // GRID_W
    steps_b = ROWS // rows_step
    return pl.pallas_call(
        _inproj_kernel,
        grid=(n // TM_IN,),
        in_specs=[
            pl.BlockSpec((TM_IN, D_MODEL), lambda i: (i, 0)),
            pl.BlockSpec((1, D_MODEL), lambda i: (0, 0)),
            pl.BlockSpec((D_MODEL, D_PROJ), lambda i: (0, 0)),
        ],
        out_specs=[
            pl.BlockSpec((TM_IN, 3 * D_ATTN), lambda i: (i, 0)),
            pl.BlockSpec((D_FOUR // 128, None, rows_step * GRID_PITCH, 128),
                         lambda i: (0, i // steps_b, i % steps_b, 0)),
        ],
        out_shape=[
            jax.ShapeDtypeStruct((n, 3 * D_ATTN), BF16),
            jax.ShapeDtypeStruct((D_FOUR // 128, n // SEQ, ROWS * GRID_PITCH, 128), F32),
        ],
        compiler_params=pltpu.CompilerParams(
            dimension_semantics=("parallel",), vmem_limit_bytes=V7X_VMEM_LIMIT),
        name="inproj",
    )(x2, g, w_bf)


_KCOL_START = (0, 8, 24, 32)
_KCOL_SHIFTED = (False, True, True, False)
_KCOL_OFF = (0, 0, 16, 32)
_COL_TYPE = (0, 1, 1, 2)


def _bias_index_tables():
    dr = np.zeros((9, 128, 512), np.int32)
    dc = np.zeros((9, 128, 512), np.int32)
    ok = np.zeros((9, 128, 512), bool)
    qi, qc = np.divmod(np.arange(128), QB_COLS)
    ki, kc = np.divmod(np.arange(512), KB_COLS)
    for rt, (q0, k0) in enumerate(((0, 0), (8, 4), (56, 48))):
        qrow = q0 + qi
        krow = k0 + ki
        rs = np.clip(qrow - WIN_H // 2, 0, ROWS - WIN_H)
        rok = (krow[None, :] >= rs[:, None]) & (krow[None, :] < rs[:, None] + WIN_H)
        drr = krow[None, :] - qrow[:, None] + (WIN_H - 1)
        for ct, (c0, kc0) in enumerate(((0, 0), (16, 8), (48, 32))):
            qcol = c0 + qc
            kcol = kc0 + kc
            cs = np.clip(qcol - WIN_W // 2, 0, GRID_W - WIN_W)
            cok = (kcol[None, :] >= cs[:, None]) & (kcol[None, :] < cs[:, None] + WIN_W)
            dcc = kcol[None, :] - qcol[:, None] + (WIN_W - 1)
            t = rt * 3 + ct
            ok[t] = rok & cok
            dr[t] = np.where(ok[t], drr, 0)
            dc[t] = np.where(ok[t], dcc, 0)
    return dr, dc, ok


_BIAS_DR, _BIAS_DC, _BIAS_OK = _bias_index_tables()


def _bias_selectors():
    ok = _BIAS_OK.reshape(3, 3, QB_ROWS, QB_COLS, KB_ROWS, KB_COLS)
    dr = _BIAS_DR.reshape(ok.shape)
    dc = _BIAS_DC.reshape(ok.shape)
    row_ok = ok.any(axis=(1, 3, 5))
    col_ok = ok.any(axis=(0, 2, 4))
    dr_r = dr.max(axis=(1, 3, 5))
    dc_c = dc.max(axis=(0, 2, 4))
    sc = (np.arange(2 * WIN_W - 1)[:, None, None, None] == dc_c[None]) & col_ok[None]
    return row_ok, dr_r, col_ok, sc.astype(np.float32)


_BIAS_ROW_OK, _BIAS_ROW_DR, _BIAS_COL_OK, _BIAS_SC = _bias_selectors()


def _bias_columns(rpb):
    sc = jnp.asarray(np.tile(_BIAS_SC, 128 // KB_COLS))
    ok = np.tile(_BIAS_COL_OK, 128 // KB_COLS)
    t1 = jnp.sum(rpb[:, :, :, None, None, None] * sc[None, None], axis=2)
    return jnp.where(ok[None, None], t1, NEG)


def _attn_kernel(q_ref, k_ref, v_ref, bcol_ref, o_ref, ksh_ref, vsh_ref, bias_ref):
    @pl.when(pl.program_id(1) == 0)
    def _():
        key_row = lax.broadcasted_iota(I32, (QB_COLS, KB_ROWS * KB_COLS), 1) // KB_COLS
        for hh in range(2):
            for rt in range(3):
                for ct in range(3):
                    for i in range(QB_ROWS):
                        acc = jnp.full((QB_COLS, KB_ROWS * KB_COLS), NEG, F32)
                        for y in range(KB_ROWS):
                            if _BIAS_ROW_OK[rt, i, y]:
                                cols = bcol_ref[hh, int(_BIAS_ROW_DR[rt, i, y]), ct]
                                cols = jnp.concatenate([cols] * (KB_ROWS * KB_COLS // 128), axis=1)
                                acc = jnp.where(key_row == y, cols, acc)
                        bias_ref[rt * 3 + ct, pl.ds(hh * 128 + i * QB_COLS, QB_COLS), :] = acc

    zpad = jnp.zeros((8, 128), F32)
    ksh_ref[...] = jnp.concatenate([k_ref[...].astype(F32)[8:], zpad], axis=0).astype(BF16)
    vsh_ref[...] = jnp.concatenate([v_ref[...].astype(F32)[8:], zpad], axis=0).astype(BF16)
    lane = lax.broadcasted_iota(I32, (1, 128), 1)
    head_masks = (lane < HEAD_DIM, lane >= HEAD_DIM)

    def row_block(rb, carry):
        rb = jnp.asarray(rb, I32)
        ks = jnp.clip(QB_ROWS * rb - WIN_H // 2, 0, ROWS - KB_ROWS)
        rt = jnp.where(rb == 0, 0, jnp.where(rb == ROWS // QB_ROWS - 1, 2, 1))
        for j in range(GRID_W // QB_COLS):
            kr, vr = (ksh_ref, vsh_ref) if _KCOL_SHIFTED[j] else (k_ref, v_ref)
            q = jnp.concatenate(
                [q_ref[pl.ds(pl.multiple_of((QB_ROWS * rb + i) * GRID_W + QB_COLS * j, 16), QB_COLS), :]
                 for i in range(QB_ROWS)], axis=0)
            kstarts = [pl.multiple_of((ks + i) * GRID_W + _KCOL_OFF[j], 16) for i in range(KB_ROWS)]
            kt = jnp.concatenate([kr[pl.ds(s, KB_COLS), :] for s in kstarts], axis=0)
            vt = jnp.concatenate([vr[pl.ds(s, KB_COLS), :] for s in kstarts], axis=0)
            qm = jnp.concatenate([jnp.where(hm, q, jnp.zeros_like(q)) for hm in head_masks], axis=0)
            s = lax.dot_general(qm, kt, (((1,), (1,)), ((), ())), preferred_element_type=F32)
            s = s + bias_ref[rt * 3 + _COL_TYPE[j]]
            e = jnp.exp(s - jnp.max(s, axis=-1, keepdims=True))
            o = jnp.dot(e.astype(BF16), vt, preferred_element_type=F32)
            o = o / jnp.sum(e, axis=-1, keepdims=True)
            out = jnp.where(head_masks[0], o[:128], o[128:]).astype(BF16)
            for i in range(QB_ROWS):
                o_ref[pl.ds(pl.multiple_of((QB_ROWS * rb + i) * GRID_W + QB_COLS * j, 16), QB_COLS), :] = (
                    out[QB_COLS * i:QB_COLS * (i + 1)])
        return carry

    lax.fori_loop(0, ROWS // QB_ROWS, row_block, 0)


def _attention(qkv3, bias_cols):
    b = qkv3.shape[0]
    n_hp = N_HEADS // 2
    blk = lambda off: pl.BlockSpec((None, SEQ, 128), lambda hp, bi: (bi, 0, off + hp))
    return pl.pallas_call(
        _attn_kernel,
        grid=(n_hp, b),
        in_specs=[
            blk(0), blk(n_hp), blk(2 * n_hp),
            pl.BlockSpec((2,) + bias_cols.shape[1:], lambda hp, bi: (hp, 0, 0, 0, 0)),
        ],
        out_specs=pl.BlockSpec((None, SEQ, 128), lambda hp, bi: (bi, 0, hp)),
        out_shape=jax.ShapeDtypeStruct((b, SEQ, D_ATTN), BF16),
        scratch_shapes=[pltpu.VMEM((SEQ, 128), BF16), pltpu.VMEM((SEQ, 128), BF16),
                        pltpu.VMEM((9, 2 * QB_ROWS * QB_COLS, KB_ROWS * KB_COLS), F32)],
        compiler_params=pltpu.CompilerParams(
            dimension_semantics=("arbitrary", "arbitrary"), vmem_limit_bytes=V7X_VMEM_LIMIT),
        name="nattn",
    )(qkv3, qkv3, qkv3, bias_cols)


Z_PITCH = 72
N_CBLK = D_FOUR // 128


def _fourier_tables():
    n = 64
    k = np.arange(n)
    ang = 2.0 * np.pi * np.outer(k, k) / n
    c64, s64 = np.cos(ang), np.sin(ang)
    w1 = np.concatenate([c64, -s64], axis=0)
    t1p = np.arange(n)[:, None, None]
    t2p = np.arange(n)[None, :, None]
    t2 = np.arange(n)[None, None, :]
    th = 2.0 * np.pi * ((t2 * (t1p + n * t2p)) % SEQ) / SEQ
    cc, ss = np.cos(th), np.sin(th)
    m2 = np.concatenate([np.concatenate([cc, ss], axis=2), np.concatenate([-ss, cc], axis=2)], axis=1)
    cbd = np.kron(np.eye(4), c64)
    sbd = np.kron(np.eye(4), s64)
    cs = np.concatenate([cbd, sbd], axis=0)
    return w1.astype(np.float32), m2.astype(np.float32), cs.astype(np.float32)


_W1_NP, _M2_NP, _CS_NP = _fourier_tables()


HALF_CBLK = N_CBLK // 2
T2_UNROLL = 4


def _fourier_kernel(u_ref, w1_ref, m2_ref, cs_ref, wbd_ref, bf_ref, y_ref, zs_ref):
    s = pl.program_id(2)

    @pl.when(s == 0)
    def _():
        for cb in range(HALF_CBLK):
            for k in range(GRID_PITCH - GRID_W):
                y_ref[cb, pl.ds(GRID_W + k, ROWS, stride=GRID_PITCH), :] = jnp.zeros((ROWS, 128), F32)

        def dft_cols(i, carry):
            for k in range(T2_UNROLL):
                t2 = i * T2_UNROLL + k
                x = jnp.concatenate([u_ref[cb, pl.ds(t2, ROWS, stride=GRID_PITCH), :] for cb in range(HALF_CBLK)],
                                    axis=1).astype(BF16)
                z = jnp.dot(w1_ref[...], x, preferred_element_type=F32)
                for cb in range(HALF_CBLK):
                    zs_ref[cb, pl.ds(t2, 128, stride=Z_PITCH), :] = z[:, cb * 128:(cb + 1) * 128]
            return carry

        lax.fori_loop(0, GRID_W // T2_UNROLL, dft_cols, 0)

    @pl.when(s > 0)
    def _():
        xs = []
        for jo in range(8):
            t1p = (s - 1) * 8 + jo
            rhs = jnp.concatenate(
                [jnp.concatenate([zs_ref[cb, pl.ds(pl.multiple_of((part * 64 + t1p) * Z_PITCH, 8), 64), :]
                                  for cb in range(HALF_CBLK)], axis=1) for part in range(2)], axis=0)
            xs.append(jnp.dot(m2_ref[t1p], rhs.astype(BF16), preferred_element_type=F32))
        xr = jnp.concatenate([x[:64] for x in xs], axis=0).astype(BF16)
        xi = jnp.concatenate([x[64:] for x in xs], axis=0).astype(BF16)
        lhs = jnp.concatenate([xr, xi], axis=1)
        f = jnp.dot(lhs, cs_ref[...], preferred_element_type=F32) * (1.0 / 512.0)
        y = jnp.dot(f.astype(BF16), wbd_ref[...], preferred_element_type=F32) + bf_ref[...]
        for jo in range(8):
            t1p = (s - 1) * 8 + jo
            for cb in range(HALF_CBLK):
                y_ref[cb, pl.ds(t1p, ROWS, stride=GRID_PITCH), :] = y[jo * 64:(jo + 1) * 64, cb * 128:(cb + 1) * 128]


def _fourier(u_p, wbd, bf):
    b = u_p.shape[1]
    w1 = jnp.asarray(_W1_NP).astype(BF16)
    m2 = jnp.asarray(_M2_NP).astype(BF16)
    cs = jnp.asarray(_CS_NP).astype(BF16)
    half_blk = pl.BlockSpec((HALF_CBLK, None, ROWS * GRID_PITCH, 128), lambda bi, hf, s: (hf, bi, 0, 0))
    return pl.pallas_call(
        _fourier_kernel,
        grid=(b, 2, 9),
        in_specs=[
            half_blk,
            pl.BlockSpec((128, 64), lambda bi, hf, s: (0, 0)),
            pl.BlockSpec((64, 128, 128), lambda bi, hf, s: (0, 0, 0)),
            pl.BlockSpec((512, 256), lambda bi, hf, s: (0, 0)),
            pl.BlockSpec((None, 256, 256), lambda bi, hf, s: (hf, 0, 0)),
            pl.BlockSpec((1, 256), lambda bi, hf, s: (0, hf)),
        ],
        out_specs=half_blk,
        out_shape=jax.ShapeDtypeStruct(u_p.shape, F32),
        scratch_shapes=[pltpu.VMEM((HALF_CBLK, 128 * Z_PITCH, 128), F32)],
        compiler_params=pltpu.CompilerParams(
            dimension_semantics=("parallel", "parallel", "arbitrary"), vmem_limit_bytes=V7X_VMEM_LIMIT),
        name="fourier",
    )(u_p, w1, m2, cs, wbd, bf)


def _mixout_kernel(oa_ref, yf_ref, x_ref, ga_ref, gf_ref, wout_ref, gm_ref, wrt_ref, br_ref, tri_ref,
                   h_ref, bucket_ref, rank_ref, cnt_ref, carry_ref):
    i = pl.program_id(0)

    @pl.when(i == 0)
    def _():
        carry_ref[...] = jnp.zeros_like(carry_ref)

    carry = carry_ref[...]
    for k in range(OUT_SUBTILES):
        carry = _mixout_subtile(k, carry, oa_ref, yf_ref, x_ref, ga_ref, gf_ref, wout_ref, gm_ref, wrt_ref,
                                br_ref, tri_ref, h_ref, bucket_ref, rank_ref)
    carry_ref[...] = carry
    cnt_ref[...] = carry


def _mixout_subtile(k, carry, oa_ref, yf_ref, x_ref, ga_ref, gf_ref, wout_ref, gm_ref, wrt_ref, br_ref, tri_ref,
                    h_ref, bucket_ref, rank_ref):
    rows_k = pl.ds(k * TM_OUT, TM_OUT)
    na = _rms(oa_ref[rows_k, :].astype(F32), ga_ref[...]).astype(BF16)
    grid_rows = [k * (TM_OUT // GRID_W) + r for r in range(TM_OUT // GRID_W)]
    yf = jnp.concatenate(
        [jnp.concatenate([yf_ref[cb, gr * GRID_PITCH:gr * GRID_PITCH + GRID_W, :] for gr in grid_rows], axis=0)
         for cb in range(N_CBLK)], axis=1)
    nf = _rms(yf, gf_ref[...]).astype(BF16)
    merged = jnp.concatenate([na, nf], axis=1)
    h = x_ref[rows_k, :] + jnp.dot(merged, wout_ref[...], preferred_element_type=F32)
    for cb in range(ROW_TILE):
        h_ref[pl.ds(k * TM_OUT * ROW_TILE + cb, TM_OUT, stride=ROW_TILE), :] = h[:, cb * 128:(cb + 1) * 128]
    hn = _rms(h, gm_ref[...]).astype(BF16)
    lt = lax.dot_general(wrt_ref[...], hn, (((1,), (1,)), ((), ())), preferred_element_type=F32)
    lt = lt + br_ref[...]
    c = [lt[k:k + 1] for k in range(N_GROUPS)]
    cmax = jnp.maximum(jnp.maximum(c[0], c[1]), jnp.maximum(c[2], c[3]))
    e = [jnp.exp(ck - cmax) for ck in c]
    esum = (e[0] + e[1]) + (e[2] + e[3])
    p = [ek / esum for ek in e]
    pmax = jnp.maximum(jnp.maximum(p[0], p[1]), jnp.maximum(p[2], p[3]))
    g = jnp.where(p[0] == pmax, 0, jnp.where(p[1] == pmax, 1, jnp.where(p[2] == pmax, 2, 3))).astype(I32)
    fine = jnp.where(g == 0, lt[8:16], jnp.where(g == 1, lt[16:24], jnp.where(g == 2, lt[24:32], lt[32:40])))
    rows = lax.broadcasted_iota(I32, fine.shape, 0)
    v1 = jnp.max(fine, axis=0, keepdims=True)
    i1 = jnp.min(jnp.where(fine == v1, rows, EPG), axis=0, keepdims=True)
    rest = jnp.where(rows == i1, -jnp.inf, fine)
    v2 = jnp.max(rest, axis=0, keepdims=True)
    i2 = jnp.min(jnp.where(rest == v2, rows, EPG), axis=0, keepdims=True)
    lo = jnp.minimum(i1, i2)
    hi = jnp.maximum(i1, i2)
    pair = lax.shift_right_logical(lo * (2 * EPG - 1 - lo), 1) + (hi - lo - 1)
    bucket = g * N_PAIRS + pair
    bucket_ref[k] = bucket
    brow = lax.broadcasted_iota(I32, (BUCKET_LANES, TM_OUT), 0)
    onehot = (brow == bucket).astype(F32)
    prefix = jnp.dot(onehot.astype(BF16), tri_ref[...], preferred_element_type=F32)
    rank = jnp.sum(onehot * (prefix + carry), axis=0, keepdims=True)
    rank_ref[k] = rank.astype(I32)
    return carry + jnp.sum(onehot, axis=1, keepdims=True)


def _mixout(oa, yf, x2, ga, gf, wout_bf, gm, wrt, br, tri):
    n = x2.shape[0]
    nt = n // TM_OUT
    rows_step = TM_OUT * OUT_SUBTILES
    full = lambda *shape: pl.BlockSpec(shape, lambda i: (0,) * len(shape))
    row3 = pl.BlockSpec((OUT_SUBTILES, 1, TM_OUT), lambda i: (i, 0, 0))
    return pl.pallas_call(
        _mixout_kernel,
        grid=(n // rows_step,),
        in_specs=[
            pl.BlockSpec((rows_step, D_ATTN), lambda i: (i, 0)),
            pl.BlockSpec((N_CBLK, rows_step // GRID_W * GRID_PITCH, 128), lambda i: (0, i, 0)),
            pl.BlockSpec((rows_step, D_MODEL), lambda i: (i, 0)),
            full(1, D_ATTN), full(1, D_FOUR), full(D_MODEL, D_MODEL), full(1, D_MODEL),
            full(BUCKET_LANES, D_MODEL), full(BUCKET_LANES, 1), full(TM_OUT, TM_OUT),
        ],
        out_specs=[
            pl.BlockSpec((rows_step * ROW_TILE, 128), lambda i: (i, 0)),
            row3, row3,
            full(BUCKET_LANES, 1),
        ],
        out_shape=[
            jax.ShapeDtypeStruct((n * ROW_TILE, 128), F32),
            jax.ShapeDtypeStruct((nt, 1, TM_OUT), I32),
            jax.ShapeDtypeStruct((nt, 1, TM_OUT), I32),
            jax.ShapeDtypeStruct((BUCKET_LANES, 1), F32),
        ],
        scratch_shapes=[pltpu.VMEM((BUCKET_LANES, 1), F32)],
        compiler_params=pltpu.CompilerParams(
            dimension_semantics=("arbitrary",), vmem_limit_bytes=V7X_VMEM_LIMIT),
        name="mixout",
    )(oa, yf, x2, ga, gf, wout_bf, gm, wrt, br, tri)


def _pair_tables():
    lo, hi = [], []
    for a in range(EPG):
        for b in range(a + 1, EPG):
            lo.append(a)
            hi.append(b)
    return np.asarray(lo, np.int32), np.asarray(hi, np.int32)


_PAIR_LO, _PAIR_HI = _pair_tables()


def _dispatch_kernel(tnv_ref, dest_ref, h_ref, hs_hbm, zbuf, zsem, sem):
    k = pl.program_id(0)
    tile_rows = TM_MOE * ROW_TILE
    n_tiles = hs_hbm.shape[0] // tile_rows

    def zero_copy(t):
        return pltpu.make_async_copy(zbuf, hs_hbm.at[pl.ds(pl.multiple_of(t * tile_rows, tile_rows), tile_rows)],
                                     zsem)

    @pl.when(k == 0)
    def _():
        zbuf[...] = jnp.zeros_like(zbuf)

        def zstart(t, c):
            @pl.when(tnv_ref[t] < TM_MOE)
            def _():
                zero_copy(t).start()
            return c

        def zwait(t, c):
            @pl.when(tnv_ref[t] < TM_MOE)
            def _():
                zero_copy(t).wait()
            return c

        lax.fori_loop(0, n_tiles, zstart, 0)
        lax.fori_loop(0, n_tiles, zwait, 0)

    def rows(r8, c):
        for u in range(8):
            r = r8 * 8 + u
            dst = pl.multiple_of(dest_ref[0, 0, r] * ROW_TILE, ROW_TILE)
            pltpu.make_async_copy(h_ref.at[pl.ds(pl.multiple_of(r * ROW_TILE, ROW_TILE), ROW_TILE)],
                                  hs_hbm.at[pl.ds(dst, ROW_TILE)], sem).start()
        return c

    lax.fori_loop(0, ROWS_PER_STEP // 8, rows, 0)
    pltpu.make_async_copy(h_ref, hs_hbm.at[pl.ds(0, ROWS_PER_STEP * ROW_TILE)], sem).wait()


def _dispatch(h_rt, dest3, tile_nv, n_slots):
    n = h_rt.shape[0] // ROW_TILE
    grid_spec = pltpu.PrefetchScalarGridSpec(
        num_scalar_prefetch=1,
        grid=(n // ROWS_PER_STEP,),
        in_specs=[
            pl.BlockSpec((1, 1, ROWS_PER_STEP), lambda k, *_: (k, 0, 0), memory_space=pltpu.SMEM),
            pl.BlockSpec((ROWS_PER_STEP * ROW_TILE, 128), lambda k, *_: (k, 0)),
        ],
        out_specs=pl.BlockSpec(memory_space=pl.ANY),
        scratch_shapes=[pltpu.VMEM((TM_MOE * ROW_TILE, 128), F32),
                        pltpu.SemaphoreType.DMA(()), pltpu.SemaphoreType.DMA(())],
    )
    return pl.pallas_call(
        _dispatch_kernel,
        grid_spec=grid_spec,
        out_shape=jax.ShapeDtypeStruct((n_slots * ROW_TILE, 128), F32),
        compiler_params=pltpu.CompilerParams(
            dimension_semantics=("arbitrary",), vmem_limit_bytes=V7X_VMEM_LIMIT),
        name="dispatch",
    )(tile_nv, dest3, h_rt)


def _combine_kernel(dest_ref, ys_hbm, o_ref, buf, sem):
    def rows(r8, c):
        for u in range(8):
            r = r8 * 8 + u
            src = pl.multiple_of(dest_ref[0, 0, r] * ROW_TILE, ROW_TILE)
            pltpu.make_async_copy(ys_hbm.at[pl.ds(src, ROW_TILE)],
                                  buf.at[pl.ds(pl.multiple_of(r * ROW_TILE, ROW_TILE), ROW_TILE)], sem).start()
        return c

    lax.fori_loop(0, ROWS_PER_STEP // 8, rows, 0)
    pltpu.make_async_copy(ys_hbm.at[pl.ds(0, ROWS_PER_STEP * ROW_TILE)], buf, sem).wait()
    for cb in range(ROW_TILE):
        o_ref[:, cb * 128:(cb + 1) * 128] = buf[pl.ds(cb, ROWS_PER_STEP, stride=ROW_TILE), :]


def _combine(ys_rt, dest3, n):
    return pl.pallas_call(
        _combine_kernel,
        grid=(n // ROWS_PER_STEP,),
        in_specs=[
            pl.BlockSpec((1, 1, ROWS_PER_STEP), lambda k: (k, 0, 0), memory_space=pltpu.SMEM),
            pl.BlockSpec(memory_space=pl.ANY),
        ],
        out_specs=pl.BlockSpec((ROWS_PER_STEP, D_MODEL), lambda k: (k, 0)),
        out_shape=jax.ShapeDtypeStruct((n, D_MODEL), F32),
        scratch_shapes=[pltpu.VMEM((ROWS_PER_STEP * ROW_TILE, 128), F32), pltpu.SemaphoreType.DMA(())],
        compiler_params=pltpu.CompilerParams(
            dimension_semantics=("arbitrary",), vmem_limit_bytes=V7X_VMEM_LIMIT),
        name="combine",
    )(dest3, ys_rt)


def _moe_kernel(tg_ref, tlo_ref, thi_ref, nused_ref, hs_ref, wg_ref, wu_ref, wd_ref, wr_ref, br_ref,
                gm_ref, gfin_ref, ys_ref):
    step = pl.program_id(0)

    @pl.when(step * TILES_PER_STEP < nused_ref[0])
    def _():
        lane = lax.broadcasted_iota(I32, (TM_MOE, BUCKET_LANES), 1)
        for k in range(TILES_PER_STEP):
            t = step * TILES_PER_STEP + k
            base = k * TM_MOE * ROW_TILE
            hrows = jnp.concatenate(
                [hs_ref[pl.ds(base + cb, TM_MOE, stride=ROW_TILE), :] for cb in range(ROW_TILE)], axis=1)
            hn = _rms(hrows, gm_ref[...]).astype(BF16)
            logits = jnp.dot(hn, wr_ref[...], preferred_element_type=F32) + br_ref[...]
            g, lo, hi = tg_ref[t], tlo_ref[t], thi_ref[t]
            coarse = jnp.where(lane < N_GROUPS, logits, -jnp.inf)
            ec = jnp.exp(coarse - jnp.max(coarse, axis=-1, keepdims=True))
            pick = lambda col, v: jnp.sum(jnp.where(lane == col, v, 0.0), axis=-1, keepdims=True)
            g_w = pick(g, ec) / jnp.sum(ec, axis=-1, keepdims=True)
            f_lo = pick(8 + g * EPG + lo, logits)
            f_hi = pick(8 + g * EPG + hi, logits)
            f_max = jnp.maximum(f_lo, f_hi)
            e_lo = jnp.exp(f_lo - f_max)
            e_hi = jnp.exp(f_hi - f_max)
            den = e_lo + e_hi

            def expert(e, w):
                gate = jnp.dot(hn, wg_ref[e], preferred_element_type=F32)
                up = jnp.dot(hn, wu_ref[e], preferred_element_type=F32)
                act = (gate * jax.nn.sigmoid(gate) * up).astype(BF16)
                return w * jnp.dot(act, wd_ref[e], preferred_element_type=F32)

            y = expert(lo, (e_lo / den) * g_w) + expert(hi, (e_hi / den) * g_w)
            res = _rms(hrows + y, gfin_ref[...])
            for cb in range(ROW_TILE):
                ys_ref[pl.ds(base + cb, TM_MOE, stride=ROW_TILE), :] = res[:, cb * 128:(cb + 1) * 128]

    @pl.when(step * TILES_PER_STEP >= nused_ref[0])
    def _():
        ys_ref[...] = jnp.zeros_like(ys_ref)


def _moe(hs_rt, tile_g, tile_lo, tile_hi, n_used, wg, wu, wd, wr, br, gm, gfin):
    rows_step = TM_MOE * TILES_PER_STEP * ROW_TILE
    n_steps = hs_rt.shape[0] // rows_step
    by_group = lambda s, tg, *_: (tg[s * TILES_PER_STEP], 0, 0, 0)
    full2 = lambda a, c: pl.BlockSpec((a, c), lambda s, *_: (0, 0))

    def hs_index(s, tg, tlo, thi, nu):
        last_step = jnp.maximum(nu[0] - 1, 0) // TILES_PER_STEP
        return (jnp.minimum(s, last_step), 0)

    grid_spec = pltpu.PrefetchScalarGridSpec(
        num_scalar_prefetch=4,
        grid=(n_steps,),
        in_specs=[
            pl.BlockSpec((rows_step, 128), hs_index),
            pl.BlockSpec((None, EPG, D_MODEL, D_EXPERT), by_group),
            pl.BlockSpec((None, EPG, D_MODEL, D_EXPERT), by_group),
            pl.BlockSpec((None, EPG, D_EXPERT, D_MODEL), by_group),
            full2(D_MODEL, BUCKET_LANES), full2(1, BUCKET_LANES), full2(1, D_MODEL), full2(1, D_MODEL),
        ],
        out_specs=pl.BlockSpec((rows_step, 128), lambda s, *_: (s, 0)),
    )
    return pl.pallas_call(
        _moe_kernel,
        grid_spec=grid_spec,
        out_shape=jax.ShapeDtypeStruct(hs_rt.shape, F32),
        compiler_params=pltpu.CompilerParams(
            dimension_semantics=("arbitrary",), vmem_limit_bytes=V7X_VMEM_LIMIT),
        name="moe",
    )(tile_g, tile_lo, tile_hi, n_used, hs_rt, wg, wu, wd, wr, br, gm, gfin)


def _bucket_plan(bucket, rank, counts, n):
    nt = n // TM_MOE + N_BUCKETS + N_GROUPS * (TILES_PER_STEP - 1)
    nt = -(-nt // TILES_PER_STEP) * TILES_PER_STEP
    tiles_b = (counts + (TM_MOE - 1)) // TM_MOE
    tiles_g = jnp.sum(tiles_b.reshape(N_GROUPS, N_PAIRS), axis=1)
    extra_g = (-tiles_g) % TILES_PER_STEP
    is_last = (np.arange(N_PAIRS) == N_PAIRS - 1)[None, :]
    tiles_b = (tiles_b.reshape(N_GROUPS, N_PAIRS) + jnp.where(is_last, extra_g[:, None], 0)).reshape(N_BUCKETS)
    tile_end = jnp.cumsum(tiles_b)
    tile_start = tile_end - tiles_b
    n_used = tile_end[-1]
    b_ids = jnp.arange(N_BUCKETS, dtype=I32)
    dest = rank + TM_MOE * jnp.sum(jnp.where(bucket[:, None] == b_ids[None, :], tile_start[None, :], 0), axis=1)
    t_idx = jnp.arange(nt, dtype=I32)
    tb = jnp.sum((tile_end[None, :] <= t_idx[:, None]).astype(I32), axis=1)
    tb_last = jnp.sum((tile_end <= n_used - 1).astype(I32))
    tb = jnp.minimum(jnp.where(t_idx < n_used, tb, tb_last), N_BUCKETS - 1)
    sel = tb[:, None] == b_ids[None, :]
    pick = lambda table: jnp.sum(jnp.where(sel, table[None, :], 0), axis=1).astype(I32)
    tile_g = tb // N_PAIRS
    tile_lo = pick(jnp.asarray(np.tile(_PAIR_LO, N_GROUPS)))
    tile_hi = pick(jnp.asarray(np.tile(_PAIR_HI, N_GROUPS)))
    nv = jnp.clip(pick(counts) - (t_idx - pick(tile_start)) * TM_MOE, 0, TM_MOE)
    tile_nv = jnp.where(t_idx < n_used, nv, 0).astype(I32)
    return dest.astype(I32), tile_g.astype(I32), tile_lo, tile_hi, tile_nv, n_used.reshape(1).astype(I32)


def kernel(x, norm_mix, w_in, rpb, w_four, b_four, g_attn_out, g_four_out, w_out, norm_moe,
           w_router_coarse, b_router_coarse, w_router_fine, b_router_fine, w_gate, w_up, w_down, norm_final):
    b, seq, d = x.shape
    assert (seq, d) == (SEQ, D_MODEL) and norm_mix.shape[0] == 1
    n = b * seq
    x2 = x.reshape(n, d)

    qkv, u = _inproj(x2, norm_mix[0][None], w_in[0].astype(BF16))

    oa = _attention(qkv.reshape(b, seq, 3 * D_ATTN), _bias_columns(rpb[0]))

    eye4 = jnp.eye(4, dtype=F32)
    wf = w_four[0].reshape(2, 4, FOUR_GROUP_DIM, FOUR_GROUP_DIM)
    wbd = (eye4[None, :, None, :, None] * wf[:, :, :, None, :]).reshape(2, 256, 256).astype(BF16)
    yf = _fourier(u, wbd, b_four[0][None]).reshape(N_CBLK, b * ROWS * GRID_PITCH, 128)

    wrt = jnp.zeros((BUCKET_LANES, d), F32)
    wrt = wrt.at[0:N_GROUPS].set(w_router_coarse[0].T).at[8:8 + N_EXPERTS].set(w_router_fine[0].T)
    br = jnp.zeros((BUCKET_LANES, 1), F32)
    br = br.at[0:N_GROUPS, 0].set(b_router_coarse[0]).at[8:8 + N_EXPERTS, 0].set(b_router_fine[0])
    tri = (np.arange(TM_OUT)[:, None] < np.arange(TM_OUT)[None, :]).astype(np.float32)
    wrt_bf = wrt.astype(BF16)
    h_rt, bucket, rank, cnt = _mixout(
        oa.reshape(n, D_ATTN), yf, x2, g_attn_out[0][None], g_four_out[0][None],
        w_out[0].astype(BF16), norm_moe[0][None], wrt_bf, br, jnp.asarray(tri, BF16))

    counts = cnt[:N_BUCKETS, 0].astype(I32)
    dest, tile_g, tile_lo, tile_hi, tile_nv, n_used = _bucket_plan(bucket.reshape(n), rank.reshape(n), counts, n)
    dest3 = dest.reshape(n // ROWS_PER_STEP, 1, ROWS_PER_STEP)
    hs_rt = _dispatch(h_rt, dest3, tile_nv, tile_nv.shape[0] * TM_MOE)
    shape_e = (N_GROUPS, EPG)
    ys_rt = _moe(hs_rt, tile_g, tile_lo, tile_hi, n_used,
                 w_gate[0].astype(BF16).reshape(shape_e + (d, D_EXPERT)),
                 w_up[0].astype(BF16).reshape(shape_e + (d, D_EXPERT)),
                 w_down[0].astype(BF16).reshape(shape_e + (D_EXPERT, d)),
                 wrt_bf.T, br.T, norm_moe[0][None], norm_final[None])
    return _combine(ys_rt, dest3, n).reshape(b, seq, d)
```

```python
import functools

import numpy as np
import jax
import jax.numpy as jnp
from jax import lax
from jax.experimental import pallas as pl
from jax.experimental.pallas import tpu as pltpu

F32 = jnp.float32
BF16 = jnp.bfloat16
I32 = jnp.int32

D_MODEL = 1024
SEQ = 4096
GRID_W = 64
ROWS = SEQ // GRID_W
D_ATTN = 512
D_FOUR = 512
N_HEADS = 8
HEAD_DIM = 64
WIN_H = 8
WIN_W = 16
N_FOUR_GROUPS = 8
FOUR_GROUP_DIM = 64
D_PROJ = 3 * D_ATTN + D_FOUR
N_GROUPS = 4
EPG = 8
N_EXPERTS = N_GROUPS * EPG
D_EXPERT = 256
EPS = 1e-6
NEG = -1e30

V7X_VMEM_LIMIT = 56 * 1024 * 1024

TM_IN = 1024
TM_OUT = 512
OUT_SUBTILES = 2
TM_MOE = 128
ROW_TILE = D_MODEL // 128
ROWS_PER_STEP = 2048
TILES_PER_STEP = 4
N_PAIRS = EPG * (EPG - 1) // 2
N_BUCKETS = N_GROUPS * N_PAIRS
BUCKET_LANES = 128

QB_ROWS = 8
QB_COLS = 16
KB_ROWS = 16
KB_COLS = 32
GRID_PITCH = 72


def _rms(x, g):
    ms = jnp.mean(x * x, axis=-1, keepdims=True)
    return x * lax.rsqrt(ms + EPS) * g


def _inproj_kernel(x_ref, g_ref, w_ref, qkv_ref, u_ref):
    xn = _rms(x_ref[...], g_ref[...]).astype(BF16)
    p = jnp.dot(xn, w_ref[...], preferred_element_type=F32)
    qkv_ref[:, :D_ATTN] = (p[:, :D_ATTN] * (HEAD_DIM ** -0.5)).astype(BF16)
    qkv_ref[:, D_ATTN:] = p[:, D_ATTN:3 * D_ATTN].astype(BF16)
    for cb in range(D_FOUR // 128):
        lanes = slice(3 * D_ATTN + cb * 128, 3 * D_ATTN + (cb + 1) * 128)
        for r in range(TM_IN // GRID_W):
            u_ref[cb, r * GRID_PITCH:r * GRID_PITCH + GRID_W, :] = p[r * GRID_W:(r + 1) * GRID_W, lanes]
            u_ref[cb, r * GRID_PITCH + GRID_W:(r + 1) * GRID_PITCH, :] = jnp.zeros((GRID_PITCH - GRID_W, 128), F32)


def _inproj(x2, g, w_bf):
    n = x2.shape[0]
    rows_step = TM_IN // GRID_W
    steps_b = ROWS // rows_step
    return pl.pallas_call(
        _inproj_kernel,
        grid=(n // TM_IN,),
        in_specs=[
            pl.BlockSpec((TM_IN, D_MODEL), lambda i: (i, 0)),
            pl.BlockSpec((1, D_MODEL), lambda i: (0, 0)),
            pl.BlockSpec((D_MODEL, D_PROJ), lambda i: (0, 0)),
        ],
        out_specs=[
            pl.BlockSpec((TM_IN, 3 * D_ATTN), lambda i: (i, 0)),
            pl.BlockSpec((D_FOUR // 128, None, rows_step * GRID_PITCH, 128),
                         lambda i: (0, i // steps_b, i % steps_b, 0)),
        ],
        out_shape=[
            jax.ShapeDtypeStruct((n, 3 * D_ATTN), BF16),
            jax.ShapeDtypeStruct((D_FOUR // 128, n // SEQ, ROWS * GRID_PITCH, 128), F32),
        ],
        compiler_params=pltpu.CompilerParams(
            dimension_semantics=("parallel",), vmem_limit_bytes=V7X_VMEM_LIMIT),
        name="inproj",
    )(x2, g, w_bf)


_KCOL_START = (0, 8, 24, 32)
_KCOL_SHIFTED = (False, True, True, False)
_KCOL_OFF = (0, 0, 16, 32)
_COL_TYPE = (0, 1, 1, 2)


def _bias_index_tables():
    dr = np.zeros((9, 128, 512), np.int32)
    dc = np.zeros((9, 128, 512), np.int32)
    ok = np.zeros((9, 128, 512), bool)
    qi, qc = np.divmod(np.arange(128), QB_COLS)
    ki, kc = np.divmod(np.arange(512), KB_COLS)
    for rt, (q0, k0) in enumerate(((0, 0), (8, 4), (56, 48))):
        qrow = q0 + qi
        krow = k0 + ki
        rs = np.clip(qrow - WIN_H // 2, 0, ROWS - WIN_H)
        rok = (krow[None, :] >= rs[:, None]) & (krow[None, :] < rs[:, None] + WIN_H)
        drr = krow[None, :] - qrow[:, None] + (WIN_H - 1)
        for ct, (c0, kc0) in enumerate(((0, 0), (16, 8), (48, 32))):
            qcol = c0 + qc
            kcol = kc0 + kc
            cs = np.clip(qcol - WIN_W // 2, 0, GRID_W - WIN_W)
            cok = (kcol[None, :] >= cs[:, None]) & (kcol[None, :] < cs[:, None] + WIN_W)
            dcc = kcol[None, :] - qcol[:, None] + (WIN_W - 1)
            t = rt * 3 + ct
            ok[t] = rok & cok
            dr[t] = np.where(ok[t], drr, 0)
            dc[t] = np.where(ok[t], dcc, 0)
    return dr, dc, ok


_BIAS_DR, _BIAS_DC, _BIAS_OK = _bias_index_tables()


def _bias_selectors():
    ok = _BIAS_OK.reshape(3, 3, QB_ROWS, QB_COLS, KB_ROWS, KB_COLS)
    dr = _BIAS_DR.reshape(ok.shape)
    dc = _BIAS_DC.reshape(ok.shape)
    row_ok = ok.any(axis=(1, 3, 5))
    col_ok = ok.any(axis=(0, 2, 4))
    dr_r = dr.max(axis=(1, 3, 5))
    dc_c = dc.max(axis=(0, 2, 4))
    sc = (np.arange(2 * WIN_W - 1)[:, None, None, None] == dc_c[None]) & col_ok[None]
    return row_ok, dr_r, col_ok, sc.astype(np.float32)


_BIAS_ROW_OK, _BIAS_ROW_DR, _BIAS_COL_OK, _BIAS_SC = _bias_selectors()


def _bias_columns(rpb):
    sc = jnp.asarray(np.tile(_BIAS_SC, 128 // KB_COLS))
    ok = np.tile(_BIAS_COL_OK, 128 // KB_COLS)
    t1 = jnp.sum(rpb[:, :, :, None, None, None] * sc[None, None], axis=2)
    return jnp.where(ok[None, None], t1, NEG)


def _attn_kernel(q_ref, k_ref, v_ref, bcol_ref, o_ref, ksh_ref, vsh_ref, bias_ref, s_ref, p_ref, l_ref):
    @pl.when(pl.program_id(1) == 0)
    def _():
        key_row = lax.broadcasted_iota(I32, (QB_COLS, KB_ROWS * KB_COLS), 1) // KB_COLS
        for hh in range(2):
            for rt in range(3):
                for ct in range(3):
                    for i in range(QB_ROWS):
                        acc = jnp.full((QB_COLS, KB_ROWS * KB_COLS), NEG, F32)
                        for y in range(KB_ROWS):
                            if _BIAS_ROW_OK[rt, i, y]:
                                cols = bcol_ref[hh, int(_BIAS_ROW_DR[rt, i, y]), ct]
                                cols = jnp.concatenate([cols] * (KB_ROWS * KB_COLS // 128), axis=1)
                                acc = jnp.where(key_row == y, cols, acc)
                        bias_ref[rt * 3 + ct, pl.ds(hh * 128 + i * QB_COLS, QB_COLS), :] = acc

    zpad = jnp.zeros((8, 128), F32)
    ksh_ref[...] = jnp.concatenate([k_ref[...].astype(F32)[8:], zpad], axis=0).astype(BF16)
    vsh_ref[...] = jnp.concatenate([v_ref[...].astype(F32)[8:], zpad], axis=0).astype(BF16)
    lane = lax.broadcasted_iota(I32, (1, 128), 1)
    head_masks = (lane < HEAD_DIM, lane >= HEAD_DIM)

    n_rb = ROWS // QB_ROWS
    n_q = GRID_W // QB_COLS

    def window(ref_plain, ref_shift, rb, j):
        ks = jnp.clip(QB_ROWS * rb - WIN_H // 2, 0, ROWS - KB_ROWS)
        src = ref_shift if _KCOL_SHIFTED[j] else ref_plain
        return jnp.concatenate(
            [src[pl.ds(pl.multiple_of((ks + i) * GRID_W + _KCOL_OFF[j], 16), KB_COLS), :] for i in range(KB_ROWS)],
            axis=0)

    def scores(rb, j):
        q = jnp.concatenate(
            [q_ref[pl.ds(pl.multiple_of((QB_ROWS * rb + i) * GRID_W + QB_COLS * j, 16), QB_COLS), :]
             for i in range(QB_ROWS)], axis=0)
        qm = jnp.concatenate([jnp.where(hm, q, jnp.zeros_like(q)) for hm in head_masks], axis=0)
        s_ref[j] = lax.dot_general(qm, window(k_ref, ksh_ref, rb, j), (((1,), (1,)), ((), ())),
                                   preferred_element_type=F32)

    def softmax(rb, j):
        rt = jnp.where(rb == 0, 0, jnp.where(rb == n_rb - 1, 2, 1))
        s = s_ref[j] + bias_ref[rt * 3 + _COL_TYPE[j]]
        e = jnp.exp(s - jnp.max(s, axis=-1, keepdims=True))
        p_ref[j] = e.astype(BF16)
        l_ref[j] = jnp.broadcast_to(jnp.sum(e, axis=-1, keepdims=True), (2 * QB_ROWS * QB_COLS, 128))

    def values(rb, j):
        o = jnp.dot(p_ref[j], window(v_ref, vsh_ref, rb, j), preferred_element_type=F32) / l_ref[j]
        out = jnp.where(head_masks[0], o[:128], o[128:]).astype(BF16)
        for i in range(QB_ROWS):
            o_ref[pl.ds(pl.multiple_of((QB_ROWS * rb + i) * GRID_W + QB_COLS * j, 16), QB_COLS), :] = (
                out[QB_COLS * i:QB_COLS * (i + 1)])

    def stage(fn, rb):
        for j in range(n_q):
            fn(jnp.asarray(rb, I32), j)

    stage(scores, 0)
    stage(softmax, 0)
    stage(scores, 1)

    def pipeline_step(i, carry):
        stage(values, i - 2)
        stage(softmax, i - 1)
        stage(scores, i)
        return carry

    lax.fori_loop(2, n_rb, pipeline_step, 0)
    stage(values, n_rb - 2)
    stage(softmax, n_rb - 1)
    stage(values, n_rb - 1)


def _attention(qkv3, bias_cols):
    b = qkv3.shape[0]
    n_hp = N_HEADS // 2
    blk = lambda off: pl.BlockSpec((None, SEQ, 128), lambda hp, bi: (bi, 0, off + hp))
    return pl.pallas_call(
        _attn_kernel,
        grid=(n_hp, b),
        in_specs=[
            blk(0), blk(n_hp), blk(2 * n_hp),
            pl.BlockSpec((2,) + bias_cols.shape[1:], lambda hp, bi: (hp, 0, 0, 0, 0)),
        ],
        out_specs=pl.BlockSpec((None, SEQ, 128), lambda hp, bi: (bi, 0, hp)),
        out_shape=jax.ShapeDtypeStruct((b, SEQ, D_ATTN), BF16),
        scratch_shapes=[pltpu.VMEM((SEQ, 128), BF16), pltpu.VMEM((SEQ, 128), BF16),
                        pltpu.VMEM((9, 2 * QB_ROWS * QB_COLS, KB_ROWS * KB_COLS), F32),
                        pltpu.VMEM((GRID_W // QB_COLS, 2 * QB_ROWS * QB_COLS, KB_ROWS * KB_COLS), F32),
                        pltpu.VMEM((GRID_W // QB_COLS, 2 * QB_ROWS * QB_COLS, KB_ROWS * KB_COLS), BF16),
                        pltpu.VMEM((GRID_W // QB_COLS, 2 * QB_ROWS * QB_COLS, 128), F32)],
        compiler_params=pltpu.CompilerParams(
            dimension_semantics=("arbitrary", "arbitrary"), vmem_limit_bytes=V7X_VMEM_LIMIT),
        name="nattn",
    )(qkv3, qkv3, qkv3, bias_cols)


Z_PITCH = 72
N_CBLK = D_FOUR // 128


def _fourier_tables():
    n = 64
    k = np.arange(n)
    ang = 2.0 * np.pi * np.outer(k, k) / n
    c64, s64 = np.cos(ang), np.sin(ang)
    w1 = np.concatenate([c64, -s64], axis=0)
    t1p = np.arange(n)[:, None, None]
    t2p = np.arange(n)[None, :, None]
    t2 = np.arange(n)[None, None, :]
    th = 2.0 * np.pi * ((t2 * (t1p + n * t2p)) % SEQ) / SEQ
    cc, ss = np.cos(th), np.sin(th)
    m2 = np.concatenate([np.concatenate([cc, ss], axis=2), np.concatenate([-ss, cc], axis=2)], axis=1)
    cbd = np.kron(np.eye(4), c64)
    sbd = np.kron(np.eye(4), s64)
    cs = np.concatenate([cbd, sbd], axis=0)
    return w1.astype(np.float32), m2.astype(np.float32), cs.astype(np.float32)


_W1_NP, _M2_NP, _CS_NP = _fourier_tables()


HALF_CBLK = N_CBLK // 2
T2_UNROLL = 4


def _fourier_kernel(u_ref, w1_ref, m2_ref, cs_ref, wbd_ref, bf_ref, y_ref, zs_ref):
    for cb in range(HALF_CBLK):
        for k in range(GRID_PITCH - GRID_W):
            y_ref[cb, pl.ds(GRID_W + k, ROWS, stride=GRID_PITCH), :] = jnp.zeros((ROWS, 128), F32)

    def dft_cols(i, carry):
        for k in range(T2_UNROLL):
            t2 = i * T2_UNROLL + k
            x = jnp.concatenate([u_ref[cb, pl.ds(t2, ROWS, stride=GRID_PITCH), :] for cb in range(HALF_CBLK)],
                                axis=1).astype(BF16)
            z = jnp.dot(w1_ref[...], x, preferred_element_type=F32)
            for cb in range(HALF_CBLK):
                zs_ref[cb, pl.ds(t2, 128, stride=Z_PITCH), :] = z[:, cb * 128:(cb + 1) * 128]
        return carry

    lax.fori_loop(0, GRID_W // T2_UNROLL, dft_cols, 0)

    def dft_rows(a, carry):
        xs = []
        for jo in range(8):
            t1p = a * 8 + jo
            rhs = jnp.concatenate(
                [jnp.concatenate([zs_ref[cb, pl.ds(pl.multiple_of((part * 64 + t1p) * Z_PITCH, 8), 64), :]
                                  for cb in range(HALF_CBLK)], axis=1) for part in range(2)], axis=0)
            xs.append(jnp.dot(m2_ref[t1p], rhs.astype(BF16), preferred_element_type=F32))
        xr = jnp.concatenate([x[:64] for x in xs], axis=0).astype(BF16)
        xi = jnp.concatenate([x[64:] for x in xs], axis=0).astype(BF16)
        lhs = jnp.concatenate([xr, xi], axis=1)
        f = jnp.dot(lhs, cs_ref[...], preferred_element_type=F32) * (1.0 / 512.0)
        y = jnp.dot(f.astype(BF16), wbd_ref[...], preferred_element_type=F32) + bf_ref[...]
        for jo in range(8):
            t1p = a * 8 + jo
            for cb in range(HALF_CBLK):
                y_ref[cb, pl.ds(t1p, ROWS, stride=GRID_PITCH), :] = y[jo * 64:(jo + 1) * 64, cb * 128:(cb + 1) * 128]
        return carry

    lax.fori_loop(0, GRID_W // 8, dft_rows, 0)


def _fourier(u_p, wbd, bf):
    b = u_p.shape[1]
    w1 = jnp.asarray(_W1_NP).astype(BF16)
    m2 = jnp.asarray(_M2_NP).astype(BF16)
    cs = jnp.asarray(_CS_NP).astype(BF16)
    half_blk = pl.BlockSpec((HALF_CBLK, None, ROWS * GRID_PITCH, 128), lambda bi, hf: (hf, bi, 0, 0))
    return pl.pallas_call(
        _fourier_kernel,
        grid=(b, 2),
        in_specs=[
            half_blk,
            pl.BlockSpec((128, 64), lambda bi, hf: (0, 0)),
            pl.BlockSpec((64, 128, 128), lambda bi, hf: (0, 0, 0)),
            pl.BlockSpec((512, 256), lambda bi, hf: (0, 0)),
            pl.BlockSpec((None, 256, 256), lambda bi, hf: (hf, 0, 0)),
            pl.BlockSpec((1, 256), lambda bi, hf: (0, hf)),
        ],
        out_specs=half_blk,
        out_shape=jax.ShapeDtypeStruct(u_p.shape, F32),
        scratch_shapes=[pltpu.VMEM((HALF_CBLK, 128 * Z_PITCH, 128), F32)],
        compiler_params=pltpu.CompilerParams(
            dimension_semantics=("parallel", "parallel"), vmem_limit_bytes=V7X_VMEM_LIMIT),
        name="fourier",
    )(u_p, w1, m2, cs, wbd, bf)


def _mixout_kernel(oa_ref, yf_ref, x_ref, ga_ref, gf_ref, wout_ref, gm_ref, wrt_ref, br_ref, tri_ref,
                   h_ref, bucket_ref, rank_ref, cnt_ref, carry_ref):
    i = pl.program_id(0)

    @pl.when(i == 0)
    def _():
        carry_ref[...] = jnp.zeros_like(carry_ref)

    carry = carry_ref[...]
    for k in range(OUT_SUBTILES):
        carry = _mixout_subtile(k, carry, oa_ref, yf_ref, x_ref, ga_ref, gf_ref, wout_ref, gm_ref, wrt_ref,
                                br_ref, tri_ref, h_ref, bucket_ref, rank_ref)
    carry_ref[...] = carry
    cnt_ref[...] = carry


def _mixout_subtile(k, carry, oa_ref, yf_ref, x_ref, ga_ref, gf_ref, wout_ref, gm_ref, wrt_ref, br_ref, tri_ref,
                    h_ref, bucket_ref, rank_ref):
    rows_k = pl.ds(k * TM_OUT, TM_OUT)
    na = _rms(oa_ref[rows_k, :].astype(F32), ga_ref[...]).astype(BF16)
    grid_rows = [k * (TM_OUT // GRID_W) + r for r in range(TM_OUT // GRID_W)]
    yf = jnp.concatenate(
        [jnp.concatenate([yf_ref[cb, gr * GRID_PITCH:gr * GRID_PITCH + GRID_W, :] for gr in grid_rows], axis=0)
         for cb in range(N_CBLK)], axis=1)
    nf = _rms(yf, gf_ref[...]).astype(BF16)
    merged = jnp.concatenate([na, nf], axis=1)
    h = x_ref[rows_k, :] + jnp.dot(merged, wout_ref[...], preferred_element_type=F32)
    for cb in range(ROW_TILE):
        h_ref[pl.ds(k * TM_OUT * ROW_TILE + cb, TM_OUT, stride=ROW_TILE), :] = h[:, cb * 128:(cb + 1) * 128]
    hn = _rms(h, gm_ref[...]).astype(BF16)
    lt = lax.dot_general(wrt_ref[...], hn, (((1,), (1,)), ((), ())), preferred_element_type=F32)
    lt = lt + br_ref[...]
    c = [lt[k:k + 1] for k in range(N_GROUPS)]
    cmax = jnp.maximum(jnp.maximum(c[0], c[1]), jnp.maximum(c[2], c[3]))
    e = [jnp.exp(ck - cmax) for ck in c]
    esum = (e[0] + e[1]) + (e[2] + e[3])
    p = [ek / esum for ek in e]
    pmax = jnp.maximum(jnp.maximum(p[0], p[1]), jnp.maximum(p[2], p[3]))
    g = jnp.where(p[0] == pmax, 0, jnp.where(p[1] == pmax, 1, jnp.where(p[2] == pmax, 2, 3))).astype(I32)
    fine = jnp.where(g == 0, lt[8:16], jnp.where(g == 1, lt[16:24], jnp.where(g == 2, lt[24:32], lt[32:40])))
    rows = lax.broadcasted_iota(I32, fine.shape, 0)
    v1 = jnp.max(fine, axis=0, keepdims=True)
    i1 = jnp.min(jnp.where(fine == v1, rows, EPG), axis=0, keepdims=True)
    rest = jnp.where(rows == i1, -jnp.inf, fine)
    v2 = jnp.max(rest, axis=0, keepdims=True)
    i2 = jnp.min(jnp.where(rest == v2, rows, EPG), axis=0, keepdims=True)
    lo = jnp.minimum(i1, i2)
    hi = jnp.maximum(i1, i2)
    pair = lax.shift_right_logical(lo * (2 * EPG - 1 - lo), 1) + (hi - lo - 1)
    bucket = g * N_PAIRS + pair
    bucket_ref[k] = bucket
    brow = lax.broadcasted_iota(I32, (BUCKET_LANES, TM_OUT), 0)
    onehot = (brow == bucket).astype(F32)
    prefix = jnp.dot(onehot.astype(BF16), tri_ref[...], preferred_element_type=F32)
    rank = jnp.sum(onehot * (prefix + carry), axis=0, keepdims=True)
    rank_ref[k] = rank.astype(I32)
    return carry + jnp.sum(onehot, axis=1, keepdims=True)


def _mixout(oa, yf, x2, ga, gf, wout_bf, gm, wrt, br, tri):
    n = x2.shape[0]
    nt = n // TM_OUT
    rows_step = TM_OUT * OUT_SUBTILES
    full = lambda *shape: pl.BlockSpec(shape, lambda i: (0,) * len(shape))
    row3 = pl.BlockSpec((OUT_SUBTILES, 1, TM_OUT), lambda i: (i, 0, 0))
    return pl.pallas_call(
        _mixout_kernel,
        grid=(n // rows_step,),
        in_specs=[
            pl.BlockSpec((rows_step, D_ATTN), lambda i: (i, 0)),
            pl.BlockSpec((N_CBLK, rows_step // GRID_W * GRID_PITCH, 128), lambda i: (0, i, 0)),
            pl.BlockSpec((rows_step, D_MODEL), lambda i: (i, 0)),
            full(1, D_ATTN), full(1, D_FOUR), full(D_MODEL, D_MODEL), full(1, D_MODEL),
            full(BUCKET_LANES, D_MODEL), full(BUCKET_LANES, 1), full(TM_OUT, TM_OUT),
        ],
        out_specs=[
            pl.BlockSpec((rows_step * ROW_TILE, 128), lambda i: (i, 0)),
            row3, row3,
            full(BUCKET_LANES, 1),
        ],
        out_shape=[
            jax.ShapeDtypeStruct((n * ROW_TILE, 128), F32),
            jax.ShapeDtypeStruct((nt, 1, TM_OUT), I32),
            jax.ShapeDtypeStruct((nt, 1, TM_OUT), I32),
            jax.ShapeDtypeStruct((BUCKET_LANES, 1), F32),
        ],
        scratch_shapes=[pltpu.VMEM((BUCKET_LANES, 1), F32)],
        compiler_params=pltpu.CompilerParams(
            dimension_semantics=("arbitrary",), vmem_limit_bytes=V7X_VMEM_LIMIT),
        name="mixout",
    )(oa, yf, x2, ga, gf, wout_bf, gm, wrt, br, tri)


def _pair_tables():
    lo, hi = [], []
    for a in range(EPG):
        for b in range(a + 1, EPG):
            lo.append(a)
            hi.append(b)
    return np.asarray(lo, np.int32), np.asarray(hi, np.int32)


_PAIR_LO, _PAIR_HI = _pair_tables()


def _dispatch_kernel(tnv_ref, dest_ref, h_ref, hs_hbm, zbuf, zsem, sem):
    k = pl.program_id(0)
    tile_rows = TM_MOE * ROW_TILE
    n_tiles = hs_hbm.shape[0] // tile_rows

    def zero_copy(t):
        return pltpu.make_async_copy(zbuf, hs_hbm.at[pl.ds(pl.multiple_of(t * tile_rows, tile_rows), tile_rows)],
                                     zsem)

    @pl.when(k == 0)
    def _():
        zbuf[...] = jnp.zeros_like(zbuf)

        def zstart(t, c):
            @pl.when(tnv_ref[t] < TM_MOE)
            def _():
                zero_copy(t).start()
            return c

        def zwait(t, c):
            @pl.when(tnv_ref[t] < TM_MOE)
            def _():
                zero_copy(t).wait()
            return c

        lax.fori_loop(0, n_tiles, zstart, 0)
        lax.fori_loop(0, n_tiles, zwait, 0)

    def rows(r8, c):
        for u in range(8):
            r = r8 * 8 + u
            dst = pl.multiple_of(dest_ref[0, 0, r] * ROW_TILE, ROW_TILE)
            pltpu.make_async_copy(h_ref.at[pl.ds(pl.multiple_of(r * ROW_TILE, ROW_TILE), ROW_TILE)],
                                  hs_hbm.at[pl.ds(dst, ROW_TILE)], sem).start()
        return c

    lax.fori_loop(0, ROWS_PER_STEP // 8, rows, 0)
    pltpu.make_async_copy(h_ref, hs_hbm.at[pl.ds(0, ROWS_PER_STEP * ROW_TILE)], sem).wait()


def _dispatch(h_rt, dest3, tile_nv, n_slots):
    n = h_rt.shape[0] // ROW_TILE
    grid_spec = pltpu.PrefetchScalarGridSpec(
        num_scalar_prefetch=1,
        grid=(n // ROWS_PER_STEP,),
        in_specs=[
            pl.BlockSpec((1, 1, ROWS_PER_STEP), lambda k, *_: (k, 0, 0), memory_space=pltpu.SMEM),
            pl.BlockSpec((ROWS_PER_STEP * ROW_TILE, 128), lambda k, *_: (k, 0)),
        ],
        out_specs=pl.BlockSpec(memory_space=pl.ANY),
        scratch_shapes=[pltpu.VMEM((TM_MOE * ROW_TILE, 128), F32),
                        pltpu.SemaphoreType.DMA(()), pltpu.SemaphoreType.DMA(())],
    )
    return pl.pallas_call(
        _dispatch_kernel,
        grid_spec=grid_spec,
        out_shape=jax.ShapeDtypeStruct((n_slots * ROW_TILE, 128), F32),
        compiler_params=pltpu.CompilerParams(
            dimension_semantics=("arbitrary",), vmem_limit_bytes=V7X_VMEM_LIMIT),
        name="dispatch",
    )(tile_nv, dest3, h_rt)


def _combine_kernel(dest_ref, ys_hbm, o_ref, buf, sem):
    def rows(r8, c):
        for u in range(8):
            r = r8 * 8 + u
            src = pl.multiple_of(dest_ref[0, 0, r] * ROW_TILE, ROW_TILE)
            pltpu.make_async_copy(ys_hbm.at[pl.ds(src, ROW_TILE)],
                                  buf.at[pl.ds(pl.multiple_of(r * ROW_TILE, ROW_TILE), ROW_TILE)], sem).start()
        return c

    lax.fori_loop(0, ROWS_PER_STEP // 8, rows, 0)
    pltpu.make_async_copy(ys_hbm.at[pl.ds(0, ROWS_PER_STEP * ROW_TILE)], buf, sem).wait()
    for cb in range(ROW_TILE):
        o_ref[:, cb * 128:(cb + 1) * 128] = buf[pl.ds(cb, ROWS_PER_STEP, stride=ROW_TILE), :]


def _combine(ys_rt, dest3, n):
    return pl.pallas_call(
        _combine_kernel,
        grid=(n // ROWS_PER_STEP,),
        in_specs=[
            pl.BlockSpec((1, 1, ROWS_PER_STEP), lambda k: (k, 0, 0), memory_space=pltpu.SMEM),
            pl.BlockSpec(memory_space=pl.ANY),
        ],
        out_specs=pl.BlockSpec((ROWS_PER_STEP, D_MODEL), lambda k: (k, 0)),
        out_shape=jax.ShapeDtypeStruct((n, D_MODEL), F32),
        scratch_shapes=[pltpu.VMEM((ROWS_PER_STEP * ROW_TILE, 128), F32), pltpu.SemaphoreType.DMA(())],
        compiler_params=pltpu.CompilerParams(
            dimension_semantics=("arbitrary",), vmem_limit_bytes=V7X_VMEM_LIMIT),
        name="combine",
    )(dest3, ys_rt)


def _moe_kernel(tg_ref, tlo_ref, thi_ref, nused_ref, hs_ref, wg_ref, wu_ref, wd_ref, wr_ref, br_ref,
                gm_ref, gfin_ref, ys_ref):
    step = pl.program_id(0)

    @pl.when(step * TILES_PER_STEP < nused_ref[0])
    def _():
        lane = lax.broadcasted_iota(I32, (TM_MOE, BUCKET_LANES), 1)
        for k in range(TILES_PER_STEP):
            t = step * TILES_PER_STEP + k
            base = k * TM_MOE * ROW_TILE
            hrows = jnp.concatenate(
                [hs_ref[pl.ds(base + cb, TM_MOE, stride=ROW_TILE), :] for cb in range(ROW_TILE)], axis=1)
            hn = _rms(hrows, gm_ref[...]).astype(BF16)
            logits = jnp.dot(hn, wr_ref[...], preferred_element_type=F32) + br_ref[...]
            g, lo, hi = tg_ref[t], tlo_ref[t], thi_ref[t]
            coarse = jnp.where(lane < N_GROUPS, logits, -jnp.inf)
            ec = jnp.exp(coarse - jnp.max(coarse, axis=-1, keepdims=True))
            pick = lambda col, v: jnp.sum(jnp.where(lane == col, v, 0.0), axis=-1, keepdims=True)
            g_w = pick(g, ec) / jnp.sum(ec, axis=-1, keepdims=True)
            f_lo = pick(8 + g * EPG + lo, logits)
            f_hi = pick(8 + g * EPG + hi, logits)
            f_max = jnp.maximum(f_lo, f_hi)
            e_lo = jnp.exp(f_lo - f_max)
            e_hi = jnp.exp(f_hi - f_max)
            den = e_lo + e_hi

            def expert(e, w):
                gate = jnp.dot(hn, wg_ref[e], preferred_element_type=F32)
                up = jnp.dot(hn, wu_ref[e], preferred_element_type=F32)
                act = (gate * jax.nn.sigmoid(gate) * up).astype(BF16)
                return w * jnp.dot(act, wd_ref[e], preferred_element_type=F32)

            y = expert(lo, (e_lo / den) * g_w) + expert(hi, (e_hi / den) * g_w)
            res = _rms(hrows + y, gfin_ref[...])
            for cb in range(ROW_TILE):
                ys_ref[pl.ds(base + cb, TM_MOE, stride=ROW_TILE), :] = res[:, cb * 128:(cb + 1) * 128]

    @pl.when(step * TILES_PER_STEP >= nused_ref[0])
    def _():
        ys_ref[...] = jnp.zeros_like(ys_ref)


def _moe(hs_rt, tile_g, tile_lo, tile_hi, n_used, wg, wu, wd, wr, br, gm, gfin):
    rows_step = TM_MOE * TILES_PER_STEP * ROW_TILE
    n_steps = hs_rt.shape[0] // rows_step
    by_group = lambda s, tg, *_: (tg[s * TILES_PER_STEP], 0, 0, 0)
    full2 = lambda a, c: pl.BlockSpec((a, c), lambda s, *_: (0, 0))

    def hs_index(s, tg, tlo, thi, nu):
        last_step = jnp.maximum(nu[0] - 1, 0) // TILES_PER_STEP
        return (jnp.minimum(s, last_step), 0)

    grid_spec = pltpu.PrefetchScalarGridSpec(
        num_scalar_prefetch=4,
        grid=(n_steps,),
        in_specs=[
            pl.BlockSpec((rows_step, 128), hs_index),
            pl.BlockSpec((None, EPG, D_MODEL, D_EXPERT), by_group),
            pl.BlockSpec((None, EPG, D_MODEL, D_EXPERT), by_group),
            pl.BlockSpec((None, EPG, D_EXPERT, D_MODEL), by_group),
            full2(D_MODEL, BUCKET_LANES), full2(1, BUCKET_LANES), full2(1, D_MODEL), full2(1, D_MODEL),
        ],
        out_specs=pl.BlockSpec((rows_step, 128), lambda s, *_: (s, 0)),
    )
    return pl.pallas_call(
        _moe_kernel,
        grid_spec=grid_spec,
        out_shape=jax.ShapeDtypeStruct(hs_rt.shape, F32),
        compiler_params=pltpu.CompilerParams(
            dimension_semantics=("arbitrary",), vmem_limit_bytes=V7X_VMEM_LIMIT),
        name="moe",
    )(tile_g, tile_lo, tile_hi, n_used, hs_rt, wg, wu, wd, wr, br, gm, gfin)


def _bucket_plan(bucket, rank, counts, n):
    nt = n // TM_MOE + N_BUCKETS + N_GROUPS * (TILES_PER_STEP - 1)
    nt = -(-nt // TILES_PER_STEP) * TILES_PER_STEP
    tiles_b = (counts + (TM_MOE - 1)) // TM_MOE
    tiles_g = jnp.sum(tiles_b.reshape(N_GROUPS, N_PAIRS), axis=1)
    extra_g = (-tiles_g) % TILES_PER_STEP
    is_last = (np.arange(N_PAIRS) == N_PAIRS - 1)[None, :]
    tiles_b = (tiles_b.reshape(N_GROUPS, N_PAIRS) + jnp.where(is_last, extra_g[:, None], 0)).reshape(N_BUCKETS)
    tile_end = jnp.cumsum(tiles_b)
    tile_start = tile_end - tiles_b
    n_used = tile_end[-1]
    b_ids = jnp.arange(N_BUCKETS, dtype=I32)
    dest = rank + TM_MOE * jnp.sum(jnp.where(bucket[:, None] == b_ids[None, :], tile_start[None, :], 0), axis=1)
    t_idx = jnp.arange(nt, dtype=I32)
    tb = jnp.sum((tile_end[None, :] <= t_idx[:, None]).astype(I32), axis=1)
    tb_last = jnp.sum((tile_end <= n_used - 1).astype(I32))
    tb = jnp.minimum(jnp.where(t_idx < n_used, tb, tb_last), N_BUCKETS - 1)
    sel = tb[:, None] == b_ids[None, :]
    pick = lambda table: jnp.sum(jnp.where(sel, table[None, :], 0), axis=1).astype(I32)
    tile_g = tb // N_PAIRS
    tile_lo = pick(jnp.asarray(np.tile(_PAIR_LO, N_GROUPS)))
    tile_hi = pick(jnp.asarray(np.tile(_PAIR_HI, N_GROUPS)))
    nv = jnp.clip(pick(counts) - (t_idx - pick(tile_start)) * TM_MOE, 0, TM_MOE)
    tile_nv = jnp.where(t_idx < n_used, nv, 0).astype(I32)
    return dest.astype(I32), tile_g.astype(I32), tile_lo, tile_hi, tile_nv, n_used.reshape(1).astype(I32)


def kernel(x, norm_mix, w_in, rpb, w_four, b_four, g_attn_out, g_four_out, w_out, norm_moe,
           w_router_coarse, b_router_coarse, w_router_fine, b_router_fine, w_gate, w_up, w_down, norm_final):
    b, seq, d = x.shape
    assert (seq, d) == (SEQ, D_MODEL) and norm_mix.shape[0] == 1
    n = b * seq
    x2 = x.reshape(n, d)

    qkv, u = _inproj(x2, norm_mix[0][None], w_in[0].astype(BF16))

    oa = _attention(qkv.reshape(b, seq, 3 * D_ATTN), _bias_columns(rpb[0]))

    eye4 = jnp.eye(4, dtype=F32)
    wf = w_four[0].reshape(2, 4, FOUR_GROUP_DIM, FOUR_GROUP_DIM)
    wbd = (eye4[None, :, None, :, None] * wf[:, :, :, None, :]).reshape(2, 256, 256).astype(BF16)
    yf = _fourier(u, wbd, b_four[0][None]).reshape(N_CBLK, b * ROWS * GRID_PITCH, 128)

    wrt = jnp.zeros((BUCKET_LANES, d), F32)
    wrt = wrt.at[0:N_GROUPS].set(w_router_coarse[0].T).at[8:8 + N_EXPERTS].set(w_router_fine[0].T)
    br = jnp.zeros((BUCKET_LANES, 1), F32)
    br = br.at[0:N_GROUPS, 0].set(b_router_coarse[0]).at[8:8 + N_EXPERTS, 0].set(b_router_fine[0])
    tri = (np.arange(TM_OUT)[:, None] < np.arange(TM_OUT)[None, :]).astype(np.float32)
    wrt_bf = wrt.astype(BF16)
    h_rt, bucket, rank, cnt = _mixout(
        oa.reshape(n, D_ATTN), yf, x2, g_attn_out[0][None], g_four_out[0][None],
        w_out[0].astype(BF16), norm_moe[0][None], wrt_bf, br, jnp.asarray(tri, BF16))

    counts = cnt[:N_BUCKETS, 0].astype(I32)
    dest, tile_g, tile_lo, tile_hi, tile_nv, n_used = _bucket_plan(bucket.reshape(n), rank.reshape(n), counts, n)
    dest3 = dest.reshape(n // ROWS_PER_STEP, 1, ROWS_PER_STEP)
    hs_rt = _dispatch(h_rt, dest3, tile_nv, tile_nv.shape[0] * TM_MOE)
    shape_e = (N_GROUPS, EPG)
    ys_rt = _moe(hs_rt, tile_g, tile_lo, tile_hi, n_used,
                 w_gate[0].astype(BF16).reshape(shape_e + (d, D_EXPERT)),
                 w_up[0].astype(BF16).reshape(shape_e + (d, D_EXPERT)),
                 w_down[0].astype(BF16).reshape(shape_e + (D_EXPERT, d)),
                 wrt_bf.T, br.T, norm_moe[0][None], norm_final[None])
    return _combine(ys_rt, dest3, n).reshape(b, seq, d)
```

```python
import functools

import numpy as np
import jax
import jax.numpy as jnp
from jax import lax
from jax.experimental import pallas as pl
from jax.experimental.pallas import tpu as pltpu

F32 = jnp.float32
BF16 = jnp.bfloat16
I32 = jnp.int32

D_MODEL = 1024
SEQ = 4096
GRID_W = 64
ROWS = SEQ // GRID_W
D_ATTN = 512
D_FOUR = 512
N_HEADS = 8
HEAD_DIM = 64
WIN_H = 8
WIN_W = 16
N_FOUR_GROUPS = 8
FOUR_GROUP_DIM = 64
D_PROJ = 3 * D_ATTN + D_FOUR
N_GROUPS = 4
EPG = 8
N_EXPERTS = N_GROUPS * EPG
D_EXPERT = 256
EPS = 1e-6
NEG = -1e30

V7X_VMEM_LIMIT = 56 * 1024 * 1024

TM_IN = 1024
TM_OUT = 512
OUT_SUBTILES = 2
TM_MOE = 128
ROW_TILE = D_MODEL // 128
ROWS_PER_STEP = 2048
TILES_PER_STEP = 4
N_PAIRS = EPG * (EPG - 1) // 2
N_BUCKETS = N_GROUPS * N_PAIRS
BUCKET_LANES = 128

QB_ROWS = 8
QB_COLS = 16
KB_ROWS = 16
KB_COLS = 32
GRID_PITCH = 72


def _rms(x, g):
    ms = jnp.mean(x * x, axis=-1, keepdims=True)
    return x * lax.rsqrt(ms + EPS) * g


def _inproj_kernel(x_ref, g_ref, w_ref, qkv_ref, u_ref):
    xn = _rms(x_ref[...], g_ref[...]).astype(BF16)
    p = jnp.dot(xn, w_ref[...], preferred_element_type=F32)
    qkv_ref[:, :D_ATTN] = (p[:, :D_ATTN] * (HEAD_DIM ** -0.5)).astype(BF16)
    qkv_ref[:, D_ATTN:] = p[:, D_ATTN:3 * D_ATTN].astype(BF16)
    for cb in range(D_FOUR // 128):
        lanes = slice(3 * D_ATTN + cb * 128, 3 * D_ATTN + (cb + 1) * 128)
        for r in range(TM_IN // GRID_W):
            u_ref[cb, r * GRID_PITCH:r * GRID_PITCH + GRID_W, :] = p[r * GRID_W:(r + 1) * GRID_W, lanes]
            u_ref[cb, r * GRID_PITCH + GRID_W:(r + 1) * GRID_PITCH, :] = jnp.zeros((GRID_PITCH - GRID_W, 128), F32)


def _inproj(x2, g, w_bf):
    n = x2.shape[0]
    rows_step = TM_IN // GRID_W
    steps_b = ROWS // rows_step
    return pl.pallas_call(
        _inproj_kernel,
        grid=(n // TM_IN,),
        in_specs=[
            pl.BlockSpec((TM_IN, D_MODEL), lambda i: (i, 0)),
            pl.BlockSpec((1, D_MODEL), lambda i: (0, 0)),
            pl.BlockSpec((D_MODEL, D_PROJ), lambda i: (0, 0)),
        ],
        out_specs=[
            pl.BlockSpec((TM_IN, 3 * D_ATTN), lambda i: (i, 0)),
            pl.BlockSpec((D_FOUR // 128, None, rows_step * GRID_PITCH, 128),
                         lambda i: (0, i // steps_b, i % steps_b, 0)),
        ],
        out_shape=[
            jax.ShapeDtypeStruct((n, 3 * D_ATTN), BF16),
            jax.ShapeDtypeStruct((D_FOUR // 128, n // SEQ, ROWS * GRID_PITCH, 128), F32),
        ],
        compiler_params=pltpu.CompilerParams(
            dimension_semantics=("parallel",), vmem_limit_bytes=V7X_VMEM_LIMIT),
        name="inproj",
    )(x2, g, w_bf)


_KCOL_START = (0, 8, 24, 32)
_KCOL_SHIFTED = (False, True, True, False)
_KCOL_OFF = (0, 0, 16, 32)
_COL_TYPE = (0, 1, 1, 2)


def _bias_index_tables():
    dr = np.zeros((9, 128, 512), np.int32)
    dc = np.zeros((9, 128, 512), np.int32)
    ok = np.zeros((9, 128, 512), bool)
    qi, qc = np.divmod(np.arange(128), QB_COLS)
    ki, kc = np.divmod(np.arange(512), KB_COLS)
    for rt, (q0, k0) in enumerate(((0, 0), (8, 4), (56, 48))):
        qrow = q0 + qi
        krow = k0 + ki
        rs = np.clip(qrow - WIN_H // 2, 0, ROWS - WIN_H)
        rok = (krow[None, :] >= rs[:, None]) & (krow[None, :] < rs[:, None] + WIN_H)
        drr = krow[None, :] - qrow[:, None] + (WIN_H - 1)
        for ct, (c0, kc0) in enumerate(((0, 0), (16, 8), (48, 32))):
            qcol = c0 + qc
            kcol = kc0 + kc
            cs = np.clip(qcol - WIN_W // 2, 0, GRID_W - WIN_W)
            cok = (kcol[None, :] >= cs[:, None]) & (kcol[None, :] < cs[:, None] + WIN_W)
            dcc = kcol[None, :] - qcol[:, None] + (WIN_W - 1)
            t = rt * 3 + ct
            ok[t] = rok & cok
            dr[t] = np.where(ok[t], drr, 0)
            dc[t] = np.where(ok[t], dcc, 0)
    return dr, dc, ok


_BIAS_DR, _BIAS_DC, _BIAS_OK = _bias_index_tables()


def _bias_selectors():
    ok = _BIAS_OK.reshape(3, 3, QB_ROWS, QB_COLS, KB_ROWS, KB_COLS)
    dr = _BIAS_DR.reshape(ok.shape)
    dc = _BIAS_DC.reshape(ok.shape)
    row_ok = ok.any(axis=(1, 3, 5))
    col_ok = ok.any(axis=(0, 2, 4))
    dr_r = dr.max(axis=(1, 3, 5))
    dc_c = dc.max(axis=(0, 2, 4))
    sc = (np.arange(2 * WIN_W - 1)[:, None, None, None] == dc_c[None]) & col_ok[None]
    return row_ok, dr_r, col_ok, sc.astype(np.float32)


_BIAS_ROW_OK, _BIAS_ROW_DR, _BIAS_COL_OK, _BIAS_SC = _bias_selectors()


def _bias_columns(rpb):
    sc = jnp.asarray(np.tile(_BIAS_SC, 128 // KB_COLS))
    ok = np.tile(_BIAS_COL_OK, 128 // KB_COLS)
    t1 = jnp.sum(rpb[:, :, :, None, None, None] * sc[None, None], axis=2)
    return jnp.where(ok[None, None], t1, NEG)


def _attn_kernel(q_ref, k_ref, v_ref, bcol_ref, o_ref, ksh_ref, vsh_ref, bias_ref, s_ref, p_ref, l_ref):
    @pl.when(pl.program_id(1) == 0)
    def _():
        key_row = lax.broadcasted_iota(I32, (QB_COLS, KB_ROWS * KB_COLS), 1) // KB_COLS
        for hh in range(2):
            for rt in range(3):
                for ct in range(3):
                    for i in range(QB_ROWS):
                        acc = jnp.full((QB_COLS, KB_ROWS * KB_COLS), NEG, F32)
                        for y in range(KB_ROWS):
                            if _BIAS_ROW_OK[rt, i, y]:
                                cols = bcol_ref[hh, int(_BIAS_ROW_DR[rt, i, y]), ct]
                                cols = jnp.concatenate([cols] * (KB_ROWS * KB_COLS // 128), axis=1)
                                acc = jnp.where(key_row == y, cols, acc)
                        bias_ref[rt * 3 + ct, pl.ds(hh * 128 + i * QB_COLS, QB_COLS), :] = acc

    zpad = jnp.zeros((8, 128), F32)
    ksh_ref[...] = jnp.concatenate([k_ref[...].astype(F32)[8:], zpad], axis=0).astype(BF16)
    vsh_ref[...] = jnp.concatenate([v_ref[...].astype(F32)[8:], zpad], axis=0).astype(BF16)
    lane = lax.broadcasted_iota(I32, (1, 128), 1)
    head_masks = (lane < HEAD_DIM, lane >= HEAD_DIM)

    n_rb = ROWS // QB_ROWS
    n_q = GRID_W // QB_COLS

    def window(ref_plain, ref_shift, rb, j):
        ks = jnp.clip(QB_ROWS * rb - WIN_H // 2, 0, ROWS - KB_ROWS)
        src = ref_shift if _KCOL_SHIFTED[j] else ref_plain
        return jnp.concatenate(
            [src[pl.ds(pl.multiple_of((ks + i) * GRID_W + _KCOL_OFF[j], 16), KB_COLS), :] for i in range(KB_ROWS)],
            axis=0)

    def scores(rb, j):
        q = jnp.concatenate(
            [q_ref[pl.ds(pl.multiple_of((QB_ROWS * rb + i) * GRID_W + QB_COLS * j, 16), QB_COLS), :]
             for i in range(QB_ROWS)], axis=0)
        qm = jnp.concatenate([jnp.where(hm, q, jnp.zeros_like(q)) for hm in head_masks], axis=0)
        s_ref[j] = lax.dot_general(qm, window(k_ref, ksh_ref, rb, j), (((1,), (1,)), ((), ())),
                                   preferred_element_type=F32)

    def softmax(rb, j):
        rt = jnp.where(rb == 0, 0, jnp.where(rb == n_rb - 1, 2, 1))
        s = s_ref[j] + bias_ref[rt * 3 + _COL_TYPE[j]]
        e = jnp.exp(s - jnp.max(s, axis=-1, keepdims=True))
        p_ref[j] = e.astype(BF16)
        l_ref[j] = jnp.broadcast_to(jnp.sum(e, axis=-1, keepdims=True), (2 * QB_ROWS * QB_COLS, 128))

    def values(rb, j):
        o = jnp.dot(p_ref[j], window(v_ref, vsh_ref, rb, j), preferred_element_type=F32) / l_ref[j]
        out = jnp.where(head_masks[0], o[:128], o[128:]).astype(BF16)
        for i in range(QB_ROWS):
            o_ref[pl.ds(pl.multiple_of((QB_ROWS * rb + i) * GRID_W + QB_COLS * j, 16), QB_COLS), :] = (
                out[QB_COLS * i:QB_COLS * (i + 1)])

    def stage(fn, rb):
        for j in range(n_q):
            fn(jnp.asarray(rb, I32), j)

    stage(scores, 0)
    stage(softmax, 0)
    stage(scores, 1)

    def pipeline_step(i, carry):
        stage(values, i - 2)
        stage(softmax, i - 1)
        stage(scores, i)
        return carry

    lax.fori_loop(2, n_rb, pipeline_step, 0)
    stage(values, n_rb - 2)
    stage(softmax, n_rb - 1)
    stage(values, n_rb - 1)


def _attention(qkv3, bias_cols):
    b = qkv3.shape[0]
    n_hp = N_HEADS // 2
    blk = lambda off: pl.BlockSpec((None, SEQ, 128), lambda hp, bi: (bi, 0, off + hp))
    return pl.pallas_call(
        _attn_kernel,
        grid=(n_hp, b),
        in_specs=[
            blk(0), blk(n_hp), blk(2 * n_hp),
            pl.BlockSpec((2,) + bias_cols.shape[1:], lambda hp, bi: (hp, 0, 0, 0, 0)),
        ],
        out_specs=pl.BlockSpec((None, SEQ, 128), lambda hp, bi: (bi, 0, hp)),
        out_shape=jax.ShapeDtypeStruct((b, SEQ, D_ATTN), BF16),
        scratch_shapes=[pltpu.VMEM((SEQ, 128), BF16), pltpu.VMEM((SEQ, 128), BF16),
                        pltpu.VMEM((9, 2 * QB_ROWS * QB_COLS, KB_ROWS * KB_COLS), F32),
                        pltpu.VMEM((GRID_W // QB_COLS, 2 * QB_ROWS * QB_COLS, KB_ROWS * KB_COLS), F32),
                        pltpu.VMEM((GRID_W // QB_COLS, 2 * QB_ROWS * QB_COLS, KB_ROWS * KB_COLS), BF16),
                        pltpu.VMEM((GRID_W // QB_COLS, 2 * QB_ROWS * QB_COLS, 128), F32)],
        compiler_params=pltpu.CompilerParams(
            dimension_semantics=("arbitrary", "arbitrary"), vmem_limit_bytes=V7X_VMEM_LIMIT),
        name="nattn",
    )(qkv3, qkv3, qkv3, bias_cols)


Z_PITCH = 72
N_CBLK = D_FOUR // 128


def _fourier_tables():
    n = 64
    k = np.arange(n)
    ang = 2.0 * np.pi * np.outer(k, k) / n
    c64, s64 = np.cos(ang), np.sin(ang)
    w1 = np.concatenate([c64, -s64], axis=0)
    t1p = np.arange(n)[:, None, None]
    t2p = np.arange(n)[None, :, None]
    t2 = np.arange(n)[None, None, :]
    th = 2.0 * np.pi * ((t2 * (t1p + n * t2p)) % SEQ) / SEQ
    cc, ss = np.cos(th), np.sin(th)
    m2 = np.concatenate([np.concatenate([cc, ss], axis=2), np.concatenate([-ss, cc], axis=2)], axis=1)
    cbd = np.kron(np.eye(4), c64)
    sbd = np.kron(np.eye(4), s64)
    cs = np.concatenate([cbd, sbd], axis=0)
    return w1.astype(np.float32), m2.astype(np.float32), cs.astype(np.float32)


_W1_NP, _M2_NP, _CS_NP = _fourier_tables()


HALF_CBLK = N_CBLK // 2
T2_UNROLL = 4


def _fourier_kernel(u_ref, w1_ref, m2_ref, cs_ref, wbd_ref, bf_ref, y_ref, zs_ref):
    for cb in range(HALF_CBLK):
        for k in range(GRID_PITCH - GRID_W):
            y_ref[cb, pl.ds(GRID_W + k, ROWS, stride=GRID_PITCH), :] = jnp.zeros((ROWS, 128), F32)

    def dft_cols(i, carry):
        for k in range(T2_UNROLL):
            t2 = i * T2_UNROLL + k
            x = jnp.concatenate([u_ref[cb, pl.ds(t2, ROWS, stride=GRID_PITCH), :] for cb in range(HALF_CBLK)],
                                axis=1).astype(BF16)
            z = jnp.dot(w1_ref[...], x, preferred_element_type=F32)
            for cb in range(HALF_CBLK):
                zs_ref[cb, pl.ds(t2, 128, stride=Z_PITCH), :] = z[:, cb * 128:(cb + 1) * 128]
        return carry

    lax.fori_loop(0, GRID_W // T2_UNROLL, dft_cols, 0)

    def dft_rows(a, carry):
        xs = []
        for jo in range(8):
            t1p = a * 8 + jo
            rhs = jnp.concatenate(
                [jnp.concatenate([zs_ref[cb, pl.ds(pl.multiple_of((part * 64 + t1p) * Z_PITCH, 8), 64), :]
                                  for cb in range(HALF_CBLK)], axis=1) for part in range(2)], axis=0)
            xs.append(jnp.dot(m2_ref[t1p], rhs.astype(BF16), preferred_element_type=F32))
        xr = jnp.concatenate([x[:64] for x in xs], axis=0).astype(BF16)
        xi = jnp.concatenate([x[64:] for x in xs], axis=0).astype(BF16)
        lhs = jnp.concatenate([xr, xi], axis=1)
        f = jnp.dot(lhs, cs_ref[...], preferred_element_type=F32) * (1.0 / 512.0)
        y = jnp.dot(f.astype(BF16), wbd_ref[...], preferred_element_type=F32) + bf_ref[...]
        for jo in range(8):
            t1p = a * 8 + jo
            for cb in range(HALF_CBLK):
                y_ref[cb, pl.ds(t1p, ROWS, stride=GRID_PITCH), :] = y[jo * 64:(jo + 1) * 64, cb * 128:(cb + 1) * 128]
        return carry

    lax.fori_loop(0, GRID_W // 8, dft_rows, 0)


def _fourier(u_p, wbd, bf):
    b = u_p.shape[1]
    w1 = jnp.asarray(_W1_NP).astype(BF16)
    m2 = jnp.asarray(_M2_NP).astype(BF16)
    cs = jnp.asarray(_CS_NP).astype(BF16)
    half_blk = pl.BlockSpec((HALF_CBLK, None, ROWS * GRID_PITCH, 128), lambda bi, hf: (hf, bi, 0, 0))
    return pl.pallas_call(
        _fourier_kernel,
        grid=(b, 2),
        in_specs=[
            half_blk,
            pl.BlockSpec((128, 64), lambda bi, hf: (0, 0)),
            pl.BlockSpec((64, 128, 128), lambda bi, hf: (0, 0, 0)),
            pl.BlockSpec((512, 256), lambda bi, hf: (0, 0)),
            pl.BlockSpec((None, 256, 256), lambda bi, hf: (hf, 0, 0)),
            pl.BlockSpec((1, 256), lambda bi, hf: (0, hf)),
        ],
        out_specs=half_blk,
        out_shape=jax.ShapeDtypeStruct(u_p.shape, F32),
        scratch_shapes=[pltpu.VMEM((HALF_CBLK, 128 * Z_PITCH, 128), F32)],
        compiler_params=pltpu.CompilerParams(
            dimension_semantics=("parallel", "parallel"), vmem_limit_bytes=V7X_VMEM_LIMIT),
        name="fourier",
    )(u_p, w1, m2, cs, wbd, bf)


def _mixout_kernel(oa_ref, yf_ref, x_ref, ga_ref, gf_ref, wout_ref, gm_ref, wrt_ref, br_ref, tri_ref,
                   h_ref, bucket_ref, rank_ref, cnt_ref, carry_ref):
    i = pl.program_id(0)

    @pl.when(i == 0)
    def _():
        carry_ref[...] = jnp.zeros_like(carry_ref)

    carry = carry_ref[...]
    for k in range(OUT_SUBTILES):
        carry = _mixout_subtile(k, carry, oa_ref, yf_ref, x_ref, ga_ref, gf_ref, wout_ref, gm_ref, wrt_ref,
                                br_ref, tri_ref, h_ref, bucket_ref, rank_ref)
    carry_ref[...] = carry
    cnt_ref[...] = carry


def _mixout_subtile(k, carry, oa_ref, yf_ref, x_ref, ga_ref, gf_ref, wout_ref, gm_ref, wrt_ref, br_ref, tri_ref,
                    h_ref, bucket_ref, rank_ref):
    rows_k = pl.ds(k * TM_OUT, TM_OUT)
    na = _rms(oa_ref[rows_k, :].astype(F32), ga_ref[...]).astype(BF16)
    grid_rows = [k * (TM_OUT // GRID_W) + r for r in range(TM_OUT // GRID_W)]
    yf = jnp.concatenate(
        [jnp.concatenate([yf_ref[cb, gr * GRID_PITCH:gr * GRID_PITCH + GRID_W, :] for gr in grid_rows], axis=0)
         for cb in range(N_CBLK)], axis=1)
    nf = _rms(yf, gf_ref[...]).astype(BF16)
    merged = jnp.concatenate([na, nf], axis=1)
    h = x_ref[rows_k, :] + jnp.dot(merged, wout_ref[...], preferred_element_type=F32)
    for cb in range(ROW_TILE):
        h_ref[pl.ds(k * TM_OUT * ROW_TILE + cb, TM_OUT, stride=ROW_TILE), :] = h[:, cb * 128:(cb + 1) * 128]
    hn = _rms(h, gm_ref[...]).astype(BF16)
    lt = lax.dot_general(wrt_ref[...], hn, (((1,), (1,)), ((), ())), preferred_element_type=F32)
    lt = lt + br_ref[...]
    c = [lt[k:k + 1] for k in range(N_GROUPS)]
    cmax = jnp.maximum(jnp.maximum(c[0], c[1]), jnp.maximum(c[2], c[3]))
    e = [jnp.exp(ck - cmax) for ck in c]
    esum = (e[0] + e[1]) + (e[2] + e[3])
    p = [ek / esum for ek in e]
    pmax = jnp.maximum(jnp.maximum(p[0], p[1]), jnp.maximum(p[2], p[3]))
    g = jnp.where(p[0] == pmax, 0, jnp.where(p[1] == pmax, 1, jnp.where(p[2] == pmax, 2, 3))).astype(I32)
    fine = jnp.where(g == 0, lt[8:16], jnp.where(g == 1, lt[16:24], jnp.where(g == 2, lt[24:32], lt[32:40])))
    rows = lax.broadcasted_iota(I32, fine.shape, 0)
    v1 = jnp.max(fine, axis=0, keepdims=True)
    i1 = jnp.min(jnp.where(fine == v1, rows, EPG), axis=0, keepdims=True)
    rest = jnp.where(rows == i1, -jnp.inf, fine)
    v2 = jnp.max(rest, axis=0, keepdims=True)
    i2 = jnp.min(jnp.where(rest == v2, rows, EPG), axis=0, keepdims=True)
    lo = jnp.minimum(i1, i2)
    hi = jnp.maximum(i1, i2)
    pair = lax.shift_right_logical(lo * (2 * EPG - 1 - lo), 1) + (hi - lo - 1)
    bucket = g * N_PAIRS + pair
    bucket_ref[k] = bucket
    brow = lax.broadcasted_iota(I32, (BUCKET_LANES, TM_OUT), 0)
    onehot = (brow == bucket).astype(F32)
    prefix = jnp.dot(onehot.astype(BF16), tri_ref[...], preferred_element_type=F32)
    rank = jnp.sum(onehot * (prefix + carry), axis=0, keepdims=True)
    rank_ref[k] = rank.astype(I32)
    return carry + jnp.sum(onehot, axis=1, keepdims=True)


def _mixout(oa, yf, x2, ga, gf, wout_bf, gm, wrt, br, tri):
    n = x2.shape[0]
    nt = n // TM_OUT
    rows_step = TM_OUT * OUT_SUBTILES
    full = lambda *shape: pl.BlockSpec(shape, lambda i: (0,) * len(shape))
    row3 = pl.BlockSpec((OUT_SUBTILES, 1, TM_OUT), lambda i: (i, 0, 0))
    return pl.pallas_call(
        _mixout_kernel,
        grid=(n // rows_step,),
        in_specs=[
            pl.BlockSpec((rows_step, D_ATTN), lambda i: (i, 0)),
            pl.BlockSpec((N_CBLK, rows_step // GRID_W * GRID_PITCH, 128), lambda i: (0, i, 0)),
            pl.BlockSpec((rows_step, D_MODEL), lambda i: (i, 0)),
            full(1, D_ATTN), full(1, D_FOUR), full(D_MODEL, D_MODEL), full(1, D_MODEL),
            full(BUCKET_LANES, D_MODEL), full(BUCKET_LANES, 1), full(TM_OUT, TM_OUT),
        ],
        out_specs=[
            pl.BlockSpec((rows_step * ROW_TILE, 128), lambda i: (i, 0)),
            row3, row3,
            full(BUCKET_LANES, 1),
        ],
        out_shape=[
            jax.ShapeDtypeStruct((n * ROW_TILE, 128), F32),
            jax.ShapeDtypeStruct((nt, 1, TM_OUT), I32),
            jax.ShapeDtypeStruct((nt, 1, TM_OUT), I32),
            jax.ShapeDtypeStruct((BUCKET_LANES, 1), F32),
        ],
        scratch_shapes=[pltpu.VMEM((BUCKET_LANES, 1), F32)],
        compiler_params=pltpu.CompilerParams(
            dimension_semantics=("arbitrary",), vmem_limit_bytes=V7X_VMEM_LIMIT),
        name="mixout",
    )(oa, yf, x2, ga, gf, wout_bf, gm, wrt, br, tri)


def _pair_tables():
    lo, hi = [], []
    for a in range(EPG):
        for b in range(a + 1, EPG):
            lo.append(a)
            hi.append(b)
    return np.asarray(lo, np.int32), np.asarray(hi, np.int32)


_PAIR_LO, _PAIR_HI = _pair_tables()


def _dispatch_kernel(tnv_ref, dest_ref, h_ref, hs_hbm, zbuf, zsem, sem):
    k = pl.program_id(0)
    tile_rows = TM_MOE * ROW_TILE
    n_tiles = hs_hbm.shape[0] // tile_rows

    def zero_copy(t):
        return pltpu.make_async_copy(zbuf, hs_hbm.at[pl.ds(pl.multiple_of(t * tile_rows, tile_rows), tile_rows)],
                                     zsem)

    @pl.when(k == 0)
    def _():
        zbuf[...] = jnp.zeros_like(zbuf)

        def zstart(t, c):
            @pl.when(tnv_ref[t] < TM_MOE)
            def _():
                zero_copy(t).start()
            return c

        def zwait(t, c):
            @pl.when(tnv_ref[t] < TM_MOE)
            def _():
                zero_copy(t).wait()
            return c

        lax.fori_loop(0, n_tiles, zstart, 0)
        lax.fori_loop(0, n_tiles, zwait, 0)

    def rows(r8, c):
        for u in range(8):
            r = r8 * 8 + u
            dst = pl.multiple_of(dest_ref[0, 0, r] * ROW_TILE, ROW_TILE)
            pltpu.make_async_copy(h_ref.at[pl.ds(pl.multiple_of(r * ROW_TILE, ROW_TILE), ROW_TILE)],
                                  hs_hbm.at[pl.ds(dst, ROW_TILE)], sem).start(priority=u % 2)
        return c

    lax.fori_loop(0, ROWS_PER_STEP // 8, rows, 0)
    pltpu.make_async_copy(h_ref, hs_hbm.at[pl.ds(0, ROWS_PER_STEP * ROW_TILE)], sem).wait()


def _dispatch(h_rt, dest3, tile_nv, n_slots):
    n = h_rt.shape[0] // ROW_TILE
    grid_spec = pltpu.PrefetchScalarGridSpec(
        num_scalar_prefetch=1,
        grid=(n // ROWS_PER_STEP,),
        in_specs=[
            pl.BlockSpec((1, 1, ROWS_PER_STEP), lambda k, *_: (k, 0, 0), memory_space=pltpu.SMEM),
            pl.BlockSpec((ROWS_PER_STEP * ROW_TILE, 128), lambda k, *_: (k, 0)),
        ],
        out_specs=pl.BlockSpec(memory_space=pl.ANY),
        scratch_shapes=[pltpu.VMEM((TM_MOE * ROW_TILE, 128), F32),
                        pltpu.SemaphoreType.DMA(()), pltpu.SemaphoreType.DMA(())],
    )
    return pl.pallas_call(
        _dispatch_kernel,
        grid_spec=grid_spec,
        out_shape=jax.ShapeDtypeStruct((n_slots * ROW_TILE, 128), F32),
        compiler_params=pltpu.CompilerParams(
            dimension_semantics=("arbitrary",), vmem_limit_bytes=V7X_VMEM_LIMIT),
        name="dispatch",
    )(tile_nv, dest3, h_rt)


def _combine_kernel(dest_ref, ys_hbm, o_ref, buf, sem):
    k = pl.program_id(0)
    n_blocks = pl.num_programs(0) - 1
    slot = k % 2

    def start_rows(r8):
        for u in range(8):
            r = r8 * 8 + u
            src = pl.multiple_of(dest_ref[0, 0, r] * ROW_TILE, ROW_TILE)
            pltpu.make_async_copy(ys_hbm.at[pl.ds(src, ROW_TILE)],
                                  buf.at[slot, pl.ds(pl.multiple_of(r * ROW_TILE, ROW_TILE), ROW_TILE)],
                                  sem.at[slot]).start(priority=u % 2)

    def unpack_rows(r8):
        base = pl.multiple_of(r8 * (8 * ROW_TILE), 8 * ROW_TILE)
        for cb in range(ROW_TILE):
            o_ref[pl.ds(pl.multiple_of(r8 * 8, 8), 8), cb * 128:(cb + 1) * 128] = (
                buf[1 - slot, pl.ds(base + cb, 8, stride=ROW_TILE), :])

    def loop(*parts):
        def body(r8, c):
            for part in parts:
                part(r8)
            return c
        lax.fori_loop(0, ROWS_PER_STEP // 8, body, 0)

    @pl.when(k > 0)
    def _():
        pltpu.make_async_copy(ys_hbm.at[pl.ds(0, ROWS_PER_STEP * ROW_TILE)], buf.at[1 - slot],
                              sem.at[1 - slot]).wait()

    @pl.when(k == 0)
    def _():
        loop(start_rows)

    @pl.when(jnp.logical_and(k > 0, k < n_blocks))
    def _():
        loop(start_rows, unpack_rows)

    @pl.when(k == n_blocks)
    def _():
        loop(unpack_rows)


def _combine(ys_rt, dest3, n):
    n_blocks = n // ROWS_PER_STEP
    return pl.pallas_call(
        _combine_kernel,
        grid=(n_blocks + 1,),
        in_specs=[
            pl.BlockSpec((1, 1, ROWS_PER_STEP), lambda k: (jnp.minimum(k, n_blocks - 1), 0, 0),
                         memory_space=pltpu.SMEM),
            pl.BlockSpec(memory_space=pl.ANY),
        ],
        out_specs=pl.BlockSpec((ROWS_PER_STEP, D_MODEL), lambda k: (jnp.maximum(k - 1, 0), 0)),
        out_shape=jax.ShapeDtypeStruct((n, D_MODEL), F32),
        scratch_shapes=[pltpu.VMEM((2, ROWS_PER_STEP * ROW_TILE, 128), F32), pltpu.SemaphoreType.DMA((2,))],
        compiler_params=pltpu.CompilerParams(
            dimension_semantics=("arbitrary",), vmem_limit_bytes=V7X_VMEM_LIMIT),
        name="combine",
    )(dest3, ys_rt)


def _moe_kernel(tg_ref, tlo_ref, thi_ref, nused_ref, hs_ref, wg_ref, wu_ref, wd_ref, wr_ref, br_ref,
                gm_ref, gfin_ref, ys_ref):
    step = pl.program_id(0)

    @pl.when(step * TILES_PER_STEP < nused_ref[0])
    def _():
        lane = lax.broadcasted_iota(I32, (TM_MOE, BUCKET_LANES), 1)
        for k in range(TILES_PER_STEP):
            t = step * TILES_PER_STEP + k
            base = k * TM_MOE * ROW_TILE
            hrows = jnp.concatenate(
                [hs_ref[pl.ds(base + cb, TM_MOE, stride=ROW_TILE), :] for cb in range(ROW_TILE)], axis=1)
            hn = _rms(hrows, gm_ref[...]).astype(BF16)
            logits = jnp.dot(hn, wr_ref[...], preferred_element_type=F32) + br_ref[...]
            g, lo, hi = tg_ref[t], tlo_ref[t], thi_ref[t]
            coarse = jnp.where(lane < N_GROUPS, logits, -jnp.inf)
            ec = jnp.exp(coarse - jnp.max(coarse, axis=-1, keepdims=True))
            pick = lambda col, v: jnp.sum(jnp.where(lane == col, v, 0.0), axis=-1, keepdims=True)
            g_w = pick(g, ec) / jnp.sum(ec, axis=-1, keepdims=True)
            f_lo = pick(8 + g * EPG + lo, logits)
            f_hi = pick(8 + g * EPG + hi, logits)
            f_max = jnp.maximum(f_lo, f_hi)
            e_lo = jnp.exp(f_lo - f_max)
            e_hi = jnp.exp(f_hi - f_max)
            den = e_lo + e_hi

            def expert(e, w):
                gate = jnp.dot(hn, wg_ref[e], preferred_element_type=F32)
                up = jnp.dot(hn, wu_ref[e], preferred_element_type=F32)
                act = (gate * jax.nn.sigmoid(gate) * up).astype(BF16)
                return w * jnp.dot(act, wd_ref[e], preferred_element_type=F32)

            y = expert(lo, (e_lo / den) * g_w) + expert(hi, (e_hi / den) * g_w)
            res = _rms(hrows + y, gfin_ref[...])
            for cb in range(ROW_TILE):
                ys_ref[pl.ds(base + cb, TM_MOE, stride=ROW_TILE), :] = res[:, cb * 128:(cb + 1) * 128]

    @pl.when(step * TILES_PER_STEP >= nused_ref[0])
    def _():
        ys_ref[...] = jnp.zeros_like(ys_ref)


def _moe(hs_rt, tile_g, tile_lo, tile_hi, n_used, wg, wu, wd, wr, br, gm, gfin):
    rows_step = TM_MOE * TILES_PER_STEP * ROW_TILE
    n_steps = hs_rt.shape[0] // rows_step
    by_group = lambda s, tg, *_: (tg[s * TILES_PER_STEP], 0, 0, 0)
    full2 = lambda a, c: pl.BlockSpec((a, c), lambda s, *_: (0, 0))

    def hs_index(s, tg, tlo, thi, nu):
        last_step = jnp.maximum(nu[0] - 1, 0) // TILES_PER_STEP
        return (jnp.minimum(s, last_step), 0)

    grid_spec = pltpu.PrefetchScalarGridSpec(
        num_scalar_prefetch=4,
        grid=(n_steps,),
        in_specs=[
            pl.BlockSpec((rows_step, 128), hs_index),
            pl.BlockSpec((None, EPG, D_MODEL, D_EXPERT), by_group),
            pl.BlockSpec((None, EPG, D_MODEL, D_EXPERT), by_group),
            pl.BlockSpec((None, EPG, D_EXPERT, D_MODEL), by_group),
            full2(D_MODEL, BUCKET_LANES), full2(1, BUCKET_LANES), full2(1, D_MODEL), full2(1, D_MODEL),
        ],
        out_specs=pl.BlockSpec((rows_step, 128), lambda s, *_: (s, 0)),
    )
    return pl.pallas_call(
        _moe_kernel,
        grid_spec=grid_spec,
        out_shape=jax.ShapeDtypeStruct(hs_rt.shape, F32),
        compiler_params=pltpu.CompilerParams(
            dimension_semantics=("arbitrary",), vmem_limit_bytes=V7X_VMEM_LIMIT),
        name="moe",
    )(tile_g, tile_lo, tile_hi, n_used, hs_rt, wg, wu, wd, wr, br, gm, gfin)


def _bucket_plan(bucket, rank, counts, n):
    nt = n // TM_MOE + N_BUCKETS + N_GROUPS * (TILES_PER_STEP - 1)
    nt = -(-nt // TILES_PER_STEP) * TILES_PER_STEP
    tiles_b = (counts + (TM_MOE - 1)) // TM_MOE
    tiles_g = jnp.sum(tiles_b.reshape(N_GROUPS, N_PAIRS), axis=1)
    extra_g = (-tiles_g) % TILES_PER_STEP
    is_last = (np.arange(N_PAIRS) == N_PAIRS - 1)[None, :]
    tiles_b = (tiles_b.reshape(N_GROUPS, N_PAIRS) + jnp.where(is_last, extra_g[:, None], 0)).reshape(N_BUCKETS)
    tile_end = jnp.cumsum(tiles_b)
    tile_start = tile_end - tiles_b
    n_used = tile_end[-1]
    b_ids = jnp.arange(N_BUCKETS, dtype=I32)
    dest = rank + TM_MOE * jnp.sum(jnp.where(bucket[:, None] == b_ids[None, :], tile_start[None, :], 0), axis=1)
    t_idx = jnp.arange(nt, dtype=I32)
    tb = jnp.sum((tile_end[None, :] <= t_idx[:, None]).astype(I32), axis=1)
    tb_last = jnp.sum((tile_end <= n_used - 1).astype(I32))
    tb = jnp.minimum(jnp.where(t_idx < n_used, tb, tb_last), N_BUCKETS - 1)
    sel = tb[:, None] == b_ids[None, :]
    pick = lambda table: jnp.sum(jnp.where(sel, table[None, :], 0), axis=1).astype(I32)
    tile_g = tb // N_PAIRS
    tile_lo = pick(jnp.asarray(np.tile(_PAIR_LO, N_GROUPS)))
    tile_hi = pick(jnp.asarray(np.tile(_PAIR_HI, N_GROUPS)))
    nv = jnp.clip(pick(counts) - (t_idx - pick(tile_start)) * TM_MOE, 0, TM_MOE)
    tile_nv = jnp.where(t_idx < n_used, nv, 0).astype(I32)
    return dest.astype(I32), tile_g.astype(I32), tile_lo, tile_hi, tile_nv, n_used.reshape(1).astype(I32)


def kernel(x, norm_mix, w_in, rpb, w_four, b_four, g_attn_out, g_four_out, w_out, norm_moe,
           w_router_coarse, b_router_coarse, w_router_fine, b_router_fine, w_gate, w_up, w_down, norm_final):
    b, seq, d = x.shape
    assert (seq, d) == (SEQ, D_MODEL) and norm_mix.shape[0] == 1
    n = b * seq
    x2 = x.reshape(n, d)

    qkv, u = _inproj(x2, norm_mix[0][None], w_in[0].astype(BF16))

    oa = _attention(qkv.reshape(b, seq, 3 * D_ATTN), _bias_columns(rpb[0]))

    eye4 = jnp.eye(4, dtype=F32)
    wf = w_four[0].reshape(2, 4, FOUR_GROUP_DIM, FOUR_GROUP_DIM)
    wbd = (eye4[None, :, None, :, None] * wf[:, :, :, None, :]).reshape(2, 256, 256).astype(BF16)
    yf = _fourier(u, wbd, b_four[0][None]).reshape(N_CBLK, b * ROWS * GRID_PITCH, 128)

    wrt = jnp.zeros((BUCKET_LANES, d), F32)
    wrt = wrt.at[0:N_GROUPS].set(w_router_coarse[0].T).at[8:8 + N_EXPERTS].set(w_router_fine[0].T)
    br = jnp.zeros((BUCKET_LANES, 1), F32)
    br = br.at[0:N_GROUPS, 0].set(b_router_coarse[0]).at[8:8 + N_EXPERTS, 0].set(b_router_fine[0])
    tri = (np.arange(TM_OUT)[:, None] < np.arange(TM_OUT)[None, :]).astype(np.float32)
    wrt_bf = wrt.astype(BF16)
    h_rt, bucket, rank, cnt = _mixout(
        oa.reshape(n, D_ATTN), yf, x2, g_attn_out[0][None], g_four_out[0][None],
        w_out[0].astype(BF16), norm_moe[0][None], wrt_bf, br, jnp.asarray(tri, BF16))

    counts = cnt[:N_BUCKETS, 0].astype(I32)
    dest, tile_g, tile_lo, tile_hi, tile_nv, n_used = _bucket_plan(bucket.reshape(n), rank.reshape(n), counts, n)
    dest3 = dest.reshape(n // ROWS_PER_STEP, 1, ROWS_PER_STEP)
    hs_rt = _dispatch(h_rt, dest3, tile_nv, tile_nv.shape[0] * TM_MOE)
    shape_e = (N_GROUPS, EPG)
    ys_rt = _moe(hs_rt, tile_g, tile_lo, tile_hi, n_used,
                 w_gate[0].astype(BF16).reshape(shape_e + (d, D_EXPERT)),
                 w_up[0].astype(BF16).reshape(shape_e + (d, D_EXPERT)),
                 w_down[0].astype(BF16).reshape(shape_e + (D_EXPERT, d)),
                 wrt_bf.T, br.T, norm_moe[0][None], norm_final[None])
    return _combine(ys_rt, dest3, n).reshape(b, seq, d)
```

```python
import functools

import numpy as np
import jax
import jax.numpy as jnp
from jax import lax
from jax.experimental import pallas as pl
from jax.experimental.pallas import tpu as pltpu

F32 = jnp.float32
BF16 = jnp.bfloat16
I32 = jnp.int32

D_MODEL = 1024
SEQ = 4096
GRID_W = 64
ROWS = SEQ // GRID_W
D_ATTN = 512
D_FOUR = 512
N_HEADS = 8
HEAD_DIM = 64
WIN_H = 8
WIN_W = 16
N_FOUR_GROUPS = 8
FOUR_GROUP_DIM = 64
D_PROJ = 3 * D_ATTN + D_FOUR
N_GROUPS = 4
EPG = 8
N_EXPERTS = N_GROUPS * EPG
D_EXPERT = 256
EPS = 1e-6
NEG = -1e30

V7X_VMEM_LIMIT = 56 * 1024 * 1024

TM_IN = 1024
TM_OUT = 512
OUT_SUBTILES = 2
TM_MOE = 128
ROW_TILE = D_MODEL // 128
ROWS_PER_STEP = 2048
TILES_PER_STEP = 4
N_PAIRS = EPG * (EPG - 1) // 2
N_BUCKETS = N_GROUPS * N_PAIRS
BUCKET_LANES = 128

QB_ROWS = 8
QB_COLS = 16
KB_ROWS = 16
KB_COLS = 32
GRID_PITCH = 72


def _rms(x, g):
    ms = jnp.mean(x * x, axis=-1, keepdims=True)
    return x * lax.rsqrt(ms + EPS) * g


def _inproj_kernel(x_ref, g_ref, w_ref, qkv_ref, u_ref):
    xn = _rms(x_ref[...], g_ref[...]).astype(BF16)
    p = jnp.dot(xn, w_ref[...], preferred_element_type=F32)
    qkv_ref[:, :D_ATTN] = (p[:, :D_ATTN] * (HEAD_DIM ** -0.5)).astype(BF16)
    qkv_ref[:, D_ATTN:] = p[:, D_ATTN:3 * D_ATTN].astype(BF16)
    for cb in range(D_FOUR // 128):
        lanes = slice(3 * D_ATTN + cb * 128, 3 * D_ATTN + (cb + 1) * 128)
        for r in range(TM_IN // GRID_W):
            u_ref[cb, r * GRID_PITCH:r * GRID_PITCH + GRID_W, :] = p[r * GRID_W:(r + 1) * GRID_W, lanes]
            u_ref[cb, r * GRID_PITCH + GRID_W:(r + 1) * GRID_PITCH, :] = jnp.zeros((GRID_PITCH - GRID_W, 128), F32)


def _inproj(x2, g, w_bf):
    n = x2.shape[0]
    rows_step = TM_IN // GRID_W
    steps_b = ROWS // rows_step
    return pl.pallas_call(
        _inproj_kernel,
        grid=(n // TM_IN,),
        in_specs=[
            pl.BlockSpec((TM_IN, D_MODEL), lambda i: (i, 0)),
            pl.BlockSpec((1, D_MODEL), lambda i: (0, 0)),
            pl.BlockSpec((D_MODEL, D_PROJ), lambda i: (0, 0)),
        ],
        out_specs=[
            pl.BlockSpec((TM_IN, 3 * D_ATTN), lambda i: (i, 0)),
            pl.BlockSpec((D_FOUR // 128, None, rows_step * GRID_PITCH, 128),
                         lambda i: (0, i // steps_b, i % steps_b, 0)),
        ],
        out_shape=[
            jax.ShapeDtypeStruct((n, 3 * D_ATTN), BF16),
            jax.ShapeDtypeStruct((D_FOUR // 128, n // SEQ, ROWS * GRID_PITCH, 128), F32),
        ],
        compiler_params=pltpu.CompilerParams(
            dimension_semantics=("parallel",), vmem_limit_bytes=V7X_VMEM_LIMIT),
        name="inproj",
    )(x2, g, w_bf)


_KCOL_START = (0, 8, 24, 32)
_KCOL_SHIFTED = (False, True, True, False)
_KCOL_OFF = (0, 0, 16, 32)
_COL_TYPE = (0, 1, 1, 2)


def _bias_index_tables():
    dr = np.zeros((9, 128, 512), np.int32)
    dc = np.zeros((9, 128, 512), np.int32)
    ok = np.zeros((9, 128, 512), bool)
    qi, qc = np.divmod(np.arange(128), QB_COLS)
    ki, kc = np.divmod(np.arange(512), KB_COLS)
    for rt, (q0, k0) in enumerate(((0, 0), (8, 4), (56, 48))):
        qrow = q0 + qi
        krow = k0 + ki
        rs = np.clip(qrow - WIN_H // 2, 0, ROWS - WIN_H)
        rok = (krow[None, :] >= rs[:, None]) & (krow[None, :] < rs[:, None] + WIN_H)
        drr = krow[None, :] - qrow[:, None] + (WIN_H - 1)
        for ct, (c0, kc0) in enumerate(((0, 0), (16, 8), (48, 32))):
            qcol = c0 + qc
            kcol = kc0 + kc
            cs = np.clip(qcol - WIN_W // 2, 0, GRID_W - WIN_W)
            cok = (kcol[None, :] >= cs[:, None]) & (kcol[None, :] < cs[:, None] + WIN_W)
            dcc = kcol[None, :] - qcol[:, None] + (WIN_W - 1)
            t = rt * 3 + ct
            ok[t] = rok & cok
            dr[t] = np.where(ok[t], drr, 0)
            dc[t] = np.where(ok[t], dcc, 0)
    return dr, dc, ok


_BIAS_DR, _BIAS_DC, _BIAS_OK = _bias_index_tables()


def _bias_selectors():
    ok = _BIAS_OK.reshape(3, 3, QB_ROWS, QB_COLS, KB_ROWS, KB_COLS)
    dr = _BIAS_DR.reshape(ok.shape)
    dc = _BIAS_DC.reshape(ok.shape)
    row_ok = ok.any(axis=(1, 3, 5))
    col_ok = ok.any(axis=(0, 2, 4))
    dr_r = dr.max(axis=(1, 3, 5))
    dc_c = dc.max(axis=(0, 2, 4))
    sc = (np.arange(2 * WIN_W - 1)[:, None, None, None] == dc_c[None]) & col_ok[None]
    return row_ok, dr_r, col_ok, sc.astype(np.float32)


_BIAS_ROW_OK, _BIAS_ROW_DR, _BIAS_COL_OK, _BIAS_SC = _bias_selectors()


def _bias_columns(rpb):
    sc = jnp.asarray(np.tile(_BIAS_SC, 128 // KB_COLS))
    ok = np.tile(_BIAS_COL_OK, 128 // KB_COLS)
    t1 = jnp.sum(rpb[:, :, :, None, None, None] * sc[None, None], axis=2)
    return jnp.where(ok[None, None], t1, NEG)


def _attn_kernel(q_ref, k_ref, v_ref, bcol_ref, o_ref, ksh_ref, vsh_ref, bias_ref, s_ref, p_ref, l_ref):
    @pl.when(pl.program_id(1) == 0)
    def _():
        key_row = lax.broadcasted_iota(I32, (QB_COLS, KB_ROWS * KB_COLS), 1) // KB_COLS
        for hh in range(2):
            for rt in range(3):
                for ct in range(3):
                    for i in range(QB_ROWS):
                        acc = jnp.full((QB_COLS, KB_ROWS * KB_COLS), NEG, F32)
                        for y in range(KB_ROWS):
                            if _BIAS_ROW_OK[rt, i, y]:
                                cols = bcol_ref[hh, int(_BIAS_ROW_DR[rt, i, y]), ct]
                                cols = jnp.concatenate([cols] * (KB_ROWS * KB_COLS // 128), axis=1)
                                acc = jnp.where(key_row == y, cols, acc)
                        bias_ref[rt * 3 + ct, pl.ds(hh * 128 + i * QB_COLS, QB_COLS), :] = acc

    zpad = jnp.zeros((8, 128), F32)
    ksh_ref[...] = jnp.concatenate([k_ref[...].astype(F32)[8:], zpad], axis=0).astype(BF16)
    vsh_ref[...] = jnp.concatenate([v_ref[...].astype(F32)[8:], zpad], axis=0).astype(BF16)
    lane = lax.broadcasted_iota(I32, (1, 128), 1)
    head_masks = (lane < HEAD_DIM, lane >= HEAD_DIM)

    n_rb = ROWS // QB_ROWS
    n_q = GRID_W // QB_COLS

    def window(ref_plain, ref_shift, rb, j):
        ks = jnp.clip(QB_ROWS * rb - WIN_H // 2, 0, ROWS - KB_ROWS)
        src = ref_shift if _KCOL_SHIFTED[j] else ref_plain
        return jnp.concatenate(
            [src[pl.ds(pl.multiple_of((ks + i) * GRID_W + _KCOL_OFF[j], 16), KB_COLS), :] for i in range(KB_ROWS)],
            axis=0)

    def scores(rb, j):
        q = jnp.concatenate(
            [q_ref[pl.ds(pl.multiple_of((QB_ROWS * rb + i) * GRID_W + QB_COLS * j, 16), QB_COLS), :]
             for i in range(QB_ROWS)], axis=0)
        qm = jnp.concatenate([jnp.where(hm, q, jnp.zeros_like(q)) for hm in head_masks], axis=0)
        s_ref[j] = lax.dot_general(qm, window(k_ref, ksh_ref, rb, j), (((1,), (1,)), ((), ())),
                                   preferred_element_type=F32)

    def softmax(rb, j):
        rt = jnp.where(rb == 0, 0, jnp.where(rb == n_rb - 1, 2, 1))
        s = s_ref[j] + bias_ref[rt * 3 + _COL_TYPE[j]]
        e = jnp.exp(s - jnp.max(s, axis=-1, keepdims=True))
        p_ref[j] = e.astype(BF16)
        l_ref[j] = jnp.broadcast_to(jnp.sum(e, axis=-1, keepdims=True), (2 * QB_ROWS * QB_COLS, 128))

    def values(rb, j):
        o = jnp.dot(p_ref[j], window(v_ref, vsh_ref, rb, j), preferred_element_type=F32) / l_ref[j]
        out = jnp.where(head_masks[0], o[:128], o[128:]).astype(BF16)
        for i in range(QB_ROWS):
            o_ref[pl.ds(pl.multiple_of((QB_ROWS * rb + i) * GRID_W + QB_COLS * j, 16), QB_COLS), :] = (
                out[QB_COLS * i:QB_COLS * (i + 1)])

    def stage(fn, rb):
        for j in range(n_q):
            fn(jnp.asarray(rb, I32), j)

    stage(scores, 0)
    stage(softmax, 0)
    stage(scores, 1)

    def pipeline_step(i, carry):
        stage(values, i - 2)
        stage(softmax, i - 1)
        stage(scores, i)
        return carry

    lax.fori_loop(2, n_rb, pipeline_step, 0)
    stage(values, n_rb - 2)
    stage(softmax, n_rb - 1)
    stage(values, n_rb - 1)


def _attention(qkv3, bias_cols):
    b = qkv3.shape[0]
    n_hp = N_HEADS // 2
    blk = lambda off: pl.BlockSpec((None, SEQ, 128), lambda hp, bi: (bi, 0, off + hp))
    return pl.pallas_call(
        _attn_kernel,
        grid=(n_hp, b),
        in_specs=[
            blk(0), blk(n_hp), blk(2 * n_hp),
            pl.BlockSpec((2,) + bias_cols.shape[1:], lambda hp, bi: (hp, 0, 0, 0, 0)),
        ],
        out_specs=pl.BlockSpec((None, SEQ, 128), lambda hp, bi: (bi, 0, hp)),
        out_shape=jax.ShapeDtypeStruct((b, SEQ, D_ATTN), BF16),
        scratch_shapes=[pltpu.VMEM((SEQ, 128), BF16), pltpu.VMEM((SEQ, 128), BF16),
                        pltpu.VMEM((9, 2 * QB_ROWS * QB_COLS, KB_ROWS * KB_COLS), F32),
                        pltpu.VMEM((GRID_W // QB_COLS, 2 * QB_ROWS * QB_COLS, KB_ROWS * KB_COLS), F32),
                        pltpu.VMEM((GRID_W // QB_COLS, 2 * QB_ROWS * QB_COLS, KB_ROWS * KB_COLS), BF16),
                        pltpu.VMEM((GRID_W // QB_COLS, 2 * QB_ROWS * QB_COLS, 128), F32)],
        compiler_params=pltpu.CompilerParams(
            dimension_semantics=("arbitrary", "arbitrary"), vmem_limit_bytes=V7X_VMEM_LIMIT),
        name="nattn",
    )(qkv3, qkv3, qkv3, bias_cols)


Z_PITCH = 72
N_CBLK = D_FOUR // 128


def _fourier_tables():
    n = 64
    k = np.arange(n)
    ang = 2.0 * np.pi * np.outer(k, k) / n
    c64, s64 = np.cos(ang), np.sin(ang)
    w1 = np.concatenate([c64, -s64], axis=0)
    t1p = np.arange(n)[:, None, None]
    t2p = np.arange(n)[None, :, None]
    t2 = np.arange(n)[None, None, :]
    th = 2.0 * np.pi * ((t2 * (t1p + n * t2p)) % SEQ) / SEQ
    cc, ss = np.cos(th), np.sin(th)
    m2 = np.concatenate([np.concatenate([cc, ss], axis=2), np.concatenate([-ss, cc], axis=2)], axis=1)
    cbd = np.kron(np.eye(4), c64)
    sbd = np.kron(np.eye(4), s64)
    cs = np.concatenate([cbd, sbd], axis=0)
    return w1.astype(np.float32), m2.astype(np.float32), cs.astype(np.float32)


_W1_NP, _M2_NP, _CS_NP = _fourier_tables()


HALF_CBLK = N_CBLK // 2
T2_UNROLL = 16


def _fourier_kernel(u_ref, w1_ref, m2_ref, cs_ref, wbd_ref, bf_ref, y_ref, zs_ref):
    for cb in range(HALF_CBLK):
        for k in range(GRID_PITCH - GRID_W):
            y_ref[cb, pl.ds(GRID_W + k, ROWS, stride=GRID_PITCH), :] = jnp.zeros((ROWS, 128), F32)

    def dft_cols(i, carry):
        for k in range(T2_UNROLL):
            t2 = i * T2_UNROLL + k
            x = jnp.concatenate([u_ref[cb, pl.ds(t2, ROWS, stride=GRID_PITCH), :] for cb in range(HALF_CBLK)],
                                axis=1).astype(BF16)
            z = jnp.dot(w1_ref[...], x, preferred_element_type=F32)
            for cb in range(HALF_CBLK):
                zs_ref[cb, pl.ds(t2, 128, stride=Z_PITCH), :] = z[:, cb * 128:(cb + 1) * 128]
        return carry

    lax.fori_loop(0, GRID_W // T2_UNROLL, dft_cols, 0)

    def dft_rows(a, carry):
        xs = []
        for jo in range(8):
            t1p = a * 8 + jo
            rhs = jnp.concatenate(
                [jnp.concatenate([zs_ref[cb, pl.ds(pl.multiple_of((part * 64 + t1p) * Z_PITCH, 8), 64), :]
                                  for cb in range(HALF_CBLK)], axis=1) for part in range(2)], axis=0)
            xs.append(jnp.dot(m2_ref[t1p], rhs.astype(BF16), preferred_element_type=F32))
        xr = jnp.concatenate([x[:64] for x in xs], axis=0).astype(BF16)
        xi = jnp.concatenate([x[64:] for x in xs], axis=0).astype(BF16)
        lhs = jnp.concatenate([xr, xi], axis=1)
        f = jnp.dot(lhs, cs_ref[...], preferred_element_type=F32) * (1.0 / 512.0)
        y = jnp.dot(f.astype(BF16), wbd_ref[...], preferred_element_type=F32) + bf_ref[...]
        for jo in range(8):
            t1p = a * 8 + jo
            for cb in range(HALF_CBLK):
                y_ref[cb, pl.ds(t1p, ROWS, stride=GRID_PITCH), :] = y[jo * 64:(jo + 1) * 64, cb * 128:(cb + 1) * 128]
        return carry

    lax.fori_loop(0, GRID_W // 8, dft_rows, 0)


def _fourier(u_p, wbd, bf):
    b = u_p.shape[1]
    w1 = jnp.asarray(_W1_NP).astype(BF16)
    m2 = jnp.asarray(_M2_NP).astype(BF16)
    cs = jnp.asarray(_CS_NP).astype(BF16)
    half_blk = pl.BlockSpec((HALF_CBLK, None, ROWS * GRID_PITCH, 128), lambda bi, hf: (hf, bi, 0, 0))
    return pl.pallas_call(
        _fourier_kernel,
        grid=(b, 2),
        in_specs=[
            half_blk,
            pl.BlockSpec((128, 64), lambda bi, hf: (0, 0)),
            pl.BlockSpec((64, 128, 128), lambda bi, hf: (0, 0, 0)),
            pl.BlockSpec((512, 256), lambda bi, hf: (0, 0)),
            pl.BlockSpec((None, 256, 256), lambda bi, hf: (hf, 0, 0)),
            pl.BlockSpec((1, 256), lambda bi, hf: (0, hf)),
        ],
        out_specs=half_blk,
        out_shape=jax.ShapeDtypeStruct(u_p.shape, F32),
        scratch_shapes=[pltpu.VMEM((HALF_CBLK, 128 * Z_PITCH, 128), F32)],
        compiler_params=pltpu.CompilerParams(
            dimension_semantics=("parallel", "parallel"), vmem_limit_bytes=V7X_VMEM_LIMIT),
        name="fourier",
    )(u_p, w1, m2, cs, wbd, bf)


def _mixout_kernel(oa_ref, yf_ref, x_ref, ga_ref, gf_ref, wout_ref, gm_ref, wrt_ref, br_ref, tri_ref,
                   h_ref, bucket_ref, rank_ref, cnt_ref, carry_ref):
    i = pl.program_id(0)

    @pl.when(i == 0)
    def _():
        carry_ref[...] = jnp.zeros_like(carry_ref)

    carry = carry_ref[...]
    for k in range(OUT_SUBTILES):
        carry = _mixout_subtile(k, carry, oa_ref, yf_ref, x_ref, ga_ref, gf_ref, wout_ref, gm_ref, wrt_ref,
                                br_ref, tri_ref, h_ref, bucket_ref, rank_ref)
    carry_ref[...] = carry
    cnt_ref[...] = carry


def _mixout_subtile(k, carry, oa_ref, yf_ref, x_ref, ga_ref, gf_ref, wout_ref, gm_ref, wrt_ref, br_ref, tri_ref,
                    h_ref, bucket_ref, rank_ref):
    rows_k = pl.ds(k * TM_OUT, TM_OUT)
    na = _rms(oa_ref[rows_k, :].astype(F32), ga_ref[...]).astype(BF16)
    grid_rows = [k * (TM_OUT // GRID_W) + r for r in range(TM_OUT // GRID_W)]
    yf = jnp.concatenate(
        [jnp.concatenate([yf_ref[cb, gr * GRID_PITCH:gr * GRID_PITCH + GRID_W, :] for gr in grid_rows], axis=0)
         for cb in range(N_CBLK)], axis=1)
    nf = _rms(yf, gf_ref[...]).astype(BF16)
    merged = jnp.concatenate([na, nf], axis=1)
    h = x_ref[rows_k, :] + jnp.dot(merged, wout_ref[...], preferred_element_type=F32)
    for cb in range(ROW_TILE):
        h_ref[pl.ds(k * TM_OUT * ROW_TILE + cb, TM_OUT, stride=ROW_TILE), :] = h[:, cb * 128:(cb + 1) * 128]
    hn = _rms(h, gm_ref[...]).astype(BF16)
    lt = lax.dot_general(wrt_ref[...], hn, (((1,), (1,)), ((), ())), preferred_element_type=F32)
    lt = lt + br_ref[...]
    c = [lt[k:k + 1] for k in range(N_GROUPS)]
    cmax = jnp.maximum(jnp.maximum(c[0], c[1]), jnp.maximum(c[2], c[3]))
    e = [jnp.exp(ck - cmax) for ck in c]
    esum = (e[0] + e[1]) + (e[2] + e[3])
    p = [ek / esum for ek in e]
    pmax = jnp.maximum(jnp.maximum(p[0], p[1]), jnp.maximum(p[2], p[3]))
    g = jnp.where(p[0] == pmax, 0, jnp.where(p[1] == pmax, 1, jnp.where(p[2] == pmax, 2, 3))).astype(I32)
    fine = jnp.where(g == 0, lt[8:16], jnp.where(g == 1, lt[16:24], jnp.where(g == 2, lt[24:32], lt[32:40])))
    rows = lax.broadcasted_iota(I32, fine.shape, 0)
    v1 = jnp.max(fine, axis=0, keepdims=True)
    i1 = jnp.min(jnp.where(fine == v1, rows, EPG), axis=0, keepdims=True)
    rest = jnp.where(rows == i1, -jnp.inf, fine)
    v2 = jnp.max(rest, axis=0, keepdims=True)
    i2 = jnp.min(jnp.where(rest == v2, rows, EPG), axis=0, keepdims=True)
    lo = jnp.minimum(i1, i2)
    hi = jnp.maximum(i1, i2)
    pair = lax.shift_right_logical(lo * (2 * EPG - 1 - lo), 1) + (hi - lo - 1)
    bucket = g * N_PAIRS + pair
    bucket_ref[k] = bucket
    brow = lax.broadcasted_iota(I32, (BUCKET_LANES, TM_OUT), 0)
    onehot = (brow == bucket).astype(F32)
    prefix = jnp.dot(onehot.astype(BF16), tri_ref[...], preferred_element_type=F32)
    rank = jnp.sum(onehot * (prefix + carry), axis=0, keepdims=True)
    rank_ref[k] = rank.astype(I32)
    return carry + jnp.sum(onehot, axis=1, keepdims=True)


def _mixout(oa, yf, x2, ga, gf, wout_bf, gm, wrt, br, tri):
    n = x2.shape[0]
    nt = n // TM_OUT
    rows_step = TM_OUT * OUT_SUBTILES
    full = lambda *shape: pl.BlockSpec(shape, lambda i: (0,) * len(shape))
    row3 = pl.BlockSpec((OUT_SUBTILES, 1, TM_OUT), lambda i: (i, 0, 0))
    return pl.pallas_call(
        _mixout_kernel,
        grid=(n // rows_step,),
        in_specs=[
            pl.BlockSpec((rows_step, D_ATTN), lambda i: (i, 0)),
            pl.BlockSpec((N_CBLK, rows_step // GRID_W * GRID_PITCH, 128), lambda i: (0, i, 0)),
            pl.BlockSpec((rows_step, D_MODEL), lambda i: (i, 0)),
            full(1, D_ATTN), full(1, D_FOUR), full(D_MODEL, D_MODEL), full(1, D_MODEL),
            full(BUCKET_LANES, D_MODEL), full(BUCKET_LANES, 1), full(TM_OUT, TM_OUT),
        ],
        out_specs=[
            pl.BlockSpec((rows_step * ROW_TILE, 128), lambda i: (i, 0)),
            row3, row3,
            full(BUCKET_LANES, 1),
        ],
        out_shape=[
            jax.ShapeDtypeStruct((n * ROW_TILE, 128), F32),
            jax.ShapeDtypeStruct((nt, 1, TM_OUT), I32),
            jax.ShapeDtypeStruct((nt, 1, TM_OUT), I32),
            jax.ShapeDtypeStruct((BUCKET_LANES, 1), F32),
        ],
        scratch_shapes=[pltpu.VMEM((BUCKET_LANES, 1), F32)],
        compiler_params=pltpu.CompilerParams(
            dimension_semantics=("arbitrary",), vmem_limit_bytes=V7X_VMEM_LIMIT),
        name="mixout",
    )(oa, yf, x2, ga, gf, wout_bf, gm, wrt, br, tri)


def _pair_tables():
    lo, hi = [], []
    for a in range(EPG):
        for b in range(a + 1, EPG):
            lo.append(a)
            hi.append(b)
    return np.asarray(lo, np.int32), np.asarray(hi, np.int32)


_PAIR_LO, _PAIR_HI = _pair_tables()


def _dispatch_kernel(tnv_ref, dest_ref, h_ref, wg_ref, wu_ref, wd_ref, hs_hbm, wg_out, wu_out, wd_out,
                     zbuf, zsem, sem):
    k = pl.program_id(0)
    wg_out[...] = wg_ref[...].astype(BF16)
    wu_out[...] = wu_ref[...].astype(BF16)
    wd_out[...] = wd_ref[...].astype(BF16)
    tile_rows = TM_MOE * ROW_TILE
    n_tiles = hs_hbm.shape[0] // tile_rows

    def zero_copy(t):
        return pltpu.make_async_copy(zbuf, hs_hbm.at[pl.ds(pl.multiple_of(t * tile_rows, tile_rows), tile_rows)],
                                     zsem)

    @pl.when(k == 0)
    def _():
        zbuf[...] = jnp.zeros_like(zbuf)

        def zstart(t, c):
            @pl.when(tnv_ref[t] < TM_MOE)
            def _():
                zero_copy(t).start()
            return c

        def zwait(t, c):
            @pl.when(tnv_ref[t] < TM_MOE)
            def _():
                zero_copy(t).wait()
            return c

        lax.fori_loop(0, n_tiles, zstart, 0)
        lax.fori_loop(0, n_tiles, zwait, 0)

    def rows(r8, c):
        for u in range(8):
            r = r8 * 8 + u
            dst = pl.multiple_of(dest_ref[0, 0, r] * ROW_TILE, ROW_TILE)
            pltpu.make_async_copy(h_ref.at[pl.ds(pl.multiple_of(r * ROW_TILE, ROW_TILE), ROW_TILE)],
                                  hs_hbm.at[pl.ds(dst, ROW_TILE)], sem).start(priority=u % 2)
        return c

    lax.fori_loop(0, ROWS_PER_STEP // 8, rows, 0)
    pltpu.make_async_copy(h_ref, hs_hbm.at[pl.ds(0, ROWS_PER_STEP * ROW_TILE)], sem).wait()


def _dispatch(h_rt, dest3, tile_nv, n_slots, w_gate, w_up, w_down):
    n = h_rt.shape[0] // ROW_TILE
    n_steps = n // ROWS_PER_STEP
    assert N_EXPERTS % n_steps == 0
    eps = N_EXPERTS // n_steps
    wspec = lambda shape: pl.BlockSpec((eps,) + shape, lambda k, *_: (k, 0, 0))
    grid_spec = pltpu.PrefetchScalarGridSpec(
        num_scalar_prefetch=1,
        grid=(n_steps,),
        in_specs=[
            pl.BlockSpec((1, 1, ROWS_PER_STEP), lambda k, *_: (k, 0, 0), memory_space=pltpu.SMEM),
            pl.BlockSpec((ROWS_PER_STEP * ROW_TILE, 128), lambda k, *_: (k, 0)),
            wspec((D_MODEL, D_EXPERT)), wspec((D_MODEL, D_EXPERT)), wspec((D_EXPERT, D_MODEL)),
        ],
        out_specs=[pl.BlockSpec(memory_space=pl.ANY),
                   wspec((D_MODEL, D_EXPERT)), wspec((D_MODEL, D_EXPERT)), wspec((D_EXPERT, D_MODEL))],
        scratch_shapes=[pltpu.VMEM((TM_MOE * ROW_TILE, 128), F32),
                        pltpu.SemaphoreType.DMA(()), pltpu.SemaphoreType.DMA(())],
    )
    return pl.pallas_call(
        _dispatch_kernel,
        grid_spec=grid_spec,
        out_shape=[jax.ShapeDtypeStruct((n_slots * ROW_TILE, 128), F32),
                   jax.ShapeDtypeStruct(w_gate.shape, BF16), jax.ShapeDtypeStruct(w_up.shape, BF16),
                   jax.ShapeDtypeStruct(w_down.shape, BF16)],
        compiler_params=pltpu.CompilerParams(
            dimension_semantics=("arbitrary",), vmem_limit_bytes=V7X_VMEM_LIMIT),
        name="dispatch",
    )(tile_nv, dest3, h_rt, w_gate, w_up, w_down)


def _combine_kernel(dest_ref, ys_hbm, o_ref, buf, sem):
    k = pl.program_id(0)
    n_blocks = pl.num_programs(0) - 1
    slot = k % 2

    def start_rows(r8):
        for u in range(8):
            r = r8 * 8 + u
            src = pl.multiple_of(dest_ref[0, 0, r] * ROW_TILE, ROW_TILE)
            pltpu.make_async_copy(ys_hbm.at[pl.ds(src, ROW_TILE)],
                                  buf.at[slot, pl.ds(pl.multiple_of(r * ROW_TILE, ROW_TILE), ROW_TILE)],
                                  sem.at[slot]).start(priority=u % 2)

    def unpack_rows(r8):
        base = pl.multiple_of(r8 * (8 * ROW_TILE), 8 * ROW_TILE)
        for cb in range(ROW_TILE):
            o_ref[pl.ds(pl.multiple_of(r8 * 8, 8), 8), cb * 128:(cb + 1) * 128] = (
                buf[1 - slot, pl.ds(base + cb, 8, stride=ROW_TILE), :])

    def loop(*parts):
        def body(r8, c):
            for part in parts:
                part(r8)
            return c
        lax.fori_loop(0, ROWS_PER_STEP // 8, body, 0)

    @pl.when(k > 0)
    def _():
        pltpu.make_async_copy(ys_hbm.at[pl.ds(0, ROWS_PER_STEP * ROW_TILE)], buf.at[1 - slot],
                              sem.at[1 - slot]).wait()

    @pl.when(k == 0)
    def _():
        loop(start_rows)

    @pl.when(jnp.logical_and(k > 0, k < n_blocks))
    def _():
        loop(start_rows, unpack_rows)

    @pl.when(k == n_blocks)
    def _():
        loop(unpack_rows)


def _combine(ys_rt, dest3, n):
    n_blocks = n // ROWS_PER_STEP
    return pl.pallas_call(
        _combine_kernel,
        grid=(n_blocks + 1,),
        in_specs=[
            pl.BlockSpec((1, 1, ROWS_PER_STEP), lambda k: (jnp.minimum(k, n_blocks - 1), 0, 0),
                         memory_space=pltpu.SMEM),
            pl.BlockSpec(memory_space=pl.ANY),
        ],
        out_specs=pl.BlockSpec((ROWS_PER_STEP, D_MODEL), lambda k: (jnp.maximum(k - 1, 0), 0)),
        out_shape=jax.ShapeDtypeStruct((n, D_MODEL), F32),
        scratch_shapes=[pltpu.VMEM((2, ROWS_PER_STEP * ROW_TILE, 128), F32), pltpu.SemaphoreType.DMA((2,))],
        compiler_params=pltpu.CompilerParams(
            dimension_semantics=("arbitrary",), vmem_limit_bytes=V7X_VMEM_LIMIT),
        name="combine",
    )(dest3, ys_rt)


def _moe_kernel(tg_ref, tlo_ref, thi_ref, nused_ref, hs_ref, wg_ref, wu_ref, wd_ref, wr_ref, br_ref,
                gm_ref, gfin_ref, ys_ref):
    step = pl.program_id(0)

    @pl.when(step * TILES_PER_STEP < nused_ref[0])
    def _():
        lane = lax.broadcasted_iota(I32, (TM_MOE, BUCKET_LANES), 1)
        for k in range(TILES_PER_STEP):
            t = step * TILES_PER_STEP + k
            base = k * TM_MOE * ROW_TILE
            hrows = jnp.concatenate(
                [hs_ref[pl.ds(base + cb, TM_MOE, stride=ROW_TILE), :] for cb in range(ROW_TILE)], axis=1)
            hn = _rms(hrows, gm_ref[...]).astype(BF16)
            logits = jnp.dot(hn, wr_ref[...], preferred_element_type=F32) + br_ref[...]
            g, lo, hi = tg_ref[t], tlo_ref[t], thi_ref[t]
            coarse = jnp.where(lane < N_GROUPS, logits, -jnp.inf)
            ec = jnp.exp(coarse - jnp.max(coarse, axis=-1, keepdims=True))
            pick = lambda col, v: jnp.sum(jnp.where(lane == col, v, 0.0), axis=-1, keepdims=True)
            g_w = pick(g, ec) / jnp.sum(ec, axis=-1, keepdims=True)
            f_lo = pick(8 + g * EPG + lo, logits)
            f_hi = pick(8 + g * EPG + hi, logits)
            f_max = jnp.maximum(f_lo, f_hi)
            e_lo = jnp.exp(f_lo - f_max)
            e_hi = jnp.exp(f_hi - f_max)
            den = e_lo + e_hi

            def expert(e, w):
                gate = jnp.dot(hn, wg_ref[e], preferred_element_type=F32)
                up = jnp.dot(hn, wu_ref[e], preferred_element_type=F32)
                act = (gate * jax.nn.sigmoid(gate) * up).astype(BF16)
                return w * jnp.dot(act, wd_ref[e], preferred_element_type=F32)

            y = expert(lo, (e_lo / den) * g_w) + expert(hi, (e_hi / den) * g_w)
            res = _rms(hrows + y, gfin_ref[...])
            for cb in range(ROW_TILE):
                ys_ref[pl.ds(base + cb, TM_MOE, stride=ROW_TILE), :] = res[:, cb * 128:(cb + 1) * 128]

    @pl.when(step * TILES_PER_STEP >= nused_ref[0])
    def _():
        ys_ref[...] = jnp.zeros_like(ys_ref)


def _moe(hs_rt, tile_g, tile_lo, tile_hi, n_used, wg, wu, wd, wr, br, gm, gfin):
    rows_step = TM_MOE * TILES_PER_STEP * ROW_TILE
    n_steps = hs_rt.shape[0] // rows_step
    by_group = lambda s, tg, *_: (tg[s * TILES_PER_STEP], 0, 0, 0)
    full2 = lambda a, c: pl.BlockSpec((a, c), lambda s, *_: (0, 0))

    def hs_index(s, tg, tlo, thi, nu):
        last_step = jnp.maximum(nu[0] - 1, 0) // TILES_PER_STEP
        return (jnp.minimum(s, last_step), 0)

    grid_spec = pltpu.PrefetchScalarGridSpec(
        num_scalar_prefetch=4,
        grid=(n_steps,),
        in_specs=[
            pl.BlockSpec((rows_step, 128), hs_index),
            pl.BlockSpec((None, EPG, D_MODEL, D_EXPERT), by_group),
            pl.BlockSpec((None, EPG, D_MODEL, D_EXPERT), by_group),
            pl.BlockSpec((None, EPG, D_EXPERT, D_MODEL), by_group),
            full2(D_MODEL, BUCKET_LANES), full2(1, BUCKET_LANES), full2(1, D_MODEL), full2(1, D_MODEL),
        ],
        out_specs=pl.BlockSpec((rows_step, 128), lambda s, *_: (s, 0)),
    )
    return pl.pallas_call(
        _moe_kernel,
        grid_spec=grid_spec,
        out_shape=jax.ShapeDtypeStruct(hs_rt.shape, F32),
        compiler_params=pltpu.CompilerParams(
            dimension_semantics=("arbitrary",), vmem_limit_bytes=V7X_VMEM_LIMIT),
        name="moe",
    )(tile_g, tile_lo, tile_hi, n_used, hs_rt, wg, wu, wd, wr, br, gm, gfin)


def _bucket_plan(bucket, rank, counts, n):
    nt = n // TM_MOE + N_BUCKETS + N_GROUPS * (TILES_PER_STEP - 1)
    nt = -(-nt // TILES_PER_STEP) * TILES_PER_STEP
    tiles_b = (counts + (TM_MOE - 1)) // TM_MOE
    tiles_g = jnp.sum(tiles_b.reshape(N_GROUPS, N_PAIRS), axis=1)
    extra_g = (-tiles_g) % TILES_PER_STEP
    is_last = (np.arange(N_PAIRS) == N_PAIRS - 1)[None, :]
    tiles_b = (tiles_b.reshape(N_GROUPS, N_PAIRS) + jnp.where(is_last, extra_g[:, None], 0)).reshape(N_BUCKETS)
    tile_end = jnp.cumsum(tiles_b)
    tile_start = tile_end - tiles_b
    n_used = tile_end[-1]
    b_ids = jnp.arange(N_BUCKETS, dtype=I32)
    dest = rank + TM_MOE * jnp.sum(jnp.where(bucket[:, None] == b_ids[None, :], tile_start[None, :], 0), axis=1)
    t_idx = jnp.arange(nt, dtype=I32)
    tb = jnp.sum((tile_end[None, :] <= t_idx[:, None]).astype(I32), axis=1)
    tb_last = jnp.sum((tile_end <= n_used - 1).astype(I32))
    tb = jnp.minimum(jnp.where(t_idx < n_used, tb, tb_last), N_BUCKETS - 1)
    sel = tb[:, None] == b_ids[None, :]
    pick = lambda table: jnp.sum(jnp.where(sel, table[None, :], 0), axis=1).astype(I32)
    tile_g = tb // N_PAIRS
    tile_lo = pick(jnp.asarray(np.tile(_PAIR_LO, N_GROUPS)))
    tile_hi = pick(jnp.asarray(np.tile(_PAIR_HI, N_GROUPS)))
    nv = jnp.clip(pick(counts) - (t_idx - pick(tile_start)) * TM_MOE, 0, TM_MOE)
    tile_nv = jnp.where(t_idx < n_used, nv, 0).astype(I32)
    return dest.astype(I32), tile_g.astype(I32), tile_lo, tile_hi, tile_nv, n_used.reshape(1).astype(I32)


def kernel(x, norm_mix, w_in, rpb, w_four, b_four, g_attn_out, g_four_out, w_out, norm_moe,
           w_router_coarse, b_router_coarse, w_router_fine, b_router_fine, w_gate, w_up, w_down, norm_final):
    b, seq, d = x.shape
    assert (seq, d) == (SEQ, D_MODEL) and norm_mix.shape[0] == 1
    n = b * seq
    x2 = x.reshape(n, d)

    qkv, u = _inproj(x2, norm_mix[0][None], w_in[0].astype(BF16))

    oa = _attention(qkv.reshape(b, seq, 3 * D_ATTN), _bias_columns(rpb[0]))

    eye4 = jnp.eye(4, dtype=F32)
    wf = w_four[0].reshape(2, 4, FOUR_GROUP_DIM, FOUR_GROUP_DIM)
    wbd = (eye4[None, :, None, :, None] * wf[:, :, :, None, :]).reshape(2, 256, 256).astype(BF16)
    yf = _fourier(u, wbd, b_four[0][None]).reshape(N_CBLK, b * ROWS * GRID_PITCH, 128)

    wrt = jnp.zeros((BUCKET_LANES, d), F32)
    wrt = wrt.at[0:N_GROUPS].set(w_router_coarse[0].T).at[8:8 + N_EXPERTS].set(w_router_fine[0].T)
    br = jnp.zeros((BUCKET_LANES, 1), F32)
    br = br.at[0:N_GROUPS, 0].set(b_router_coarse[0]).at[8:8 + N_EXPERTS, 0].set(b_router_fine[0])
    tri = (np.arange(TM_OUT)[:, None] < np.arange(TM_OUT)[None, :]).astype(np.float32)
    wrt_bf = wrt.astype(BF16)
    h_rt, bucket, rank, cnt = _mixout(
        oa.reshape(n, D_ATTN), yf, x2, g_attn_out[0][None], g_four_out[0][None],
        w_out[0].astype(BF16), norm_moe[0][None], wrt_bf, br, jnp.asarray(tri, BF16))

    counts = cnt[:N_BUCKETS, 0].astype(I32)
    dest, tile_g, tile_lo, tile_hi, tile_nv, n_used = _bucket_plan(bucket.reshape(n), rank.reshape(n), counts, n)
    dest3 = dest.reshape(n // ROWS_PER_STEP, 1, ROWS_PER_STEP)
    hs_rt, wg_bf, wu_bf, wd_bf = _dispatch(h_rt, dest3, tile_nv, tile_nv.shape[0] * TM_MOE,
                                           w_gate[0], w_up[0], w_down[0])
    shape_e = (N_GROUPS, EPG)
    ys_rt = _moe(hs_rt, tile_g, tile_lo, tile_hi, n_used,
                 wg_bf.reshape(shape_e + (d, D_EXPERT)),
                 wu_bf.reshape(shape_e + (d, D_EXPERT)),
                 wd_bf.reshape(shape_e + (D_EXPERT, d)),
                 wrt_bf.T, br.T, norm_moe[0][None], norm_final[None])
    return _combine(ys_rt, dest3, n).reshape(b, seq, d)
```

```python
import functools

import numpy as np
import jax
import jax.numpy as jnp
from jax import lax
from jax.experimental import pallas as pl
from jax.experimental.pallas import tpu as pltpu

F32 = jnp.float32
BF16 = jnp.bfloat16
I32 = jnp.int32

D_MODEL = 1024
SEQ = 4096
GRID_W = 64
ROWS = SEQ // GRID_W
D_ATTN = 512
D_FOUR = 512
N_HEADS = 8
HEAD_DIM = 64
WIN_H = 8
WIN_W = 16
N_FOUR_GROUPS = 8
FOUR_GROUP_DIM = 64
D_PROJ = 3 * D_ATTN + D_FOUR
N_GROUPS = 4
EPG = 8
N_EXPERTS = N_GROUPS * EPG
D_EXPERT = 256
EPS = 1e-6
NEG = -1e30

V7X_VMEM_LIMIT = 56 * 1024 * 1024

TM_IN = 1024
TM_OUT = 512
OUT_SUBTILES = 2
TM_MOE = 128
ROW_TILE = D_MODEL // 128
ROWS_PER_STEP = 2048
TILES_PER_STEP = 4
N_PAIRS = EPG * (EPG - 1) // 2
N_BUCKETS = N_GROUPS * N_PAIRS
BUCKET_LANES = 128

QB_ROWS = 8
QB_COLS = 16
KB_ROWS = 16
KB_COLS = 32
GRID_PITCH = 72


def _rms(x, g):
    ms = jnp.mean(x * x, axis=-1, keepdims=True)
    return x * lax.rsqrt(ms + EPS) * g


def _inproj_kernel(x_ref, g_ref, w_ref, qkv_ref, u_ref):
    xn = _rms(x_ref[...], g_ref[...]).astype(BF16)
    p = jnp.dot(xn, w_ref[...], preferred_element_type=F32)
    qkv_ref[:, :D_ATTN] = (p[:, :D_ATTN] * (HEAD_DIM ** -0.5)).astype(BF16)
    qkv_ref[:, D_ATTN:] = p[:, D_ATTN:3 * D_ATTN].astype(BF16)
    for cb in range(D_FOUR // 128):
        lanes = slice(3 * D_ATTN + cb * 128, 3 * D_ATTN + (cb + 1) * 128)
        for r in range(TM_IN // GRID_W):
            u_ref[cb, r * GRID_PITCH:r * GRID_PITCH + GRID_W, :] = p[r * GRID_W:(r + 1) * GRID_W, lanes]
            u_ref[cb, r * GRID_PITCH + GRID_W:(r + 1) * GRID_PITCH, :] = jnp.zeros((GRID_PITCH - GRID_W, 128), F32)


def _inproj(x2, g, w_bf):
    n = x2.shape[0]
    rows_step = TM_IN // GRID_W
    steps_b = ROWS // rows_step
    return pl.pallas_call(
        _inproj_kernel,
        grid=(n // TM_IN,),
        in_specs=[
            pl.BlockSpec((TM_IN, D_MODEL), lambda i: (i, 0)),
            pl.BlockSpec((1, D_MODEL), lambda i: (0, 0)),
            pl.BlockSpec((D_MODEL, D_PROJ), lambda i: (0, 0)),
        ],
        out_specs=[
            pl.BlockSpec((TM_IN, 3 * D_ATTN), lambda i: (i, 0)),
            pl.BlockSpec((D_FOUR // 128, None, rows_step * GRID_PITCH, 128),
                         lambda i: (0, i // steps_b, i % steps_b, 0)),
        ],
        out_shape=[
            jax.ShapeDtypeStruct((n, 3 * D_ATTN), BF16),
            jax.ShapeDtypeStruct((D_FOUR // 128, n // SEQ, ROWS * GRID_PITCH, 128), F32),
        ],
        compiler_params=pltpu.CompilerParams(
            dimension_semantics=("parallel",), vmem_limit_bytes=V7X_VMEM_LIMIT),
        name="inproj",
    )(x2, g, w_bf)


_KCOL_START = (0, 8, 24, 32)
_KCOL_SHIFTED = (False, True, True, False)
_KCOL_OFF = (0, 0, 16, 32)
_COL_TYPE = (0, 1, 1, 2)


def _bias_index_tables():
    dr = np.zeros((9, 128, 512), np.int32)
    dc = np.zeros((9, 128, 512), np.int32)
    ok = np.zeros((9, 128, 512), bool)
    qi, qc = np.divmod(np.arange(128), QB_COLS)
    ki, kc = np.divmod(np.arange(512), KB_COLS)
    for rt, (q0, k0) in enumerate(((0, 0), (8, 4), (56, 48))):
        qrow = q0 + qi
        krow = k0 + ki
        rs = np.clip(qrow - WIN_H // 2, 0, ROWS - WIN_H)
        rok = (krow[None, :] >= rs[:, None]) & (krow[None, :] < rs[:, None] + WIN_H)
        drr = krow[None, :] - qrow[:, None] + (WIN_H - 1)
        for ct, (c0, kc0) in enumerate(((0, 0), (16, 8), (48, 32))):
            qcol = c0 + qc
            kcol = kc0 + kc
            cs = np.clip(qcol - WIN_W // 2, 0, GRID_W - WIN_W)
            cok = (kcol[None, :] >= cs[:, None]) & (kcol[None, :] < cs[:, None] + WIN_W)
            dcc = kcol[None, :] - qcol[:, None] + (WIN_W - 1)
            t = rt * 3 + ct
            ok[t] = rok & cok
            dr[t] = np.where(ok[t], drr, 0)
            dc[t] = np.where(ok[t], dcc, 0)
    return dr, dc, ok


_BIAS_DR, _BIAS_DC, _BIAS_OK = _bias_index_tables()


def _bias_selectors():
    ok = _BIAS_OK.reshape(3, 3, QB_ROWS, QB_COLS, KB_ROWS, KB_COLS)
    dr = _BIAS_DR.reshape(ok.shape)
    dc = _BIAS_DC.reshape(ok.shape)
    row_ok = ok.any(axis=(1, 3, 5))
    col_ok = ok.any(axis=(0, 2, 4))
    dr_r = dr.max(axis=(1, 3, 5))
    dc_c = dc.max(axis=(0, 2, 4))
    sc = (np.arange(2 * WIN_W - 1)[:, None, None, None] == dc_c[None]) & col_ok[None]
    return row_ok, dr_r, col_ok, sc.astype(np.float32)


_BIAS_ROW_OK, _BIAS_ROW_DR, _BIAS_COL_OK, _BIAS_SC = _bias_selectors()


def _bias_columns(rpb):
    sc = jnp.asarray(np.tile(_BIAS_SC, 128 // KB_COLS))
    ok = np.tile(_BIAS_COL_OK, 128 // KB_COLS)
    t1 = jnp.sum(rpb[:, :, :, None, None, None] * sc[None, None], axis=2)
    return jnp.where(ok[None, None], t1, NEG)


def _attn_kernel(q_ref, k_ref, v_ref, bcol_ref, o_ref, ksh_ref, vsh_ref, bias_ref, s_ref, p_ref, l_ref):
    @pl.when(pl.program_id(1) == 0)
    def _():
        key_row = lax.broadcasted_iota(I32, (QB_COLS, KB_ROWS * KB_COLS), 1) // KB_COLS
        for hh in range(2):
            for rt in range(3):
                for ct in range(3):
                    for i in range(QB_ROWS):
                        acc = jnp.full((QB_COLS, KB_ROWS * KB_COLS), NEG, F32)
                        for y in range(KB_ROWS):
                            if _BIAS_ROW_OK[rt, i, y]:
                                cols = bcol_ref[hh, int(_BIAS_ROW_DR[rt, i, y]), ct]
                                cols = jnp.concatenate([cols] * (KB_ROWS * KB_COLS // 128), axis=1)
                                acc = jnp.where(key_row == y, cols, acc)
                        bias_ref[rt * 3 + ct, pl.ds(hh * 128 + i * QB_COLS, QB_COLS), :] = acc

    zpad = jnp.zeros((8, 128), F32)
    ksh_ref[...] = jnp.concatenate([k_ref[...].astype(F32)[8:], zpad], axis=0).astype(BF16)
    vsh_ref[...] = jnp.concatenate([v_ref[...].astype(F32)[8:], zpad], axis=0).astype(BF16)
    lane = lax.broadcasted_iota(I32, (1, 128), 1)
    head_masks = (lane < HEAD_DIM, lane >= HEAD_DIM)

    n_rb = ROWS // QB_ROWS
    n_q = GRID_W // QB_COLS

    def window(ref_plain, ref_shift, rb, j):
        ks = jnp.clip(QB_ROWS * rb - WIN_H // 2, 0, ROWS - KB_ROWS)
        src = ref_shift if _KCOL_SHIFTED[j] else ref_plain
        return jnp.concatenate(
            [src[pl.ds(pl.multiple_of((ks + i) * GRID_W + _KCOL_OFF[j], 16), KB_COLS), :] for i in range(KB_ROWS)],
            axis=0)

    def scores(rb, j):
        q = jnp.concatenate(
            [q_ref[pl.ds(pl.multiple_of((QB_ROWS * rb + i) * GRID_W + QB_COLS * j, 16), QB_COLS), :]
             for i in range(QB_ROWS)], axis=0)
        qm = jnp.concatenate([jnp.where(hm, q, jnp.zeros_like(q)) for hm in head_masks], axis=0)
        s_ref[j] = lax.dot_general(qm, window(k_ref, ksh_ref, rb, j), (((1,), (1,)), ((), ())),
                                   preferred_element_type=F32)

    def softmax(rb, j):
        rt = jnp.where(rb == 0, 0, jnp.where(rb == n_rb - 1, 2, 1))
        s = s_ref[j] + bias_ref[rt * 3 + _COL_TYPE[j]]
        e = jnp.exp(s - jnp.max(s, axis=-1, keepdims=True))
        p_ref[j] = e.astype(BF16)
        l_ref[j] = jnp.broadcast_to(jnp.sum(e, axis=-1, keepdims=True), (2 * QB_ROWS * QB_COLS, 128))

    def values(rb, j):
        o = jnp.dot(p_ref[j], window(v_ref, vsh_ref, rb, j), preferred_element_type=F32) / l_ref[j]
        out = jnp.where(head_masks[0], o[:128], o[128:]).astype(BF16)
        for i in range(QB_ROWS):
            o_ref[pl.ds(pl.multiple_of((QB_ROWS * rb + i) * GRID_W + QB_COLS * j, 16), QB_COLS), :] = (
                out[QB_COLS * i:QB_COLS * (i + 1)])

    def stage(fn, rb):
        for j in range(n_q):
            fn(jnp.asarray(rb, I32), j)

    stage(scores, 0)
    stage(softmax, 0)
    stage(scores, 1)

    def pipeline_step(i, carry):
        stage(values, i - 2)
        stage(softmax, i - 1)
        stage(scores, i)
        return carry

    lax.fori_loop(2, n_rb, pipeline_step, 0)
    stage(values, n_rb - 2)
    stage(softmax, n_rb - 1)
    stage(values, n_rb - 1)


def _attention(qkv3, bias_cols):
    b = qkv3.shape[0]
    n_hp = N_HEADS // 2
    blk = lambda off: pl.BlockSpec((None, SEQ, 128), lambda hp, bi: (bi, 0, off + hp))
    return pl.pallas_call(
        _attn_kernel,
        grid=(n_hp, b),
        in_specs=[
            blk(0), blk(n_hp), blk(2 * n_hp),
            pl.BlockSpec((2,) + bias_cols.shape[1:], lambda hp, bi: (hp, 0, 0, 0, 0)),
        ],
        out_specs=pl.BlockSpec((None, SEQ, 128), lambda hp, bi: (bi, 0, hp)),
        out_shape=jax.ShapeDtypeStruct((b, SEQ, D_ATTN), BF16),
        scratch_shapes=[pltpu.VMEM((SEQ, 128), BF16), pltpu.VMEM((SEQ, 128), BF16),
                        pltpu.VMEM((9, 2 * QB_ROWS * QB_COLS, KB_ROWS * KB_COLS), F32),
                        pltpu.VMEM((GRID_W // QB_COLS, 2 * QB_ROWS * QB_COLS, KB_ROWS * KB_COLS), F32),
                        pltpu.VMEM((GRID_W // QB_COLS, 2 * QB_ROWS * QB_COLS, KB_ROWS * KB_COLS), BF16),
                        pltpu.VMEM((GRID_W // QB_COLS, 2 * QB_ROWS * QB_COLS, 128), F32)],
        compiler_params=pltpu.CompilerParams(
            dimension_semantics=("arbitrary", "arbitrary"), vmem_limit_bytes=V7X_VMEM_LIMIT),
        name="nattn",
    )(qkv3, qkv3, qkv3, bias_cols)


Z_PITCH = 72
N_CBLK = D_FOUR // 128


def _fourier_tables():
    n = 64
    k = np.arange(n)
    ang = 2.0 * np.pi * np.outer(k, k) / n
    c64, s64 = np.cos(ang), np.sin(ang)
    w1 = np.concatenate([c64, -s64], axis=0)
    t1p = np.arange(n)[:, None, None]
    t2p = np.arange(n)[None, :, None]
    t2 = np.arange(n)[None, None, :]
    th = 2.0 * np.pi * ((t2 * (t1p + n * t2p)) % SEQ) / SEQ
    cc, ss = np.cos(th), np.sin(th)
    m2 = np.concatenate([np.concatenate([cc, ss], axis=2), np.concatenate([-ss, cc], axis=2)], axis=1)
    cbd = np.kron(np.eye(4), c64)
    sbd = np.kron(np.eye(4), s64)
    cs = np.concatenate([cbd, sbd], axis=0)
    return w1.astype(np.float32), m2.astype(np.float32), cs.astype(np.float32)


_W1_NP, _M2_NP, _CS_NP = _fourier_tables()


HALF_CBLK = N_CBLK // 2
T2_UNROLL = 16


def _fourier_kernel(u_ref, w1_ref, m2_ref, cs_ref, wbd_ref, bf_ref, y_ref, zs_ref):
    for cb in range(HALF_CBLK):
        for k in range(GRID_PITCH - GRID_W):
            y_ref[cb, pl.ds(GRID_W + k, ROWS, stride=GRID_PITCH), :] = jnp.zeros((ROWS, 128), F32)

    def dft_cols(i, carry):
        for k in range(T2_UNROLL):
            t2 = i * T2_UNROLL + k
            x = jnp.concatenate([u_ref[cb, pl.ds(t2, ROWS, stride=GRID_PITCH), :] for cb in range(HALF_CBLK)],
                                axis=1).astype(BF16)
            z = jnp.dot(w1_ref[...], x, preferred_element_type=F32)
            for cb in range(HALF_CBLK):
                zs_ref[cb, pl.ds(t2, 128, stride=Z_PITCH), :] = z[:, cb * 128:(cb + 1) * 128]
        return carry

    lax.fori_loop(0, GRID_W // T2_UNROLL, dft_cols, 0)

    def dft_rows(a, carry):
        xs = []
        for jo in range(8):
            t1p = a * 8 + jo
            rhs = jnp.concatenate(
                [jnp.concatenate([zs_ref[cb, pl.ds(pl.multiple_of((part * 64 + t1p) * Z_PITCH, 8), 64), :]
                                  for cb in range(HALF_CBLK)], axis=1) for part in range(2)], axis=0)
            xs.append(jnp.dot(m2_ref[t1p], rhs.astype(BF16), preferred_element_type=F32))
        xr = jnp.concatenate([x[:64] for x in xs], axis=0).astype(BF16)
        xi = jnp.concatenate([x[64:] for x in xs], axis=0).astype(BF16)
        lhs = jnp.concatenate([xr, xi], axis=1)
        f = jnp.dot(lhs, cs_ref[...], preferred_element_type=F32) * (1.0 / 512.0)
        y = jnp.dot(f.astype(BF16), wbd_ref[...], preferred_element_type=F32) + bf_ref[...]
        for jo in range(8):
            t1p = a * 8 + jo
            for cb in range(HALF_CBLK):
                y_ref[cb, pl.ds(t1p, ROWS, stride=GRID_PITCH), :] = y[jo * 64:(jo + 1) * 64, cb * 128:(cb + 1) * 128]
        return carry

    lax.fori_loop(0, GRID_W // 8, dft_rows, 0)


def _fourier(u_p, wbd, bf):
    b = u_p.shape[1]
    w1 = jnp.asarray(_W1_NP).astype(BF16)
    m2 = jnp.asarray(_M2_NP).astype(BF16)
    cs = jnp.asarray(_CS_NP).astype(BF16)
    half_blk = pl.BlockSpec((HALF_CBLK, None, ROWS * GRID_PITCH, 128), lambda bi, hf: (hf, bi, 0, 0))
    return pl.pallas_call(
        _fourier_kernel,
        grid=(b, 2),
        in_specs=[
            half_blk,
            pl.BlockSpec((128, 64), lambda bi, hf: (0, 0)),
            pl.BlockSpec((64, 128, 128), lambda bi, hf: (0, 0, 0)),
            pl.BlockSpec((512, 256), lambda bi, hf: (0, 0)),
            pl.BlockSpec((None, 256, 256), lambda bi, hf: (hf, 0, 0)),
            pl.BlockSpec((1, 256), lambda bi, hf: (0, hf)),
        ],
        out_specs=half_blk,
        out_shape=jax.ShapeDtypeStruct(u_p.shape, F32),
        scratch_shapes=[pltpu.VMEM((HALF_CBLK, 128 * Z_PITCH, 128), F32)],
        compiler_params=pltpu.CompilerParams(
            dimension_semantics=("parallel", "parallel"), vmem_limit_bytes=V7X_VMEM_LIMIT),
        name="fourier",
    )(u_p, w1, m2, cs, wbd, bf)


def _mixout_kernel(oa_ref, yf_ref, x_ref, ga_ref, gf_ref, wout_ref, gm_ref, wrt_ref, br_ref, tri_ref,
                   h_ref, bucket_ref, rank_ref, cnt_ref, carry_ref):
    i = pl.program_id(0)

    @pl.when(i == 0)
    def _():
        carry_ref[...] = jnp.zeros_like(carry_ref)

    carry = carry_ref[...]
    for k in range(OUT_SUBTILES):
        carry = _mixout_subtile(k, carry, oa_ref, yf_ref, x_ref, ga_ref, gf_ref, wout_ref, gm_ref, wrt_ref,
                                br_ref, tri_ref, h_ref, bucket_ref, rank_ref)
    carry_ref[...] = carry
    cnt_ref[...] = carry


def _mixout_subtile(k, carry, oa_ref, yf_ref, x_ref, ga_ref, gf_ref, wout_ref, gm_ref, wrt_ref, br_ref, tri_ref,
                    h_ref, bucket_ref, rank_ref):
    rows_k = pl.ds(k * TM_OUT, TM_OUT)
    na = _rms(oa_ref[rows_k, :].astype(F32), ga_ref[...]).astype(BF16)
    grid_rows = [k * (TM_OUT // GRID_W) + r for r in range(TM_OUT // GRID_W)]
    yf = jnp.concatenate(
        [jnp.concatenate([yf_ref[cb, gr * GRID_PITCH:gr * GRID_PITCH + GRID_W, :] for gr in grid_rows], axis=0)
         for cb in range(N_CBLK)], axis=1)
    nf = _rms(yf, gf_ref[...]).astype(BF16)
    merged = jnp.concatenate([na, nf], axis=1)
    h = x_ref[rows_k, :] + jnp.dot(merged, wout_ref[...], preferred_element_type=F32)
    for cb in range(ROW_TILE):
        h_ref[pl.ds(k * TM_OUT * ROW_TILE + cb, TM_OUT, stride=ROW_TILE), :] = h[:, cb * 128:(cb + 1) * 128]
    hn = _rms(h, gm_ref[...]).astype(BF16)
    lt = lax.dot_general(wrt_ref[...], hn, (((1,), (1,)), ((), ())), preferred_element_type=F32)
    lt = lt + br_ref[...]
    c = [lt[k:k + 1] for k in range(N_GROUPS)]
    cmax = jnp.maximum(jnp.maximum(c[0], c[1]), jnp.maximum(c[2], c[3]))
    e = [jnp.exp(ck - cmax) for ck in c]
    esum = (e[0] + e[1]) + (e[2] + e[3])
    p = [ek / esum for ek in e]
    pmax = jnp.maximum(jnp.maximum(p[0], p[1]), jnp.maximum(p[2], p[3]))
    g = jnp.where(p[0] == pmax, 0, jnp.where(p[1] == pmax, 1, jnp.where(p[2] == pmax, 2, 3))).astype(I32)
    fine = jnp.where(g == 0, lt[8:16], jnp.where(g == 1, lt[16:24], jnp.where(g == 2, lt[24:32], lt[32:40])))
    rows = lax.broadcasted_iota(I32, fine.shape, 0)
    v1 = jnp.max(fine, axis=0, keepdims=True)
    i1 = jnp.min(jnp.where(fine == v1, rows, EPG), axis=0, keepdims=True)
    rest = jnp.where(rows == i1, -jnp.inf, fine)
    v2 = jnp.max(rest, axis=0, keepdims=True)
    i2 = jnp.min(jnp.where(rest == v2, rows, EPG), axis=0, keepdims=True)
    lo = jnp.minimum(i1, i2)
    hi = jnp.maximum(i1, i2)
    pair = lax.shift_right_logical(lo * (2 * EPG - 1 - lo), 1) + (hi - lo - 1)
    bucket = g * N_PAIRS + pair
    bucket_ref[k] = bucket
    brow = lax.broadcasted_iota(I32, (BUCKET_LANES, TM_OUT), 0)
    onehot = (brow == bucket).astype(F32)
    prefix = jnp.dot(onehot.astype(BF16), tri_ref[...], preferred_element_type=F32)
    rank = jnp.sum(onehot * (prefix + carry), axis=0, keepdims=True)
    rank_ref[k] = rank.astype(I32)
    return carry + jnp.sum(onehot, axis=1, keepdims=True)


def _mixout(oa, yf, x2, ga, gf, wout_bf, gm, wrt, br, tri):
    n = x2.shape[0]
    nt = n // TM_OUT
    rows_step = TM_OUT * OUT_SUBTILES
    full = lambda *shape: pl.BlockSpec(shape, lambda i: (0,) * len(shape))
    row3 = pl.BlockSpec((OUT_SUBTILES, 1, TM_OUT), lambda i: (i, 0, 0))
    return pl.pallas_call(
        _mixout_kernel,
        grid=(n // rows_step,),
        in_specs=[
            pl.BlockSpec((rows_step, D_ATTN), lambda i: (i, 0)),
            pl.BlockSpec((N_CBLK, rows_step // GRID_W * GRID_PITCH, 128), lambda i: (0, i, 0)),
            pl.BlockSpec((rows_step, D_MODEL), lambda i: (i, 0)),
            full(1, D_ATTN), full(1, D_FOUR), full(D_MODEL, D_MODEL), full(1, D_MODEL),
            full(BUCKET_LANES, D_MODEL), full(BUCKET_LANES, 1), full(TM_OUT, TM_OUT),
        ],
        out_specs=[
            pl.BlockSpec((rows_step * ROW_TILE, 128), lambda i: (i, 0)),
            row3, row3,
            full(BUCKET_LANES, 1),
        ],
        out_shape=[
            jax.ShapeDtypeStruct((n * ROW_TILE, 128), F32),
            jax.ShapeDtypeStruct((nt, 1, TM_OUT), I32),
            jax.ShapeDtypeStruct((nt, 1, TM_OUT), I32),
            jax.ShapeDtypeStruct((BUCKET_LANES, 1), F32),
        ],
        scratch_shapes=[pltpu.VMEM((BUCKET_LANES, 1), F32)],
        compiler_params=pltpu.CompilerParams(
            dimension_semantics=("arbitrary",), vmem_limit_bytes=V7X_VMEM_LIMIT),
        name="mixout",
    )(oa, yf, x2, ga, gf, wout_bf, gm, wrt, br, tri)


def _pair_tables():
    lo, hi = [], []
    for a in range(EPG):
        for b in range(a + 1, EPG):
            lo.append(a)
            hi.append(b)
    return np.asarray(lo, np.int32), np.asarray(hi, np.int32)


_PAIR_LO, _PAIR_HI = _pair_tables()


def _dispatch_kernel(tnv_ref, dest_ref, h_ref, wg_ref, wu_ref, wd_ref, hs_hbm, wg_out, wu_out, wd_out,
                     zbuf, zsem, sem):
    k = pl.program_id(0)
    wg_out[...] = wg_ref[...].astype(BF16)
    wu_out[...] = wu_ref[...].astype(BF16)
    wd_out[...] = wd_ref[...].astype(BF16)
    tile_rows = TM_MOE * ROW_TILE
    n_tiles = hs_hbm.shape[0] // tile_rows

    def zero_copy(t):
        return pltpu.make_async_copy(zbuf, hs_hbm.at[pl.ds(pl.multiple_of(t * tile_rows, tile_rows), tile_rows)],
                                     zsem)

    @pl.when(k == 0)
    def _():
        zbuf[...] = jnp.zeros_like(zbuf)

        def zstart(t, c):
            @pl.when(tnv_ref[t] < TM_MOE)
            def _():
                zero_copy(t).start()
            return c

        def zwait(t, c):
            @pl.when(tnv_ref[t] < TM_MOE)
            def _():
                zero_copy(t).wait()
            return c

        lax.fori_loop(0, n_tiles, zstart, 0)
        lax.fori_loop(0, n_tiles, zwait, 0)

    def rows(r8, c):
        for u in range(8):
            r = r8 * 8 + u
            dst = pl.multiple_of(dest_ref[0, 0, r] * ROW_TILE, ROW_TILE)
            pltpu.make_async_copy(h_ref.at[pl.ds(pl.multiple_of(r * ROW_TILE, ROW_TILE), ROW_TILE)],
                                  hs_hbm.at[pl.ds(dst, ROW_TILE)], sem).start(priority=u % 2)
        return c

    lax.fori_loop(0, ROWS_PER_STEP // 8, rows, 0)
    pltpu.make_async_copy(h_ref, hs_hbm.at[pl.ds(0, ROWS_PER_STEP * ROW_TILE)], sem).wait()


def _dispatch(h_rt, dest3, tile_nv, n_slots, w_gate, w_up, w_down):
    n = h_rt.shape[0] // ROW_TILE
    n_steps = n // ROWS_PER_STEP
    assert N_EXPERTS % n_steps == 0
    eps = N_EXPERTS // n_steps
    wspec = lambda shape: pl.BlockSpec((eps,) + shape, lambda k, *_: (k, 0, 0))
    grid_spec = pltpu.PrefetchScalarGridSpec(
        num_scalar_prefetch=1,
        grid=(n_steps,),
        in_specs=[
            pl.BlockSpec((1, 1, ROWS_PER_STEP), lambda k, *_: (k, 0, 0), memory_space=pltpu.SMEM),
            pl.BlockSpec((ROWS_PER_STEP * ROW_TILE, 128), lambda k, *_: (k, 0)),
            wspec((D_MODEL, D_EXPERT)), wspec((D_MODEL, D_EXPERT)), wspec((D_EXPERT, D_MODEL)),
        ],
        out_specs=[pl.BlockSpec(memory_space=pl.ANY),
                   wspec((D_MODEL, D_EXPERT)), wspec((D_MODEL, D_EXPERT)), wspec((D_EXPERT, D_MODEL))],
        scratch_shapes=[pltpu.VMEM((TM_MOE * ROW_TILE, 128), F32),
                        pltpu.SemaphoreType.DMA(()), pltpu.SemaphoreType.DMA(())],
    )
    return pl.pallas_call(
        _dispatch_kernel,
        grid_spec=grid_spec,
        out_shape=[jax.ShapeDtypeStruct((n_slots * ROW_TILE, 128), F32),
                   jax.ShapeDtypeStruct(w_gate.shape, BF16), jax.ShapeDtypeStruct(w_up.shape, BF16),
                   jax.ShapeDtypeStruct(w_down.shape, BF16)],
        compiler_params=pltpu.CompilerParams(
            dimension_semantics=("arbitrary",), vmem_limit_bytes=V7X_VMEM_LIMIT),
        name="dispatch",
    )(tile_nv, dest3, h_rt, w_gate, w_up, w_down)


def _combine_kernel(dest_ref, ys_hbm, o_ref, buf, sem):
    k = pl.program_id(0)
    n_blocks = pl.num_programs(0) - 1
    slot = k % 2

    def start_rows(r8):
        for u in range(8):
            r = r8 * 8 + u
            src = pl.multiple_of(dest_ref[0, 0, r] * ROW_TILE, ROW_TILE)
            pltpu.make_async_copy(ys_hbm.at[pl.ds(src, ROW_TILE)],
                                  buf.at[slot, pl.ds(pl.multiple_of(r * ROW_TILE, ROW_TILE), ROW_TILE)],
                                  sem.at[slot]).start(priority=u % 2)

    def unpack_rows(r8):
        base = pl.multiple_of(r8 * (8 * ROW_TILE), 8 * ROW_TILE)
        for cb in range(ROW_TILE):
            o_ref[pl.ds(pl.multiple_of(r8 * 8, 8), 8), cb * 128:(cb + 1) * 128] = (
                buf[1 - slot, pl.ds(base + cb, 8, stride=ROW_TILE), :])

    def loop(*parts):
        def body(r8, c):
            for part in parts:
                part(r8)
            return c
        lax.fori_loop(0, ROWS_PER_STEP // 8, body, 0)

    @pl.when(k > 0)
    def _():
        pltpu.make_async_copy(ys_hbm.at[pl.ds(0, ROWS_PER_STEP * ROW_TILE)], buf.at[1 - slot],
                              sem.at[1 - slot]).wait()

    @pl.when(k == 0)
    def _():
        loop(start_rows)

    @pl.when(jnp.logical_and(k > 0, k < n_blocks))
    def _():
        loop(start_rows, unpack_rows)

    @pl.when(k == n_blocks)
    def _():
        loop(unpack_rows)


def _combine(ys_rt, dest3, n):
    n_blocks = n // ROWS_PER_STEP
    return pl.pallas_call(
        _combine_kernel,
        grid=(n_blocks + 1,),
        in_specs=[
            pl.BlockSpec((1, 1, ROWS_PER_STEP), lambda k: (jnp.minimum(k, n_blocks - 1), 0, 0),
                         memory_space=pltpu.SMEM),
            pl.BlockSpec(memory_space=pl.ANY),
        ],
        out_specs=pl.BlockSpec((ROWS_PER_STEP, D_MODEL), lambda k: (jnp.maximum(k - 1, 0), 0)),
        out_shape=jax.ShapeDtypeStruct((n, D_MODEL), F32),
        scratch_shapes=[pltpu.VMEM((2, ROWS_PER_STEP * ROW_TILE, 128), F32), pltpu.SemaphoreType.DMA((2,))],
        compiler_params=pltpu.CompilerParams(
            dimension_semantics=("arbitrary",), vmem_limit_bytes=V7X_VMEM_LIMIT),
        name="combine",
    )(dest3, ys_rt)


def _moe_kernel(tg_ref, tlo_ref, thi_ref, nused_ref, hs_ref, wg_ref, wu_ref, wd_ref, wr_ref, br_ref,
                gm_ref, gfin_ref, ys_ref):
    step = pl.program_id(0)

    def tiles(t, k, n_tiles):
        rows = n_tiles * TM_MOE
        base = k * TM_MOE * ROW_TILE
        lane = lax.broadcasted_iota(I32, (rows, BUCKET_LANES), 1)
        hrows = jnp.concatenate(
            [hs_ref[pl.ds(base + cb, rows, stride=ROW_TILE), :] for cb in range(ROW_TILE)], axis=1)
        hn = _rms(hrows, gm_ref[...]).astype(BF16)
        logits = jnp.dot(hn, wr_ref[...], preferred_element_type=F32) + br_ref[...]
        g, lo, hi = tg_ref[t], tlo_ref[t], thi_ref[t]
        coarse = jnp.where(lane < N_GROUPS, logits, -jnp.inf)
        ec = jnp.exp(coarse - jnp.max(coarse, axis=-1, keepdims=True))
        pick = lambda col, v: jnp.sum(jnp.where(lane == col, v, 0.0), axis=-1, keepdims=True)
        g_w = pick(g, ec) / jnp.sum(ec, axis=-1, keepdims=True)
        f_lo = pick(8 + g * EPG + lo, logits)
        f_hi = pick(8 + g * EPG + hi, logits)
        f_max = jnp.maximum(f_lo, f_hi)
        e_lo = jnp.exp(f_lo - f_max)
        e_hi = jnp.exp(f_hi - f_max)
        den = e_lo + e_hi

        def expert(e, w):
            gate = jnp.dot(hn, wg_ref[e], preferred_element_type=F32)
            up = jnp.dot(hn, wu_ref[e], preferred_element_type=F32)
            act = (gate * jax.nn.sigmoid(gate) * up).astype(BF16)
            return w * jnp.dot(act, wd_ref[e], preferred_element_type=F32)

        y = expert(lo, (e_lo / den) * g_w) + expert(hi, (e_hi / den) * g_w)
        res = _rms(hrows + y, gfin_ref[...])
        for cb in range(ROW_TILE):
            ys_ref[pl.ds(base + cb, rows, stride=ROW_TILE), :] = res[:, cb * 128:(cb + 1) * 128]

    @pl.when(step * TILES_PER_STEP < nused_ref[0])
    def _():
        for p in range(TILES_PER_STEP // 2):
            t = step * TILES_PER_STEP + 2 * p
            same = jnp.logical_and(tlo_ref[t] == tlo_ref[t + 1], thi_ref[t] == thi_ref[t + 1])

            @pl.when(same)
            def _():
                tiles(t, 2 * p, 2)

            @pl.when(jnp.logical_not(same))
            def _():
                tiles(t, 2 * p, 1)
                tiles(t + 1, 2 * p + 1, 1)

    @pl.when(step * TILES_PER_STEP >= nused_ref[0])
    def _():
        ys_ref[...] = jnp.zeros_like(ys_ref)


def _moe(hs_rt, tile_g, tile_lo, tile_hi, n_used, wg, wu, wd, wr, br, gm, gfin):
    rows_step = TM_MOE * TILES_PER_STEP * ROW_TILE
    n_steps = hs_rt.shape[0] // rows_step
    by_group = lambda s, tg, *_: (tg[s * TILES_PER_STEP], 0, 0, 0)
    full2 = lambda a, c: pl.BlockSpec((a, c), lambda s, *_: (0, 0))

    def hs_index(s, tg, tlo, thi, nu):
        last_step = jnp.maximum(nu[0] - 1, 0) // TILES_PER_STEP
        return (jnp.minimum(s, last_step), 0)

    grid_spec = pltpu.PrefetchScalarGridSpec(
        num_scalar_prefetch=4,
        grid=(n_steps,),
        in_specs=[
            pl.BlockSpec((rows_step, 128), hs_index),
            pl.BlockSpec((None, EPG, D_MODEL, D_EXPERT), by_group),
            pl.BlockSpec((None, EPG, D_MODEL, D_EXPERT), by_group),
            pl.BlockSpec((None, EPG, D_EXPERT, D_MODEL), by_group),
            full2(D_MODEL, BUCKET_LANES), full2(1, BUCKET_LANES), full2(1, D_MODEL), full2(1, D_MODEL),
        ],
        out_specs=pl.BlockSpec((rows_step, 128), lambda s, *_: (s, 0)),
    )
    return pl.pallas_call(
        _moe_kernel,
        grid_spec=grid_spec,
        out_shape=jax.ShapeDtypeStruct(hs_rt.shape, F32),
        compiler_params=pltpu.CompilerParams(
            dimension_semantics=("arbitrary",), vmem_limit_bytes=V7X_VMEM_LIMIT),
        name="moe",
    )(tile_g, tile_lo, tile_hi, n_used, hs_rt, wg, wu, wd, wr, br, gm, gfin)


def _bucket_plan(bucket, rank, counts, n):
    nt = n // TM_MOE + N_BUCKETS + N_GROUPS * (TILES_PER_STEP - 1)
    nt = -(-nt // TILES_PER_STEP) * TILES_PER_STEP
    tiles_b = (counts + (TM_MOE - 1)) // TM_MOE
    tiles_g = jnp.sum(tiles_b.reshape(N_GROUPS, N_PAIRS), axis=1)
    extra_g = (-tiles_g) % TILES_PER_STEP
    is_last = (np.arange(N_PAIRS) == N_PAIRS - 1)[None, :]
    tiles_b = (tiles_b.reshape(N_GROUPS, N_PAIRS) + jnp.where(is_last, extra_g[:, None], 0)).reshape(N_BUCKETS)
    tile_end = jnp.cumsum(tiles_b)
    tile_start = tile_end - tiles_b
    n_used = tile_end[-1]
    b_ids = jnp.arange(N_BUCKETS, dtype=I32)
    dest = rank + TM_MOE * jnp.sum(jnp.where(bucket[:, None] == b_ids[None, :], tile_start[None, :], 0), axis=1)
    t_idx = jnp.arange(nt, dtype=I32)
    tb = jnp.sum((tile_end[None, :] <= t_idx[:, None]).astype(I32), axis=1)
    tb_last = jnp.sum((tile_end <= n_used - 1).astype(I32))
    tb = jnp.minimum(jnp.where(t_idx < n_used, tb, tb_last), N_BUCKETS - 1)
    sel = tb[:, None] == b_ids[None, :]
    pick = lambda table: jnp.sum(jnp.where(sel, table[None, :], 0), axis=1).astype(I32)
    tile_g = tb // N_PAIRS
    tile_lo = pick(jnp.asarray(np.tile(_PAIR_LO, N_GROUPS)))
    tile_hi = pick(jnp.asarray(np.tile(_PAIR_HI, N_GROUPS)))
    nv = jnp.clip(pick(counts) - (t_idx - pick(tile_start)) * TM_MOE, 0, TM_MOE)
    tile_nv = jnp.where(t_idx < n_used, nv, 0).astype(I32)
    return dest.astype(I32), tile_g.astype(I32), tile_lo, tile_hi, tile_nv, n_used.reshape(1).astype(I32)


def kernel(x, norm_mix, w_in, rpb, w_four, b_four, g_attn_out, g_four_out, w_out, norm_moe,
           w_router_coarse, b_router_coarse, w_router_fine, b_router_fine, w_gate, w_up, w_down, norm_final):
    b, seq, d = x.shape
    assert (seq, d) == (SEQ, D_MODEL) and norm_mix.shape[0] == 1
    n = b * seq
    x2 = x.reshape(n, d)

    qkv, u = _inproj(x2, norm_mix[0][None], w_in[0].astype(BF16))

    oa = _attention(qkv.reshape(b, seq, 3 * D_ATTN), _bias_columns(rpb[0]))

    eye4 = jnp.eye(4, dtype=F32)
    wf = w_four[0].reshape(2, 4, FOUR_GROUP_DIM, FOUR_GROUP_DIM)
    wbd = (eye4[None, :, None, :, None] * wf[:, :, :, None, :]).reshape(2, 256, 256).astype(BF16)
    yf = _fourier(u, wbd, b_four[0][None]).reshape(N_CBLK, b * ROWS * GRID_PITCH, 128)

    wrt = jnp.zeros((BUCKET_LANES, d), F32)
    wrt = wrt.at[0:N_GROUPS].set(w_router_coarse[0].T).at[8:8 + N_EXPERTS].set(w_router_fine[0].T)
    br = jnp.zeros((BUCKET_LANES, 1), F32)
    br = br.at[0:N_GROUPS, 0].set(b_router_coarse[0]).at[8:8 + N_EXPERTS, 0].set(b_router_fine[0])
    tri = (np.arange(TM_OUT)[:, None] < np.arange(TM_OUT)[None, :]).astype(np.float32)
    wrt_bf = wrt.astype(BF16)
    h_rt, bucket, rank, cnt = _mixout(
        oa.reshape(n, D_ATTN), yf, x2, g_attn_out[0][None], g_four_out[0][None],
        w_out[0].astype(BF16), norm_moe[0][None], wrt_bf, br, jnp.asarray(tri, BF16))

    counts = cnt[:N_BUCKETS, 0].astype(I32)
    dest, tile_g, tile_lo, tile_hi, tile_nv, n_used = _bucket_plan(bucket.reshape(n), rank.reshape(n), counts, n)
    dest3 = dest.reshape(n // ROWS_PER_STEP, 1, ROWS_PER_STEP)
    hs_rt, wg_bf, wu_bf, wd_bf = _dispatch(h_rt, dest3, tile_nv, tile_nv.shape[0] * TM_MOE,
                                           w_gate[0], w_up[0], w_down[0])
    shape_e = (N_GROUPS, EPG)
    ys_rt = _moe(hs_rt, tile_g, tile_lo, tile_hi, n_used,
                 wg_bf.reshape(shape_e + (d, D_EXPERT)),
                 wu_bf.reshape(shape_e + (d, D_EXPERT)),
                 wd_bf.reshape(shape_e + (D_EXPERT, d)),
                 wrt_bf.T, br.T, norm_moe[0][None], norm_final[None])
    return _combine(ys_rt, dest3, n).reshape(b, seq, d)
```

```python
import functools

import numpy as np
import jax
import jax.numpy as jnp
from jax import lax
from jax.experimental import pallas as pl
from jax.experimental.pallas import tpu as pltpu

F32 = jnp.float32
BF16 = jnp.bfloat16
I32 = jnp.int32

D_MODEL = 1024
SEQ = 4096
GRID_W = 64
ROWS = SEQ // GRID_W
D_ATTN = 512
D_FOUR = 512
N_HEADS = 8
HEAD_DIM = 64
WIN_H = 8
WIN_W = 16
N_FOUR_GROUPS = 8
FOUR_GROUP_DIM = 64
D_PROJ = 3 * D_ATTN + D_FOUR
N_GROUPS = 4
EPG = 8
N_EXPERTS = N_GROUPS * EPG
D_EXPERT = 256
EPS = 1e-6
NEG = -1e30

V7X_VMEM_LIMIT = 56 * 1024 * 1024

TM_IN = 1024
TM_OUT = 1024
OUT_SUBTILES = 1
TM_MOE = 128
ROW_TILE = D_MODEL // 128
ROWS_PER_STEP = 2048
TILES_PER_STEP = 4
N_PAIRS = EPG * (EPG - 1) // 2
N_BUCKETS = N_GROUPS * N_PAIRS
BUCKET_LANES = 128

QB_ROWS = 8
QB_COLS = 16
KB_ROWS = 16
KB_COLS = 32
GRID_PITCH = 72


def _rms(x, g):
    ms = jnp.mean(x * x, axis=-1, keepdims=True)
    return x * lax.rsqrt(ms + EPS) * g


def _inproj_kernel(x_ref, g_ref, w_ref, qkv_ref, u_ref):
    xn = _rms(x_ref[...], g_ref[...]).astype(BF16)
    p = jnp.dot(xn, w_ref[...], preferred_element_type=F32)
    qkv_ref[:, :D_ATTN] = (p[:, :D_ATTN] * (HEAD_DIM ** -0.5)).astype(BF16)
    qkv_ref[:, D_ATTN:] = p[:, D_ATTN:3 * D_ATTN].astype(BF16)
    for cb in range(D_FOUR // 128):
        lanes = slice(3 * D_ATTN + cb * 128, 3 * D_ATTN + (cb + 1) * 128)
        for r in range(TM_IN // GRID_W):
            u_ref[cb, r * GRID_PITCH:r * GRID_PITCH + GRID_W, :] = p[r * GRID_W:(r + 1) * GRID_W, lanes]
            u_ref[cb, r * GRID_PITCH + GRID_W:(r + 1) * GRID_PITCH, :] = jnp.zeros((GRID_PITCH - GRID_W, 128), F32)


def _inproj(x2, g, w_bf):
    n = x2.shape[0]
    rows_step = TM_IN // GRID_W
    steps_b = ROWS // rows_step
    return pl.pallas_call(
        _inproj_kernel,
        grid=(n // TM_IN,),
        in_specs=[
            pl.BlockSpec((TM_IN, D_MODEL), lambda i: (i, 0)),
            pl.BlockSpec((1, D_MODEL), lambda i: (0, 0)),
            pl.BlockSpec((D_MODEL, D_PROJ), lambda i: (0, 0)),
        ],
        out_specs=[
            pl.BlockSpec((TM_IN, 3 * D_ATTN), lambda i: (i, 0)),
            pl.BlockSpec((D_FOUR // 128, None, rows_step * GRID_PITCH, 128),
                         lambda i: (0, i // steps_b, i % steps_b, 0)),
        ],
        out_shape=[
            jax.ShapeDtypeStruct((n, 3 * D_ATTN), BF16),
            jax.ShapeDtypeStruct((D_FOUR // 128, n // SEQ, ROWS * GRID_PITCH, 128), F32),
        ],
        compiler_params=pltpu.CompilerParams(
            dimension_semantics=("parallel",), vmem_limit_bytes=V7X_VMEM_LIMIT),
        name="inproj",
    )(x2, g, w_bf)


_KCOL_START = (0, 8, 24, 32)
_KCOL_SHIFTED = (False, True, True, False)
_KCOL_OFF = (0, 0, 16, 32)
_COL_TYPE = (0, 1, 1, 2)


def _bias_index_tables():
    dr = np.zeros((9, 128, 512), np.int32)
    dc = np.zeros((9, 128, 512), np.int32)
    ok = np.zeros((9, 128, 512), bool)
    qi, qc = np.divmod(np.arange(128), QB_COLS)
    ki, kc = np.divmod(np.arange(512), KB_COLS)
    for rt, (q0, k0) in enumerate(((0, 0), (8, 4), (56, 48))):
        qrow = q0 + qi
        krow = k0 + ki
        rs = np.clip(qrow - WIN_H // 2, 0, ROWS - WIN_H)
        rok = (krow[None, :] >= rs[:, None]) & (krow[None, :] < rs[:, None] + WIN_H)
        drr = krow[None, :] - qrow[:, None] + (WIN_H - 1)
        for ct, (c0, kc0) in enumerate(((0, 0), (16, 8), (48, 32))):
            qcol = c0 + qc
            kcol = kc0 + kc
            cs = np.clip(qcol - WIN_W // 2, 0, GRID_W - WIN_W)
            cok = (kcol[None, :] >= cs[:, None]) & (kcol[None, :] < cs[:, None] + WIN_W)
            dcc = kcol[None, :] - qcol[:, None] + (WIN_W - 1)
            t = rt * 3 + ct
            ok[t] = rok & cok
            dr[t] = np.where(ok[t], drr, 0)
            dc[t] = np.where(ok[t], dcc, 0)
    return dr, dc, ok


_BIAS_DR, _BIAS_DC, _BIAS_OK = _bias_index_tables()


def _bias_selectors():
    ok = _BIAS_OK.reshape(3, 3, QB_ROWS, QB_COLS, KB_ROWS, KB_COLS)
    dr = _BIAS_DR.reshape(ok.shape)
    dc = _BIAS_DC.reshape(ok.shape)
    row_ok = ok.any(axis=(1, 3, 5))
    col_ok = ok.any(axis=(0, 2, 4))
    dr_r = dr.max(axis=(1, 3, 5))
    dc_c = dc.max(axis=(0, 2, 4))
    sc = (np.arange(2 * WIN_W - 1)[:, None, None, None] == dc_c[None]) & col_ok[None]
    return row_ok, dr_r, col_ok, sc.astype(np.float32)


_BIAS_ROW_OK, _BIAS_ROW_DR, _BIAS_COL_OK, _BIAS_SC = _bias_selectors()


def _bias_columns(rpb):
    sc = jnp.asarray(np.tile(_BIAS_SC, 128 // KB_COLS))
    ok = np.tile(_BIAS_COL_OK, 128 // KB_COLS)
    t1 = jnp.sum(rpb[:, :, :, None, None, None] * sc[None, None], axis=2)
    return jnp.where(ok[None, None], t1, NEG)


def _attn_kernel(q_ref, k_ref, v_ref, bcol_ref, o_ref, ksh_ref, vsh_ref, bias_ref, s_ref, p_ref, l_ref):
    @pl.when(pl.program_id(1) == 0)
    def _():
        key_row = lax.broadcasted_iota(I32, (QB_COLS, KB_ROWS * KB_COLS), 1) // KB_COLS
        for hh in range(2):
            for rt in range(3):
                for ct in range(3):
                    for i in range(QB_ROWS):
                        acc = jnp.full((QB_COLS, KB_ROWS * KB_COLS), NEG, F32)
                        for y in range(KB_ROWS):
                            if _BIAS_ROW_OK[rt, i, y]:
                                cols = bcol_ref[hh, int(_BIAS_ROW_DR[rt, i, y]), ct]
                                cols = jnp.concatenate([cols] * (KB_ROWS * KB_COLS // 128), axis=1)
                                acc = jnp.where(key_row == y, cols, acc)
                        bias_ref[rt * 3 + ct, pl.ds(hh * 128 + i * QB_COLS, QB_COLS), :] = acc

    zpad = jnp.zeros((8, 128), F32)
    ksh_ref[...] = jnp.concatenate([k_ref[...].astype(F32)[8:], zpad], axis=0).astype(BF16)
    vsh_ref[...] = jnp.concatenate([v_ref[...].astype(F32)[8:], zpad], axis=0).astype(BF16)
    lane = lax.broadcasted_iota(I32, (1, 128), 1)
    head_masks = (lane < HEAD_DIM, lane >= HEAD_DIM)

    n_rb = ROWS // QB_ROWS
    n_q = GRID_W // QB_COLS

    def window(ref_plain, ref_shift, rb, j):
        ks = jnp.clip(QB_ROWS * rb - WIN_H // 2, 0, ROWS - KB_ROWS)
        src = ref_shift if _KCOL_SHIFTED[j] else ref_plain
        return jnp.concatenate(
            [src[pl.ds(pl.multiple_of((ks + i) * GRID_W + _KCOL_OFF[j], 16), KB_COLS), :] for i in range(KB_ROWS)],
            axis=0)

    def scores(rb, j):
        q = jnp.concatenate(
            [q_ref[pl.ds(pl.multiple_of((QB_ROWS * rb + i) * GRID_W + QB_COLS * j, 16), QB_COLS), :]
             for i in range(QB_ROWS)], axis=0)
        qm = jnp.concatenate([jnp.where(hm, q, jnp.zeros_like(q)) for hm in head_masks], axis=0)
        s_ref[j] = lax.dot_general(qm, window(k_ref, ksh_ref, rb, j), (((1,), (1,)), ((), ())),
                                   preferred_element_type=F32)

    def softmax(rb, j):
        rt = jnp.where(rb == 0, 0, jnp.where(rb == n_rb - 1, 2, 1))
        s = s_ref[j] + bias_ref[rt * 3 + _COL_TYPE[j]]
        e = jnp.exp(s - jnp.max(s, axis=-1, keepdims=True))
        p_ref[j] = e.astype(BF16)
        l_ref[j] = jnp.broadcast_to(jnp.sum(e, axis=-1, keepdims=True), (2 * QB_ROWS * QB_COLS, 128))

    def values(rb, j):
        o = jnp.dot(p_ref[j], window(v_ref, vsh_ref, rb, j), preferred_element_type=F32) / l_ref[j]
        out = jnp.where(head_masks[0], o[:128], o[128:]).astype(BF16)
        for i in range(QB_ROWS):
            o_ref[pl.ds(pl.multiple_of((QB_ROWS * rb + i) * GRID_W + QB_COLS * j, 16), QB_COLS), :] = (
                out[QB_COLS * i:QB_COLS * (i + 1)])

    def stage(fn, rb):
        for j in range(n_q):
            fn(jnp.asarray(rb, I32), j)

    stage(scores, 0)
    stage(softmax, 0)
    stage(scores, 1)

    def pipeline_step(i, carry):
        stage(values, i - 2)
        stage(softmax, i - 1)
        stage(scores, i)
        return carry

    lax.fori_loop(2, n_rb, pipeline_step, 0)
    stage(values, n_rb - 2)
    stage(softmax, n_rb - 1)
    stage(values, n_rb - 1)


def _attention(qkv3, bias_cols):
    b = qkv3.shape[0]
    n_hp = N_HEADS // 2
    blk = lambda off: pl.BlockSpec((None, SEQ, 128), lambda hp, bi: (bi, 0, off + hp))
    return pl.pallas_call(
        _attn_kernel,
        grid=(n_hp, b),
        in_specs=[
            blk(0), blk(n_hp), blk(2 * n_hp),
            pl.BlockSpec((2,) + bias_cols.shape[1:], lambda hp, bi: (hp, 0, 0, 0, 0)),
        ],
        out_specs=pl.BlockSpec((None, SEQ, 128), lambda hp, bi: (bi, 0, hp)),
        out_shape=jax.ShapeDtypeStruct((b, SEQ, D_ATTN), BF16),
        scratch_shapes=[pltpu.VMEM((SEQ, 128), BF16), pltpu.VMEM((SEQ, 128), BF16),
                        pltpu.VMEM((9, 2 * QB_ROWS * QB_COLS, KB_ROWS * KB_COLS), F32),
                        pltpu.VMEM((GRID_W // QB_COLS, 2 * QB_ROWS * QB_COLS, KB_ROWS * KB_COLS), F32),
                        pltpu.VMEM((GRID_W // QB_COLS, 2 * QB_ROWS * QB_COLS, KB_ROWS * KB_COLS), BF16),
                        pltpu.VMEM((GRID_W // QB_COLS, 2 * QB_ROWS * QB_COLS, 128), F32)],
        compiler_params=pltpu.CompilerParams(
            dimension_semantics=("arbitrary", "arbitrary"), vmem_limit_bytes=V7X_VMEM_LIMIT),
        name="nattn",
    )(qkv3, qkv3, qkv3, bias_cols)


Z_PITCH = 72
N_CBLK = D_FOUR // 128


def _fourier_tables():
    n = 64
    k = np.arange(n)
    ang = 2.0 * np.pi * np.outer(k, k) / n
    c64, s64 = np.cos(ang), np.sin(ang)
    w1 = np.concatenate([c64, -s64], axis=0)
    t1p = np.arange(n)[:, None, None]
    t2p = np.arange(n)[None, :, None]
    t2 = np.arange(n)[None, None, :]
    th = 2.0 * np.pi * ((t2 * (t1p + n * t2p)) % SEQ) / SEQ
    cc, ss = np.cos(th), np.sin(th)
    m2 = np.concatenate([np.concatenate([cc, ss], axis=2), np.concatenate([-ss, cc], axis=2)], axis=1)
    cbd = np.kron(np.eye(4), c64)
    sbd = np.kron(np.eye(4), s64)
    cs = np.concatenate([cbd, sbd], axis=0)
    return w1.astype(np.float32), m2.astype(np.float32), cs.astype(np.float32)


_W1_NP, _M2_NP, _CS_NP = _fourier_tables()


HALF_CBLK = N_CBLK // 2
T2_UNROLL = 16


def _fourier_kernel(u_ref, w1_ref, m2_ref, cs_ref, wbd_ref, bf_ref, y_ref, zs_ref):
    for cb in range(HALF_CBLK):
        for k in range(GRID_PITCH - GRID_W):
            y_ref[cb, pl.ds(GRID_W + k, ROWS, stride=GRID_PITCH), :] = jnp.zeros((ROWS, 128), F32)

    def dft_cols(i, carry):
        for k in range(T2_UNROLL):
            t2 = i * T2_UNROLL + k
            x = jnp.concatenate([u_ref[cb, pl.ds(t2, ROWS, stride=GRID_PITCH), :] for cb in range(HALF_CBLK)],
                                axis=1).astype(BF16)
            z = jnp.dot(w1_ref[...], x, preferred_element_type=F32)
            for cb in range(HALF_CBLK):
                zs_ref[cb, pl.ds(t2, 128, stride=Z_PITCH), :] = z[:, cb * 128:(cb + 1) * 128]
        return carry

    lax.fori_loop(0, GRID_W // T2_UNROLL, dft_cols, 0)

    def dft_rows(a, carry):
        xs = []
        for jo in range(8):
            t1p = a * 8 + jo
            rhs = jnp.concatenate(
                [jnp.concatenate([zs_ref[cb, pl.ds(pl.multiple_of((part * 64 + t1p) * Z_PITCH, 8), 64), :]
                                  for cb in range(HALF_CBLK)], axis=1) for part in range(2)], axis=0)
            xs.append(jnp.dot(m2_ref[t1p], rhs.astype(BF16), preferred_element_type=F32))
        xr = jnp.concatenate([x[:64] for x in xs], axis=0).astype(BF16)
        xi = jnp.concatenate([x[64:] for x in xs], axis=0).astype(BF16)
        lhs = jnp.concatenate([xr, xi], axis=1)
        f = jnp.dot(lhs, cs_ref[...], preferred_element_type=F32) * (1.0 / 512.0)
        y = jnp.dot(f.astype(BF16), wbd_ref[...], preferred_element_type=F32) + bf_ref[...]
        for jo in range(8):
            t1p = a * 8 + jo
            for cb in range(HALF_CBLK):
                y_ref[cb, pl.ds(t1p, ROWS, stride=GRID_PITCH), :] = y[jo * 64:(jo + 1) * 64, cb * 128:(cb + 1) * 128]
        return carry

    lax.fori_loop(0, GRID_W // 8, dft_rows, 0)


def _fourier(u_p, wbd, bf):
    b = u_p.shape[1]
    w1 = jnp.asarray(_W1_NP).astype(BF16)
    m2 = jnp.asarray(_M2_NP).astype(BF16)
    cs = jnp.asarray(_CS_NP).astype(BF16)
    half_blk = pl.BlockSpec((HALF_CBLK, None, ROWS * GRID_PITCH, 128), lambda bi, hf: (hf, bi, 0, 0))
    return pl.pallas_call(
        _fourier_kernel,
        grid=(b, 2),
        in_specs=[
            half_blk,
            pl.BlockSpec((128, 64), lambda bi, hf: (0, 0)),
            pl.BlockSpec((64, 128, 128), lambda bi, hf: (0, 0, 0)),
            pl.BlockSpec((512, 256), lambda bi, hf: (0, 0)),
            pl.BlockSpec((None, 256, 256), lambda bi, hf: (hf, 0, 0)),
            pl.BlockSpec((1, 256), lambda bi, hf: (0, hf)),
        ],
        out_specs=half_blk,
        out_shape=jax.ShapeDtypeStruct(u_p.shape, F32),
        scratch_shapes=[pltpu.VMEM((HALF_CBLK, 128 * Z_PITCH, 128), F32)],
        compiler_params=pltpu.CompilerParams(
            dimension_semantics=("parallel", "parallel"), vmem_limit_bytes=V7X_VMEM_LIMIT),
        name="fourier",
    )(u_p, w1, m2, cs, wbd, bf)


def _mixout_kernel(oa_ref, yf_ref, x_ref, ga_ref, gf_ref, wout_ref, gm_ref, wrt_ref, br_ref, tri_ref,
                   h_ref, bucket_ref, rank_ref, cnt_ref, carry_ref):
    i = pl.program_id(0)

    @pl.when(i == 0)
    def _():
        carry_ref[...] = jnp.zeros_like(carry_ref)

    carry = carry_ref[...]
    for k in range(OUT_SUBTILES):
        carry = _mixout_subtile(k, carry, oa_ref, yf_ref, x_ref, ga_ref, gf_ref, wout_ref, gm_ref, wrt_ref,
                                br_ref, tri_ref, h_ref, bucket_ref, rank_ref)
    carry_ref[...] = carry
    cnt_ref[...] = carry


def _mixout_subtile(k, carry, oa_ref, yf_ref, x_ref, ga_ref, gf_ref, wout_ref, gm_ref, wrt_ref, br_ref, tri_ref,
                    h_ref, bucket_ref, rank_ref):
    rows_k = pl.ds(k * TM_OUT, TM_OUT)
    na = _rms(oa_ref[rows_k, :].astype(F32), ga_ref[...]).astype(BF16)
    grid_rows = [k * (TM_OUT // GRID_W) + r for r in range(TM_OUT // GRID_W)]
    yf = jnp.concatenate(
        [jnp.concatenate([yf_ref[cb, gr * GRID_PITCH:gr * GRID_PITCH + GRID_W, :] for gr in grid_rows], axis=0)
         for cb in range(N_CBLK)], axis=1)
    nf = _rms(yf, gf_ref[...]).astype(BF16)
    merged = jnp.concatenate([na, nf], axis=1)
    h = x_ref[rows_k, :] + jnp.dot(merged, wout_ref[...], preferred_element_type=F32)
    for cb in range(ROW_TILE):
        h_ref[pl.ds(k * TM_OUT * ROW_TILE + cb, TM_OUT, stride=ROW_TILE), :] = h[:, cb * 128:(cb + 1) * 128]
    hn = _rms(h, gm_ref[...]).astype(BF16)
    lt = lax.dot_general(wrt_ref[...], hn, (((1,), (1,)), ((), ())), preferred_element_type=F32)
    lt = lt + br_ref[...]
    c = [lt[k:k + 1] for k in range(N_GROUPS)]
    cmax = jnp.maximum(jnp.maximum(c[0], c[1]), jnp.maximum(c[2], c[3]))
    e = [jnp.exp(ck - cmax) for ck in c]
    esum = (e[0] + e[1]) + (e[2] + e[3])
    p = [ek / esum for ek in e]
    pmax = jnp.maximum(jnp.maximum(p[0], p[1]), jnp.maximum(p[2], p[3]))
    g = jnp.where(p[0] == pmax, 0, jnp.where(p[1] == pmax, 1, jnp.where(p[2] == pmax, 2, 3))).astype(I32)
    fine = jnp.where(g == 0, lt[8:16], jnp.where(g == 1, lt[16:24], jnp.where(g == 2, lt[24:32], lt[32:40])))
    rows = lax.broadcasted_iota(I32, fine.shape, 0)
    v1 = jnp.max(fine, axis=0, keepdims=True)
    i1 = jnp.min(jnp.where(fine == v1, rows, EPG), axis=0, keepdims=True)
    rest = jnp.where(rows == i1, -jnp.inf, fine)
    v2 = jnp.max(rest, axis=0, keepdims=True)
    i2 = jnp.min(jnp.where(rest == v2, rows, EPG), axis=0, keepdims=True)
    lo = jnp.minimum(i1, i2)
    hi = jnp.maximum(i1, i2)
    pair = lax.shift_right_logical(lo * (2 * EPG - 1 - lo), 1) + (hi - lo - 1)
    bucket = g * N_PAIRS + pair
    bucket_ref[k] = bucket
    brow = lax.broadcasted_iota(I32, (BUCKET_LANES, TM_OUT), 0)
    onehot = (brow == bucket).astype(F32)
    prefix = jnp.dot(onehot.astype(BF16), tri_ref[...], preferred_element_type=F32)
    rank = jnp.sum(onehot * (prefix + carry), axis=0, keepdims=True)
    rank_ref[k] = rank.astype(I32)
    return carry + jnp.sum(onehot, axis=1, keepdims=True)


def _mixout(oa, yf, x2, ga, gf, wout_bf, gm, wrt, br, tri):
    n = x2.shape[0]
    nt = n // TM_OUT
    rows_step = TM_OUT * OUT_SUBTILES
    full = lambda *shape: pl.BlockSpec(shape, lambda i: (0,) * len(shape))
    row3 = pl.BlockSpec((OUT_SUBTILES, 1, TM_OUT), lambda i: (i, 0, 0))
    return pl.pallas_call(
        _mixout_kernel,
        grid=(n // rows_step,),
        in_specs=[
            pl.BlockSpec((rows_step, D_ATTN), lambda i: (i, 0)),
            pl.BlockSpec((N_CBLK, rows_step // GRID_W * GRID_PITCH, 128), lambda i: (0, i, 0)),
            pl.BlockSpec((rows_step, D_MODEL), lambda i: (i, 0)),
            full(1, D_ATTN), full(1, D_FOUR), full(D_MODEL, D_MODEL), full(1, D_MODEL),
            full(BUCKET_LANES, D_MODEL), full(BUCKET_LANES, 1), full(TM_OUT, TM_OUT),
        ],
        out_specs=[
            pl.BlockSpec((rows_step * ROW_TILE, 128), lambda i: (i, 0)),
            row3, row3,
            full(BUCKET_LANES, 1),
        ],
        out_shape=[
            jax.ShapeDtypeStruct((n * ROW_TILE, 128), F32),
            jax.ShapeDtypeStruct((nt, 1, TM_OUT), I32),
            jax.ShapeDtypeStruct((nt, 1, TM_OUT), I32),
            jax.ShapeDtypeStruct((BUCKET_LANES, 1), F32),
        ],
        scratch_shapes=[pltpu.VMEM((BUCKET_LANES, 1), F32)],
        compiler_params=pltpu.CompilerParams(
            dimension_semantics=("arbitrary",), vmem_limit_bytes=V7X_VMEM_LIMIT),
        name="mixout",
    )(oa, yf, x2, ga, gf, wout_bf, gm, wrt, br, tri)


def _pair_tables():
    lo, hi = [], []
    for a in range(EPG):
        for b in range(a + 1, EPG):
            lo.append(a)
            hi.append(b)
    return np.asarray(lo, np.int32), np.asarray(hi, np.int32)


_PAIR_LO, _PAIR_HI = _pair_tables()


def _dispatch_kernel(tnv_ref, dest_ref, h_ref, wg_ref, wu_ref, wd_ref, hs_hbm, wg_out, wu_out, wd_out,
                     zbuf, zsem, sem):
    k = pl.program_id(0)
    wg_out[...] = wg_ref[...].astype(BF16)
    wu_out[...] = wu_ref[...].astype(BF16)
    wd_out[...] = wd_ref[...].astype(BF16)
    tile_rows = TM_MOE * ROW_TILE
    n_tiles = hs_hbm.shape[0] // tile_rows

    def zero_copy(t):
        return pltpu.make_async_copy(zbuf, hs_hbm.at[pl.ds(pl.multiple_of(t * tile_rows, tile_rows), tile_rows)],
                                     zsem)

    @pl.when(k == 0)
    def _():
        zbuf[...] = jnp.zeros_like(zbuf)

        def zstart(t, c):
            @pl.when(tnv_ref[t] < TM_MOE)
            def _():
                zero_copy(t).start()
            return c

        def zwait(t, c):
            @pl.when(tnv_ref[t] < TM_MOE)
            def _():
                zero_copy(t).wait()
            return c

        lax.fori_loop(0, n_tiles, zstart, 0)
        lax.fori_loop(0, n_tiles, zwait, 0)

    def rows(r8, c):
        for u in range(8):
            r = r8 * 8 + u
            dst = pl.multiple_of(dest_ref[0, 0, r] * ROW_TILE, ROW_TILE)
            pltpu.make_async_copy(h_ref.at[pl.ds(pl.multiple_of(r * ROW_TILE, ROW_TILE), ROW_TILE)],
                                  hs_hbm.at[pl.ds(dst, ROW_TILE)], sem).start(priority=u % 2)
        return c

    lax.fori_loop(0, ROWS_PER_STEP // 8, rows, 0)
    pltpu.make_async_copy(h_ref, hs_hbm.at[pl.ds(0, ROWS_PER_STEP * ROW_TILE)], sem).wait()


def _dispatch(h_rt, dest3, tile_nv, n_slots, w_gate, w_up, w_down):
    n = h_rt.shape[0] // ROW_TILE
    n_steps = n // ROWS_PER_STEP
    assert N_EXPERTS % n_steps == 0
    eps = N_EXPERTS // n_steps
    wspec = lambda shape: pl.BlockSpec((eps,) + shape, lambda k, *_: (k, 0, 0))
    grid_spec = pltpu.PrefetchScalarGridSpec(
        num_scalar_prefetch=1,
        grid=(n_steps,),
        in_specs=[
            pl.BlockSpec((1, 1, ROWS_PER_STEP), lambda k, *_: (k, 0, 0), memory_space=pltpu.SMEM),
            pl.BlockSpec((ROWS_PER_STEP * ROW_TILE, 128), lambda k, *_: (k, 0)),
            wspec((D_MODEL, D_EXPERT)), wspec((D_MODEL, D_EXPERT)), wspec((D_EXPERT, D_MODEL)),
        ],
        out_specs=[pl.BlockSpec(memory_space=pl.ANY),
                   wspec((D_MODEL, D_EXPERT)), wspec((D_MODEL, D_EXPERT)), wspec((D_EXPERT, D_MODEL))],
        scratch_shapes=[pltpu.VMEM((TM_MOE * ROW_TILE, 128), F32),
                        pltpu.SemaphoreType.DMA(()), pltpu.SemaphoreType.DMA(())],
    )
    return pl.pallas_call(
        _dispatch_kernel,
        grid_spec=grid_spec,
        out_shape=[jax.ShapeDtypeStruct((n_slots * ROW_TILE, 128), F32),
                   jax.ShapeDtypeStruct(w_gate.shape, BF16), jax.ShapeDtypeStruct(w_up.shape, BF16),
                   jax.ShapeDtypeStruct(w_down.shape, BF16)],
        compiler_params=pltpu.CompilerParams(
            dimension_semantics=("arbitrary",), vmem_limit_bytes=V7X_VMEM_LIMIT),
        name="dispatch",
    )(tile_nv, dest3, h_rt, w_gate, w_up, w_down)


def _combine_kernel(dest_ref, ys_hbm, o_ref, buf, sem):
    k = pl.program_id(0)
    n_blocks = pl.num_programs(0) - 1
    slot = k % 2

    def start_rows(r8):
        for u in range(8):
            r = r8 * 8 + u
            src = pl.multiple_of(dest_ref[0, 0, r] * ROW_TILE, ROW_TILE)
            pltpu.make_async_copy(ys_hbm.at[pl.ds(src, ROW_TILE)],
                                  buf.at[slot, pl.ds(pl.multiple_of(r * ROW_TILE, ROW_TILE), ROW_TILE)],
                                  sem.at[slot]).start(priority=u % 2)

    def unpack_rows(r8):
        base = pl.multiple_of(r8 * (8 * ROW_TILE), 8 * ROW_TILE)
        for cb in range(ROW_TILE):
            o_ref[pl.ds(pl.multiple_of(r8 * 8, 8), 8), cb * 128:(cb + 1) * 128] = (
                buf[1 - slot, pl.ds(base + cb, 8, stride=ROW_TILE), :])

    def loop(*parts):
        def body(r8, c):
            for part in parts:
                part(r8)
            return c
        lax.fori_loop(0, ROWS_PER_STEP // 8, body, 0)

    @pl.when(k > 0)
    def _():
        pltpu.make_async_copy(ys_hbm.at[pl.ds(0, ROWS_PER_STEP * ROW_TILE)], buf.at[1 - slot],
                              sem.at[1 - slot]).wait()

    @pl.when(k == 0)
    def _():
        loop(start_rows)

    @pl.when(jnp.logical_and(k > 0, k < n_blocks))
    def _():
        loop(start_rows, unpack_rows)

    @pl.when(k == n_blocks)
    def _():
        loop(unpack_rows)


def _combine(ys_rt, dest3, n):
    n_blocks = n // ROWS_PER_STEP
    return pl.pallas_call(
        _combine_kernel,
        grid=(n_blocks + 1,),
        in_specs=[
            pl.BlockSpec((1, 1, ROWS_PER_STEP), lambda k: (jnp.minimum(k, n_blocks - 1), 0, 0),
                         memory_space=pltpu.SMEM),
            pl.BlockSpec(memory_space=pl.ANY),
        ],
        out_specs=pl.BlockSpec((ROWS_PER_STEP, D_MODEL), lambda k: (jnp.maximum(k - 1, 0), 0)),
        out_shape=jax.ShapeDtypeStruct((n, D_MODEL), F32),
        scratch_shapes=[pltpu.VMEM((2, ROWS_PER_STEP * ROW_TILE, 128), F32), pltpu.SemaphoreType.DMA((2,))],
        compiler_params=pltpu.CompilerParams(
            dimension_semantics=("arbitrary",), vmem_limit_bytes=V7X_VMEM_LIMIT),
        name="combine",
    )(dest3, ys_rt)


def _moe_kernel(tg_ref, tlo_ref, thi_ref, nused_ref, hs_ref, wg_ref, wu_ref, wd_ref, wr_ref, br_ref,
                gm_ref, gfin_ref, ys_ref):
    step = pl.program_id(0)

    def tiles(t, k, n_tiles):
        rows = n_tiles * TM_MOE
        base = k * TM_MOE * ROW_TILE
        lane = lax.broadcasted_iota(I32, (rows, BUCKET_LANES), 1)
        hrows = jnp.concatenate(
            [hs_ref[pl.ds(base + cb, rows, stride=ROW_TILE), :] for cb in range(ROW_TILE)], axis=1)
        hn = _rms(hrows, gm_ref[...]).astype(BF16)
        logits = jnp.dot(hn, wr_ref[...], preferred_element_type=F32) + br_ref[...]
        g, lo, hi = tg_ref[t], tlo_ref[t], thi_ref[t]
        coarse = jnp.where(lane < N_GROUPS, logits, -jnp.inf)
        ec = jnp.exp(coarse - jnp.max(coarse, axis=-1, keepdims=True))
        pick = lambda col, v: jnp.sum(jnp.where(lane == col, v, 0.0), axis=-1, keepdims=True)
        g_w = pick(g, ec) / jnp.sum(ec, axis=-1, keepdims=True)
        f_lo = pick(8 + g * EPG + lo, logits)
        f_hi = pick(8 + g * EPG + hi, logits)
        f_max = jnp.maximum(f_lo, f_hi)
        e_lo = jnp.exp(f_lo - f_max)
        e_hi = jnp.exp(f_hi - f_max)
        den = e_lo + e_hi

        def expert(e, w):
            gate = jnp.dot(hn, wg_ref[e], preferred_element_type=F32)
            up = jnp.dot(hn, wu_ref[e], preferred_element_type=F32)
            act = (gate * jax.nn.sigmoid(gate) * up).astype(BF16)
            return w * jnp.dot(act, wd_ref[e], preferred_element_type=F32)

        y = expert(lo, (e_lo / den) * g_w) + expert(hi, (e_hi / den) * g_w)
        res = _rms(hrows + y, gfin_ref[...])
        for cb in range(ROW_TILE):
            ys_ref[pl.ds(base + cb, rows, stride=ROW_TILE), :] = res[:, cb * 128:(cb + 1) * 128]

    @pl.when(step * TILES_PER_STEP < nused_ref[0])
    def _():
        for p in range(TILES_PER_STEP // 2):
            t = step * TILES_PER_STEP + 2 * p
            same = jnp.logical_and(tlo_ref[t] == tlo_ref[t + 1], thi_ref[t] == thi_ref[t + 1])

            @pl.when(same)
            def _():
                tiles(t, 2 * p, 2)

            @pl.when(jnp.logical_not(same))
            def _():
                tiles(t, 2 * p, 1)
                tiles(t + 1, 2 * p + 1, 1)

    @pl.when(step * TILES_PER_STEP >= nused_ref[0])
    def _():
        ys_ref[...] = jnp.zeros_like(ys_ref)


def _moe(hs_rt, tile_g, tile_lo, tile_hi, n_used, wg, wu, wd, wr, br, gm, gfin):
    rows_step = TM_MOE * TILES_PER_STEP * ROW_TILE
    n_steps = hs_rt.shape[0] // rows_step
    by_group = lambda s, tg, *_: (tg[s * TILES_PER_STEP], 0, 0, 0)
    full2 = lambda a, c: pl.BlockSpec((a, c), lambda s, *_: (0, 0))

    def hs_index(s, tg, tlo, thi, nu):
        last_step = jnp.maximum(nu[0] - 1, 0) // TILES_PER_STEP
        return (jnp.minimum(s, last_step), 0)

    grid_spec = pltpu.PrefetchScalarGridSpec(
        num_scalar_prefetch=4,
        grid=(n_steps,),
        in_specs=[
            pl.BlockSpec((rows_step, 128), hs_index),
            pl.BlockSpec((None, EPG, D_MODEL, D_EXPERT), by_group),
            pl.BlockSpec((None, EPG, D_MODEL, D_EXPERT), by_group),
            pl.BlockSpec((None, EPG, D_EXPERT, D_MODEL), by_group),
            full2(D_MODEL, BUCKET_LANES), full2(1, BUCKET_LANES), full2(1, D_MODEL), full2(1, D_MODEL),
        ],
        out_specs=pl.BlockSpec((rows_step, 128), lambda s, *_: (s, 0)),
    )
    return pl.pallas_call(
        _moe_kernel,
        grid_spec=grid_spec,
        out_shape=jax.ShapeDtypeStruct(hs_rt.shape, F32),
        compiler_params=pltpu.CompilerParams(
            dimension_semantics=("arbitrary",), vmem_limit_bytes=V7X_VMEM_LIMIT),
        name="moe",
    )(tile_g, tile_lo, tile_hi, n_used, hs_rt, wg, wu, wd, wr, br, gm, gfin)


def _bucket_plan(bucket, rank, counts, n):
    nt = n // TM_MOE + N_BUCKETS + N_GROUPS * (TILES_PER_STEP - 1)
    nt = -(-nt // TILES_PER_STEP) * TILES_PER_STEP
    tiles_b = (counts + (TM_MOE - 1)) // TM_MOE
    tiles_g = jnp.sum(tiles_b.reshape(N_GROUPS, N_PAIRS), axis=1)
    extra_g = (-tiles_g) % TILES_PER_STEP
    is_last = (np.arange(N_PAIRS) == N_PAIRS - 1)[None, :]
    tiles_b = (tiles_b.reshape(N_GROUPS, N_PAIRS) + jnp.where(is_last, extra_g[:, None], 0)).reshape(N_BUCKETS)
    tile_end = jnp.cumsum(tiles_b)
    tile_start = tile_end - tiles_b
    n_used = tile_end[-1]
    b_ids = jnp.arange(N_BUCKETS, dtype=I32)
    dest = rank + TM_MOE * jnp.sum(jnp.where(bucket[:, None] == b_ids[None, :], tile_start[None, :], 0), axis=1)
    t_idx = jnp.arange(nt, dtype=I32)
    tb = jnp.sum((tile_end[None, :] <= t_idx[:, None]).astype(I32), axis=1)
    tb_last = jnp.sum((tile_end <= n_used - 1).astype(I32))
    tb = jnp.minimum(jnp.where(t_idx < n_used, tb, tb_last), N_BUCKETS - 1)
    sel = tb[:, None] == b_ids[None, :]
    pick = lambda table: jnp.sum(jnp.where(sel, table[None, :], 0), axis=1).astype(I32)
    tile_g = tb // N_PAIRS
    tile_lo = pick(jnp.asarray(np.tile(_PAIR_LO, N_GROUPS)))
    tile_hi = pick(jnp.asarray(np.tile(_PAIR_HI, N_GROUPS)))
    nv = jnp.clip(pick(counts) - (t_idx - pick(tile_start)) * TM_MOE, 0, TM_MOE)
    tile_nv = jnp.where(t_idx < n_used, nv, 0).astype(I32)
    return dest.astype(I32), tile_g.astype(I32), tile_lo, tile_hi, tile_nv, n_used.reshape(1).astype(I32)


def kernel(x, norm_mix, w_in, rpb, w_four, b_four, g_attn_out, g_four_out, w_out, norm_moe,
           w_router_coarse, b_router_coarse, w_router_fine, b_router_fine, w_gate, w_up, w_down, norm_final):
    b, seq, d = x.shape
    assert (seq, d) == (SEQ, D_MODEL) and norm_mix.shape[0] == 1
    n = b * seq
    x2 = x.reshape(n, d)

    qkv, u = _inproj(x2, norm_mix[0][None], w_in[0].astype(BF16))

    oa = _attention(qkv.reshape(b, seq, 3 * D_ATTN), _bias_columns(rpb[0]))

    eye4 = jnp.eye(4, dtype=F32)
    wf = w_four[0].reshape(2, 4, FOUR_GROUP_DIM, FOUR_GROUP_DIM)
    wbd = (eye4[None, :, None, :, None] * wf[:, :, :, None, :]).reshape(2, 256, 256).astype(BF16)
    yf = _fourier(u, wbd, b_four[0][None]).reshape(N_CBLK, b * ROWS * GRID_PITCH, 128)

    wrt = jnp.zeros((BUCKET_LANES, d), F32)
    wrt = wrt.at[0:N_GROUPS].set(w_router_coarse[0].T).at[8:8 + N_EXPERTS].set(w_router_fine[0].T)
    br = jnp.zeros((BUCKET_LANES, 1), F32)
    br = br.at[0:N_GROUPS, 0].set(b_router_coarse[0]).at[8:8 + N_EXPERTS, 0].set(b_router_fine[0])
    tri = (np.arange(TM_OUT)[:, None] < np.arange(TM_OUT)[None, :]).astype(np.float32)
    wrt_bf = wrt.astype(BF16)
    h_rt, bucket, rank, cnt = _mixout(
        oa.reshape(n, D_ATTN), yf, x2, g_attn_out[0][None], g_four_out[0][None],
        w_out[0].astype(BF16), norm_moe[0][None], wrt_bf, br, jnp.asarray(tri, BF16))

    counts = cnt[:N_BUCKETS, 0].astype(I32)
    dest, tile_g, tile_lo, tile_hi, tile_nv, n_used = _bucket_plan(bucket.reshape(n), rank.reshape(n), counts, n)
    dest3 = dest.reshape(n // ROWS_PER_STEP, 1, ROWS_PER_STEP)
    hs_rt, wg_bf, wu_bf, wd_bf = _dispatch(h_rt, dest3, tile_nv, tile_nv.shape[0] * TM_MOE,
                                           w_gate[0], w_up[0], w_down[0])
    shape_e = (N_GROUPS, EPG)
    ys_rt = _moe(hs_rt, tile_g, tile_lo, tile_hi, n_used,
                 wg_bf.reshape(shape_e + (d, D_EXPERT)),
                 wu_bf.reshape(shape_e + (d, D_EXPERT)),
                 wd_bf.reshape(shape_e + (D_EXPERT, d)),
                 wrt_bf.T, br.T, norm_moe[0][None], norm_final[None])
    return _combine(ys_rt, dest3, n).reshape(b, seq, d)
```

```python
import numpy as np
import jax
import jax.numpy as jnp
from jax import lax
from jax.experimental import pallas as pl
from jax.experimental.pallas import tpu as pltpu

F32 = jnp.float32
BF16 = jnp.bfloat16
I32 = jnp.int32

D_MODEL = 1024
SEQ = 4096
GRID_W = 64
ROWS = SEQ // GRID_W
D_ATTN = 512
D_FOUR = 512
N_HEADS = 8
HEAD_DIM = 64
WIN_H = 8
WIN_W = 16
N_FOUR_GROUPS = 8
FOUR_GROUP_DIM = 64
D_PROJ = 3 * D_ATTN + D_FOUR
N_GROUPS = 4
EPG = 8
N_EXPERTS = N_GROUPS * EPG
D_EXPERT = 256
EPS = 1e-6
NEG = -1e30

V7X_VMEM_LIMIT = 56 * 1024 * 1024

TM_IN = 1024
TM_OUT = 1024
OUT_SUBTILES = 1
TM_MOE = 128
ROW_TILE = D_MODEL // 128
ROWS_PER_STEP = 2048
TILES_PER_STEP = 4
N_PAIRS = EPG * (EPG - 1) // 2
N_BUCKETS = N_GROUPS * N_PAIRS
BUCKET_LANES = 128

QB_ROWS = 8
QB_COLS = 16
KB_ROWS = 16
KB_COLS = 32
GRID_PITCH = 72


def _rms(x, g):
    ms = jnp.mean(x * x, axis=-1, keepdims=True)
    return x * lax.rsqrt(ms + EPS) * g


def _inproj_kernel(x_ref, g_ref, w_ref, qkv_ref, u_ref):
    xn = _rms(x_ref[...], g_ref[...]).astype(BF16)
    p = jnp.dot(xn, w_ref[...], preferred_element_type=F32)
    qkv_ref[:, :D_ATTN] = (p[:, :D_ATTN] * (HEAD_DIM ** -0.5)).astype(BF16)
    qkv_ref[:, D_ATTN:] = p[:, D_ATTN:3 * D_ATTN].astype(BF16)
    for cb in range(D_FOUR // 128):
        lanes = slice(3 * D_ATTN + cb * 128, 3 * D_ATTN + (cb + 1) * 128)
        for r in range(TM_IN // GRID_W):
            u_ref[cb, r * GRID_PITCH:r * GRID_PITCH + GRID_W, :] = p[r * GRID_W:(r + 1) * GRID_W, lanes]
            u_ref[cb, r * GRID_PITCH + GRID_W:(r + 1) * GRID_PITCH, :] = jnp.zeros((GRID_PITCH - GRID_W, 128), F32)


def _inproj(x2, g, w_bf):
    n = x2.shape[0]
    rows_step = TM_IN // GRID_W
    steps_b = ROWS // rows_step
    return pl.pallas_call(
        _inproj_kernel,
        grid=(n // TM_IN,),
        in_specs=[
            pl.BlockSpec((TM_IN, D_MODEL), lambda i: (i, 0)),
            pl.BlockSpec((1, D_MODEL), lambda i: (0, 0)),
            pl.BlockSpec((D_MODEL, D_PROJ), lambda i: (0, 0)),
        ],
        out_specs=[
            pl.BlockSpec((TM_IN, 3 * D_ATTN), lambda i: (i, 0)),
            pl.BlockSpec((D_FOUR // 128, None, rows_step * GRID_PITCH, 128),
                         lambda i: (0, i // steps_b, i % steps_b, 0)),
        ],
        out_shape=[
            jax.ShapeDtypeStruct((n, 3 * D_ATTN), BF16),
            jax.ShapeDtypeStruct((D_FOUR // 128, n // SEQ, ROWS * GRID_PITCH, 128), F32),
        ],
        compiler_params=pltpu.CompilerParams(
            dimension_semantics=("parallel",), vmem_limit_bytes=V7X_VMEM_LIMIT),
        name="inproj",
    )(x2, g, w_bf)


_KCOL_SHIFTED = (False, True, True, False)
_KCOL_OFF = (0, 0, 16, 32)
_COL_TYPE = (0, 1, 1, 2)


def _bias_index_tables():
    dr = np.zeros((9, 128, 512), np.int32)
    dc = np.zeros((9, 128, 512), np.int32)
    ok = np.zeros((9, 128, 512), bool)
    qi, qc = np.divmod(np.arange(128), QB_COLS)
    ki, kc = np.divmod(np.arange(512), KB_COLS)
    for rt, (q0, k0) in enumerate(((0, 0), (8, 4), (56, 48))):
        qrow = q0 + qi
        krow = k0 + ki
        rs = np.clip(qrow - WIN_H // 2, 0, ROWS - WIN_H)
        rok = (krow[None, :] >= rs[:, None]) & (krow[None, :] < rs[:, None] + WIN_H)
        drr = krow[None, :] - qrow[:, None] + (WIN_H - 1)
        for ct, (c0, kc0) in enumerate(((0, 0), (16, 8), (48, 32))):
            qcol = c0 + qc
            kcol = kc0 + kc
            cs = np.clip(qcol - WIN_W // 2, 0, GRID_W - WIN_W)
            cok = (kcol[None, :] >= cs[:, None]) & (kcol[None, :] < cs[:, None] + WIN_W)
            dcc = kcol[None, :] - qcol[:, None] + (WIN_W - 1)
            t = rt * 3 + ct
            ok[t] = rok & cok
            dr[t] = np.where(ok[t], drr, 0)
            dc[t] = np.where(ok[t], dcc, 0)
    return dr, dc, ok


_BIAS_DR, _BIAS_DC, _BIAS_OK = _bias_index_tables()


def _bias_selectors():
    ok = _BIAS_OK.reshape(3, 3, QB_ROWS, QB_COLS, KB_ROWS, KB_COLS)
    dr = _BIAS_DR.reshape(ok.shape)
    dc = _BIAS_DC.reshape(ok.shape)
    row_ok = ok.any(axis=(1, 3, 5))
    col_ok = ok.any(axis=(0, 2, 4))
    dr_r = dr.max(axis=(1, 3, 5))
    dc_c = dc.max(axis=(0, 2, 4))
    sc = (np.arange(2 * WIN_W - 1)[:, None, None, None] == dc_c[None]) & col_ok[None]
    return row_ok, dr_r, col_ok, sc.astype(np.float32)


_BIAS_ROW_OK, _BIAS_ROW_DR, _BIAS_COL_OK, _BIAS_SC = _bias_selectors()


def _bias_columns(rpb):
    sc = jnp.asarray(np.tile(_BIAS_SC, 128 // KB_COLS))
    ok = np.tile(_BIAS_COL_OK, 128 // KB_COLS)
    t1 = jnp.sum(rpb[:, :, :, None, None, None] * sc[None, None], axis=2)
    return jnp.where(ok[None, None], t1, NEG)


def _attn_kernel(q_ref, k_ref, v_ref, bcol_ref, o_ref, ksh_ref, vsh_ref, bias_ref, s_ref, p_ref, l_ref):
    @pl.when(pl.program_id(1) == 0)
    def _():
        key_row = lax.broadcasted_iota(I32, (QB_COLS, KB_ROWS * KB_COLS), 1) // KB_COLS
        for hh in range(2):
            for rt in range(3):
                for ct in range(3):
                    for i in range(QB_ROWS):
                        acc = jnp.full((QB_COLS, KB_ROWS * KB_COLS), NEG, F32)
                        for y in range(KB_ROWS):
                            if _BIAS_ROW_OK[rt, i, y]:
                                cols = bcol_ref[hh, int(_BIAS_ROW_DR[rt, i, y]), ct]
                                cols = jnp.concatenate([cols] * (KB_ROWS * KB_COLS // 128), axis=1)
                                acc = jnp.where(key_row == y, cols, acc)
                        bias_ref[rt * 3 + ct, pl.ds(hh * 128 + i * QB_COLS, QB_COLS), :] = acc

    zpad = jnp.zeros((8, 128), F32)
    ksh_ref[...] = jnp.concatenate([k_ref[...].astype(F32)[8:], zpad], axis=0).astype(BF16)
    vsh_ref[...] = jnp.concatenate([v_ref[...].astype(F32)[8:], zpad], axis=0).astype(BF16)
    lane = lax.broadcasted_iota(I32, (1, 128), 1)
    head_masks = (lane < HEAD_DIM, lane >= HEAD_DIM)

    n_rb = ROWS // QB_ROWS
    n_q = GRID_W // QB_COLS

    def window(ref_plain, ref_shift, rb, j):
        ks = jnp.clip(QB_ROWS * rb - WIN_H // 2, 0, ROWS - KB_ROWS)
        src = ref_shift if _KCOL_SHIFTED[j] else ref_plain
        return jnp.concatenate(
            [src[pl.ds(pl.multiple_of((ks + i) * GRID_W + _KCOL_OFF[j], 16), KB_COLS), :] for i in range(KB_ROWS)],
            axis=0)

    def scores(rb, j):
        q = jnp.concatenate(
            [q_ref[pl.ds(pl.multiple_of((QB_ROWS * rb + i) * GRID_W + QB_COLS * j, 16), QB_COLS), :]
             for i in range(QB_ROWS)], axis=0)
        qm = jnp.concatenate([jnp.where(hm, q, jnp.zeros_like(q)) for hm in head_masks], axis=0)
        s_ref[j] = lax.dot_general(qm, window(k_ref, ksh_ref, rb, j), (((1,), (1,)), ((), ())),
                                   preferred_element_type=F32)

    def softmax(rb, j):
        rt = jnp.where(rb == 0, 0, jnp.where(rb == n_rb - 1, 2, 1))
        s = s_ref[j] + bias_ref[rt * 3 + _COL_TYPE[j]]
        e = jnp.exp(s - jnp.max(s, axis=-1, keepdims=True))
        p_ref[j] = e.astype(BF16)
        l_ref[j] = jnp.broadcast_to(jnp.sum(e, axis=-1, keepdims=True), (2 * QB_ROWS * QB_COLS, 128))

    def values(rb, j):
        o = jnp.dot(p_ref[j], window(v_ref, vsh_ref, rb, j), preferred_element_type=F32) / l_ref[j]
        out = jnp.where(head_masks[0], o[:128], o[128:]).astype(BF16)
        for i in range(QB_ROWS):
            o_ref[pl.ds(pl.multiple_of((QB_ROWS * rb + i) * GRID_W + QB_COLS * j, 16), QB_COLS), :] = (
                out[QB_COLS * i:QB_COLS * (i + 1)])

    def stage(fn, rb):
        for j in range(n_q):
            fn(jnp.asarray(rb, I32), j)

    stage(scores, 0)
    stage(softmax, 0)
    stage(scores, 1)

    def pipeline_step(i, carry):
        stage(values, i - 2)
        stage(softmax, i - 1)
        stage(scores, i)
        return carry

    lax.fori_loop(2, n_rb, pipeline_step, 0)
    stage(values, n_rb - 2)
    stage(softmax, n_rb - 1)
    stage(values, n_rb - 1)


def _attention(qkv3, bias_cols):
    b = qkv3.shape[0]
    n_hp = N_HEADS // 2
    blk = lambda off: pl.BlockSpec((None, SEQ, 128), lambda hp, bi: (bi, 0, off + hp))
    return pl.pallas_call(
        _attn_kernel,
        grid=(n_hp, b),
        in_specs=[
            blk(0), blk(n_hp), blk(2 * n_hp),
            pl.BlockSpec((2,) + bias_cols.shape[1:], lambda hp, bi: (hp, 0, 0, 0, 0)),
        ],
        out_specs=pl.BlockSpec((None, SEQ, 128), lambda hp, bi: (bi, 0, hp)),
        out_shape=jax.ShapeDtypeStruct((b, SEQ, D_ATTN), BF16),
        scratch_shapes=[pltpu.VMEM((SEQ, 128), BF16), pltpu.VMEM((SEQ, 128), BF16),
                        pltpu.VMEM((9, 2 * QB_ROWS * QB_COLS, KB_ROWS * KB_COLS), F32),
                        pltpu.VMEM((GRID_W // QB_COLS, 2 * QB_ROWS * QB_COLS, KB_ROWS * KB_COLS), F32),
                        pltpu.VMEM((GRID_W // QB_COLS, 2 * QB_ROWS * QB_COLS, KB_ROWS * KB_COLS), BF16),
                        pltpu.VMEM((GRID_W // QB_COLS, 2 * QB_ROWS * QB_COLS, 128), F32)],
        compiler_params=pltpu.CompilerParams(
            dimension_semantics=("arbitrary", "arbitrary"), vmem_limit_bytes=V7X_VMEM_LIMIT),
        name="nattn",
    )(qkv3, qkv3, qkv3, bias_cols)


Z_PITCH = 72
N_CBLK = D_FOUR // 128


def _fourier_tables():
    n = 64
    k = np.arange(n)
    ang = 2.0 * np.pi * np.outer(k, k) / n
    c64, s64 = np.cos(ang), np.sin(ang)
    w1 = np.concatenate([c64, -s64], axis=0)
    t1p = np.arange(n)[:, None, None]
    t2p = np.arange(n)[None, :, None]
    t2 = np.arange(n)[None, None, :]
    th = 2.0 * np.pi * ((t2 * (t1p + n * t2p)) % SEQ) / SEQ
    cc, ss = np.cos(th), np.sin(th)
    m2 = np.concatenate([np.concatenate([cc, ss], axis=2), np.concatenate([-ss, cc], axis=2)], axis=1)
    cbd = np.kron(np.eye(4), c64)
    sbd = np.kron(np.eye(4), s64)
    cs = np.concatenate([cbd, sbd], axis=0)
    return w1.astype(np.float32), m2.astype(np.float32), cs.astype(np.float32)


_W1_NP, _M2_NP, _CS_NP = _fourier_tables()


HALF_CBLK = N_CBLK // 2
T2_UNROLL = 16


def _fourier_kernel(u_ref, w1_ref, m2_ref, cs_ref, wbd_ref, bf_ref, y_ref, zs_ref):
    for cb in range(HALF_CBLK):
        for k in range(GRID_PITCH - GRID_W):
            y_ref[cb, pl.ds(GRID_W + k, ROWS, stride=GRID_PITCH), :] = jnp.zeros((ROWS, 128), F32)

    def dft_cols(i, carry):
        for k in range(T2_UNROLL):
            t2 = i * T2_UNROLL + k
            x = jnp.concatenate([u_ref[cb, pl.ds(t2, ROWS, stride=GRID_PITCH), :] for cb in range(HALF_CBLK)],
                                axis=1).astype(BF16)
            z = jnp.dot(w1_ref[...], x, preferred_element_type=F32)
            for cb in range(HALF_CBLK):
                zs_ref[cb, pl.ds(t2, 128, stride=Z_PITCH), :] = z[:, cb * 128:(cb + 1) * 128]
        return carry

    lax.fori_loop(0, GRID_W // T2_UNROLL, dft_cols, 0)

    def dft_rows(a, carry):
        xs = []
        for jo in range(8):
            t1p = a * 8 + jo
            rhs = jnp.concatenate(
                [jnp.concatenate([zs_ref[cb, pl.ds(pl.multiple_of((part * 64 + t1p) * Z_PITCH, 8), 64), :]
                                  for cb in range(HALF_CBLK)], axis=1) for part in range(2)], axis=0)
            xs.append(jnp.dot(m2_ref[t1p], rhs.astype(BF16), preferred_element_type=F32))
        xr = jnp.concatenate([x[:64] for x in xs], axis=0).astype(BF16)
        xi = jnp.concatenate([x[64:] for x in xs], axis=0).astype(BF16)
        lhs = jnp.concatenate([xr, xi], axis=1)
        f = jnp.dot(lhs, cs_ref[...], preferred_element_type=F32) * (1.0 / 512.0)
        y = jnp.dot(f.astype(BF16), wbd_ref[...], preferred_element_type=F32) + bf_ref[...]
        for jo in range(8):
            t1p = a * 8 + jo
            for cb in range(HALF_CBLK):
                y_ref[cb, pl.ds(t1p, ROWS, stride=GRID_PITCH), :] = y[jo * 64:(jo + 1) * 64, cb * 128:(cb + 1) * 128]
        return carry

    lax.fori_loop(0, GRID_W // 8, dft_rows, 0)


def _fourier(u_p, wbd, bf):
    b = u_p.shape[1]
    w1 = jnp.asarray(_W1_NP).astype(BF16)
    m2 = jnp.asarray(_M2_NP).astype(BF16)
    cs = jnp.asarray(_CS_NP).astype(BF16)
    half_blk = pl.BlockSpec((HALF_CBLK, None, ROWS * GRID_PITCH, 128), lambda bi, hf: (hf, bi, 0, 0))
    return pl.pallas_call(
        _fourier_kernel,
        grid=(b, 2),
        in_specs=[
            half_blk,
            pl.BlockSpec((128, 64), lambda bi, hf: (0, 0)),
            pl.BlockSpec((64, 128, 128), lambda bi, hf: (0, 0, 0)),
            pl.BlockSpec((512, 256), lambda bi, hf: (0, 0)),
            pl.BlockSpec((None, 256, 256), lambda bi, hf: (hf, 0, 0)),
            pl.BlockSpec((1, 256), lambda bi, hf: (0, hf)),
        ],
        out_specs=half_blk,
        out_shape=jax.ShapeDtypeStruct(u_p.shape, F32),
        scratch_shapes=[pltpu.VMEM((HALF_CBLK, 128 * Z_PITCH, 128), F32)],
        compiler_params=pltpu.CompilerParams(
            dimension_semantics=("parallel", "parallel"), vmem_limit_bytes=V7X_VMEM_LIMIT),
        name="fourier",
    )(u_p, w1, m2, cs, wbd, bf)


def _mixout_kernel(oa_ref, yf_ref, x_ref, ga_ref, gf_ref, wout_ref, gm_ref, wrt_ref, br_ref, tri_ref,
                   h_ref, bucket_ref, rank_ref, cnt_ref, carry_ref):
    i = pl.program_id(0)

    @pl.when(i == 0)
    def _():
        carry_ref[...] = jnp.zeros_like(carry_ref)

    carry = carry_ref[...]
    for k in range(OUT_SUBTILES):
        carry = _mixout_subtile(k, carry, oa_ref, yf_ref, x_ref, ga_ref, gf_ref, wout_ref, gm_ref, wrt_ref,
                                br_ref, tri_ref, h_ref, bucket_ref, rank_ref)
    carry_ref[...] = carry
    cnt_ref[...] = carry


def _mixout_subtile(k, carry, oa_ref, yf_ref, x_ref, ga_ref, gf_ref, wout_ref, gm_ref, wrt_ref, br_ref, tri_ref,
                    h_ref, bucket_ref, rank_ref):
    rows_k = pl.ds(k * TM_OUT, TM_OUT)
    na = _rms(oa_ref[rows_k, :].astype(F32), ga_ref[...]).astype(BF16)
    grid_rows = [k * (TM_OUT // GRID_W) + r for r in range(TM_OUT // GRID_W)]
    yf = jnp.concatenate(
        [jnp.concatenate([yf_ref[cb, gr * GRID_PITCH:gr * GRID_PITCH + GRID_W, :] for gr in grid_rows], axis=0)
         for cb in range(N_CBLK)], axis=1)
    nf = _rms(yf, gf_ref[...]).astype(BF16)
    merged = jnp.concatenate([na, nf], axis=1)
    h = x_ref[rows_k, :] + jnp.dot(merged, wout_ref[...], preferred_element_type=F32)
    for cb in range(ROW_TILE):
        h_ref[pl.ds(k * TM_OUT * ROW_TILE + cb, TM_OUT, stride=ROW_TILE), :] = h[:, cb * 128:(cb + 1) * 128]
    hn = _rms(h, gm_ref[...]).astype(BF16)
    lt = lax.dot_general(wrt_ref[...], hn, (((1,), (1,)), ((), ())), preferred_element_type=F32)
    lt = lt + br_ref[...]
    c = [lt[k:k + 1] for k in range(N_GROUPS)]
    cmax = jnp.maximum(jnp.maximum(c[0], c[1]), jnp.maximum(c[2], c[3]))
    e = [jnp.exp(ck - cmax) for ck in c]
    esum = (e[0] + e[1]) + (e[2] + e[3])
    p = [ek / esum for ek in e]
    pmax = jnp.maximum(jnp.maximum(p[0], p[1]), jnp.maximum(p[2], p[3]))
    g = jnp.where(p[0] == pmax, 0, jnp.where(p[1] == pmax, 1, jnp.where(p[2] == pmax, 2, 3))).astype(I32)
    fine = jnp.where(g == 0, lt[8:16], jnp.where(g == 1, lt[16:24], jnp.where(g == 2, lt[24:32], lt[32:40])))
    rows = lax.broadcasted_iota(I32, fine.shape, 0)
    v1 = jnp.max(fine, axis=0, keepdims=True)
    i1 = jnp.min(jnp.where(fine == v1, rows, EPG), axis=0, keepdims=True)
    rest = jnp.where(rows == i1, -jnp.inf, fine)
    v2 = jnp.max(rest, axis=0, keepdims=True)
    i2 = jnp.min(jnp.where(rest == v2, rows, EPG), axis=0, keepdims=True)
    lo = jnp.minimum(i1, i2)
    hi = jnp.maximum(i1, i2)
    pair = lax.shift_right_logical(lo * (2 * EPG - 1 - lo), 1) + (hi - lo - 1)
    bucket = g * N_PAIRS + pair
    bucket_ref[k] = bucket
    brow = lax.broadcasted_iota(I32, (BUCKET_LANES, TM_OUT), 0)
    onehot = (brow == bucket).astype(F32)
    prefix = jnp.dot(onehot.astype(BF16), tri_ref[...], preferred_element_type=F32)
    rank = jnp.sum(onehot * (prefix + carry), axis=0, keepdims=True)
    rank_ref[k] = rank.astype(I32)
    return carry + jnp.sum(onehot, axis=1, keepdims=True)


def _mixout(oa, yf, x2, ga, gf, wout_bf, gm, wrt, br, tri):
    n = x2.shape[0]
    nt = n // TM_OUT
    rows_step = TM_OUT * OUT_SUBTILES
    full = lambda *shape: pl.BlockSpec(shape, lambda i: (0,) * len(shape))
    row3 = pl.BlockSpec((OUT_SUBTILES, 1, TM_OUT), lambda i: (i, 0, 0))
    return pl.pallas_call(
        _mixout_kernel,
        grid=(n // rows_step,),
        in_specs=[
            pl.BlockSpec((rows_step, D_ATTN), lambda i: (i, 0)),
            pl.BlockSpec((N_CBLK, rows_step // GRID_W * GRID_PITCH, 128), lambda i: (0, i, 0)),
            pl.BlockSpec((rows_step, D_MODEL), lambda i: (i, 0)),
            full(1, D_ATTN), full(1, D_FOUR), full(D_MODEL, D_MODEL), full(1, D_MODEL),
            full(BUCKET_LANES, D_MODEL), full(BUCKET_LANES, 1), full(TM_OUT, TM_OUT),
        ],
        out_specs=[
            pl.BlockSpec((rows_step * ROW_TILE, 128), lambda i: (i, 0)),
            row3, row3,
            full(BUCKET_LANES, 1),
        ],
        out_shape=[
            jax.ShapeDtypeStruct((n * ROW_TILE, 128), F32),
            jax.ShapeDtypeStruct((nt, 1, TM_OUT), I32),
            jax.ShapeDtypeStruct((nt, 1, TM_OUT), I32),
            jax.ShapeDtypeStruct((BUCKET_LANES, 1), F32),
        ],
        scratch_shapes=[pltpu.VMEM((BUCKET_LANES, 1), F32)],
        compiler_params=pltpu.CompilerParams(
            dimension_semantics=("arbitrary",), vmem_limit_bytes=V7X_VMEM_LIMIT),
        name="mixout",
    )(oa, yf, x2, ga, gf, wout_bf, gm, wrt, br, tri)


def _pair_tables():
    lo, hi = [], []
    for a in range(EPG):
        for b in range(a + 1, EPG):
            lo.append(a)
            hi.append(b)
    return np.asarray(lo, np.int32), np.asarray(hi, np.int32)


_PAIR_LO, _PAIR_HI = _pair_tables()


def _dispatch_kernel(tnv_ref, dest_ref, h_ref, wg_ref, wu_ref, wd_ref, hs_hbm, wg_out, wu_out, wd_out,
                     zbuf, zsem, sem):
    k = pl.program_id(0)
    wg_out[...] = wg_ref[...].astype(BF16)
    wu_out[...] = wu_ref[...].astype(BF16)
    wd_out[...] = wd_ref[...].astype(BF16)
    tile_rows = TM_MOE * ROW_TILE
    n_tiles = hs_hbm.shape[0] // tile_rows

    def for_tiles(pred, which, action):
        def body(t, c):
            @pl.when(pred(tnv_ref[t]))
            def _():
                dst = hs_hbm.at[pl.ds(pl.multiple_of(t * tile_rows, tile_rows), tile_rows)]
                action(pltpu.make_async_copy(zbuf, dst, zsem.at[which]))
            return c
        lax.fori_loop(0, n_tiles, body, 0)

    partly_owned = lambda nv: jnp.logical_and(nv > 0, nv < TM_MOE)
    unowned = lambda nv: nv == 0

    @pl.when(k == 0)
    def _():
        zbuf[...] = jnp.zeros_like(zbuf)
        for_tiles(partly_owned, 0, lambda c: c.start())
        for_tiles(unowned, 1, lambda c: c.start())
        for_tiles(partly_owned, 0, lambda c: c.wait())

    @pl.when(k == pl.num_programs(0) - 1)
    def _():
        for_tiles(unowned, 1, lambda c: c.wait())

    def rows(r8, c):
        for u in range(8):
            r = r8 * 8 + u
            dst = pl.multiple_of(dest_ref[0, 0, r] * ROW_TILE, ROW_TILE)
            pltpu.make_async_copy(h_ref.at[pl.ds(pl.multiple_of(r * ROW_TILE, ROW_TILE), ROW_TILE)],
                                  hs_hbm.at[pl.ds(dst, ROW_TILE)], sem).start(priority=u % 2)
        return c

    lax.fori_loop(0, ROWS_PER_STEP // 8, rows, 0)
    pltpu.make_async_copy(h_ref, hs_hbm.at[pl.ds(0, ROWS_PER_STEP * ROW_TILE)], sem).wait()


def _dispatch(h_rt, dest3, tile_nv, n_slots, w_gate, w_up, w_down):
    n = h_rt.shape[0] // ROW_TILE
    n_steps = n // ROWS_PER_STEP
    assert N_EXPERTS % n_steps == 0
    eps = N_EXPERTS // n_steps
    wspec = lambda shape: pl.BlockSpec((eps,) + shape, lambda k, *_: (k, 0, 0))
    grid_spec = pltpu.PrefetchScalarGridSpec(
        num_scalar_prefetch=1,
        grid=(n_steps,),
        in_specs=[
            pl.BlockSpec((1, 1, ROWS_PER_STEP), lambda k, *_: (k, 0, 0), memory_space=pltpu.SMEM),
            pl.BlockSpec((ROWS_PER_STEP * ROW_TILE, 128), lambda k, *_: (k, 0)),
            wspec((D_MODEL, D_EXPERT)), wspec((D_MODEL, D_EXPERT)), wspec((D_EXPERT, D_MODEL)),
        ],
        out_specs=[pl.BlockSpec(memory_space=pl.ANY),
                   wspec((D_MODEL, D_EXPERT)), wspec((D_MODEL, D_EXPERT)), wspec((D_EXPERT, D_MODEL))],
        scratch_shapes=[pltpu.VMEM((TM_MOE * ROW_TILE, 128), F32),
                        pltpu.SemaphoreType.DMA((2,)), pltpu.SemaphoreType.DMA(())],
    )
    return pl.pallas_call(
        _dispatch_kernel,
        grid_spec=grid_spec,
        out_shape=[jax.ShapeDtypeStruct((n_slots * ROW_TILE, 128), F32),
                   jax.ShapeDtypeStruct(w_gate.shape, BF16), jax.ShapeDtypeStruct(w_up.shape, BF16),
                   jax.ShapeDtypeStruct(w_down.shape, BF16)],
        compiler_params=pltpu.CompilerParams(
            dimension_semantics=("arbitrary",), vmem_limit_bytes=V7X_VMEM_LIMIT),
        name="dispatch",
    )(tile_nv, dest3, h_rt, w_gate, w_up, w_down)


def _combine_kernel(dest_ref, ys_hbm, o_ref, buf, sem):
    k = pl.program_id(0)
    n_blocks = pl.num_programs(0) - 1
    slot = k % 2

    def start_rows(r8):
        for u in range(8):
            r = r8 * 8 + u
            src = pl.multiple_of(dest_ref[0, 0, r] * ROW_TILE, ROW_TILE)
            pltpu.make_async_copy(ys_hbm.at[pl.ds(src, ROW_TILE)],
                                  buf.at[slot, pl.ds(pl.multiple_of(r * ROW_TILE, ROW_TILE), ROW_TILE)],
                                  sem.at[slot]).start(priority=u % 2)

    def unpack_rows(r8):
        base = pl.multiple_of(r8 * (8 * ROW_TILE), 8 * ROW_TILE)
        for cb in range(ROW_TILE):
            o_ref[pl.ds(pl.multiple_of(r8 * 8, 8), 8), cb * 128:(cb + 1) * 128] = (
                buf[1 - slot, pl.ds(base + cb, 8, stride=ROW_TILE), :])

    def loop(*parts):
        def body(r8, c):
            for part in parts:
                part(r8)
            return c
        lax.fori_loop(0, ROWS_PER_STEP // 8, body, 0)

    @pl.when(k > 0)
    def _():
        pltpu.make_async_copy(ys_hbm.at[pl.ds(0, ROWS_PER_STEP * ROW_TILE)], buf.at[1 - slot],
                              sem.at[1 - slot]).wait()

    @pl.when(k == 0)
    def _():
        loop(start_rows)

    @pl.when(jnp.logical_and(k > 0, k < n_blocks))
    def _():
        loop(start_rows, unpack_rows)

    @pl.when(k == n_blocks)
    def _():
        loop(unpack_rows)


def _combine(ys_rt, dest3, n):
    n_blocks = n // ROWS_PER_STEP
    return pl.pallas_call(
        _combine_kernel,
        grid=(n_blocks + 1,),
        in_specs=[
            pl.BlockSpec((1, 1, ROWS_PER_STEP), lambda k: (jnp.minimum(k, n_blocks - 1), 0, 0),
                         memory_space=pltpu.SMEM),
            pl.BlockSpec(memory_space=pl.ANY),
        ],
        out_specs=pl.BlockSpec((ROWS_PER_STEP, D_MODEL), lambda k: (jnp.maximum(k - 1, 0), 0)),
        out_shape=jax.ShapeDtypeStruct((n, D_MODEL), F32),
        scratch_shapes=[pltpu.VMEM((2, ROWS_PER_STEP * ROW_TILE, 128), F32), pltpu.SemaphoreType.DMA((2,))],
        compiler_params=pltpu.CompilerParams(
            dimension_semantics=("arbitrary",), vmem_limit_bytes=V7X_VMEM_LIMIT),
        name="combine",
    )(dest3, ys_rt)


def _moe_kernel(tg_ref, tlo_ref, thi_ref, nused_ref, hs_ref, wg_ref, wu_ref, wd_ref, wr_ref, br_ref,
                gm_ref, gfin_ref, ys_ref):
    step = pl.program_id(0)

    def tiles(t, k, n_tiles):
        rows = n_tiles * TM_MOE
        base = k * TM_MOE * ROW_TILE
        lane = lax.broadcasted_iota(I32, (rows, BUCKET_LANES), 1)
        hrows = jnp.concatenate(
            [hs_ref[pl.ds(base + cb, rows, stride=ROW_TILE), :] for cb in range(ROW_TILE)], axis=1)
        hn = _rms(hrows, gm_ref[...]).astype(BF16)
        logits = jnp.dot(hn, wr_ref[...], preferred_element_type=F32) + br_ref[...]
        g, lo, hi = tg_ref[t], tlo_ref[t], thi_ref[t]
        coarse = jnp.where(lane < N_GROUPS, logits, -jnp.inf)
        ec = jnp.exp(coarse - jnp.max(coarse, axis=-1, keepdims=True))
        pick = lambda col, v: jnp.sum(jnp.where(lane == col, v, 0.0), axis=-1, keepdims=True)
        g_w = pick(g, ec) / jnp.sum(ec, axis=-1, keepdims=True)
        f_lo = pick(8 + g * EPG + lo, logits)
        f_hi = pick(8 + g * EPG + hi, logits)
        f_max = jnp.maximum(f_lo, f_hi)
        e_lo = jnp.exp(f_lo - f_max)
        e_hi = jnp.exp(f_hi - f_max)
        den = e_lo + e_hi

        def expert(e, w):
            gate = jnp.dot(hn, wg_ref[e], preferred_element_type=F32)
            up = jnp.dot(hn, wu_ref[e], preferred_element_type=F32)
            act = (gate * jax.nn.sigmoid(gate) * up).astype(BF16)
            return w * jnp.dot(act, wd_ref[e], preferred_element_type=F32)

        y = expert(lo, (e_lo / den) * g_w) + expert(hi, (e_hi / den) * g_w)
        res = _rms(hrows + y, gfin_ref[...])
        for cb in range(ROW_TILE):
            ys_ref[pl.ds(base + cb, rows, stride=ROW_TILE), :] = res[:, cb * 128:(cb + 1) * 128]

    @pl.when(step * TILES_PER_STEP < nused_ref[0])
    def _():
        for p in range(TILES_PER_STEP // 2):
            t = step * TILES_PER_STEP + 2 * p
            same = jnp.logical_and(tlo_ref[t] == tlo_ref[t + 1], thi_ref[t] == thi_ref[t + 1])

            @pl.when(same)
            def _():
                tiles(t, 2 * p, 2)

            @pl.when(jnp.logical_not(same))
            def _():
                tiles(t, 2 * p, 1)
                tiles(t + 1, 2 * p + 1, 1)

    @pl.when(step * TILES_PER_STEP >= nused_ref[0])
    def _():
        ys_ref[...] = jnp.zeros_like(ys_ref)


def _moe(hs_rt, tile_g, tile_lo, tile_hi, n_used, wg, wu, wd, wr, br, gm, gfin):
    rows_step = TM_MOE * TILES_PER_STEP * ROW_TILE
    n_steps = hs_rt.shape[0] // rows_step
    by_group = lambda s, tg, *_: (tg[s * TILES_PER_STEP], 0, 0, 0)
    full2 = lambda a, c: pl.BlockSpec((a, c), lambda s, *_: (0, 0))

    def hs_index(s, tg, tlo, thi, nu):
        last_step = jnp.maximum(nu[0] - 1, 0) // TILES_PER_STEP
        return (jnp.minimum(s, last_step), 0)

    grid_spec = pltpu.PrefetchScalarGridSpec(
        num_scalar_prefetch=4,
        grid=(n_steps,),
        in_specs=[
            pl.BlockSpec((rows_step, 128), hs_index),
            pl.BlockSpec((None, EPG, D_MODEL, D_EXPERT), by_group),
            pl.BlockSpec((None, EPG, D_MODEL, D_EXPERT), by_group),
            pl.BlockSpec((None, EPG, D_EXPERT, D_MODEL), by_group),
            full2(D_MODEL, BUCKET_LANES), full2(1, BUCKET_LANES), full2(1, D_MODEL), full2(1, D_MODEL),
        ],
        out_specs=pl.BlockSpec((rows_step, 128), lambda s, *_: (s, 0)),
    )
    return pl.pallas_call(
        _moe_kernel,
        grid_spec=grid_spec,
        out_shape=jax.ShapeDtypeStruct(hs_rt.shape, F32),
        compiler_params=pltpu.CompilerParams(
            dimension_semantics=("arbitrary",), vmem_limit_bytes=V7X_VMEM_LIMIT),
        name="moe",
    )(tile_g, tile_lo, tile_hi, n_used, hs_rt, wg, wu, wd, wr, br, gm, gfin)


def _bucket_plan(bucket, rank, counts, n):
    nt = n // TM_MOE + N_BUCKETS + N_GROUPS * (TILES_PER_STEP - 1)
    nt = -(-nt // TILES_PER_STEP) * TILES_PER_STEP
    tiles_b = (counts + (TM_MOE - 1)) // TM_MOE
    tiles_g = jnp.sum(tiles_b.reshape(N_GROUPS, N_PAIRS), axis=1)
    extra_g = (-tiles_g) % TILES_PER_STEP
    is_last = (np.arange(N_PAIRS) == N_PAIRS - 1)[None, :]
    tiles_b = (tiles_b.reshape(N_GROUPS, N_PAIRS) + jnp.where(is_last, extra_g[:, None], 0)).reshape(N_BUCKETS)
    tile_end = jnp.cumsum(tiles_b)
    tile_start = tile_end - tiles_b
    n_used = tile_end[-1]
    b_ids = jnp.arange(N_BUCKETS, dtype=I32)
    dest = rank + TM_MOE * jnp.sum(jnp.where(bucket[:, None] == b_ids[None, :], tile_start[None, :], 0), axis=1)
    t_idx = jnp.arange(nt, dtype=I32)
    tb = jnp.sum((tile_end[None, :] <= t_idx[:, None]).astype(I32), axis=1)
    tb_last = jnp.sum((tile_end <= n_used - 1).astype(I32))
    tb = jnp.minimum(jnp.where(t_idx < n_used, tb, tb_last), N_BUCKETS - 1)
    sel = tb[:, None] == b_ids[None, :]
    pick = lambda table: jnp.sum(jnp.where(sel, table[None, :], 0), axis=1).astype(I32)
    tile_g = tb // N_PAIRS
    tile_lo = pick(jnp.asarray(np.tile(_PAIR_LO, N_GROUPS)))
    tile_hi = pick(jnp.asarray(np.tile(_PAIR_HI, N_GROUPS)))
    nv = jnp.clip(pick(counts) - (t_idx - pick(tile_start)) * TM_MOE, 0, TM_MOE)
    tile_nv = jnp.where(t_idx < n_used, nv, 0).astype(I32)
    return dest.astype(I32), tile_g.astype(I32), tile_lo, tile_hi, tile_nv, n_used.reshape(1).astype(I32)


def kernel(x, norm_mix, w_in, rpb, w_four, b_four, g_attn_out, g_four_out, w_out, norm_moe,
           w_router_coarse, b_router_coarse, w_router_fine, b_router_fine, w_gate, w_up, w_down, norm_final):
    b, seq, d = x.shape
    assert (seq, d) == (SEQ, D_MODEL) and norm_mix.shape[0] == 1
    n = b * seq
    x2 = x.reshape(n, d)

    qkv, u = _inproj(x2, norm_mix[0][None], w_in[0].astype(BF16))

    oa = _attention(qkv.reshape(b, seq, 3 * D_ATTN), _bias_columns(rpb[0]))

    eye4 = jnp.eye(4, dtype=F32)
    wf = w_four[0].reshape(2, 4, FOUR_GROUP_DIM, FOUR_GROUP_DIM)
    wbd = (eye4[None, :, None, :, None] * wf[:, :, :, None, :]).reshape(2, 256, 256).astype(BF16)
    yf = _fourier(u, wbd, b_four[0][None]).reshape(N_CBLK, b * ROWS * GRID_PITCH, 128)

    wrt = jnp.zeros((BUCKET_LANES, d), F32)
    wrt = wrt.at[0:N_GROUPS].set(w_router_coarse[0].T).at[8:8 + N_EXPERTS].set(w_router_fine[0].T)
    br = jnp.zeros((BUCKET_LANES, 1), F32)
    br = br.at[0:N_GROUPS, 0].set(b_router_coarse[0]).at[8:8 + N_EXPERTS, 0].set(b_router_fine[0])
    tri = (np.arange(TM_OUT)[:, None] < np.arange(TM_OUT)[None, :]).astype(np.float32)
    wrt_bf = wrt.astype(BF16)
    h_rt, bucket, rank, cnt = _mixout(
        oa.reshape(n, D_ATTN), yf, x2, g_attn_out[0][None], g_four_out[0][None],
        w_out[0].astype(BF16), norm_moe[0][None], wrt_bf, br, jnp.asarray(tri, BF16))

    counts = cnt[:N_BUCKETS, 0].astype(I32)
    dest, tile_g, tile_lo, tile_hi, tile_nv, n_used = _bucket_plan(bucket.reshape(n), rank.reshape(n), counts, n)
    dest3 = dest.reshape(n // ROWS_PER_STEP, 1, ROWS_PER_STEP)
    hs_rt, wg_bf, wu_bf, wd_bf = _dispatch(h_rt, dest3, tile_nv, tile_nv.shape[0] * TM_MOE,
                                           w_gate[0], w_up[0], w_down[0])
    shape_e = (N_GROUPS, EPG)
    ys_rt = _moe(hs_rt, tile_g, tile_lo, tile_hi, n_used,
                 wg_bf.reshape(shape_e + (d, D_EXPERT)),
                 wu_bf.reshape(shape_e + (d, D_EXPERT)),
                 wd_bf.reshape(shape_e + (D_EXPERT, d)),
                 wrt_bf.T, br.T, norm_moe[0][None], norm_final[None])
    return _combine(ys_rt, dest3, n).reshape(b, seq, d)
```

```python
import numpy as np
import jax
import jax.numpy as jnp
from jax import lax
from jax.experimental import pallas as pl
from jax.experimental.pallas import tpu as pltpu

F32 = jnp.float32
BF16 = jnp.bfloat16
I32 = jnp.int32

D_MODEL = 1024
SEQ = 4096
GRID_W = 64
ROWS = SEQ // GRID_W
D_ATTN = 512
D_FOUR = 512
N_HEADS = 8
HEAD_DIM = 64
WIN_H = 8
WIN_W = 16
N_FOUR_GROUPS = 8
FOUR_GROUP_DIM = 64
D_PROJ = 3 * D_ATTN + D_FOUR
N_GROUPS = 4
EPG = 8
N_EXPERTS = N_GROUPS * EPG
D_EXPERT = 256
EPS = 1e-6
NEG = -1e30

V7X_VMEM_LIMIT = 56 * 1024 * 1024

TM_IN = 1024
TM_OUT = 1024
OUT_SUBTILES = 1
TM_MOE = 128
ROW_TILE = D_MODEL // 128
ROWS_PER_STEP = 2048
TILES_PER_STEP = 8
N_PAIRS = EPG * (EPG - 1) // 2
N_BUCKETS = N_GROUPS * N_PAIRS
BUCKET_LANES = 128

QB_ROWS = 8
QB_COLS = 16
KB_ROWS = 16
KB_COLS = 32
GRID_PITCH = 72


def _rms(x, g):
    ms = jnp.mean(x * x, axis=-1, keepdims=True)
    return x * lax.rsqrt(ms + EPS) * g


def _inproj_kernel(x_ref, g_ref, w_ref, qkv_ref, u_ref):
    xn = _rms(x_ref[...], g_ref[...]).astype(BF16)
    p = jnp.dot(xn, w_ref[...], preferred_element_type=F32)
    qkv_ref[:, :D_ATTN] = (p[:, :D_ATTN] * (HEAD_DIM ** -0.5)).astype(BF16)
    qkv_ref[:, D_ATTN:] = p[:, D_ATTN:3 * D_ATTN].astype(BF16)
    for cb in range(D_FOUR // 128):
        lanes = slice(3 * D_ATTN + cb * 128, 3 * D_ATTN + (cb + 1) * 128)
        for r in range(TM_IN // GRID_W):
            u_ref[cb, r * GRID_PITCH:r * GRID_PITCH + GRID_W, :] = p[r * GRID_W:(r + 1) * GRID_W, lanes]
            u_ref[cb, r * GRID_PITCH + GRID_W:(r + 1) * GRID_PITCH, :] = jnp.zeros((GRID_PITCH - GRID_W, 128), F32)


def _inproj(x2, g, w_bf):
    n = x2.shape[0]
    rows_step = TM_IN // GRID_W
    steps_b = ROWS // rows_step
    return pl.pallas_call(
        _inproj_kernel,
        grid=(n // TM_IN,),
        in_specs=[
            pl.BlockSpec((TM_IN, D_MODEL), lambda i: (i, 0)),
            pl.BlockSpec((1, D_MODEL), lambda i: (0, 0)),
            pl.BlockSpec((D_MODEL, D_PROJ), lambda i: (0, 0)),
        ],
        out_specs=[
            pl.BlockSpec((TM_IN, 3 * D_ATTN), lambda i: (i, 0)),
            pl.BlockSpec((D_FOUR // 128, None, rows_step * GRID_PITCH, 128),
                         lambda i: (0, i // steps_b, i % steps_b, 0)),
        ],
        out_shape=[
            jax.ShapeDtypeStruct((n, 3 * D_ATTN), BF16),
            jax.ShapeDtypeStruct((D_FOUR // 128, n // SEQ, ROWS * GRID_PITCH, 128), F32),
        ],
        compiler_params=pltpu.CompilerParams(
            dimension_semantics=("parallel",), vmem_limit_bytes=V7X_VMEM_LIMIT),
        name="inproj",
    )(x2, g, w_bf)


_KCOL_SHIFTED = (False, True, True, False)
_KCOL_OFF = (0, 0, 16, 32)
_COL_TYPE = (0, 1, 1, 2)


def _bias_index_tables():
    dr = np.zeros((9, 128, 512), np.int32)
    dc = np.zeros((9, 128, 512), np.int32)
    ok = np.zeros((9, 128, 512), bool)
    qi, qc = np.divmod(np.arange(128), QB_COLS)
    ki, kc = np.divmod(np.arange(512), KB_COLS)
    for rt, (q0, k0) in enumerate(((0, 0), (8, 4), (56, 48))):
        qrow = q0 + qi
        krow = k0 + ki
        rs = np.clip(qrow - WIN_H // 2, 0, ROWS - WIN_H)
        rok = (krow[None, :] >= rs[:, None]) & (krow[None, :] < rs[:, None] + WIN_H)
        drr = krow[None, :] - qrow[:, None] + (WIN_H - 1)
        for ct, (c0, kc0) in enumerate(((0, 0), (16, 8), (48, 32))):
            qcol = c0 + qc
            kcol = kc0 + kc
            cs = np.clip(qcol - WIN_W // 2, 0, GRID_W - WIN_W)
            cok = (kcol[None, :] >= cs[:, None]) & (kcol[None, :] < cs[:, None] + WIN_W)
            dcc = kcol[None, :] - qcol[:, None] + (WIN_W - 1)
            t = rt * 3 + ct
            ok[t] = rok & cok
            dr[t] = np.where(ok[t], drr, 0)
            dc[t] = np.where(ok[t], dcc, 0)
    return dr, dc, ok


_BIAS_DR, _BIAS_DC, _BIAS_OK = _bias_index_tables()


def _bias_selectors():
    ok = _BIAS_OK.reshape(3, 3, QB_ROWS, QB_COLS, KB_ROWS, KB_COLS)
    dr = _BIAS_DR.reshape(ok.shape)
    dc = _BIAS_DC.reshape(ok.shape)
    row_ok = ok.any(axis=(1, 3, 5))
    col_ok = ok.any(axis=(0, 2, 4))
    dr_r = dr.max(axis=(1, 3, 5))
    dc_c = dc.max(axis=(0, 2, 4))
    sc = (np.arange(2 * WIN_W - 1)[:, None, None, None] == dc_c[None]) & col_ok[None]
    return row_ok, dr_r, col_ok, sc.astype(np.float32)


_BIAS_ROW_OK, _BIAS_ROW_DR, _BIAS_COL_OK, _BIAS_SC = _bias_selectors()


def _bias_columns(rpb):
    sc = jnp.asarray(np.tile(_BIAS_SC, 128 // KB_COLS))
    ok = np.tile(_BIAS_COL_OK, 128 // KB_COLS)
    t1 = jnp.sum(rpb[:, :, :, None, None, None] * sc[None, None], axis=2)
    return jnp.where(ok[None, None], t1, NEG)


def _attn_kernel(q_ref, k_ref, v_ref, bcol_ref, o_ref, ksh_ref, vsh_ref, bias_ref, s_ref, p_ref, l_ref):
    @pl.when(pl.program_id(1) == 0)
    def _():
        key_row = lax.broadcasted_iota(I32, (QB_COLS, KB_ROWS * KB_COLS), 1) // KB_COLS
        for hh in range(2):
            for rt in range(3):
                for ct in range(3):
                    for i in range(QB_ROWS):
                        acc = jnp.full((QB_COLS, KB_ROWS * KB_COLS), NEG, F32)
                        for y in range(KB_ROWS):
                            if _BIAS_ROW_OK[rt, i, y]:
                                cols = bcol_ref[hh, int(_BIAS_ROW_DR[rt, i, y]), ct]
                                cols = jnp.concatenate([cols] * (KB_ROWS * KB_COLS // 128), axis=1)
                                acc = jnp.where(key_row == y, cols, acc)
                        bias_ref[rt * 3 + ct, pl.ds(hh * 128 + i * QB_COLS, QB_COLS), :] = acc

    zpad = jnp.zeros((8, 128), F32)
    ksh_ref[...] = jnp.concatenate([k_ref[...].astype(F32)[8:], zpad], axis=0).astype(BF16)
    vsh_ref[...] = jnp.concatenate([v_ref[...].astype(F32)[8:], zpad], axis=0).astype(BF16)
    lane = lax.broadcasted_iota(I32, (1, 128), 1)
    head_masks = (lane < HEAD_DIM, lane >= HEAD_DIM)

    n_rb = ROWS // QB_ROWS
    n_q = GRID_W // QB_COLS

    def window(ref_plain, ref_shift, rb, j):
        ks = jnp.clip(QB_ROWS * rb - WIN_H // 2, 0, ROWS - KB_ROWS)
        src = ref_shift if _KCOL_SHIFTED[j] else ref_plain
        return jnp.concatenate(
            [src[pl.ds(pl.multiple_of((ks + i) * GRID_W + _KCOL_OFF[j], 16), KB_COLS), :] for i in range(KB_ROWS)],
            axis=0)

    def scores(rb, j):
        q = jnp.concatenate(
            [q_ref[pl.ds(pl.multiple_of((QB_ROWS * rb + i) * GRID_W + QB_COLS * j, 16), QB_COLS), :]
             for i in range(QB_ROWS)], axis=0)
        qm = jnp.concatenate([jnp.where(hm, q, jnp.zeros_like(q)) for hm in head_masks], axis=0)
        s_ref[j] = lax.dot_general(qm, window(k_ref, ksh_ref, rb, j), (((1,), (1,)), ((), ())),
                                   preferred_element_type=F32)

    def softmax(rb, j):
        rt = jnp.where(rb == 0, 0, jnp.where(rb == n_rb - 1, 2, 1))
        s = s_ref[j] + bias_ref[rt * 3 + _COL_TYPE[j]]
        e = jnp.exp(s - jnp.max(s, axis=-1, keepdims=True))
        p_ref[j] = e.astype(BF16)
        l_ref[j] = jnp.broadcast_to(jnp.sum(e, axis=-1, keepdims=True), (2 * QB_ROWS * QB_COLS, 128))

    def values(rb, j):
        o = jnp.dot(p_ref[j], window(v_ref, vsh_ref, rb, j), preferred_element_type=F32) / l_ref[j]
        out = jnp.where(head_masks[0], o[:128], o[128:]).astype(BF16)
        for i in range(QB_ROWS):
            o_ref[pl.ds(pl.multiple_of((QB_ROWS * rb + i) * GRID_W + QB_COLS * j, 16), QB_COLS), :] = (
                out[QB_COLS * i:QB_COLS * (i + 1)])

    def stage(fn, rb):
        for j in range(n_q):
            fn(jnp.asarray(rb, I32), j)

    stage(scores, 0)
    stage(softmax, 0)
    stage(scores, 1)

    def pipeline_step(i, carry):
        stage(values, i - 2)
        stage(softmax, i - 1)
        stage(scores, i)
        return carry

    lax.fori_loop(2, n_rb, pipeline_step, 0)
    stage(values, n_rb - 2)
    stage(softmax, n_rb - 1)
    stage(values, n_rb - 1)


def _attention(qkv3, bias_cols):
    b = qkv3.shape[0]
    n_hp = N_HEADS // 2
    blk = lambda off: pl.BlockSpec((None, SEQ, 128), lambda hp, bi: (bi, 0, off + hp))
    return pl.pallas_call(
        _attn_kernel,
        grid=(n_hp, b),
        in_specs=[
            blk(0), blk(n_hp), blk(2 * n_hp),
            pl.BlockSpec((2,) + bias_cols.shape[1:], lambda hp, bi: (hp, 0, 0, 0, 0)),
        ],
        out_specs=pl.BlockSpec((None, SEQ, 128), lambda hp, bi: (bi, 0, hp)),
        out_shape=jax.ShapeDtypeStruct((b, SEQ, D_ATTN), BF16),
        scratch_shapes=[pltpu.VMEM((SEQ, 128), BF16), pltpu.VMEM((SEQ, 128), BF16),
                        pltpu.VMEM((9, 2 * QB_ROWS * QB_COLS, KB_ROWS * KB_COLS), F32),
                        pltpu.VMEM((GRID_W // QB_COLS, 2 * QB_ROWS * QB_COLS, KB_ROWS * KB_COLS), F32),
                        pltpu.VMEM((GRID_W // QB_COLS, 2 * QB_ROWS * QB_COLS, KB_ROWS * KB_COLS), BF16),
                        pltpu.VMEM((GRID_W // QB_COLS, 2 * QB_ROWS * QB_COLS, 128), F32)],
        compiler_params=pltpu.CompilerParams(
            dimension_semantics=("arbitrary", "arbitrary"), vmem_limit_bytes=V7X_VMEM_LIMIT),
        name="nattn",
    )(qkv3, qkv3, qkv3, bias_cols)


Z_PITCH = 72
N_CBLK = D_FOUR // 128


def _fourier_tables():
    n = 64
    k = np.arange(n)
    ang = 2.0 * np.pi * np.outer(k, k) / n
    c64, s64 = np.cos(ang), np.sin(ang)
    w1 = np.concatenate([c64, -s64], axis=0)
    t1p = np.arange(n)[:, None, None]
    t2p = np.arange(n)[None, :, None]
    t2 = np.arange(n)[None, None, :]
    th = 2.0 * np.pi * ((t2 * (t1p + n * t2p)) % SEQ) / SEQ
    cc, ss = np.cos(th), np.sin(th)
    m2 = np.concatenate([np.concatenate([cc, ss], axis=2), np.concatenate([-ss, cc], axis=2)], axis=1)
    cbd = np.kron(np.eye(4), c64)
    sbd = np.kron(np.eye(4), s64)
    cs = np.concatenate([cbd, sbd], axis=0)
    return w1.astype(np.float32), m2.astype(np.float32), cs.astype(np.float32)


_W1_NP, _M2_NP, _CS_NP = _fourier_tables()


HALF_CBLK = N_CBLK // 2
T2_UNROLL = 16


def _fourier_kernel(u_ref, w1_ref, m2_ref, cs_ref, wbd_ref, bf_ref, y_ref, zs_ref):
    for cb in range(HALF_CBLK):
        for k in range(GRID_PITCH - GRID_W):
            y_ref[cb, pl.ds(GRID_W + k, ROWS, stride=GRID_PITCH), :] = jnp.zeros((ROWS, 128), F32)

    def dft_cols(i, carry):
        for k in range(T2_UNROLL):
            t2 = i * T2_UNROLL + k
            x = jnp.concatenate([u_ref[cb, pl.ds(t2, ROWS, stride=GRID_PITCH), :] for cb in range(HALF_CBLK)],
                                axis=1).astype(BF16)
            z = jnp.dot(w1_ref[...], x, preferred_element_type=F32)
            for cb in range(HALF_CBLK):
                zs_ref[cb, pl.ds(t2, 128, stride=Z_PITCH), :] = z[:, cb * 128:(cb + 1) * 128]
        return carry

    lax.fori_loop(0, GRID_W // T2_UNROLL, dft_cols, 0)

    def dft_rows(a, carry):
        xs = []
        for jo in range(8):
            t1p = a * 8 + jo
            rhs = jnp.concatenate(
                [jnp.concatenate([zs_ref[cb, pl.ds(pl.multiple_of((part * 64 + t1p) * Z_PITCH, 8), 64), :]
                                  for cb in range(HALF_CBLK)], axis=1) for part in range(2)], axis=0)
            xs.append(jnp.dot(m2_ref[t1p], rhs.astype(BF16), preferred_element_type=F32))
        xr = jnp.concatenate([x[:64] for x in xs], axis=0).astype(BF16)
        xi = jnp.concatenate([x[64:] for x in xs], axis=0).astype(BF16)
        lhs = jnp.concatenate([xr, xi], axis=1)
        f = jnp.dot(lhs, cs_ref[...], preferred_element_type=F32) * (1.0 / 512.0)
        y = jnp.dot(f.astype(BF16), wbd_ref[...], preferred_element_type=F32) + bf_ref[...]
        for jo in range(8):
            t1p = a * 8 + jo
            for cb in range(HALF_CBLK):
                y_ref[cb, pl.ds(t1p, ROWS, stride=GRID_PITCH), :] = y[jo * 64:(jo + 1) * 64, cb * 128:(cb + 1) * 128]
        return carry

    lax.fori_loop(0, GRID_W // 8, dft_rows, 0)


def _fourier(u_p, wbd, bf):
    b = u_p.shape[1]
    w1 = jnp.asarray(_W1_NP).astype(BF16)
    m2 = jnp.asarray(_M2_NP).astype(BF16)
    cs = jnp.asarray(_CS_NP).astype(BF16)
    half_blk = pl.BlockSpec((HALF_CBLK, None, ROWS * GRID_PITCH, 128), lambda bi, hf: (hf, bi, 0, 0))
    return pl.pallas_call(
        _fourier_kernel,
        grid=(b, 2),
        in_specs=[
            half_blk,
            pl.BlockSpec((128, 64), lambda bi, hf: (0, 0)),
            pl.BlockSpec((64, 128, 128), lambda bi, hf: (0, 0, 0)),
            pl.BlockSpec((512, 256), lambda bi, hf: (0, 0)),
            pl.BlockSpec((None, 256, 256), lambda bi, hf: (hf, 0, 0)),
            pl.BlockSpec((1, 256), lambda bi, hf: (0, hf)),
        ],
        out_specs=half_blk,
        out_shape=jax.ShapeDtypeStruct(u_p.shape, F32),
        scratch_shapes=[pltpu.VMEM((HALF_CBLK, 128 * Z_PITCH, 128), F32)],
        compiler_params=pltpu.CompilerParams(
            dimension_semantics=("parallel", "parallel"), vmem_limit_bytes=V7X_VMEM_LIMIT),
        name="fourier",
    )(u_p, w1, m2, cs, wbd, bf)


def _mixout_kernel(oa_ref, yf_ref, x_ref, ga_ref, gf_ref, wout_ref, gm_ref, wrt_ref, br_ref, tri_ref,
                   h_ref, bucket_ref, rank_ref, cnt_ref, carry_ref):
    i = pl.program_id(0)

    @pl.when(i == 0)
    def _():
        carry_ref[...] = jnp.zeros_like(carry_ref)

    carry = carry_ref[...]
    for k in range(OUT_SUBTILES):
        carry = _mixout_subtile(k, carry, oa_ref, yf_ref, x_ref, ga_ref, gf_ref, wout_ref, gm_ref, wrt_ref,
                                br_ref, tri_ref, h_ref, bucket_ref, rank_ref)
    carry_ref[...] = carry
    cnt_ref[...] = carry


def _mixout_subtile(k, carry, oa_ref, yf_ref, x_ref, ga_ref, gf_ref, wout_ref, gm_ref, wrt_ref, br_ref, tri_ref,
                    h_ref, bucket_ref, rank_ref):
    rows_k = pl.ds(k * TM_OUT, TM_OUT)
    na = _rms(oa_ref[rows_k, :].astype(F32), ga_ref[...]).astype(BF16)
    grid_rows = [k * (TM_OUT // GRID_W) + r for r in range(TM_OUT // GRID_W)]
    yf = jnp.concatenate(
        [jnp.concatenate([yf_ref[cb, gr * GRID_PITCH:gr * GRID_PITCH + GRID_W, :] for gr in grid_rows], axis=0)
         for cb in range(N_CBLK)], axis=1)
    nf = _rms(yf, gf_ref[...]).astype(BF16)
    merged = jnp.concatenate([na, nf], axis=1)
    h = x_ref[rows_k, :] + jnp.dot(merged, wout_ref[...], preferred_element_type=F32)
    for cb in range(ROW_TILE):
        h_ref[pl.ds(k * TM_OUT * ROW_TILE + cb, TM_OUT, stride=ROW_TILE), :] = h[:, cb * 128:(cb + 1) * 128]
    hn = _rms(h, gm_ref[...]).astype(BF16)
    lt = lax.dot_general(wrt_ref[...], hn, (((1,), (1,)), ((), ())), preferred_element_type=F32)
    lt = lt + br_ref[...]
    c = [lt[k:k + 1] for k in range(N_GROUPS)]
    cmax = jnp.maximum(jnp.maximum(c[0], c[1]), jnp.maximum(c[2], c[3]))
    e = [jnp.exp(ck - cmax) for ck in c]
    esum = (e[0] + e[1]) + (e[2] + e[3])
    p = [ek / esum for ek in e]
    pmax = jnp.maximum(jnp.maximum(p[0], p[1]), jnp.maximum(p[2], p[3]))
    g = jnp.where(p[0] == pmax, 0, jnp.where(p[1] == pmax, 1, jnp.where(p[2] == pmax, 2, 3))).astype(I32)
    fine = jnp.where(g == 0, lt[8:16], jnp.where(g == 1, lt[16:24], jnp.where(g == 2, lt[24:32], lt[32:40])))
    rows = lax.broadcasted_iota(I32, fine.shape, 0)
    v1 = jnp.max(fine, axis=0, keepdims=True)
    i1 = jnp.min(jnp.where(fine == v1, rows, EPG), axis=0, keepdims=True)
    rest = jnp.where(rows == i1, -jnp.inf, fine)
    v2 = jnp.max(rest, axis=0, keepdims=True)
    i2 = jnp.min(jnp.where(rest == v2, rows, EPG), axis=0, keepdims=True)
    lo = jnp.minimum(i1, i2)
    hi = jnp.maximum(i1, i2)
    pair = lax.shift_right_logical(lo * (2 * EPG - 1 - lo), 1) + (hi - lo - 1)
    bucket = g * N_PAIRS + pair
    bucket_ref[k] = bucket
    brow = lax.broadcasted_iota(I32, (BUCKET_LANES, TM_OUT), 0)
    onehot = (brow == bucket).astype(F32)
    prefix = jnp.dot(onehot.astype(BF16), tri_ref[...], preferred_element_type=F32)
    rank = jnp.sum(onehot * (prefix + carry), axis=0, keepdims=True)
    rank_ref[k] = rank.astype(I32)
    return carry + jnp.sum(onehot, axis=1, keepdims=True)


def _mixout(oa, yf, x2, ga, gf, wout_bf, gm, wrt, br, tri):
    n = x2.shape[0]
    nt = n // TM_OUT
    rows_step = TM_OUT * OUT_SUBTILES
    full = lambda *shape: pl.BlockSpec(shape, lambda i: (0,) * len(shape))
    row3 = pl.BlockSpec((OUT_SUBTILES, 1, TM_OUT), lambda i: (i, 0, 0))
    return pl.pallas_call(
        _mixout_kernel,
        grid=(n // rows_step,),
        in_specs=[
            pl.BlockSpec((rows_step, D_ATTN), lambda i: (i, 0)),
            pl.BlockSpec((N_CBLK, rows_step // GRID_W * GRID_PITCH, 128), lambda i: (0, i, 0)),
            pl.BlockSpec((rows_step, D_MODEL), lambda i: (i, 0)),
            full(1, D_ATTN), full(1, D_FOUR), full(D_MODEL, D_MODEL), full(1, D_MODEL),
            full(BUCKET_LANES, D_MODEL), full(BUCKET_LANES, 1), full(TM_OUT, TM_OUT),
        ],
        out_specs=[
            pl.BlockSpec((rows_step * ROW_TILE, 128), lambda i: (i, 0)),
            row3, row3,
            full(BUCKET_LANES, 1),
        ],
        out_shape=[
            jax.ShapeDtypeStruct((n * ROW_TILE, 128), F32),
            jax.ShapeDtypeStruct((nt, 1, TM_OUT), I32),
            jax.ShapeDtypeStruct((nt, 1, TM_OUT), I32),
            jax.ShapeDtypeStruct((BUCKET_LANES, 1), F32),
        ],
        scratch_shapes=[pltpu.VMEM((BUCKET_LANES, 1), F32)],
        compiler_params=pltpu.CompilerParams(
            dimension_semantics=("arbitrary",), vmem_limit_bytes=V7X_VMEM_LIMIT),
        name="mixout",
    )(oa, yf, x2, ga, gf, wout_bf, gm, wrt, br, tri)


def _pair_tables():
    lo, hi = [], []
    for a in range(EPG):
        for b in range(a + 1, EPG):
            lo.append(a)
            hi.append(b)
    return np.asarray(lo, np.int32), np.asarray(hi, np.int32)


_PAIR_LO, _PAIR_HI = _pair_tables()


def _dispatch_kernel(tnv_ref, dest_ref, h_ref, wg_ref, wu_ref, wd_ref, hs_hbm, wg_out, wu_out, wd_out,
                     zbuf, zsem, sem):
    k = pl.program_id(0)
    wg_out[...] = wg_ref[...].astype(BF16)
    wu_out[...] = wu_ref[...].astype(BF16)
    wd_out[...] = wd_ref[...].astype(BF16)
    tile_rows = TM_MOE * ROW_TILE
    n_tiles = hs_hbm.shape[0] // tile_rows

    def for_tiles(pred, which, action):
        def body(t, c):
            @pl.when(pred(tnv_ref[t]))
            def _():
                dst = hs_hbm.at[pl.ds(pl.multiple_of(t * tile_rows, tile_rows), tile_rows)]
                action(pltpu.make_async_copy(zbuf, dst, zsem.at[which]))
            return c
        lax.fori_loop(0, n_tiles, body, 0)

    partly_owned = lambda nv: jnp.logical_and(nv > 0, nv < TM_MOE)
    unowned = lambda nv: nv == 0

    @pl.when(k == 0)
    def _():
        zbuf[...] = jnp.zeros_like(zbuf)
        for_tiles(partly_owned, 0, lambda c: c.start())
        for_tiles(unowned, 1, lambda c: c.start())
        for_tiles(partly_owned, 0, lambda c: c.wait())

    @pl.when(k == pl.num_programs(0) - 1)
    def _():
        for_tiles(unowned, 1, lambda c: c.wait())

    def rows(r8, c):
        for u in range(8):
            r = r8 * 8 + u
            dst = pl.multiple_of(dest_ref[0, 0, r] * ROW_TILE, ROW_TILE)
            pltpu.make_async_copy(h_ref.at[pl.ds(pl.multiple_of(r * ROW_TILE, ROW_TILE), ROW_TILE)],
                                  hs_hbm.at[pl.ds(dst, ROW_TILE)], sem).start(priority=u % 2)
        return c

    lax.fori_loop(0, ROWS_PER_STEP // 8, rows, 0)
    pltpu.make_async_copy(h_ref, hs_hbm.at[pl.ds(0, ROWS_PER_STEP * ROW_TILE)], sem).wait()


def _dispatch(h_rt, dest3, tile_nv, n_slots, w_gate, w_up, w_down):
    n = h_rt.shape[0] // ROW_TILE
    n_steps = n // ROWS_PER_STEP
    assert N_EXPERTS % n_steps == 0
    eps = N_EXPERTS // n_steps
    wspec = lambda shape: pl.BlockSpec((eps,) + shape, lambda k, *_: (k, 0, 0))
    grid_spec = pltpu.PrefetchScalarGridSpec(
        num_scalar_prefetch=1,
        grid=(n_steps,),
        in_specs=[
            pl.BlockSpec((1, 1, ROWS_PER_STEP), lambda k, *_: (k, 0, 0), memory_space=pltpu.SMEM),
            pl.BlockSpec((ROWS_PER_STEP * ROW_TILE, 128), lambda k, *_: (k, 0)),
            wspec((D_MODEL, D_EXPERT)), wspec((D_MODEL, D_EXPERT)), wspec((D_EXPERT, D_MODEL)),
        ],
        out_specs=[pl.BlockSpec(memory_space=pl.ANY),
                   wspec((D_MODEL, D_EXPERT)), wspec((D_MODEL, D_EXPERT)), wspec((D_EXPERT, D_MODEL))],
        scratch_shapes=[pltpu.VMEM((TM_MOE * ROW_TILE, 128), F32),
                        pltpu.SemaphoreType.DMA((2,)), pltpu.SemaphoreType.DMA(())],
    )
    return pl.pallas_call(
        _dispatch_kernel,
        grid_spec=grid_spec,
        out_shape=[jax.ShapeDtypeStruct((n_slots * ROW_TILE, 128), F32),
                   jax.ShapeDtypeStruct(w_gate.shape, BF16), jax.ShapeDtypeStruct(w_up.shape, BF16),
                   jax.ShapeDtypeStruct(w_down.shape, BF16)],
        compiler_params=pltpu.CompilerParams(
            dimension_semantics=("arbitrary",), vmem_limit_bytes=V7X_VMEM_LIMIT),
        name="dispatch",
    )(tile_nv, dest3, h_rt, w_gate, w_up, w_down)


def _combine_kernel(dest_ref, ys_hbm, o_ref, buf, sem):
    k = pl.program_id(0)
    n_blocks = pl.num_programs(0) - 1
    slot = k % 2

    def start_rows(r8):
        for u in range(8):
            r = r8 * 8 + u
            src = pl.multiple_of(dest_ref[0, 0, r] * ROW_TILE, ROW_TILE)
            pltpu.make_async_copy(ys_hbm.at[pl.ds(src, ROW_TILE)],
                                  buf.at[slot, pl.ds(pl.multiple_of(r * ROW_TILE, ROW_TILE), ROW_TILE)],
                                  sem.at[slot]).start(priority=u % 2)

    def unpack_rows(r8):
        base = pl.multiple_of(r8 * (8 * ROW_TILE), 8 * ROW_TILE)
        for cb in range(ROW_TILE):
            o_ref[pl.ds(pl.multiple_of(r8 * 8, 8), 8), cb * 128:(cb + 1) * 128] = (
                buf[1 - slot, pl.ds(base + cb, 8, stride=ROW_TILE), :])

    def loop(*parts):
        def body(r8, c):
            for part in parts:
                part(r8)
            return c
        lax.fori_loop(0, ROWS_PER_STEP // 8, body, 0)

    @pl.when(k > 0)
    def _():
        pltpu.make_async_copy(ys_hbm.at[pl.ds(0, ROWS_PER_STEP * ROW_TILE)], buf.at[1 - slot],
                              sem.at[1 - slot]).wait()

    @pl.when(k == 0)
    def _():
        loop(start_rows)

    @pl.when(jnp.logical_and(k > 0, k < n_blocks))
    def _():
        loop(start_rows, unpack_rows)

    @pl.when(k == n_blocks)
    def _():
        loop(unpack_rows)


def _combine(ys_rt, dest3, n):
    n_blocks = n // ROWS_PER_STEP
    return pl.pallas_call(
        _combine_kernel,
        grid=(n_blocks + 1,),
        in_specs=[
            pl.BlockSpec((1, 1, ROWS_PER_STEP), lambda k: (jnp.minimum(k, n_blocks - 1), 0, 0),
                         memory_space=pltpu.SMEM),
            pl.BlockSpec(memory_space=pl.ANY),
        ],
        out_specs=pl.BlockSpec((ROWS_PER_STEP, D_MODEL), lambda k: (jnp.maximum(k - 1, 0), 0)),
        out_shape=jax.ShapeDtypeStruct((n, D_MODEL), F32),
        scratch_shapes=[pltpu.VMEM((2, ROWS_PER_STEP * ROW_TILE, 128), F32), pltpu.SemaphoreType.DMA((2,))],
        compiler_params=pltpu.CompilerParams(
            dimension_semantics=("arbitrary",), vmem_limit_bytes=V7X_VMEM_LIMIT),
        name="combine",
    )(dest3, ys_rt)


def _moe_kernel(tg_ref, tlo_ref, thi_ref, nused_ref, hs_ref, wg_ref, wu_ref, wd_ref, wr_ref, br_ref,
                gm_ref, gfin_ref, ys_ref):
    step = pl.program_id(0)

    def tiles(t, k, n_tiles):
        rows = n_tiles * TM_MOE
        base = k * TM_MOE * ROW_TILE
        lane = lax.broadcasted_iota(I32, (rows, BUCKET_LANES), 1)
        hrows = jnp.concatenate(
            [hs_ref[pl.ds(base + cb, rows, stride=ROW_TILE), :] for cb in range(ROW_TILE)], axis=1)
        hn = _rms(hrows, gm_ref[...]).astype(BF16)
        logits = jnp.dot(hn, wr_ref[...], preferred_element_type=F32) + br_ref[...]
        g, lo, hi = tg_ref[t], tlo_ref[t], thi_ref[t]
        coarse = jnp.where(lane < N_GROUPS, logits, -jnp.inf)
        ec = jnp.exp(coarse - jnp.max(coarse, axis=-1, keepdims=True))
        pick = lambda col, v: jnp.sum(jnp.where(lane == col, v, 0.0), axis=-1, keepdims=True)
        g_w = pick(g, ec) / jnp.sum(ec, axis=-1, keepdims=True)
        f_lo = pick(8 + g * EPG + lo, logits)
        f_hi = pick(8 + g * EPG + hi, logits)
        f_max = jnp.maximum(f_lo, f_hi)
        e_lo = jnp.exp(f_lo - f_max)
        e_hi = jnp.exp(f_hi - f_max)
        den = e_lo + e_hi

        def expert(e, w):
            gate = jnp.dot(hn, wg_ref[e], preferred_element_type=F32)
            up = jnp.dot(hn, wu_ref[e], preferred_element_type=F32)
            act = (gate * jax.nn.sigmoid(gate) * up).astype(BF16)
            return w * jnp.dot(act, wd_ref[e], preferred_element_type=F32)

        y = expert(lo, (e_lo / den) * g_w) + expert(hi, (e_hi / den) * g_w)
        res = _rms(hrows + y, gfin_ref[...])
        for cb in range(ROW_TILE):
            ys_ref[pl.ds(base + cb, rows, stride=ROW_TILE), :] = res[:, cb * 128:(cb + 1) * 128]

    @pl.when(step * TILES_PER_STEP < nused_ref[0])
    def _():
        for p in range(TILES_PER_STEP // 2):
            t = step * TILES_PER_STEP + 2 * p
            same = jnp.logical_and(tlo_ref[t] == tlo_ref[t + 1], thi_ref[t] == thi_ref[t + 1])

            @pl.when(same)
            def _():
                tiles(t, 2 * p, 2)

            @pl.when(jnp.logical_not(same))
            def _():
                tiles(t, 2 * p, 1)
                tiles(t + 1, 2 * p + 1, 1)

    @pl.when(step * TILES_PER_STEP >= nused_ref[0])
    def _():
        ys_ref[...] = jnp.zeros_like(ys_ref)


def _moe(hs_rt, tile_g, tile_lo, tile_hi, n_used, wg, wu, wd, wr, br, gm, gfin):
    rows_step = TM_MOE * TILES_PER_STEP * ROW_TILE
    n_steps = hs_rt.shape[0] // rows_step
    by_group = lambda s, tg, *_: (tg[s * TILES_PER_STEP], 0, 0, 0)
    full2 = lambda a, c: pl.BlockSpec((a, c), lambda s, *_: (0, 0))

    def hs_index(s, tg, tlo, thi, nu):
        last_step = jnp.maximum(nu[0] - 1, 0) // TILES_PER_STEP
        return (jnp.minimum(s, last_step), 0)

    grid_spec = pltpu.PrefetchScalarGridSpec(
        num_scalar_prefetch=4,
        grid=(n_steps,),
        in_specs=[
            pl.BlockSpec((rows_step, 128), hs_index),
            pl.BlockSpec((None, EPG, D_MODEL, D_EXPERT), by_group),
            pl.BlockSpec((None, EPG, D_MODEL, D_EXPERT), by_group),
            pl.BlockSpec((None, EPG, D_EXPERT, D_MODEL), by_group),
            full2(D_MODEL, BUCKET_LANES), full2(1, BUCKET_LANES), full2(1, D_MODEL), full2(1, D_MODEL),
        ],
        out_specs=pl.BlockSpec((rows_step, 128), lambda s, *_: (s, 0)),
    )
    return pl.pallas_call(
        _moe_kernel,
        grid_spec=grid_spec,
        out_shape=jax.ShapeDtypeStruct(hs_rt.shape, F32),
        compiler_params=pltpu.CompilerParams(
            dimension_semantics=("arbitrary",), vmem_limit_bytes=V7X_VMEM_LIMIT),
        name="moe",
    )(tile_g, tile_lo, tile_hi, n_used, hs_rt, wg, wu, wd, wr, br, gm, gfin)


def _bucket_plan(bucket, rank, counts, n):
    nt = n // TM_MOE + N_BUCKETS + N_GROUPS * (TILES_PER_STEP - 1)
    nt = -(-nt // TILES_PER_STEP) * TILES_PER_STEP
    tiles_b = (counts + (TM_MOE - 1)) // TM_MOE
    tiles_g = jnp.sum(tiles_b.reshape(N_GROUPS, N_PAIRS), axis=1)
    extra_g = (-tiles_g) % TILES_PER_STEP
    is_last = (np.arange(N_PAIRS) == N_PAIRS - 1)[None, :]
    tiles_b = (tiles_b.reshape(N_GROUPS, N_PAIRS) + jnp.where(is_last, extra_g[:, None], 0)).reshape(N_BUCKETS)
    tile_end = jnp.cumsum(tiles_b)
    tile_start = tile_end - tiles_b
    n_used = tile_end[-1]
    b_ids = jnp.arange(N_BUCKETS, dtype=I32)
    dest = rank + TM_MOE * jnp.sum(jnp.where(bucket[:, None] == b_ids[None, :], tile_start[None, :], 0), axis=1)
    t_idx = jnp.arange(nt, dtype=I32)
    tb = jnp.sum((tile_end[None, :] <= t_idx[:, None]).astype(I32), axis=1)
    tb_last = jnp.sum((tile_end <= n_used - 1).astype(I32))
    tb = jnp.minimum(jnp.where(t_idx < n_used, tb, tb_last), N_BUCKETS - 1)
    sel = tb[:, None] == b_ids[None, :]
    pick = lambda table: jnp.sum(jnp.where(sel, table[None, :], 0), axis=1).astype(I32)
    tile_g = tb // N_PAIRS
    tile_lo = pick(jnp.asarray(np.tile(_PAIR_LO, N_GROUPS)))
    tile_hi = pick(jnp.asarray(np.tile(_PAIR_HI, N_GROUPS)))
    nv = jnp.clip(pick(counts) - (t_idx - pick(tile_start)) * TM_MOE, 0, TM_MOE)
    tile_nv = jnp.where(t_idx < n_used, nv, 0).astype(I32)
    return dest.astype(I32), tile_g.astype(I32), tile_lo, tile_hi, tile_nv, n_used.reshape(1).astype(I32)


def kernel(x, norm_mix, w_in, rpb, w_four, b_four, g_attn_out, g_four_out, w_out, norm_moe,
           w_router_coarse, b_router_coarse, w_router_fine, b_router_fine, w_gate, w_up, w_down, norm_final):
    b, seq, d = x.shape
    assert (seq, d) == (SEQ, D_MODEL) and norm_mix.shape[0] == 1
    n = b * seq
    x2 = x.reshape(n, d)

    qkv, u = _inproj(x2, norm_mix[0][None], w_in[0].astype(BF16))

    oa = _attention(qkv.reshape(b, seq, 3 * D_ATTN), _bias_columns(rpb[0]))

    eye4 = jnp.eye(4, dtype=F32)
    wf = w_four[0].reshape(2, 4, FOUR_GROUP_DIM, FOUR_GROUP_DIM)
    wbd = (eye4[None, :, None, :, None] * wf[:, :, :, None, :]).reshape(2, 256, 256).astype(BF16)
    yf = _fourier(u, wbd, b_four[0][None]).reshape(N_CBLK, b * ROWS * GRID_PITCH, 128)

    wrt = jnp.zeros((BUCKET_LANES, d), F32)
    wrt = wrt.at[0:N_GROUPS].set(w_router_coarse[0].T).at[8:8 + N_EXPERTS].set(w_router_fine[0].T)
    br = jnp.zeros((BUCKET_LANES, 1), F32)
    br = br.at[0:N_GROUPS, 0].set(b_router_coarse[0]).at[8:8 + N_EXPERTS, 0].set(b_router_fine[0])
    tri = (np.arange(TM_OUT)[:, None] < np.arange(TM_OUT)[None, :]).astype(np.float32)
    wrt_bf = wrt.astype(BF16)
    h_rt, bucket, rank, cnt = _mixout(
        oa.reshape(n, D_ATTN), yf, x2, g_attn_out[0][None], g_four_out[0][None],
        w_out[0].astype(BF16), norm_moe[0][None], wrt_bf, br, jnp.asarray(tri, BF16))

    counts = cnt[:N_BUCKETS, 0].astype(I32)
    dest, tile_g, tile_lo, tile_hi, tile_nv, n_used = _bucket_plan(bucket.reshape(n), rank.reshape(n), counts, n)
    dest3 = dest.reshape(n // ROWS_PER_STEP, 1, ROWS_PER_STEP)
    hs_rt, wg_bf, wu_bf, wd_bf = _dispatch(h_rt, dest3, tile_nv, tile_nv.shape[0] * TM_MOE,
                                           w_gate[0], w_up[0], w_down[0])
    shape_e = (N_GROUPS, EPG)
    ys_rt = _moe(hs_rt, tile_g, tile_lo, tile_hi, n_used,
                 wg_bf.reshape(shape_e + (d, D_EXPERT)),
                 wu_bf.reshape(shape_e + (d, D_EXPERT)),
                 wd_bf.reshape(shape_e + (D_EXPERT, d)),
                 wrt_bf.T, br.T, norm_moe[0][None], norm_final[None])
    return _combine(ys_rt, dest3, n).reshape(b, seq, d)
```

```python
import numpy as np
import jax
import jax.numpy as jnp
from jax import lax
from jax.experimental import pallas as pl
from jax.experimental.pallas import tpu as pltpu

F32 = jnp.float32
BF16 = jnp.bfloat16
I32 = jnp.int32

D_MODEL = 1024
SEQ = 4096
GRID_W = 64
ROWS = SEQ // GRID_W
D_ATTN = 512
D_FOUR = 512
N_HEADS = 8
HEAD_DIM = 64
WIN_H = 8
WIN_W = 16
N_FOUR_GROUPS = 8
FOUR_GROUP_DIM = 64
D_PROJ = 3 * D_ATTN + D_FOUR
N_GROUPS = 4
EPG = 8
N_EXPERTS = N_GROUPS * EPG
D_EXPERT = 256
EPS = 1e-6
NEG = -1e30

V7X_VMEM_LIMIT = 56 * 1024 * 1024

TM_IN = 1024
TM_OUT = 1024
OUT_SUBTILES = 1
TM_MOE = 128
ROW_TILE = D_MODEL // 128
ROWS_PER_STEP = 2048
TILES_PER_STEP = 8
N_PAIRS = EPG * (EPG - 1) // 2
N_BUCKETS = N_GROUPS * N_PAIRS
BUCKET_LANES = 128

QB_ROWS = 8
QB_COLS = 16
KB_ROWS = 16
KB_COLS = 32
GRID_PITCH = 72


def _rms(x, g):
    ms = jnp.mean(x * x, axis=-1, keepdims=True)
    return x * lax.rsqrt(ms + EPS) * g


def _inproj_kernel(x_ref, g_ref, w_ref, qkv_ref, u_ref):
    xn = _rms(x_ref[...], g_ref[...]).astype(BF16)
    p = jnp.dot(xn, w_ref[...], preferred_element_type=F32)
    qkv_ref[:, :D_ATTN] = (p[:, :D_ATTN] * (HEAD_DIM ** -0.5)).astype(BF16)
    qkv_ref[:, D_ATTN:] = p[:, D_ATTN:3 * D_ATTN].astype(BF16)
    for cb in range(D_FOUR // 128):
        lanes = slice(3 * D_ATTN + cb * 128, 3 * D_ATTN + (cb + 1) * 128)
        for r in range(TM_IN // GRID_W):
            u_ref[cb, r * GRID_PITCH:r * GRID_PITCH + GRID_W, :] = p[r * GRID_W:(r + 1) * GRID_W, lanes]
            u_ref[cb, r * GRID_PITCH + GRID_W:(r + 1) * GRID_PITCH, :] = jnp.zeros((GRID_PITCH - GRID_W, 128), F32)


def _inproj(x2, g, w_bf):
    n = x2.shape[0]
    rows_step = TM_IN // GRID_W
    steps_b = ROWS // rows_step
    return pl.pallas_call(
        _inproj_kernel,
        grid=(n // TM_IN,),
        in_specs=[
            pl.BlockSpec((TM_IN, D_MODEL), lambda i: (i, 0)),
            pl.BlockSpec((1, D_MODEL), lambda i: (0, 0)),
            pl.BlockSpec((D_MODEL, D_PROJ), lambda i: (0, 0)),
        ],
        out_specs=[
            pl.BlockSpec((TM_IN, 3 * D_ATTN), lambda i: (i, 0)),
            pl.BlockSpec((D_FOUR // 128, None, rows_step * GRID_PITCH, 128),
                         lambda i: (0, i // steps_b, i % steps_b, 0)),
        ],
        out_shape=[
            jax.ShapeDtypeStruct((n, 3 * D_ATTN), BF16),
            jax.ShapeDtypeStruct((D_FOUR // 128, n // SEQ, ROWS * GRID_PITCH, 128), F32),
        ],
        compiler_params=pltpu.CompilerParams(
            dimension_semantics=("parallel",), vmem_limit_bytes=V7X_VMEM_LIMIT),
        name="inproj",
    )(x2, g, w_bf)


_KCOL_SHIFTED = (False, True, True, False)
_KCOL_OFF = (0, 0, 16, 32)
_COL_TYPE = (0, 1, 1, 2)


def _bias_index_tables():
    dr = np.zeros((9, 128, 512), np.int32)
    dc = np.zeros((9, 128, 512), np.int32)
    ok = np.zeros((9, 128, 512), bool)
    qi, qc = np.divmod(np.arange(128), QB_COLS)
    ki, kc = np.divmod(np.arange(512), KB_COLS)
    for rt, (q0, k0) in enumerate(((0, 0), (8, 4), (56, 48))):
        qrow = q0 + qi
        krow = k0 + ki
        rs = np.clip(qrow - WIN_H // 2, 0, ROWS - WIN_H)
        rok = (krow[None, :] >= rs[:, None]) & (krow[None, :] < rs[:, None] + WIN_H)
        drr = krow[None, :] - qrow[:, None] + (WIN_H - 1)
        for ct, (c0, kc0) in enumerate(((0, 0), (16, 8), (48, 32))):
            qcol = c0 + qc
            kcol = kc0 + kc
            cs = np.clip(qcol - WIN_W // 2, 0, GRID_W - WIN_W)
            cok = (kcol[None, :] >= cs[:, None]) & (kcol[None, :] < cs[:, None] + WIN_W)
            dcc = kcol[None, :] - qcol[:, None] + (WIN_W - 1)
            t = rt * 3 + ct
            ok[t] = rok & cok
            dr[t] = np.where(ok[t], drr, 0)
            dc[t] = np.where(ok[t], dcc, 0)
    return dr, dc, ok


_BIAS_DR, _BIAS_DC, _BIAS_OK = _bias_index_tables()


def _bias_selectors():
    ok = _BIAS_OK.reshape(3, 3, QB_ROWS, QB_COLS, KB_ROWS, KB_COLS)
    dr = _BIAS_DR.reshape(ok.shape)
    dc = _BIAS_DC.reshape(ok.shape)
    row_ok = ok.any(axis=(1, 3, 5))
    col_ok = ok.any(axis=(0, 2, 4))
    dr_r = dr.max(axis=(1, 3, 5))
    dc_c = dc.max(axis=(0, 2, 4))
    sc = (np.arange(2 * WIN_W - 1)[:, None, None, None] == dc_c[None]) & col_ok[None]
    return row_ok, dr_r, col_ok, sc.astype(np.float32)


_BIAS_ROW_OK, _BIAS_ROW_DR, _BIAS_COL_OK, _BIAS_SC = _bias_selectors()


def _bias_columns(rpb):
    sc = jnp.asarray(np.tile(_BIAS_SC, 128 // KB_COLS))
    ok = np.tile(_BIAS_COL_OK, 128 // KB_COLS)
    t1 = jnp.sum(rpb[:, :, :, None, None, None] * sc[None, None], axis=2)
    return jnp.where(ok[None, None], t1, NEG)


def _attn_kernel(q_ref, k_ref, v_ref, bcol_ref, o_ref, ksh_ref, vsh_ref, bias_ref, s_ref, p_ref, l_ref):
    @pl.when(pl.program_id(1) == 0)
    def _():
        key_row = lax.broadcasted_iota(I32, (QB_COLS, KB_ROWS * KB_COLS), 1) // KB_COLS
        for hh in range(2):
            for rt in range(3):
                for ct in range(3):
                    for i in range(QB_ROWS):
                        acc = jnp.full((QB_COLS, KB_ROWS * KB_COLS), NEG, F32)
                        for y in range(KB_ROWS):
                            if _BIAS_ROW_OK[rt, i, y]:
                                cols = bcol_ref[hh, int(_BIAS_ROW_DR[rt, i, y]), ct]
                                cols = jnp.concatenate([cols] * (KB_ROWS * KB_COLS // 128), axis=1)
                                acc = jnp.where(key_row == y, cols, acc)
                        bias_ref[rt * 3 + ct, pl.ds(hh * 128 + i * QB_COLS, QB_COLS), :] = acc

    zpad = jnp.zeros((8, 128), F32)
    ksh_ref[...] = jnp.concatenate([k_ref[...].astype(F32)[8:], zpad], axis=0).astype(BF16)
    vsh_ref[...] = jnp.concatenate([v_ref[...].astype(F32)[8:], zpad], axis=0).astype(BF16)
    lane = lax.broadcasted_iota(I32, (1, 128), 1)
    head_masks = (lane < HEAD_DIM, lane >= HEAD_DIM)

    n_rb = ROWS // QB_ROWS
    n_q = GRID_W // QB_COLS

    def window(ref_plain, ref_shift, rb, j):
        ks = jnp.clip(QB_ROWS * rb - WIN_H // 2, 0, ROWS - KB_ROWS)
        src = ref_shift if _KCOL_SHIFTED[j] else ref_plain
        return jnp.concatenate(
            [src[pl.ds(pl.multiple_of((ks + i) * GRID_W + _KCOL_OFF[j], 16), KB_COLS), :] for i in range(KB_ROWS)],
            axis=0)

    def scores(rb, j):
        q = jnp.concatenate(
            [q_ref[pl.ds(pl.multiple_of((QB_ROWS * rb + i) * GRID_W + QB_COLS * j, 16), QB_COLS), :]
             for i in range(QB_ROWS)], axis=0)
        qm = jnp.concatenate([jnp.where(hm, q, jnp.zeros_like(q)) for hm in head_masks], axis=0)
        s_ref[j] = lax.dot_general(qm, window(k_ref, ksh_ref, rb, j), (((1,), (1,)), ((), ())),
                                   preferred_element_type=F32)

    def softmax(rb, j):
        rt = jnp.where(rb == 0, 0, jnp.where(rb == n_rb - 1, 2, 1))
        s = s_ref[j] + bias_ref[rt * 3 + _COL_TYPE[j]]
        e = jnp.exp(s - jnp.max(s, axis=-1, keepdims=True))
        p_ref[j] = e.astype(BF16)
        l_ref[j] = jnp.broadcast_to(jnp.sum(e, axis=-1, keepdims=True), (2 * QB_ROWS * QB_COLS, 128))

    def values(rb, j):
        o = jnp.dot(p_ref[j], window(v_ref, vsh_ref, rb, j), preferred_element_type=F32) / l_ref[j]
        out = jnp.where(head_masks[0], o[:128], o[128:]).astype(BF16)
        for i in range(QB_ROWS):
            o_ref[pl.ds(pl.multiple_of((QB_ROWS * rb + i) * GRID_W + QB_COLS * j, 16), QB_COLS), :] = (
                out[QB_COLS * i:QB_COLS * (i + 1)])

    def stage(fn, rb):
        for j in range(n_q):
            fn(jnp.asarray(rb, I32), j)

    stage(scores, 0)
    stage(softmax, 0)
    stage(scores, 1)

    def pipeline_step(i, carry):
        stage(values, i - 2)
        stage(softmax, i - 1)
        stage(scores, i)
        return carry

    lax.fori_loop(2, n_rb, pipeline_step, 0)
    stage(values, n_rb - 2)
    stage(softmax, n_rb - 1)
    stage(values, n_rb - 1)


def _attention(qkv3, bias_cols):
    b = qkv3.shape[0]
    n_hp = N_HEADS // 2
    blk = lambda off: pl.BlockSpec((None, SEQ, 128), lambda hp, bi: (bi, 0, off + hp))
    return pl.pallas_call(
        _attn_kernel,
        grid=(n_hp, b),
        in_specs=[
            blk(0), blk(n_hp), blk(2 * n_hp),
            pl.BlockSpec((2,) + bias_cols.shape[1:], lambda hp, bi: (hp, 0, 0, 0, 0)),
        ],
        out_specs=pl.BlockSpec((None, SEQ, 128), lambda hp, bi: (bi, 0, hp)),
        out_shape=jax.ShapeDtypeStruct((b, SEQ, D_ATTN), BF16),
        scratch_shapes=[pltpu.VMEM((SEQ, 128), BF16), pltpu.VMEM((SEQ, 128), BF16),
                        pltpu.VMEM((9, 2 * QB_ROWS * QB_COLS, KB_ROWS * KB_COLS), F32),
                        pltpu.VMEM((GRID_W // QB_COLS, 2 * QB_ROWS * QB_COLS, KB_ROWS * KB_COLS), F32),
                        pltpu.VMEM((GRID_W // QB_COLS, 2 * QB_ROWS * QB_COLS, KB_ROWS * KB_COLS), BF16),
                        pltpu.VMEM((GRID_W // QB_COLS, 2 * QB_ROWS * QB_COLS, 128), F32)],
        compiler_params=pltpu.CompilerParams(
            dimension_semantics=("arbitrary", "arbitrary"), vmem_limit_bytes=V7X_VMEM_LIMIT),
        name="nattn",
    )(qkv3, qkv3, qkv3, bias_cols)


Z_PITCH = 72
N_CBLK = D_FOUR // 128


def _fourier_tables():
    n = 64
    k = np.arange(n)
    ang = 2.0 * np.pi * np.outer(k, k) / n
    c64, s64 = np.cos(ang), np.sin(ang)
    w1 = np.concatenate([c64, -s64], axis=0)
    t1p = np.arange(n)[:, None, None]
    t2p = np.arange(n)[None, :, None]
    t2 = np.arange(n)[None, None, :]
    th = 2.0 * np.pi * ((t2 * (t1p + n * t2p)) % SEQ) / SEQ
    cc, ss = np.cos(th), np.sin(th)
    m2 = np.concatenate([np.concatenate([cc, ss], axis=2), np.concatenate([-ss, cc], axis=2)], axis=1)
    cbd = np.kron(np.eye(4), c64)
    sbd = np.kron(np.eye(4), s64)
    cs = np.concatenate([cbd, sbd], axis=0)
    return w1.astype(np.float32), m2.astype(np.float32), cs.astype(np.float32)


_W1_NP, _M2_NP, _CS_NP = _fourier_tables()


HALF_CBLK = N_CBLK // 2
T2_UNROLL = 16


def _fourier_kernel(u_ref, w1_ref, m2_ref, cs_ref, wbd_ref, bf_ref, y_ref, zs_ref):
    for cb in range(HALF_CBLK):
        for k in range(GRID_PITCH - GRID_W):
            y_ref[cb, pl.ds(GRID_W + k, ROWS, stride=GRID_PITCH), :] = jnp.zeros((ROWS, 128), F32)

    def dft_cols(i, carry):
        for k in range(T2_UNROLL):
            t2 = i * T2_UNROLL + k
            x = jnp.concatenate([u_ref[cb, pl.ds(t2, ROWS, stride=GRID_PITCH), :] for cb in range(HALF_CBLK)],
                                axis=1).astype(BF16)
            z = jnp.dot(w1_ref[...], x, preferred_element_type=F32)
            for cb in range(HALF_CBLK):
                zs_ref[cb, pl.ds(t2, 128, stride=Z_PITCH), :] = z[:, cb * 128:(cb + 1) * 128]
        return carry

    lax.fori_loop(0, GRID_W // T2_UNROLL, dft_cols, 0)

    def dft_rows(a, carry):
        xs = []
        for jo in range(8):
            t1p = a * 8 + jo
            rhs = jnp.concatenate(
                [jnp.concatenate([zs_ref[cb, pl.ds(pl.multiple_of((part * 64 + t1p) * Z_PITCH, 8), 64), :]
                                  for cb in range(HALF_CBLK)], axis=1) for part in range(2)], axis=0)
            xs.append(jnp.dot(m2_ref[t1p], rhs.astype(BF16), preferred_element_type=F32))
        xr = jnp.concatenate([x[:64] for x in xs], axis=0).astype(BF16)
        xi = jnp.concatenate([x[64:] for x in xs], axis=0).astype(BF16)
        lhs = jnp.concatenate([xr, xi], axis=1)
        f = jnp.dot(lhs, cs_ref[...], preferred_element_type=F32) * (1.0 / 512.0)
        y = jnp.dot(f.astype(BF16), wbd_ref[...], preferred_element_type=F32) + bf_ref[...]
        for jo in range(8):
            t1p = a * 8 + jo
            for cb in range(HALF_CBLK):
                y_ref[cb, pl.ds(t1p, ROWS, stride=GRID_PITCH), :] = y[jo * 64:(jo + 1) * 64, cb * 128:(cb + 1) * 128]
        return carry

    lax.fori_loop(0, GRID_W // 8, dft_rows, 0)


def _fourier(u_p, wbd, bf):
    b = u_p.shape[1]
    w1 = jnp.asarray(_W1_NP).astype(BF16)
    m2 = jnp.asarray(_M2_NP).astype(BF16)
    cs = jnp.asarray(_CS_NP).astype(BF16)
    half_blk = pl.BlockSpec((HALF_CBLK, None, ROWS * GRID_PITCH, 128), lambda bi, hf: (hf, bi, 0, 0))
    return pl.pallas_call(
        _fourier_kernel,
        grid=(b, 2),
        in_specs=[
            half_blk,
            pl.BlockSpec((128, 64), lambda bi, hf: (0, 0)),
            pl.BlockSpec((64, 128, 128), lambda bi, hf: (0, 0, 0)),
            pl.BlockSpec((512, 256), lambda bi, hf: (0, 0)),
            pl.BlockSpec((None, 256, 256), lambda bi, hf: (hf, 0, 0)),
            pl.BlockSpec((1, 256), lambda bi, hf: (0, hf)),
        ],
        out_specs=half_blk,
        out_shape=jax.ShapeDtypeStruct(u_p.shape, F32),
        scratch_shapes=[pltpu.VMEM((HALF_CBLK, 128 * Z_PITCH, 128), F32)],
        compiler_params=pltpu.CompilerParams(
            dimension_semantics=("parallel", "parallel"), vmem_limit_bytes=V7X_VMEM_LIMIT),
        name="fourier",
    )(u_p, w1, m2, cs, wbd, bf)


def _mixout_kernel(oa_ref, yf_ref, x_ref, ga_ref, gf_ref, wout_ref, gm_ref, wrt_ref, br_ref, tri_ref,
                   h_ref, bucket_ref, rank_ref, cnt_ref, carry_ref):
    i = pl.program_id(0)

    @pl.when(i == 0)
    def _():
        carry_ref[...] = jnp.zeros_like(carry_ref)

    carry = carry_ref[...]
    for k in range(OUT_SUBTILES):
        carry = _mixout_subtile(k, carry, oa_ref, yf_ref, x_ref, ga_ref, gf_ref, wout_ref, gm_ref, wrt_ref,
                                br_ref, tri_ref, h_ref, bucket_ref, rank_ref)
    carry_ref[...] = carry
    cnt_ref[...] = carry


def _mixout_subtile(k, carry, oa_ref, yf_ref, x_ref, ga_ref, gf_ref, wout_ref, gm_ref, wrt_ref, br_ref, tri_ref,
                    h_ref, bucket_ref, rank_ref):
    rows_k = pl.ds(k * TM_OUT, TM_OUT)
    na = _rms(oa_ref[rows_k, :].astype(F32), ga_ref[...]).astype(BF16)
    grid_rows = [k * (TM_OUT // GRID_W) + r for r in range(TM_OUT // GRID_W)]
    yf = jnp.concatenate(
        [jnp.concatenate([yf_ref[cb, gr * GRID_PITCH:gr * GRID_PITCH + GRID_W, :] for gr in grid_rows], axis=0)
         for cb in range(N_CBLK)], axis=1)
    nf = _rms(yf, gf_ref[...]).astype(BF16)
    merged = jnp.concatenate([na, nf], axis=1)
    h = x_ref[rows_k, :] + jnp.dot(merged, wout_ref[...], preferred_element_type=F32)
    for cb in range(ROW_TILE):
        h_ref[pl.ds(k * TM_OUT * ROW_TILE + cb, TM_OUT, stride=ROW_TILE), :] = h[:, cb * 128:(cb + 1) * 128]
    hn = _rms(h, gm_ref[...]).astype(BF16)
    lt = lax.dot_general(wrt_ref[...], hn, (((1,), (1,)), ((), ())), preferred_element_type=F32)
    lt = lt + br_ref[...]
    c = [lt[k:k + 1] for k in range(N_GROUPS)]
    cmax = jnp.maximum(jnp.maximum(c[0], c[1]), jnp.maximum(c[2], c[3]))
    e = [jnp.exp(ck - cmax) for ck in c]
    esum = (e[0] + e[1]) + (e[2] + e[3])
    p = [ek / esum for ek in e]
    pmax = jnp.maximum(jnp.maximum(p[0], p[1]), jnp.maximum(p[2], p[3]))
    g = jnp.where(p[0] == pmax, 0, jnp.where(p[1] == pmax, 1, jnp.where(p[2] == pmax, 2, 3))).astype(I32)
    fine = jnp.where(g == 0, lt[8:16], jnp.where(g == 1, lt[16:24], jnp.where(g == 2, lt[24:32], lt[32:40])))
    rows = lax.broadcasted_iota(I32, fine.shape, 0)
    v1 = jnp.max(fine, axis=0, keepdims=True)
    i1 = jnp.min(jnp.where(fine == v1, rows, EPG), axis=0, keepdims=True)
    rest = jnp.where(rows == i1, -jnp.inf, fine)
    v2 = jnp.max(rest, axis=0, keepdims=True)
    i2 = jnp.min(jnp.where(rest == v2, rows, EPG), axis=0, keepdims=True)
    lo = jnp.minimum(i1, i2)
    hi = jnp.maximum(i1, i2)
    pair = lax.shift_right_logical(lo * (2 * EPG - 1 - lo), 1) + (hi - lo - 1)
    bucket = g * N_PAIRS + pair
    bucket_ref[k] = bucket
    brow = lax.broadcasted_iota(I32, (BUCKET_LANES, TM_OUT), 0)
    onehot = (brow == bucket).astype(F32)
    prefix = jnp.dot(onehot.astype(BF16), tri_ref[...], preferred_element_type=F32)
    rank = jnp.sum(onehot * (prefix + carry), axis=0, keepdims=True)
    rank_ref[k] = rank.astype(I32)
    return carry + jnp.sum(onehot, axis=1, keepdims=True)


def _mixout(oa, yf, x2, ga, gf, wout_bf, gm, wrt, br, tri):
    n = x2.shape[0]
    nt = n // TM_OUT
    rows_step = TM_OUT * OUT_SUBTILES
    full = lambda *shape: pl.BlockSpec(shape, lambda i: (0,) * len(shape))
    row3 = pl.BlockSpec((OUT_SUBTILES, 1, TM_OUT), lambda i: (i, 0, 0))
    return pl.pallas_call(
        _mixout_kernel,
        grid=(n // rows_step,),
        in_specs=[
            pl.BlockSpec((rows_step, D_ATTN), lambda i: (i, 0)),
            pl.BlockSpec((N_CBLK, rows_step // GRID_W * GRID_PITCH, 128), lambda i: (0, i, 0)),
            pl.BlockSpec((rows_step, D_MODEL), lambda i: (i, 0)),
            full(1, D_ATTN), full(1, D_FOUR), full(D_MODEL, D_MODEL), full(1, D_MODEL),
            full(BUCKET_LANES, D_MODEL), full(BUCKET_LANES, 1), full(TM_OUT, TM_OUT),
        ],
        out_specs=[
            pl.BlockSpec((rows_step * ROW_TILE, 128), lambda i: (i, 0)),
            row3, row3,
            full(BUCKET_LANES, 1),
        ],
        out_shape=[
            jax.ShapeDtypeStruct((n * ROW_TILE, 128), F32),
            jax.ShapeDtypeStruct((nt, 1, TM_OUT), I32),
            jax.ShapeDtypeStruct((nt, 1, TM_OUT), I32),
            jax.ShapeDtypeStruct((BUCKET_LANES, 1), F32),
        ],
        scratch_shapes=[pltpu.VMEM((BUCKET_LANES, 1), F32)],
        compiler_params=pltpu.CompilerParams(
            dimension_semantics=("arbitrary",), vmem_limit_bytes=V7X_VMEM_LIMIT),
        name="mixout",
    )(oa, yf, x2, ga, gf, wout_bf, gm, wrt, br, tri)


def _pair_tables():
    lo, hi = [], []
    for a in range(EPG):
        for b in range(a + 1, EPG):
            lo.append(a)
            hi.append(b)
    return np.asarray(lo, np.int32), np.asarray(hi, np.int32)


_PAIR_LO, _PAIR_HI = _pair_tables()


def _dispatch_kernel(tnv_ref, dest_ref, h_ref, wg_ref, wu_ref, wd_ref, hs_hbm, wg_out, wu_out, wd_out,
                     zbuf, zsem, sem):
    k = pl.program_id(0)
    tile_rows = TM_MOE * ROW_TILE
    n_tiles = hs_hbm.shape[0] // tile_rows

    def for_tiles(pred, which, action):
        def body(t, c):
            @pl.when(pred(tnv_ref[t]))
            def _():
                dst = hs_hbm.at[pl.ds(pl.multiple_of(t * tile_rows, tile_rows), tile_rows)]
                action(pltpu.make_async_copy(zbuf, dst, zsem.at[which]))
            return c
        lax.fori_loop(0, n_tiles, body, 0)

    partly_owned = lambda nv: jnp.logical_and(nv > 0, nv < TM_MOE)
    unowned = lambda nv: nv == 0

    @pl.when(k == 0)
    def _():
        zbuf[...] = jnp.zeros_like(zbuf)
        for_tiles(partly_owned, 0, lambda c: c.start())
        for_tiles(unowned, 1, lambda c: c.start())
        for_tiles(partly_owned, 0, lambda c: c.wait())

    @pl.when(k == pl.num_programs(0) - 1)
    def _():
        for_tiles(unowned, 1, lambda c: c.wait())

    def rows(r8, c):
        for u in range(8):
            r = r8 * 8 + u
            dst = pl.multiple_of(dest_ref[0, 0, r] * ROW_TILE, ROW_TILE)
            pltpu.make_async_copy(h_ref.at[pl.ds(pl.multiple_of(r * ROW_TILE, ROW_TILE), ROW_TILE)],
                                  hs_hbm.at[pl.ds(dst, ROW_TILE)], sem).start(priority=u % 2)
        return c

    lax.fori_loop(0, ROWS_PER_STEP // 8, rows, 0)
    wg_out[...] = wg_ref[...].astype(BF16)
    wu_out[...] = wu_ref[...].astype(BF16)
    wd_out[...] = wd_ref[...].astype(BF16)
    pltpu.make_async_copy(h_ref, hs_hbm.at[pl.ds(0, ROWS_PER_STEP * ROW_TILE)], sem).wait()


def _dispatch(h_rt, dest3, tile_nv, n_slots, w_gate, w_up, w_down):
    n = h_rt.shape[0] // ROW_TILE
    n_steps = n // ROWS_PER_STEP
    assert N_EXPERTS % n_steps == 0
    eps = N_EXPERTS // n_steps
    wspec = lambda shape: pl.BlockSpec((eps,) + shape, lambda k, *_: (k, 0, 0))
    grid_spec = pltpu.PrefetchScalarGridSpec(
        num_scalar_prefetch=1,
        grid=(n_steps,),
        in_specs=[
            pl.BlockSpec((1, 1, ROWS_PER_STEP), lambda k, *_: (k, 0, 0), memory_space=pltpu.SMEM),
            pl.BlockSpec((ROWS_PER_STEP * ROW_TILE, 128), lambda k, *_: (k, 0)),
            wspec((D_MODEL, D_EXPERT)), wspec((D_MODEL, D_EXPERT)), wspec((D_EXPERT, D_MODEL)),
        ],
        out_specs=[pl.BlockSpec(memory_space=pl.ANY),
                   wspec((D_MODEL, D_EXPERT)), wspec((D_MODEL, D_EXPERT)), wspec((D_EXPERT, D_MODEL))],
        scratch_shapes=[pltpu.VMEM((TM_MOE * ROW_TILE, 128), F32),
                        pltpu.SemaphoreType.DMA((2,)), pltpu.SemaphoreType.DMA(())],
    )
    return pl.pallas_call(
        _dispatch_kernel,
        grid_spec=grid_spec,
        out_shape=[jax.ShapeDtypeStruct((n_slots * ROW_TILE, 128), F32),
                   jax.ShapeDtypeStruct(w_gate.shape, BF16), jax.ShapeDtypeStruct(w_up.shape, BF16),
                   jax.ShapeDtypeStruct(w_down.shape, BF16)],
        compiler_params=pltpu.CompilerParams(
            dimension_semantics=("arbitrary",), vmem_limit_bytes=V7X_VMEM_LIMIT),
        name="dispatch",
    )(tile_nv, dest3, h_rt, w_gate, w_up, w_down)


def _combine_kernel(dest_ref, ys_hbm, o_ref, buf, sem):
    k = pl.program_id(0)
    n_blocks = pl.num_programs(0) - 1
    slot = k % 2

    def start_rows(r8):
        for u in range(8):
            r = r8 * 8 + u
            src = pl.multiple_of(dest_ref[0, 0, r] * ROW_TILE, ROW_TILE)
            pltpu.make_async_copy(ys_hbm.at[pl.ds(src, ROW_TILE)],
                                  buf.at[slot, pl.ds(pl.multiple_of(r * ROW_TILE, ROW_TILE), ROW_TILE)],
                                  sem.at[slot]).start(priority=u % 2)

    def unpack_rows(r8):
        base = pl.multiple_of(r8 * (8 * ROW_TILE), 8 * ROW_TILE)
        for cb in range(ROW_TILE):
            o_ref[pl.ds(pl.multiple_of(r8 * 8, 8), 8), cb * 128:(cb + 1) * 128] = (
                buf[1 - slot, pl.ds(base + cb, 8, stride=ROW_TILE), :])

    def loop(*parts):
        def body(r8, c):
            for part in parts:
                part(r8)
            return c
        lax.fori_loop(0, ROWS_PER_STEP // 8, body, 0)

    @pl.when(k > 0)
    def _():
        pltpu.make_async_copy(ys_hbm.at[pl.ds(0, ROWS_PER_STEP * ROW_TILE)], buf.at[1 - slot],
                              sem.at[1 - slot]).wait()

    @pl.when(k == 0)
    def _():
        loop(start_rows)

    @pl.when(jnp.logical_and(k > 0, k < n_blocks))
    def _():
        loop(start_rows, unpack_rows)

    @pl.when(k == n_blocks)
    def _():
        loop(unpack_rows)


def _combine(ys_rt, dest3, n):
    n_blocks = n // ROWS_PER_STEP
    return pl.pallas_call(
        _combine_kernel,
        grid=(n_blocks + 1,),
        in_specs=[
            pl.BlockSpec((1, 1, ROWS_PER_STEP), lambda k: (jnp.minimum(k, n_blocks - 1), 0, 0),
                         memory_space=pltpu.SMEM),
            pl.BlockSpec(memory_space=pl.ANY),
        ],
        out_specs=pl.BlockSpec((ROWS_PER_STEP, D_MODEL), lambda k: (jnp.maximum(k - 1, 0), 0)),
        out_shape=jax.ShapeDtypeStruct((n, D_MODEL), F32),
        scratch_shapes=[pltpu.VMEM((2, ROWS_PER_STEP * ROW_TILE, 128), F32), pltpu.SemaphoreType.DMA((2,))],
        compiler_params=pltpu.CompilerParams(
            dimension_semantics=("arbitrary",), vmem_limit_bytes=V7X_VMEM_LIMIT),
        name="combine",
    )(dest3, ys_rt)


def _moe_kernel(tg_ref, tlo_ref, thi_ref, nused_ref, hs_ref, wg_ref, wu_ref, wd_ref, wr_ref, br_ref,
                gm_ref, gfin_ref, ys_ref):
    step = pl.program_id(0)

    def tiles(t, k, n_tiles):
        rows = n_tiles * TM_MOE
        base = k * TM_MOE * ROW_TILE
        lane = lax.broadcasted_iota(I32, (rows, BUCKET_LANES), 1)
        hrows = jnp.concatenate(
            [hs_ref[pl.ds(base + cb, rows, stride=ROW_TILE), :] for cb in range(ROW_TILE)], axis=1)
        hn = _rms(hrows, gm_ref[...]).astype(BF16)
        logits = jnp.dot(hn, wr_ref[...], preferred_element_type=F32) + br_ref[...]
        g, lo, hi = tg_ref[t], tlo_ref[t], thi_ref[t]
        coarse = jnp.where(lane < N_GROUPS, logits, -jnp.inf)
        ec = jnp.exp(coarse - jnp.max(coarse, axis=-1, keepdims=True))
        pick = lambda col, v: jnp.sum(jnp.where(lane == col, v, 0.0), axis=-1, keepdims=True)
        g_w = pick(g, ec) / jnp.sum(ec, axis=-1, keepdims=True)
        f_lo = pick(8 + g * EPG + lo, logits)
        f_hi = pick(8 + g * EPG + hi, logits)
        f_max = jnp.maximum(f_lo, f_hi)
        e_lo = jnp.exp(f_lo - f_max)
        e_hi = jnp.exp(f_hi - f_max)
        den = e_lo + e_hi

        def expert(e, w):
            gate = jnp.dot(hn, wg_ref[e], preferred_element_type=F32)
            up = jnp.dot(hn, wu_ref[e], preferred_element_type=F32)
            act = (gate * jax.nn.sigmoid(gate) * up).astype(BF16)
            return w * jnp.dot(act, wd_ref[e], preferred_element_type=F32)

        y = expert(lo, (e_lo / den) * g_w) + expert(hi, (e_hi / den) * g_w)
        res = _rms(hrows + y, gfin_ref[...])
        for cb in range(ROW_TILE):
            ys_ref[pl.ds(base + cb, rows, stride=ROW_TILE), :] = res[:, cb * 128:(cb + 1) * 128]

    @pl.when(step * TILES_PER_STEP < nused_ref[0])
    def _():
        for p in range(TILES_PER_STEP // 2):
            t = step * TILES_PER_STEP + 2 * p
            same = jnp.logical_and(tlo_ref[t] == tlo_ref[t + 1], thi_ref[t] == thi_ref[t + 1])

            @pl.when(same)
            def _():
                tiles(t, 2 * p, 2)

            @pl.when(jnp.logical_not(same))
            def _():
                tiles(t, 2 * p, 1)
                tiles(t + 1, 2 * p + 1, 1)

    @pl.when(step * TILES_PER_STEP >= nused_ref[0])
    def _():
        ys_ref[...] = jnp.zeros_like(ys_ref)


def _moe(hs_rt, tile_g, tile_lo, tile_hi, n_used, wg, wu, wd, wr, br, gm, gfin):
    rows_step = TM_MOE * TILES_PER_STEP * ROW_TILE
    n_steps = hs_rt.shape[0] // rows_step
    by_group = lambda s, tg, *_: (tg[s * TILES_PER_STEP], 0, 0, 0)
    full2 = lambda a, c: pl.BlockSpec((a, c), lambda s, *_: (0, 0))

    def hs_index(s, tg, tlo, thi, nu):
        last_step = jnp.maximum(nu[0] - 1, 0) // TILES_PER_STEP
        return (jnp.minimum(s, last_step), 0)

    grid_spec = pltpu.PrefetchScalarGridSpec(
        num_scalar_prefetch=4,
        grid=(n_steps,),
        in_specs=[
            pl.BlockSpec((rows_step, 128), hs_index),
            pl.BlockSpec((None, EPG, D_MODEL, D_EXPERT), by_group),
            pl.BlockSpec((None, EPG, D_MODEL, D_EXPERT), by_group),
            pl.BlockSpec((None, EPG, D_EXPERT, D_MODEL), by_group),
            full2(D_MODEL, BUCKET_LANES), full2(1, BUCKET_LANES), full2(1, D_MODEL), full2(1, D_MODEL),
        ],
        out_specs=pl.BlockSpec((rows_step, 128), lambda s, *_: (s, 0)),
    )
    return pl.pallas_call(
        _moe_kernel,
        grid_spec=grid_spec,
        out_shape=jax.ShapeDtypeStruct(hs_rt.shape, F32),
        compiler_params=pltpu.CompilerParams(
            dimension_semantics=("arbitrary",), vmem_limit_bytes=V7X_VMEM_LIMIT),
        name="moe",
    )(tile_g, tile_lo, tile_hi, n_used, hs_rt, wg, wu, wd, wr, br, gm, gfin)


def _bucket_plan(bucket, rank, counts, n):
    nt = n // TM_MOE + N_BUCKETS + N_GROUPS * (TILES_PER_STEP - 1)
    nt = -(-nt // TILES_PER_STEP) * TILES_PER_STEP
    tiles_b = (counts + (TM_MOE - 1)) // TM_MOE
    tiles_g = jnp.sum(tiles_b.reshape(N_GROUPS, N_PAIRS), axis=1)
    extra_g = (-tiles_g) % TILES_PER_STEP
    is_last = (np.arange(N_PAIRS) == N_PAIRS - 1)[None, :]
    tiles_b = (tiles_b.reshape(N_GROUPS, N_PAIRS) + jnp.where(is_last, extra_g[:, None], 0)).reshape(N_BUCKETS)
    tile_end = jnp.cumsum(tiles_b)
    tile_start = tile_end - tiles_b
    n_used = tile_end[-1]
    b_ids = jnp.arange(N_BUCKETS, dtype=I32)
    dest = rank + TM_MOE * jnp.sum(jnp.where(bucket[:, None] == b_ids[None, :], tile_start[None, :], 0), axis=1)
    t_idx = jnp.arange(nt, dtype=I32)
    tb = jnp.sum((tile_end[None, :] <= t_idx[:, None]).astype(I32), axis=1)
    tb_last = jnp.sum((tile_end <= n_used - 1).astype(I32))
    tb = jnp.minimum(jnp.where(t_idx < n_used, tb, tb_last), N_BUCKETS - 1)
    sel = tb[:, None] == b_ids[None, :]
    pick = lambda table: jnp.sum(jnp.where(sel, table[None, :], 0), axis=1).astype(I32)
    tile_g = tb // N_PAIRS
    tile_lo = pick(jnp.asarray(np.tile(_PAIR_LO, N_GROUPS)))
    tile_hi = pick(jnp.asarray(np.tile(_PAIR_HI, N_GROUPS)))
    nv = jnp.clip(pick(counts) - (t_idx - pick(tile_start)) * TM_MOE, 0, TM_MOE)
    tile_nv = jnp.where(t_idx < n_used, nv, 0).astype(I32)
    return dest.astype(I32), tile_g.astype(I32), tile_lo, tile_hi, tile_nv, n_used.reshape(1).astype(I32)


def kernel(x, norm_mix, w_in, rpb, w_four, b_four, g_attn_out, g_four_out, w_out, norm_moe,
           w_router_coarse, b_router_coarse, w_router_fine, b_router_fine, w_gate, w_up, w_down, norm_final):
    b, seq, d = x.shape
    assert (seq, d) == (SEQ, D_MODEL) and norm_mix.shape[0] == 1
    n = b * seq
    x2 = x.reshape(n, d)

    qkv, u = _inproj(x2, norm_mix[0][None], w_in[0].astype(BF16))

    oa = _attention(qkv.reshape(b, seq, 3 * D_ATTN), _bias_columns(rpb[0]))

    eye4 = jnp.eye(4, dtype=F32)
    wf = w_four[0].reshape(2, 4, FOUR_GROUP_DIM, FOUR_GROUP_DIM)
    wbd = (eye4[None, :, None, :, None] * wf[:, :, :, None, :]).reshape(2, 256, 256).astype(BF16)
    yf = _fourier(u, wbd, b_four[0][None]).reshape(N_CBLK, b * ROWS * GRID_PITCH, 128)

    wrt = jnp.zeros((BUCKET_LANES, d), F32)
    wrt = wrt.at[0:N_GROUPS].set(w_router_coarse[0].T).at[8:8 + N_EXPERTS].set(w_router_fine[0].T)
    br = jnp.zeros((BUCKET_LANES, 1), F32)
    br = br.at[0:N_GROUPS, 0].set(b_router_coarse[0]).at[8:8 + N_EXPERTS, 0].set(b_router_fine[0])
    tri = (np.arange(TM_OUT)[:, None] < np.arange(TM_OUT)[None, :]).astype(np.float32)
    wrt_bf = wrt.astype(BF16)
    h_rt, bucket, rank, cnt = _mixout(
        oa.reshape(n, D_ATTN), yf, x2, g_attn_out[0][None], g_four_out[0][None],
        w_out[0].astype(BF16), norm_moe[0][None], wrt_bf, br, jnp.asarray(tri, BF16))

    counts = cnt[:N_BUCKETS, 0].astype(I32)
    dest, tile_g, tile_lo, tile_hi, tile_nv, n_used = _bucket_plan(bucket.reshape(n), rank.reshape(n), counts, n)
    dest3 = dest.reshape(n // ROWS_PER_STEP, 1, ROWS_PER_STEP)
    hs_rt, wg_bf, wu_bf, wd_bf = _dispatch(h_rt, dest3, tile_nv, tile_nv.shape[0] * TM_MOE,
                                           w_gate[0], w_up[0], w_down[0])
    shape_e = (N_GROUPS, EPG)
    ys_rt = _moe(hs_rt, tile_g, tile_lo, tile_hi, n_used,
                 wg_bf.reshape(shape_e + (d, D_EXPERT)),
                 wu_bf.reshape(shape_e + (d, D_EXPERT)),
                 wd_bf.reshape(shape_e + (D_EXPERT, d)),
                 wrt_bf.T, br.T, norm_moe[0][None], norm_final[None])
    return _combine(ys_rt, dest3, n).reshape(b, seq, d)
```

```python
import numpy as np
import jax
import jax.numpy as jnp
from jax import lax
from jax.experimental import pallas as pl
from jax.experimental.pallas import tpu as pltpu

F32 = jnp.float32
BF16 = jnp.bfloat16
I32 = jnp.int32

D_MODEL = 1024
SEQ = 4096
GRID_W = 64
ROWS = SEQ // GRID_W
D_ATTN = 512
D_FOUR = 512
N_HEADS = 8
HEAD_DIM = 64
WIN_H = 8
WIN_W = 16
N_FOUR_GROUPS = 8
FOUR_GROUP_DIM = 64
D_PROJ = 3 * D_ATTN + D_FOUR
N_GROUPS = 4
EPG = 8
N_EXPERTS = N_GROUPS * EPG
D_EXPERT = 256
EPS = 1e-6
NEG = -1e30

V7X_VMEM_LIMIT = 56 * 1024 * 1024

TM_IN = 1024
TM_OUT = 1024
OUT_SUBTILES = 1
TM_MOE = 128
ROW_TILE = D_MODEL // 128
ROWS_PER_STEP = 2048
TILES_PER_STEP = 8
N_PAIRS = EPG * (EPG - 1) // 2
N_BUCKETS = N_GROUPS * N_PAIRS
BUCKET_LANES = 128

QB_ROWS = 8
QB_COLS = 16
KB_ROWS = 16
KB_COLS = 32
GRID_PITCH = 72


def _rms(x, g):
    ms = jnp.mean(x * x, axis=-1, keepdims=True)
    return x * lax.rsqrt(ms + EPS) * g


def _inproj_kernel(x_ref, g_ref, w_ref, qkv_ref, u_ref):
    xn = _rms(x_ref[...], g_ref[...]).astype(BF16)
    p = jnp.dot(xn, w_ref[...], preferred_element_type=F32)
    qkv_ref[:, :D_ATTN] = (p[:, :D_ATTN] * (HEAD_DIM ** -0.5)).astype(BF16)
    qkv_ref[:, D_ATTN:] = p[:, D_ATTN:3 * D_ATTN].astype(BF16)
    for cb in range(D_FOUR // 128):
        lanes = slice(3 * D_ATTN + cb * 128, 3 * D_ATTN + (cb + 1) * 128)
        for r in range(TM_IN // GRID_W):
            u_ref[cb, r * GRID_PITCH:r * GRID_PITCH + GRID_W, :] = p[r * GRID_W:(r + 1) * GRID_W, lanes]
            u_ref[cb, r * GRID_PITCH + GRID_W:(r + 1) * GRID_PITCH, :] = jnp.zeros((GRID_PITCH - GRID_W, 128), F32)


def _inproj(x2, g, w_bf):
    n = x2.shape[0]
    rows_step = TM_IN // GRID_W
    steps_b = ROWS // rows_step
    return pl.pallas_call(
        _inproj_kernel,
        grid=(n // TM_IN,),
        in_specs=[
            pl.BlockSpec((TM_IN, D_MODEL), lambda i: (i, 0)),
            pl.BlockSpec((1, D_MODEL), lambda i: (0, 0)),
            pl.BlockSpec((D_MODEL, D_PROJ), lambda i: (0, 0)),
        ],
        out_specs=[
            pl.BlockSpec((TM_IN, 3 * D_ATTN), lambda i: (i, 0)),
            pl.BlockSpec((D_FOUR // 128, None, rows_step * GRID_PITCH, 128),
                         lambda i: (0, i // steps_b, i % steps_b, 0)),
        ],
        out_shape=[
            jax.ShapeDtypeStruct((n, 3 * D_ATTN), BF16),
            jax.ShapeDtypeStruct((D_FOUR // 128, n // SEQ, ROWS * GRID_PITCH, 128), F32),
        ],
        compiler_params=pltpu.CompilerParams(
            dimension_semantics=("parallel",), vmem_limit_bytes=V7X_VMEM_LIMIT),
        name="inproj",
    )(x2, g, w_bf)


_KCOL_SHIFTED = (False, True, True, False)
_KCOL_OFF = (0, 0, 16, 32)
_COL_TYPE = (0, 1, 1, 2)


def _bias_index_tables():
    dr = np.zeros((9, 128, 512), np.int32)
    dc = np.zeros((9, 128, 512), np.int32)
    ok = np.zeros((9, 128, 512), bool)
    qi, qc = np.divmod(np.arange(128), QB_COLS)
    ki, kc = np.divmod(np.arange(512), KB_COLS)
    for rt, (q0, k0) in enumerate(((0, 0), (8, 4), (56, 48))):
        qrow = q0 + qi
        krow = k0 + ki
        rs = np.clip(qrow - WIN_H // 2, 0, ROWS - WIN_H)
        rok = (krow[None, :] >= rs[:, None]) & (krow[None, :] < rs[:, None] + WIN_H)
        drr = krow[None, :] - qrow[:, None] + (WIN_H - 1)
        for ct, (c0, kc0) in enumerate(((0, 0), (16, 8), (48, 32))):
            qcol = c0 + qc
            kcol = kc0 + kc
            cs = np.clip(qcol - WIN_W // 2, 0, GRID_W - WIN_W)
            cok = (kcol[None, :] >= cs[:, None]) & (kcol[None, :] < cs[:, None] + WIN_W)
            dcc = kcol[None, :] - qcol[:, None] + (WIN_W - 1)
            t = rt * 3 + ct
            ok[t] = rok & cok
            dr[t] = np.where(ok[t], drr, 0)
            dc[t] = np.where(ok[t], dcc, 0)
    return dr, dc, ok


_BIAS_DR, _BIAS_DC, _BIAS_OK = _bias_index_tables()


def _bias_selectors():
    ok = _BIAS_OK.reshape(3, 3, QB_ROWS, QB_COLS, KB_ROWS, KB_COLS)
    dr = _BIAS_DR.reshape(ok.shape)
    dc = _BIAS_DC.reshape(ok.shape)
    row_ok = ok.any(axis=(1, 3, 5))
    col_ok = ok.any(axis=(0, 2, 4))
    dr_r = dr.max(axis=(1, 3, 5))
    dc_c = dc.max(axis=(0, 2, 4))
    sc = (np.arange(2 * WIN_W - 1)[:, None, None, None] == dc_c[None]) & col_ok[None]
    return row_ok, dr_r, col_ok, sc.astype(np.float32)


_BIAS_ROW_OK, _BIAS_ROW_DR, _BIAS_COL_OK, _BIAS_SC = _bias_selectors()


def _bias_columns(rpb):
    sc = jnp.asarray(np.tile(_BIAS_SC, 128 // KB_COLS))
    ok = np.tile(_BIAS_COL_OK, 128 // KB_COLS)
    t1 = jnp.sum(rpb[:, :, :, None, None, None] * sc[None, None], axis=2)
    return jnp.where(ok[None, None], t1, NEG)


def _attn_kernel(q_ref, k_ref, v_ref, bcol_ref, wg_ref, wu_ref, wd_ref, o_ref, wg_out, wu_out, wd_out,
                 ksh_ref, vsh_ref, bias_ref, s_ref, p_ref, l_ref):
    wg_out[...] = wg_ref[...].astype(BF16)
    wu_out[...] = wu_ref[...].astype(BF16)
    wd_out[...] = wd_ref[...].astype(BF16)

    @pl.when(pl.program_id(1) == 0)
    def _():
        key_row = lax.broadcasted_iota(I32, (QB_COLS, KB_ROWS * KB_COLS), 1) // KB_COLS
        for hh in range(2):
            for rt in range(3):
                for ct in range(3):
                    for i in range(QB_ROWS):
                        acc = jnp.full((QB_COLS, KB_ROWS * KB_COLS), NEG, F32)
                        for y in range(KB_ROWS):
                            if _BIAS_ROW_OK[rt, i, y]:
                                cols = bcol_ref[hh, int(_BIAS_ROW_DR[rt, i, y]), ct]
                                cols = jnp.concatenate([cols] * (KB_ROWS * KB_COLS // 128), axis=1)
                                acc = jnp.where(key_row == y, cols, acc)
                        bias_ref[rt * 3 + ct, pl.ds(hh * 128 + i * QB_COLS, QB_COLS), :] = acc

    zpad = jnp.zeros((8, 128), F32)
    ksh_ref[...] = jnp.concatenate([k_ref[...].astype(F32)[8:], zpad], axis=0).astype(BF16)
    vsh_ref[...] = jnp.concatenate([v_ref[...].astype(F32)[8:], zpad], axis=0).astype(BF16)
    lane = lax.broadcasted_iota(I32, (1, 128), 1)
    head_masks = (lane < HEAD_DIM, lane >= HEAD_DIM)

    n_rb = ROWS // QB_ROWS
    n_q = GRID_W // QB_COLS

    def window(ref_plain, ref_shift, rb, j):
        ks = jnp.clip(QB_ROWS * rb - WIN_H // 2, 0, ROWS - KB_ROWS)
        src = ref_shift if _KCOL_SHIFTED[j] else ref_plain
        return jnp.concatenate(
            [src[pl.ds(pl.multiple_of((ks + i) * GRID_W + _KCOL_OFF[j], 16), KB_COLS), :] for i in range(KB_ROWS)],
            axis=0)

    def scores(rb, j):
        q = jnp.concatenate(
            [q_ref[pl.ds(pl.multiple_of((QB_ROWS * rb + i) * GRID_W + QB_COLS * j, 16), QB_COLS), :]
             for i in range(QB_ROWS)], axis=0)
        qm = jnp.concatenate([jnp.where(hm, q, jnp.zeros_like(q)) for hm in head_masks], axis=0)
        s_ref[j] = lax.dot_general(qm, window(k_ref, ksh_ref, rb, j), (((1,), (1,)), ((), ())),
                                   preferred_element_type=F32)

    def softmax(rb, j):
        rt = jnp.where(rb == 0, 0, jnp.where(rb == n_rb - 1, 2, 1))
        s = s_ref[j] + bias_ref[rt * 3 + _COL_TYPE[j]]
        e = jnp.exp(s - jnp.max(s, axis=-1, keepdims=True))
        p_ref[j] = e.astype(BF16)
        l_ref[j] = jnp.broadcast_to(jnp.sum(e, axis=-1, keepdims=True), (2 * QB_ROWS * QB_COLS, 128))

    def values(rb, j):
        o = jnp.dot(p_ref[j], window(v_ref, vsh_ref, rb, j), preferred_element_type=F32) / l_ref[j]
        out = jnp.where(head_masks[0], o[:128], o[128:]).astype(BF16)
        for i in range(QB_ROWS):
            o_ref[pl.ds(pl.multiple_of((QB_ROWS * rb + i) * GRID_W + QB_COLS * j, 16), QB_COLS), :] = (
                out[QB_COLS * i:QB_COLS * (i + 1)])

    def stage(fn, rb):
        for j in range(n_q):
            fn(jnp.asarray(rb, I32), j)

    stage(scores, 0)
    stage(softmax, 0)
    stage(scores, 1)

    def pipeline_step(i, carry):
        stage(values, i - 2)
        stage(softmax, i - 1)
        stage(scores, i)
        return carry

    lax.fori_loop(2, n_rb, pipeline_step, 0)
    stage(values, n_rb - 2)
    stage(softmax, n_rb - 1)
    stage(values, n_rb - 1)


def _attention(qkv3, bias_cols, w_gate, w_up, w_down):
    b = qkv3.shape[0]
    n_hp = N_HEADS // 2
    assert N_EXPERTS % (n_hp * b) == 0
    eps = N_EXPERTS // (n_hp * b)
    wspec = lambda shape: pl.BlockSpec((eps,) + shape, lambda hp, bi: (hp * b + bi, 0, 0))
    wspecs = [wspec((D_MODEL, D_EXPERT)), wspec((D_MODEL, D_EXPERT)), wspec((D_EXPERT, D_MODEL))]
    blk = lambda off: pl.BlockSpec((None, SEQ, 128), lambda hp, bi: (bi, 0, off + hp))
    return pl.pallas_call(
        _attn_kernel,
        grid=(n_hp, b),
        in_specs=[
            blk(0), blk(n_hp), blk(2 * n_hp),
            pl.BlockSpec((2,) + bias_cols.shape[1:], lambda hp, bi: (hp, 0, 0, 0, 0)),
        ] + wspecs,
        out_specs=[pl.BlockSpec((None, SEQ, 128), lambda hp, bi: (bi, 0, hp))] + wspecs,
        out_shape=[jax.ShapeDtypeStruct((b, SEQ, D_ATTN), BF16), jax.ShapeDtypeStruct(w_gate.shape, BF16),
                   jax.ShapeDtypeStruct(w_up.shape, BF16), jax.ShapeDtypeStruct(w_down.shape, BF16)],
        scratch_shapes=[pltpu.VMEM((SEQ, 128), BF16), pltpu.VMEM((SEQ, 128), BF16),
                        pltpu.VMEM((9, 2 * QB_ROWS * QB_COLS, KB_ROWS * KB_COLS), F32),
                        pltpu.VMEM((GRID_W // QB_COLS, 2 * QB_ROWS * QB_COLS, KB_ROWS * KB_COLS), F32),
                        pltpu.VMEM((GRID_W // QB_COLS, 2 * QB_ROWS * QB_COLS, KB_ROWS * KB_COLS), BF16),
                        pltpu.VMEM((GRID_W // QB_COLS, 2 * QB_ROWS * QB_COLS, 128), F32)],
        compiler_params=pltpu.CompilerParams(
            dimension_semantics=("arbitrary", "arbitrary"), vmem_limit_bytes=V7X_VMEM_LIMIT),
        name="nattn",
    )(qkv3, qkv3, qkv3, bias_cols, w_gate, w_up, w_down)


Z_PITCH = 72
N_CBLK = D_FOUR // 128


def _fourier_tables():
    n = 64
    k = np.arange(n)
    ang = 2.0 * np.pi * np.outer(k, k) / n
    c64, s64 = np.cos(ang), np.sin(ang)
    w1 = np.concatenate([c64, -s64], axis=0)
    t1p = np.arange(n)[:, None, None]
    t2p = np.arange(n)[None, :, None]
    t2 = np.arange(n)[None, None, :]
    th = 2.0 * np.pi * ((t2 * (t1p + n * t2p)) % SEQ) / SEQ
    cc, ss = np.cos(th), np.sin(th)
    m2 = np.concatenate([np.concatenate([cc, ss], axis=2), np.concatenate([-ss, cc], axis=2)], axis=1)
    cbd = np.kron(np.eye(4), c64)
    sbd = np.kron(np.eye(4), s64)
    cs = np.concatenate([cbd, sbd], axis=0)
    return w1.astype(np.float32), m2.astype(np.float32), cs.astype(np.float32)


_W1_NP, _M2_NP, _CS_NP = _fourier_tables()


HALF_CBLK = N_CBLK // 2
T2_UNROLL = 16


def _fourier_kernel(u_ref, w1_ref, m2_ref, cs_ref, wbd_ref, bf_ref, y_ref, zs_ref):
    for cb in range(HALF_CBLK):
        for k in range(GRID_PITCH - GRID_W):
            y_ref[cb, pl.ds(GRID_W + k, ROWS, stride=GRID_PITCH), :] = jnp.zeros((ROWS, 128), F32)

    def dft_cols(i, carry):
        for k in range(T2_UNROLL):
            t2 = i * T2_UNROLL + k
            x = jnp.concatenate([u_ref[cb, pl.ds(t2, ROWS, stride=GRID_PITCH), :] for cb in range(HALF_CBLK)],
                                axis=1).astype(BF16)
            z = jnp.dot(w1_ref[...], x, preferred_element_type=F32)
            for cb in range(HALF_CBLK):
                zs_ref[cb, pl.ds(t2, 128, stride=Z_PITCH), :] = z[:, cb * 128:(cb + 1) * 128]
        return carry

    lax.fori_loop(0, GRID_W // T2_UNROLL, dft_cols, 0)

    def dft_rows(a, carry):
        xs = []
        for jo in range(8):
            t1p = a * 8 + jo
            rhs = jnp.concatenate(
                [jnp.concatenate([zs_ref[cb, pl.ds(pl.multiple_of((part * 64 + t1p) * Z_PITCH, 8), 64), :]
                                  for cb in range(HALF_CBLK)], axis=1) for part in range(2)], axis=0)
            xs.append(jnp.dot(m2_ref[t1p], rhs.astype(BF16), preferred_element_type=F32))
        xr = jnp.concatenate([x[:64] for x in xs], axis=0).astype(BF16)
        xi = jnp.concatenate([x[64:] for x in xs], axis=0).astype(BF16)
        lhs = jnp.concatenate([xr, xi], axis=1)
        f = jnp.dot(lhs, cs_ref[...], preferred_element_type=F32) * (1.0 / 512.0)
        y = jnp.dot(f.astype(BF16), wbd_ref[...], preferred_element_type=F32) + bf_ref[...]
        for jo in range(8):
            t1p = a * 8 + jo
            for cb in range(HALF_CBLK):
                y_ref[cb, pl.ds(t1p, ROWS, stride=GRID_PITCH), :] = y[jo * 64:(jo + 1) * 64, cb * 128:(cb + 1) * 128]
        return carry

    lax.fori_loop(0, GRID_W // 8, dft_rows, 0)


def _fourier(u_p, wbd, bf):
    b = u_p.shape[1]
    w1 = jnp.asarray(_W1_NP).astype(BF16)
    m2 = jnp.asarray(_M2_NP).astype(BF16)
    cs = jnp.asarray(_CS_NP).astype(BF16)
    half_blk = pl.BlockSpec((HALF_CBLK, None, ROWS * GRID_PITCH, 128), lambda bi, hf: (hf, bi, 0, 0))
    return pl.pallas_call(
        _fourier_kernel,
        grid=(b, 2),
        in_specs=[
            half_blk,
            pl.BlockSpec((128, 64), lambda bi, hf: (0, 0)),
            pl.BlockSpec((64, 128, 128), lambda bi, hf: (0, 0, 0)),
            pl.BlockSpec((512, 256), lambda bi, hf: (0, 0)),
            pl.BlockSpec((None, 256, 256), lambda bi, hf: (hf, 0, 0)),
            pl.BlockSpec((1, 256), lambda bi, hf: (0, hf)),
        ],
        out_specs=half_blk,
        out_shape=jax.ShapeDtypeStruct(u_p.shape, F32),
        scratch_shapes=[pltpu.VMEM((HALF_CBLK, 128 * Z_PITCH, 128), F32)],
        compiler_params=pltpu.CompilerParams(
            dimension_semantics=("parallel", "parallel"), vmem_limit_bytes=V7X_VMEM_LIMIT),
        name="fourier",
    )(u_p, w1, m2, cs, wbd, bf)


def _mixout_kernel(oa_ref, yf_ref, x_ref, ga_ref, gf_ref, wout_ref, gm_ref, wrt_ref, br_ref, tri_ref,
                   h_ref, bucket_ref, rank_ref, cnt_ref, carry_ref):
    i = pl.program_id(0)

    @pl.when(i == 0)
    def _():
        carry_ref[...] = jnp.zeros_like(carry_ref)

    carry = carry_ref[...]
    for k in range(OUT_SUBTILES):
        carry = _mixout_subtile(k, carry, oa_ref, yf_ref, x_ref, ga_ref, gf_ref, wout_ref, gm_ref, wrt_ref,
                                br_ref, tri_ref, h_ref, bucket_ref, rank_ref)
    carry_ref[...] = carry
    cnt_ref[...] = carry


def _mixout_subtile(k, carry, oa_ref, yf_ref, x_ref, ga_ref, gf_ref, wout_ref, gm_ref, wrt_ref, br_ref, tri_ref,
                    h_ref, bucket_ref, rank_ref):
    rows_k = pl.ds(k * TM_OUT, TM_OUT)
    na = _rms(oa_ref[rows_k, :].astype(F32), ga_ref[...]).astype(BF16)
    grid_rows = [k * (TM_OUT // GRID_W) + r for r in range(TM_OUT // GRID_W)]
    yf = jnp.concatenate(
        [jnp.concatenate([yf_ref[cb, gr * GRID_PITCH:gr * GRID_PITCH + GRID_W, :] for gr in grid_rows], axis=0)
         for cb in range(N_CBLK)], axis=1)
    nf = _rms(yf, gf_ref[...]).astype(BF16)
    merged = jnp.concatenate([na, nf], axis=1)
    h = x_ref[rows_k, :] + jnp.dot(merged, wout_ref[...], preferred_element_type=F32)
    for cb in range(ROW_TILE):
        h_ref[pl.ds(k * TM_OUT * ROW_TILE + cb, TM_OUT, stride=ROW_TILE), :] = h[:, cb * 128:(cb + 1) * 128]
    hn = _rms(h, gm_ref[...]).astype(BF16)
    lt = lax.dot_general(wrt_ref[...], hn, (((1,), (1,)), ((), ())), preferred_element_type=F32)
    lt = lt + br_ref[...]
    c = [lt[k:k + 1] for k in range(N_GROUPS)]
    cmax = jnp.maximum(jnp.maximum(c[0], c[1]), jnp.maximum(c[2], c[3]))
    e = [jnp.exp(ck - cmax) for ck in c]
    esum = (e[0] + e[1]) + (e[2] + e[3])
    p = [ek / esum for ek in e]
    pmax = jnp.maximum(jnp.maximum(p[0], p[1]), jnp.maximum(p[2], p[3]))
    g = jnp.where(p[0] == pmax, 0, jnp.where(p[1] == pmax, 1, jnp.where(p[2] == pmax, 2, 3))).astype(I32)
    fine = jnp.where(g == 0, lt[8:16], jnp.where(g == 1, lt[16:24], jnp.where(g == 2, lt[24:32], lt[32:40])))
    rows = lax.broadcasted_iota(I32, fine.shape, 0)
    v1 = jnp.max(fine, axis=0, keepdims=True)
    i1 = jnp.min(jnp.where(fine == v1, rows, EPG), axis=0, keepdims=True)
    rest = jnp.where(rows == i1, -jnp.inf, fine)
    v2 = jnp.max(rest, axis=0, keepdims=True)
    i2 = jnp.min(jnp.where(rest == v2, rows, EPG), axis=0, keepdims=True)
    lo = jnp.minimum(i1, i2)
    hi = jnp.maximum(i1, i2)
    pair = lax.shift_right_logical(lo * (2 * EPG - 1 - lo), 1) + (hi - lo - 1)
    bucket = g * N_PAIRS + pair
    bucket_ref[k] = bucket
    brow = lax.broadcasted_iota(I32, (BUCKET_LANES, TM_OUT), 0)
    onehot = (brow == bucket).astype(F32)
    prefix = jnp.dot(onehot.astype(BF16), tri_ref[...], preferred_element_type=F32)
    rank = jnp.sum(onehot * (prefix + carry), axis=0, keepdims=True)
    rank_ref[k] = rank.astype(I32)
    return carry + jnp.sum(onehot, axis=1, keepdims=True)


def _mixout(oa, yf, x2, ga, gf, wout_bf, gm, wrt, br, tri):
    n = x2.shape[0]
    nt = n // TM_OUT
    rows_step = TM_OUT * OUT_SUBTILES
    full = lambda *shape: pl.BlockSpec(shape, lambda i: (0,) * len(shape))
    row3 = pl.BlockSpec((OUT_SUBTILES, 1, TM_OUT), lambda i: (i, 0, 0))
    return pl.pallas_call(
        _mixout_kernel,
        grid=(n // rows_step,),
        in_specs=[
            pl.BlockSpec((rows_step, D_ATTN), lambda i: (i, 0)),
            pl.BlockSpec((N_CBLK, rows_step // GRID_W * GRID_PITCH, 128), lambda i: (0, i, 0)),
            pl.BlockSpec((rows_step, D_MODEL), lambda i: (i, 0)),
            full(1, D_ATTN), full(1, D_FOUR), full(D_MODEL, D_MODEL), full(1, D_MODEL),
            full(BUCKET_LANES, D_MODEL), full(BUCKET_LANES, 1), full(TM_OUT, TM_OUT),
        ],
        out_specs=[
            pl.BlockSpec((rows_step * ROW_TILE, 128), lambda i: (i, 0)),
            row3, row3,
            full(BUCKET_LANES, 1),
        ],
        out_shape=[
            jax.ShapeDtypeStruct((n * ROW_TILE, 128), F32),
            jax.ShapeDtypeStruct((nt, 1, TM_OUT), I32),
            jax.ShapeDtypeStruct((nt, 1, TM_OUT), I32),
            jax.ShapeDtypeStruct((BUCKET_LANES, 1), F32),
        ],
        scratch_shapes=[pltpu.VMEM((BUCKET_LANES, 1), F32)],
        compiler_params=pltpu.CompilerParams(
            dimension_semantics=("arbitrary",), vmem_limit_bytes=V7X_VMEM_LIMIT),
        name="mixout",
    )(oa, yf, x2, ga, gf, wout_bf, gm, wrt, br, tri)


def _pair_tables():
    lo, hi = [], []
    for a in range(EPG):
        for b in range(a + 1, EPG):
            lo.append(a)
            hi.append(b)
    return np.asarray(lo, np.int32), np.asarray(hi, np.int32)


_PAIR_LO, _PAIR_HI = _pair_tables()


def _dispatch_kernel(tnv_ref, dest_ref, h_ref, hs_hbm, zbuf, zsem, sem):
    k = pl.program_id(0)
    tile_rows = TM_MOE * ROW_TILE
    n_tiles = hs_hbm.shape[0] // tile_rows

    def for_tiles(pred, which, action):
        def body(t, c):
            @pl.when(pred(tnv_ref[t]))
            def _():
                dst = hs_hbm.at[pl.ds(pl.multiple_of(t * tile_rows, tile_rows), tile_rows)]
                action(pltpu.make_async_copy(zbuf, dst, zsem.at[which]))
            return c
        lax.fori_loop(0, n_tiles, body, 0)

    partly_owned = lambda nv: jnp.logical_and(nv > 0, nv < TM_MOE)
    unowned = lambda nv: nv == 0

    @pl.when(k == 0)
    def _():
        zbuf[...] = jnp.zeros_like(zbuf)
        for_tiles(partly_owned, 0, lambda c: c.start())
        for_tiles(unowned, 1, lambda c: c.start())
        for_tiles(partly_owned, 0, lambda c: c.wait())

    @pl.when(k == pl.num_programs(0) - 1)
    def _():
        for_tiles(unowned, 1, lambda c: c.wait())

    def rows(r8, c):
        for u in range(8):
            r = r8 * 8 + u
            dst = pl.multiple_of(dest_ref[0, 0, r] * ROW_TILE, ROW_TILE)
            pltpu.make_async_copy(h_ref.at[pl.ds(pl.multiple_of(r * ROW_TILE, ROW_TILE), ROW_TILE)],
                                  hs_hbm.at[pl.ds(dst, ROW_TILE)], sem).start(priority=u % 2)
        return c

    lax.fori_loop(0, ROWS_PER_STEP // 8, rows, 0)
    pltpu.make_async_copy(h_ref, hs_hbm.at[pl.ds(0, ROWS_PER_STEP * ROW_TILE)], sem).wait()


def _dispatch(h_rt, dest3, tile_nv, n_slots):
    n = h_rt.shape[0] // ROW_TILE
    grid_spec = pltpu.PrefetchScalarGridSpec(
        num_scalar_prefetch=1,
        grid=(n // ROWS_PER_STEP,),
        in_specs=[
            pl.BlockSpec((1, 1, ROWS_PER_STEP), lambda k, *_: (k, 0, 0), memory_space=pltpu.SMEM),
            pl.BlockSpec((ROWS_PER_STEP * ROW_TILE, 128), lambda k, *_: (k, 0)),
        ],
        out_specs=pl.BlockSpec(memory_space=pl.ANY),
        scratch_shapes=[pltpu.VMEM((TM_MOE * ROW_TILE, 128), F32),
                        pltpu.SemaphoreType.DMA((2,)), pltpu.SemaphoreType.DMA(())],
    )
    return pl.pallas_call(
        _dispatch_kernel,
        grid_spec=grid_spec,
        out_shape=jax.ShapeDtypeStruct((n_slots * ROW_TILE, 128), F32),
        compiler_params=pltpu.CompilerParams(
            dimension_semantics=("arbitrary",), vmem_limit_bytes=V7X_VMEM_LIMIT),
        name="dispatch",
    )(tile_nv, dest3, h_rt)


def _combine_kernel(dest_ref, ys_hbm, o_ref, buf, sem):
    k = pl.program_id(0)
    n_blocks = pl.num_programs(0) - 1
    slot = k % 2

    def start_rows(r8):
        for u in range(8):
            r = r8 * 8 + u
            src = pl.multiple_of(dest_ref[0, 0, r] * ROW_TILE, ROW_TILE)
            pltpu.make_async_copy(ys_hbm.at[pl.ds(src, ROW_TILE)],
                                  buf.at[slot, pl.ds(pl.multiple_of(r * ROW_TILE, ROW_TILE), ROW_TILE)],
                                  sem.at[slot]).start(priority=u % 2)

    def unpack_rows(r8):
        base = pl.multiple_of(r8 * (8 * ROW_TILE), 8 * ROW_TILE)
        for cb in range(ROW_TILE):
            o_ref[pl.ds(pl.multiple_of(r8 * 8, 8), 8), cb * 128:(cb + 1) * 128] = (
                buf[1 - slot, pl.ds(base + cb, 8, stride=ROW_TILE), :])

    def loop(*parts):
        def body(r8, c):
            for part in parts:
                part(r8)
            return c
        lax.fori_loop(0, ROWS_PER_STEP // 8, body, 0)

    @pl.when(k > 0)
    def _():
        pltpu.make_async_copy(ys_hbm.at[pl.ds(0, ROWS_PER_STEP * ROW_TILE)], buf.at[1 - slot],
                              sem.at[1 - slot]).wait()

    @pl.when(k == 0)
    def _():
        loop(start_rows)

    @pl.when(jnp.logical_and(k > 0, k < n_blocks))
    def _():
        loop(start_rows, unpack_rows)

    @pl.when(k == n_blocks)
    def _():
        loop(unpack_rows)


def _combine(ys_rt, dest3, n):
    n_blocks = n // ROWS_PER_STEP
    return pl.pallas_call(
        _combine_kernel,
        grid=(n_blocks + 1,),
        in_specs=[
            pl.BlockSpec((1, 1, ROWS_PER_STEP), lambda k: (jnp.minimum(k, n_blocks - 1), 0, 0),
                         memory_space=pltpu.SMEM),
            pl.BlockSpec(memory_space=pl.ANY),
        ],
        out_specs=pl.BlockSpec((ROWS_PER_STEP, D_MODEL), lambda k: (jnp.maximum(k - 1, 0), 0)),
        out_shape=jax.ShapeDtypeStruct((n, D_MODEL), F32),
        scratch_shapes=[pltpu.VMEM((2, ROWS_PER_STEP * ROW_TILE, 128), F32), pltpu.SemaphoreType.DMA((2,))],
        compiler_params=pltpu.CompilerParams(
            dimension_semantics=("arbitrary",), vmem_limit_bytes=V7X_VMEM_LIMIT),
        name="combine",
    )(dest3, ys_rt)


def _moe_kernel(tg_ref, tlo_ref, thi_ref, nused_ref, hs_ref, wg_ref, wu_ref, wd_ref, wr_ref, br_ref,
                gm_ref, gfin_ref, ys_ref):
    step = pl.program_id(0)

    def tiles(t, k, n_tiles):
        rows = n_tiles * TM_MOE
        base = k * TM_MOE * ROW_TILE
        lane = lax.broadcasted_iota(I32, (rows, BUCKET_LANES), 1)
        hrows = jnp.concatenate(
            [hs_ref[pl.ds(base + cb, rows, stride=ROW_TILE), :] for cb in range(ROW_TILE)], axis=1)
        hn = _rms(hrows, gm_ref[...]).astype(BF16)
        logits = jnp.dot(hn, wr_ref[...], preferred_element_type=F32) + br_ref[...]
        g, lo, hi = tg_ref[t], tlo_ref[t], thi_ref[t]
        coarse = jnp.where(lane < N_GROUPS, logits, -jnp.inf)
        ec = jnp.exp(coarse - jnp.max(coarse, axis=-1, keepdims=True))
        pick = lambda col, v: jnp.sum(jnp.where(lane == col, v, 0.0), axis=-1, keepdims=True)
        g_w = pick(g, ec) / jnp.sum(ec, axis=-1, keepdims=True)
        f_lo = pick(8 + g * EPG + lo, logits)
        f_hi = pick(8 + g * EPG + hi, logits)
        f_max = jnp.maximum(f_lo, f_hi)
        e_lo = jnp.exp(f_lo - f_max)
        e_hi = jnp.exp(f_hi - f_max)
        den = e_lo + e_hi

        def expert(e, w):
            gate = jnp.dot(hn, wg_ref[e], preferred_element_type=F32)
            up = jnp.dot(hn, wu_ref[e], preferred_element_type=F32)
            act = (gate * jax.nn.sigmoid(gate) * up).astype(BF16)
            return w * jnp.dot(act, wd_ref[e], preferred_element_type=F32)

        y = expert(lo, (e_lo / den) * g_w) + expert(hi, (e_hi / den) * g_w)
        res = _rms(hrows + y, gfin_ref[...])
        for cb in range(ROW_TILE):
            ys_ref[pl.ds(base + cb, rows, stride=ROW_TILE), :] = res[:, cb * 128:(cb + 1) * 128]

    @pl.when(step * TILES_PER_STEP < nused_ref[0])
    def _():
        for p in range(TILES_PER_STEP // 2):
            t = step * TILES_PER_STEP + 2 * p
            same = jnp.logical_and(tlo_ref[t] == tlo_ref[t + 1], thi_ref[t] == thi_ref[t + 1])

            @pl.when(same)
            def _():
                tiles(t, 2 * p, 2)

            @pl.when(jnp.logical_not(same))
            def _():
                tiles(t, 2 * p, 1)
                tiles(t + 1, 2 * p + 1, 1)

    @pl.when(step * TILES_PER_STEP >= nused_ref[0])
    def _():
        ys_ref[...] = jnp.zeros_like(ys_ref)


def _moe(hs_rt, tile_g, tile_lo, tile_hi, n_used, wg, wu, wd, wr, br, gm, gfin):
    rows_step = TM_MOE * TILES_PER_STEP * ROW_TILE
    n_steps = hs_rt.shape[0] // rows_step
    by_group = lambda s, tg, *_: (tg[s * TILES_PER_STEP], 0, 0, 0)
    full2 = lambda a, c: pl.BlockSpec((a, c), lambda s, *_: (0, 0))

    def hs_index(s, tg, tlo, thi, nu):
        last_step = jnp.maximum(nu[0] - 1, 0) // TILES_PER_STEP
        return (jnp.minimum(s, last_step), 0)

    grid_spec = pltpu.PrefetchScalarGridSpec(
        num_scalar_prefetch=4,
        grid=(n_steps,),
        in_specs=[
            pl.BlockSpec((rows_step, 128), hs_index),
            pl.BlockSpec((None, EPG, D_MODEL, D_EXPERT), by_group),
            pl.BlockSpec((None, EPG, D_MODEL, D_EXPERT), by_group),
            pl.BlockSpec((None, EPG, D_EXPERT, D_MODEL), by_group),
            full2(D_MODEL, BUCKET_LANES), full2(1, BUCKET_LANES), full2(1, D_MODEL), full2(1, D_MODEL),
        ],
        out_specs=pl.BlockSpec((rows_step, 128), lambda s, *_: (s, 0)),
    )
    return pl.pallas_call(
        _moe_kernel,
        grid_spec=grid_spec,
        out_shape=jax.ShapeDtypeStruct(hs_rt.shape, F32),
        compiler_params=pltpu.CompilerParams(
            dimension_semantics=("arbitrary",), vmem_limit_bytes=V7X_VMEM_LIMIT),
        name="moe",
    )(tile_g, tile_lo, tile_hi, n_used, hs_rt, wg, wu, wd, wr, br, gm, gfin)


def _bucket_plan(bucket, rank, counts, n):
    nt = n // TM_MOE + N_BUCKETS + N_GROUPS * (TILES_PER_STEP - 1)
    nt = -(-nt // TILES_PER_STEP) * TILES_PER_STEP
    tiles_b = (counts + (TM_MOE - 1)) // TM_MOE
    tiles_g = jnp.sum(tiles_b.reshape(N_GROUPS, N_PAIRS), axis=1)
    extra_g = (-tiles_g) % TILES_PER_STEP
    is_last = (np.arange(N_PAIRS) == N_PAIRS - 1)[None, :]
    tiles_b = (tiles_b.reshape(N_GROUPS, N_PAIRS) + jnp.where(is_last, extra_g[:, None], 0)).reshape(N_BUCKETS)
    tile_end = jnp.cumsum(tiles_b)
    tile_start = tile_end - tiles_b
    n_used = tile_end[-1]
    b_ids = jnp.arange(N_BUCKETS, dtype=I32)
    dest = rank + TM_MOE * jnp.sum(jnp.where(bucket[:, None] == b_ids[None, :], tile_start[None, :], 0), axis=1)
    t_idx = jnp.arange(nt, dtype=I32)
    tb = jnp.sum((tile_end[None, :] <= t_idx[:, None]).astype(I32), axis=1)
    tb_last = jnp.sum((tile_end <= n_used - 1).astype(I32))
    tb = jnp.minimum(jnp.where(t_idx < n_used, tb, tb_last), N_BUCKETS - 1)
    sel = tb[:, None] == b_ids[None, :]
    pick = lambda table: jnp.sum(jnp.where(sel, table[None, :], 0), axis=1).astype(I32)
    tile_g = tb // N_PAIRS
    tile_lo = pick(jnp.asarray(np.tile(_PAIR_LO, N_GROUPS)))
    tile_hi = pick(jnp.asarray(np.tile(_PAIR_HI, N_GROUPS)))
    nv = jnp.clip(pick(counts) - (t_idx - pick(tile_start)) * TM_MOE, 0, TM_MOE)
    tile_nv = jnp.where(t_idx < n_used, nv, 0).astype(I32)
    return dest.astype(I32), tile_g.astype(I32), tile_lo, tile_hi, tile_nv, n_used.reshape(1).astype(I32)


def kernel(x, norm_mix, w_in, rpb, w_four, b_four, g_attn_out, g_four_out, w_out, norm_moe,
           w_router_coarse, b_router_coarse, w_router_fine, b_router_fine, w_gate, w_up, w_down, norm_final):
    b, seq, d = x.shape
    assert (seq, d) == (SEQ, D_MODEL) and norm_mix.shape[0] == 1
    n = b * seq
    x2 = x.reshape(n, d)

    qkv, u = _inproj(x2, norm_mix[0][None], w_in[0].astype(BF16))

    oa, wg_bf, wu_bf, wd_bf = _attention(qkv.reshape(b, seq, 3 * D_ATTN), _bias_columns(rpb[0]),
                                         w_gate[0], w_up[0], w_down[0])

    eye4 = jnp.eye(4, dtype=F32)
    wf = w_four[0].reshape(2, 4, FOUR_GROUP_DIM, FOUR_GROUP_DIM)
    wbd = (eye4[None, :, None, :, None] * wf[:, :, :, None, :]).reshape(2, 256, 256).astype(BF16)
    yf = _fourier(u, wbd, b_four[0][None]).reshape(N_CBLK, b * ROWS * GRID_PITCH, 128)

    wrt = jnp.zeros((BUCKET_LANES, d), F32)
    wrt = wrt.at[0:N_GROUPS].set(w_router_coarse[0].T).at[8:8 + N_EXPERTS].set(w_router_fine[0].T)
    br = jnp.zeros((BUCKET_LANES, 1), F32)
    br = br.at[0:N_GROUPS, 0].set(b_router_coarse[0]).at[8:8 + N_EXPERTS, 0].set(b_router_fine[0])
    tri = (np.arange(TM_OUT)[:, None] < np.arange(TM_OUT)[None, :]).astype(np.float32)
    wrt_bf = wrt.astype(BF16)
    h_rt, bucket, rank, cnt = _mixout(
        oa.reshape(n, D_ATTN), yf, x2, g_attn_out[0][None], g_four_out[0][None],
        w_out[0].astype(BF16), norm_moe[0][None], wrt_bf, br, jnp.asarray(tri, BF16))

    counts = cnt[:N_BUCKETS, 0].astype(I32)
    dest, tile_g, tile_lo, tile_hi, tile_nv, n_used = _bucket_plan(bucket.reshape(n), rank.reshape(n), counts, n)
    dest3 = dest.reshape(n // ROWS_PER_STEP, 1, ROWS_PER_STEP)
    hs_rt = _dispatch(h_rt, dest3, tile_nv, tile_nv.shape[0] * TM_MOE)
    shape_e = (N_GROUPS, EPG)
    ys_rt = _moe(hs_rt, tile_g, tile_lo, tile_hi, n_used,
                 wg_bf.reshape(shape_e + (d, D_EXPERT)),
                 wu_bf.reshape(shape_e + (d, D_EXPERT)),
                 wd_bf.reshape(shape_e + (D_EXPERT, d)),
                 wrt_bf.T, br.T, norm_moe[0][None], norm_final[None])
    return _combine(ys_rt, dest3, n).reshape(b, seq, d)
```

```python
import numpy as np
import jax
import jax.numpy as jnp
from jax import lax
from jax.experimental import pallas as pl
from jax.experimental.pallas import tpu as pltpu

F32 = jnp.float32
BF16 = jnp.bfloat16
I32 = jnp.int32

D_MODEL = 1024
SEQ = 4096
GRID_W = 64
ROWS = SEQ // GRID_W
D_ATTN = 512
D_FOUR = 512
N_HEADS = 8
HEAD_DIM = 64
WIN_H = 8
WIN_W = 16
N_FOUR_GROUPS = 8
FOUR_GROUP_DIM = 64
D_PROJ = 3 * D_ATTN + D_FOUR
N_GROUPS = 4
EPG = 8
N_EXPERTS = N_GROUPS * EPG
D_EXPERT = 256
EPS = 1e-6
NEG = -1e30

V7X_VMEM_LIMIT = 56 * 1024 * 1024

TM_IN = 1024
TM_OUT = 1024
OUT_SUBTILES = 1
TM_MOE = 128
ROW_TILE = D_MODEL // 128
ROWS_PER_STEP = 2048
DISPATCH_ROWS = 4096
TILES_PER_STEP = 8
N_PAIRS = EPG * (EPG - 1) // 2
N_BUCKETS = N_GROUPS * N_PAIRS
BUCKET_LANES = 128

QB_ROWS = 8
QB_COLS = 16
KB_ROWS = 16
KB_COLS = 32
GRID_PITCH = 72


def _rms(x, g):
    ms = jnp.mean(x * x, axis=-1, keepdims=True)
    return x * lax.rsqrt(ms + EPS) * g


def _inproj_kernel(x_ref, g_ref, w_ref, qkv_ref, u_ref):
    xn = _rms(x_ref[...], g_ref[...]).astype(BF16)
    p = jnp.dot(xn, w_ref[...], preferred_element_type=F32)
    qkv_ref[:, :D_ATTN] = (p[:, :D_ATTN] * (HEAD_DIM ** -0.5)).astype(BF16)
    qkv_ref[:, D_ATTN:] = p[:, D_ATTN:3 * D_ATTN].astype(BF16)
    for cb in range(D_FOUR // 128):
        lanes = slice(3 * D_ATTN + cb * 128, 3 * D_ATTN + (cb + 1) * 128)
        for r in range(TM_IN // GRID_W):
            u_ref[cb, r * GRID_PITCH:r * GRID_PITCH + GRID_W, :] = p[r * GRID_W:(r + 1) * GRID_W, lanes]
            u_ref[cb, r * GRID_PITCH + GRID_W:(r + 1) * GRID_PITCH, :] = jnp.zeros((GRID_PITCH - GRID_W, 128), F32)


def _inproj(x2, g, w_bf):
    n = x2.shape[0]
    rows_step = TM_IN // GRID_W
    steps_b = ROWS // rows_step
    return pl.pallas_call(
        _inproj_kernel,
        grid=(n // TM_IN,),
        in_specs=[
            pl.BlockSpec((TM_IN, D_MODEL), lambda i: (i, 0)),
            pl.BlockSpec((1, D_MODEL), lambda i: (0, 0)),
            pl.BlockSpec((D_MODEL, D_PROJ), lambda i: (0, 0)),
        ],
        out_specs=[
            pl.BlockSpec((TM_IN, 3 * D_ATTN), lambda i: (i, 0)),
            pl.BlockSpec((D_FOUR // 128, None, rows_step * GRID_PITCH, 128),
                         lambda i: (0, i // steps_b, i % steps_b, 0)),
        ],
        out_shape=[
            jax.ShapeDtypeStruct((n, 3 * D_ATTN), BF16),
            jax.ShapeDtypeStruct((D_FOUR // 128, n // SEQ, ROWS * GRID_PITCH, 128), F32),
        ],
        compiler_params=pltpu.CompilerParams(
            dimension_semantics=("parallel",), vmem_limit_bytes=V7X_VMEM_LIMIT),
        name="inproj",
    )(x2, g, w_bf)


_KCOL_SHIFTED = (False, True, True, False)
_KCOL_OFF = (0, 0, 16, 32)
_COL_TYPE = (0, 1, 1, 2)


def _bias_index_tables():
    dr = np.zeros((9, 128, 512), np.int32)
    dc = np.zeros((9, 128, 512), np.int32)
    ok = np.zeros((9, 128, 512), bool)
    qi, qc = np.divmod(np.arange(128), QB_COLS)
    ki, kc = np.divmod(np.arange(512), KB_COLS)
    for rt, (q0, k0) in enumerate(((0, 0), (8, 4), (56, 48))):
        qrow = q0 + qi
        krow = k0 + ki
        rs = np.clip(qrow - WIN_H // 2, 0, ROWS - WIN_H)
        rok = (krow[None, :] >= rs[:, None]) & (krow[None, :] < rs[:, None] + WIN_H)
        drr = krow[None, :] - qrow[:, None] + (WIN_H - 1)
        for ct, (c0, kc0) in enumerate(((0, 0), (16, 8), (48, 32))):
            qcol = c0 + qc
            kcol = kc0 + kc
            cs = np.clip(qcol - WIN_W // 2, 0, GRID_W - WIN_W)
            cok = (kcol[None, :] >= cs[:, None]) & (kcol[None, :] < cs[:, None] + WIN_W)
            dcc = kcol[None, :] - qcol[:, None] + (WIN_W - 1)
            t = rt * 3 + ct
            ok[t] = rok & cok
            dr[t] = np.where(ok[t], drr, 0)
            dc[t] = np.where(ok[t], dcc, 0)
    return dr, dc, ok


_BIAS_DR, _BIAS_DC, _BIAS_OK = _bias_index_tables()


def _bias_selectors():
    ok = _BIAS_OK.reshape(3, 3, QB_ROWS, QB_COLS, KB_ROWS, KB_COLS)
    dr = _BIAS_DR.reshape(ok.shape)
    dc = _BIAS_DC.reshape(ok.shape)
    row_ok = ok.any(axis=(1, 3, 5))
    col_ok = ok.any(axis=(0, 2, 4))
    dr_r = dr.max(axis=(1, 3, 5))
    dc_c = dc.max(axis=(0, 2, 4))
    sc = (np.arange(2 * WIN_W - 1)[:, None, None, None] == dc_c[None]) & col_ok[None]
    return row_ok, dr_r, col_ok, sc.astype(np.float32)


_BIAS_ROW_OK, _BIAS_ROW_DR, _BIAS_COL_OK, _BIAS_SC = _bias_selectors()


def _bias_columns(rpb):
    sc = jnp.asarray(np.tile(_BIAS_SC, 128 // KB_COLS))
    ok = np.tile(_BIAS_COL_OK, 128 // KB_COLS)
    t1 = jnp.sum(rpb[:, :, :, None, None, None] * sc[None, None], axis=2)
    return jnp.where(ok[None, None], t1, NEG)


def _attn_kernel(q_ref, k_ref, v_ref, bcol_ref, wg_ref, wu_ref, wd_ref, o_ref, wg_out, wu_out, wd_out,
                 ksh_ref, vsh_ref, bias_ref, s_ref, p_ref, l_ref):
    wg_out[...] = wg_ref[...].astype(BF16)
    wu_out[...] = wu_ref[...].astype(BF16)
    wd_out[...] = wd_ref[...].astype(BF16)

    @pl.when(pl.program_id(1) == 0)
    def _():
        key_row = lax.broadcasted_iota(I32, (QB_COLS, KB_ROWS * KB_COLS), 1) // KB_COLS
        for hh in range(2):
            for rt in range(3):
                for ct in range(3):
                    for i in range(QB_ROWS):
                        acc = jnp.full((QB_COLS, KB_ROWS * KB_COLS), NEG, F32)
                        for y in range(KB_ROWS):
                            if _BIAS_ROW_OK[rt, i, y]:
                                cols = bcol_ref[hh, int(_BIAS_ROW_DR[rt, i, y]), ct]
                                cols = jnp.concatenate([cols] * (KB_ROWS * KB_COLS // 128), axis=1)
                                acc = jnp.where(key_row == y, cols, acc)
                        bias_ref[rt * 3 + ct, pl.ds(hh * 128 + i * QB_COLS, QB_COLS), :] = acc

    zpad = jnp.zeros((8, 128), F32)
    ksh_ref[...] = jnp.concatenate([k_ref[...].astype(F32)[8:], zpad], axis=0).astype(BF16)
    vsh_ref[...] = jnp.concatenate([v_ref[...].astype(F32)[8:], zpad], axis=0).astype(BF16)
    lane = lax.broadcasted_iota(I32, (1, 128), 1)
    head_masks = (lane < HEAD_DIM, lane >= HEAD_DIM)

    n_rb = ROWS // QB_ROWS
    n_q = GRID_W // QB_COLS

    def window(ref_plain, ref_shift, rb, j):
        ks = jnp.clip(QB_ROWS * rb - WIN_H // 2, 0, ROWS - KB_ROWS)
        src = ref_shift if _KCOL_SHIFTED[j] else ref_plain
        return jnp.concatenate(
            [src[pl.ds(pl.multiple_of((ks + i) * GRID_W + _KCOL_OFF[j], 16), KB_COLS), :] for i in range(KB_ROWS)],
            axis=0)

    def scores(rb, j):
        q = jnp.concatenate(
            [q_ref[pl.ds(pl.multiple_of((QB_ROWS * rb + i) * GRID_W + QB_COLS * j, 16), QB_COLS), :]
             for i in range(QB_ROWS)], axis=0)
        qm = jnp.concatenate([jnp.where(hm, q, jnp.zeros_like(q)) for hm in head_masks], axis=0)
        s_ref[j] = lax.dot_general(qm, window(k_ref, ksh_ref, rb, j), (((1,), (1,)), ((), ())),
                                   preferred_element_type=F32)

    def softmax(rb, j):
        rt = jnp.where(rb == 0, 0, jnp.where(rb == n_rb - 1, 2, 1))
        s = s_ref[j] + bias_ref[rt * 3 + _COL_TYPE[j]]
        e = jnp.exp(s - jnp.max(s, axis=-1, keepdims=True))
        p_ref[j] = e.astype(BF16)
        l_ref[j] = jnp.broadcast_to(jnp.sum(e, axis=-1, keepdims=True), (2 * QB_ROWS * QB_COLS, 128))

    def values(rb, j):
        o = jnp.dot(p_ref[j], window(v_ref, vsh_ref, rb, j), preferred_element_type=F32) / l_ref[j]
        out = jnp.where(head_masks[0], o[:128], o[128:]).astype(BF16)
        for i in range(QB_ROWS):
            o_ref[pl.ds(pl.multiple_of((QB_ROWS * rb + i) * GRID_W + QB_COLS * j, 16), QB_COLS), :] = (
                out[QB_COLS * i:QB_COLS * (i + 1)])

    def stage(fn, rb):
        for j in range(n_q):
            fn(jnp.asarray(rb, I32), j)

    stage(scores, 0)
    stage(softmax, 0)
    stage(scores, 1)

    def pipeline_step(i, carry):
        stage(values, i - 2)
        stage(softmax, i - 1)
        stage(scores, i)
        return carry

    lax.fori_loop(2, n_rb, pipeline_step, 0)
    stage(values, n_rb - 2)
    stage(softmax, n_rb - 1)
    stage(values, n_rb - 1)


def _attention(qkv3, bias_cols, w_gate, w_up, w_down):
    b = qkv3.shape[0]
    n_hp = N_HEADS // 2
    assert N_EXPERTS % (n_hp * b) == 0
    eps = N_EXPERTS // (n_hp * b)
    wspec = lambda shape: pl.BlockSpec((eps,) + shape, lambda hp, bi: (hp * b + bi, 0, 0))
    wspecs = [wspec((D_MODEL, D_EXPERT)), wspec((D_MODEL, D_EXPERT)), wspec((D_EXPERT, D_MODEL))]
    blk = lambda off: pl.BlockSpec((None, SEQ, 128), lambda hp, bi: (bi, 0, off + hp))
    return pl.pallas_call(
        _attn_kernel,
        grid=(n_hp, b),
        in_specs=[
            blk(0), blk(n_hp), blk(2 * n_hp),
            pl.BlockSpec((2,) + bias_cols.shape[1:], lambda hp, bi: (hp, 0, 0, 0, 0)),
        ] + wspecs,
        out_specs=[pl.BlockSpec((None, SEQ, 128), lambda hp, bi: (bi, 0, hp))] + wspecs,
        out_shape=[jax.ShapeDtypeStruct((b, SEQ, D_ATTN), BF16), jax.ShapeDtypeStruct(w_gate.shape, BF16),
                   jax.ShapeDtypeStruct(w_up.shape, BF16), jax.ShapeDtypeStruct(w_down.shape, BF16)],
        scratch_shapes=[pltpu.VMEM((SEQ, 128), BF16), pltpu.VMEM((SEQ, 128), BF16),
                        pltpu.VMEM((9, 2 * QB_ROWS * QB_COLS, KB_ROWS * KB_COLS), F32),
                        pltpu.VMEM((GRID_W // QB_COLS, 2 * QB_ROWS * QB_COLS, KB_ROWS * KB_COLS), F32),
                        pltpu.VMEM((GRID_W // QB_COLS, 2 * QB_ROWS * QB_COLS, KB_ROWS * KB_COLS), BF16),
                        pltpu.VMEM((GRID_W // QB_COLS, 2 * QB_ROWS * QB_COLS, 128), F32)],
        compiler_params=pltpu.CompilerParams(
            dimension_semantics=("arbitrary", "arbitrary"), vmem_limit_bytes=V7X_VMEM_LIMIT),
        name="nattn",
    )(qkv3, qkv3, qkv3, bias_cols, w_gate, w_up, w_down)


Z_PITCH = 72
N_CBLK = D_FOUR // 128


def _fourier_tables():
    n = 64
    k = np.arange(n)
    ang = 2.0 * np.pi * np.outer(k, k) / n
    c64, s64 = np.cos(ang), np.sin(ang)
    w1 = np.concatenate([c64, -s64], axis=0)
    t1p = np.arange(n)[:, None, None]
    t2p = np.arange(n)[None, :, None]
    t2 = np.arange(n)[None, None, :]
    th = 2.0 * np.pi * ((t2 * (t1p + n * t2p)) % SEQ) / SEQ
    cc, ss = np.cos(th), np.sin(th)
    m2 = np.concatenate([np.concatenate([cc, ss], axis=2), np.concatenate([-ss, cc], axis=2)], axis=1)
    cbd = np.kron(np.eye(4), c64)
    sbd = np.kron(np.eye(4), s64)
    cs = np.concatenate([cbd, sbd], axis=0)
    return w1.astype(np.float32), m2.astype(np.float32), cs.astype(np.float32)


_W1_NP, _M2_NP, _CS_NP = _fourier_tables()


HALF_CBLK = N_CBLK // 2
T2_UNROLL = 16


def _fourier_kernel(u_ref, w1_ref, m2_ref, cs_ref, wbd_ref, bf_ref, y_ref, zs_ref):
    for cb in range(HALF_CBLK):
        for k in range(GRID_PITCH - GRID_W):
            y_ref[cb, pl.ds(GRID_W + k, ROWS, stride=GRID_PITCH), :] = jnp.zeros((ROWS, 128), F32)

    def dft_cols(i, carry):
        for k in range(T2_UNROLL):
            t2 = i * T2_UNROLL + k
            x = jnp.concatenate([u_ref[cb, pl.ds(t2, ROWS, stride=GRID_PITCH), :] for cb in range(HALF_CBLK)],
                                axis=1).astype(BF16)
            z = jnp.dot(w1_ref[...], x, preferred_element_type=F32)
            for cb in range(HALF_CBLK):
                zs_ref[cb, pl.ds(t2, 128, stride=Z_PITCH), :] = z[:, cb * 128:(cb + 1) * 128]
        return carry

    lax.fori_loop(0, GRID_W // T2_UNROLL, dft_cols, 0)

    def dft_rows(a, carry):
        xs = []
        for jo in range(8):
            t1p = a * 8 + jo
            rhs = jnp.concatenate(
                [jnp.concatenate([zs_ref[cb, pl.ds(pl.multiple_of((part * 64 + t1p) * Z_PITCH, 8), 64), :]
                                  for cb in range(HALF_CBLK)], axis=1) for part in range(2)], axis=0)
            xs.append(jnp.dot(m2_ref[t1p], rhs.astype(BF16), preferred_element_type=F32))
        xr = jnp.concatenate([x[:64] for x in xs], axis=0).astype(BF16)
        xi = jnp.concatenate([x[64:] for x in xs], axis=0).astype(BF16)
        lhs = jnp.concatenate([xr, xi], axis=1)
        f = jnp.dot(lhs, cs_ref[...], preferred_element_type=F32) * (1.0 / 512.0)
        y = jnp.dot(f.astype(BF16), wbd_ref[...], preferred_element_type=F32) + bf_ref[...]
        for jo in range(8):
            t1p = a * 8 + jo
            for cb in range(HALF_CBLK):
                y_ref[cb, pl.ds(t1p, ROWS, stride=GRID_PITCH), :] = y[jo * 64:(jo + 1) * 64, cb * 128:(cb + 1) * 128]
        return carry

    lax.fori_loop(0, GRID_W // 8, dft_rows, 0)


def _fourier(u_p, wbd, bf):
    b = u_p.shape[1]
    w1 = jnp.asarray(_W1_NP).astype(BF16)
    m2 = jnp.asarray(_M2_NP).astype(BF16)
    cs = jnp.asarray(_CS_NP).astype(BF16)
    half_blk = pl.BlockSpec((HALF_CBLK, None, ROWS * GRID_PITCH, 128), lambda bi, hf: (hf, bi, 0, 0))
    return pl.pallas_call(
        _fourier_kernel,
        grid=(b, 2),
        in_specs=[
            half_blk,
            pl.BlockSpec((128, 64), lambda bi, hf: (0, 0)),
            pl.BlockSpec((64, 128, 128), lambda bi, hf: (0, 0, 0)),
            pl.BlockSpec((512, 256), lambda bi, hf: (0, 0)),
            pl.BlockSpec((None, 256, 256), lambda bi, hf: (hf, 0, 0)),
            pl.BlockSpec((1, 256), lambda bi, hf: (0, hf)),
        ],
        out_specs=half_blk,
        out_shape=jax.ShapeDtypeStruct(u_p.shape, F32),
        scratch_shapes=[pltpu.VMEM((HALF_CBLK, 128 * Z_PITCH, 128), F32)],
        compiler_params=pltpu.CompilerParams(
            dimension_semantics=("parallel", "parallel"), vmem_limit_bytes=V7X_VMEM_LIMIT),
        name="fourier",
    )(u_p, w1, m2, cs, wbd, bf)


def _mixout_kernel(oa_ref, yf_ref, x_ref, ga_ref, gf_ref, wout_ref, gm_ref, wrt_ref, br_ref, tri_ref,
                   h_ref, bucket_ref, rank_ref, cnt_ref, carry_ref):
    i = pl.program_id(0)

    @pl.when(i == 0)
    def _():
        carry_ref[...] = jnp.zeros_like(carry_ref)

    carry = carry_ref[...]
    for k in range(OUT_SUBTILES):
        carry = _mixout_subtile(k, carry, oa_ref, yf_ref, x_ref, ga_ref, gf_ref, wout_ref, gm_ref, wrt_ref,
                                br_ref, tri_ref, h_ref, bucket_ref, rank_ref)
    carry_ref[...] = carry
    cnt_ref[...] = carry


def _mixout_subtile(k, carry, oa_ref, yf_ref, x_ref, ga_ref, gf_ref, wout_ref, gm_ref, wrt_ref, br_ref, tri_ref,
                    h_ref, bucket_ref, rank_ref):
    rows_k = pl.ds(k * TM_OUT, TM_OUT)
    na = _rms(oa_ref[rows_k, :].astype(F32), ga_ref[...]).astype(BF16)
    grid_rows = [k * (TM_OUT // GRID_W) + r for r in range(TM_OUT // GRID_W)]
    yf = jnp.concatenate(
        [jnp.concatenate([yf_ref[cb, gr * GRID_PITCH:gr * GRID_PITCH + GRID_W, :] for gr in grid_rows], axis=0)
         for cb in range(N_CBLK)], axis=1)
    nf = _rms(yf, gf_ref[...]).astype(BF16)
    merged = jnp.concatenate([na, nf], axis=1)
    h = x_ref[rows_k, :] + jnp.dot(merged, wout_ref[...], preferred_element_type=F32)
    for cb in range(ROW_TILE):
        h_ref[pl.ds(k * TM_OUT * ROW_TILE + cb, TM_OUT, stride=ROW_TILE), :] = h[:, cb * 128:(cb + 1) * 128]
    hn = _rms(h, gm_ref[...]).astype(BF16)
    lt = lax.dot_general(wrt_ref[...], hn, (((1,), (1,)), ((), ())), preferred_element_type=F32)
    lt = lt + br_ref[...]
    c = [lt[k:k + 1] for k in range(N_GROUPS)]
    cmax = jnp.maximum(jnp.maximum(c[0], c[1]), jnp.maximum(c[2], c[3]))
    e = [jnp.exp(ck - cmax) for ck in c]
    esum = (e[0] + e[1]) + (e[2] + e[3])
    p = [ek / esum for ek in e]
    pmax = jnp.maximum(jnp.maximum(p[0], p[1]), jnp.maximum(p[2], p[3]))
    g = jnp.where(p[0] == pmax, 0, jnp.where(p[1] == pmax, 1, jnp.where(p[2] == pmax, 2, 3))).astype(I32)
    fine = jnp.where(g == 0, lt[8:16], jnp.where(g == 1, lt[16:24], jnp.where(g == 2, lt[24:32], lt[32:40])))
    rows = lax.broadcasted_iota(I32, fine.shape, 0)
    v1 = jnp.max(fine, axis=0, keepdims=True)
    i1 = jnp.min(jnp.where(fine == v1, rows, EPG), axis=0, keepdims=True)
    rest = jnp.where(rows == i1, -jnp.inf, fine)
    v2 = jnp.max(rest, axis=0, keepdims=True)
    i2 = jnp.min(jnp.where(rest == v2, rows, EPG), axis=0, keepdims=True)
    lo = jnp.minimum(i1, i2)
    hi = jnp.maximum(i1, i2)
    pair = lax.shift_right_logical(lo * (2 * EPG - 1 - lo), 1) + (hi - lo - 1)
    bucket = g * N_PAIRS + pair
    bucket_ref[k] = bucket
    brow = lax.broadcasted_iota(I32, (BUCKET_LANES, TM_OUT), 0)
    onehot = (brow == bucket).astype(F32)
    prefix = jnp.dot(onehot.astype(BF16), tri_ref[...], preferred_element_type=F32)
    rank = jnp.sum(onehot * (prefix + carry), axis=0, keepdims=True)
    rank_ref[k] = rank.astype(I32)
    return carry + jnp.sum(onehot, axis=1, keepdims=True)


def _mixout(oa, yf, x2, ga, gf, wout_bf, gm, wrt, br, tri):
    n = x2.shape[0]
    nt = n // TM_OUT
    rows_step = TM_OUT * OUT_SUBTILES
    full = lambda *shape: pl.BlockSpec(shape, lambda i: (0,) * len(shape))
    row3 = pl.BlockSpec((OUT_SUBTILES, 1, TM_OUT), lambda i: (i, 0, 0))
    return pl.pallas_call(
        _mixout_kernel,
        grid=(n // rows_step,),
        in_specs=[
            pl.BlockSpec((rows_step, D_ATTN), lambda i: (i, 0)),
            pl.BlockSpec((N_CBLK, rows_step // GRID_W * GRID_PITCH, 128), lambda i: (0, i, 0)),
            pl.BlockSpec((rows_step, D_MODEL), lambda i: (i, 0)),
            full(1, D_ATTN), full(1, D_FOUR), full(D_MODEL, D_MODEL), full(1, D_MODEL),
            full(BUCKET_LANES, D_MODEL), full(BUCKET_LANES, 1), full(TM_OUT, TM_OUT),
        ],
        out_specs=[
            pl.BlockSpec((rows_step * ROW_TILE, 128), lambda i: (i, 0)),
            row3, row3,
            full(BUCKET_LANES, 1),
        ],
        out_shape=[
            jax.ShapeDtypeStruct((n * ROW_TILE, 128), F32),
            jax.ShapeDtypeStruct((nt, 1, TM_OUT), I32),
            jax.ShapeDtypeStruct((nt, 1, TM_OUT), I32),
            jax.ShapeDtypeStruct((BUCKET_LANES, 1), F32),
        ],
        scratch_shapes=[pltpu.VMEM((BUCKET_LANES, 1), F32)],
        compiler_params=pltpu.CompilerParams(
            dimension_semantics=("arbitrary",), vmem_limit_bytes=V7X_VMEM_LIMIT),
        name="mixout",
    )(oa, yf, x2, ga, gf, wout_bf, gm, wrt, br, tri)


def _pair_tables():
    lo, hi = [], []
    for a in range(EPG):
        for b in range(a + 1, EPG):
            lo.append(a)
            hi.append(b)
    return np.asarray(lo, np.int32), np.asarray(hi, np.int32)


_PAIR_LO, _PAIR_HI = _pair_tables()


def _dispatch_kernel(tnv_ref, dest_ref, h_ref, hs_hbm, zbuf, zsem, sem):
    k = pl.program_id(0)
    tile_rows = TM_MOE * ROW_TILE
    n_tiles = hs_hbm.shape[0] // tile_rows

    def for_tiles(pred, which, action):
        def body(t, c):
            @pl.when(pred(tnv_ref[t]))
            def _():
                dst = hs_hbm.at[pl.ds(pl.multiple_of(t * tile_rows, tile_rows), tile_rows)]
                action(pltpu.make_async_copy(zbuf, dst, zsem.at[which]))
            return c
        lax.fori_loop(0, n_tiles, body, 0)

    partly_owned = lambda nv: jnp.logical_and(nv > 0, nv < TM_MOE)
    unowned = lambda nv: nv == 0

    @pl.when(k == 0)
    def _():
        zbuf[...] = jnp.zeros_like(zbuf)
        for_tiles(partly_owned, 0, lambda c: c.start())
        for_tiles(unowned, 1, lambda c: c.start())
        for_tiles(partly_owned, 0, lambda c: c.wait())

    @pl.when(k == pl.num_programs(0) - 1)
    def _():
        for_tiles(unowned, 1, lambda c: c.wait())

    def rows(r8, c):
        for u in range(8):
            r = r8 * 8 + u
            dst = pl.multiple_of(dest_ref[0, 0, r] * ROW_TILE, ROW_TILE)
            pltpu.make_async_copy(h_ref.at[pl.ds(pl.multiple_of(r * ROW_TILE, ROW_TILE), ROW_TILE)],
                                  hs_hbm.at[pl.ds(dst, ROW_TILE)], sem).start(priority=u % 2)
        return c

    lax.fori_loop(0, DISPATCH_ROWS // 8, rows, 0)
    pltpu.make_async_copy(h_ref, hs_hbm.at[pl.ds(0, DISPATCH_ROWS * ROW_TILE)], sem).wait()


def _dispatch(h_rt, dest, tile_nv, n_slots):
    n = h_rt.shape[0] // ROW_TILE
    dest3 = dest.reshape(n // DISPATCH_ROWS, 1, DISPATCH_ROWS)
    grid_spec = pltpu.PrefetchScalarGridSpec(
        num_scalar_prefetch=1,
        grid=(n // DISPATCH_ROWS,),
        in_specs=[
            pl.BlockSpec((1, 1, DISPATCH_ROWS), lambda k, *_: (k, 0, 0), memory_space=pltpu.SMEM),
            pl.BlockSpec((DISPATCH_ROWS * ROW_TILE, 128), lambda k, *_: (k, 0)),
        ],
        out_specs=pl.BlockSpec(memory_space=pl.ANY),
        scratch_shapes=[pltpu.VMEM((TM_MOE * ROW_TILE, 128), F32),
                        pltpu.SemaphoreType.DMA((2,)), pltpu.SemaphoreType.DMA(())],
    )
    return pl.pallas_call(
        _dispatch_kernel,
        grid_spec=grid_spec,
        out_shape=jax.ShapeDtypeStruct((n_slots * ROW_TILE, 128), F32),
        compiler_params=pltpu.CompilerParams(
            dimension_semantics=("arbitrary",), vmem_limit_bytes=V7X_VMEM_LIMIT),
        name="dispatch",
    )(tile_nv, dest3, h_rt)


def _combine_kernel(dest_ref, ys_hbm, o_ref, buf, sem):
    k = pl.program_id(0)
    n_blocks = pl.num_programs(0) - 1
    slot = k % 2

    def start_rows(r8):
        for u in range(8):
            r = r8 * 8 + u
            src = pl.multiple_of(dest_ref[0, 0, r] * ROW_TILE, ROW_TILE)
            pltpu.make_async_copy(ys_hbm.at[pl.ds(src, ROW_TILE)],
                                  buf.at[slot, pl.ds(pl.multiple_of(r * ROW_TILE, ROW_TILE), ROW_TILE)],
                                  sem.at[slot]).start(priority=u % 2)

    def unpack_rows(r8):
        base = pl.multiple_of(r8 * (8 * ROW_TILE), 8 * ROW_TILE)
        for cb in range(ROW_TILE):
            o_ref[pl.ds(pl.multiple_of(r8 * 8, 8), 8), cb * 128:(cb + 1) * 128] = (
                buf[1 - slot, pl.ds(base + cb, 8, stride=ROW_TILE), :])

    def loop(*parts):
        def body(r8, c):
            for part in parts:
                part(r8)
            return c
        lax.fori_loop(0, ROWS_PER_STEP // 8, body, 0)

    @pl.when(k > 0)
    def _():
        pltpu.make_async_copy(ys_hbm.at[pl.ds(0, ROWS_PER_STEP * ROW_TILE)], buf.at[1 - slot],
                              sem.at[1 - slot]).wait()

    @pl.when(k == 0)
    def _():
        loop(start_rows)

    @pl.when(jnp.logical_and(k > 0, k < n_blocks))
    def _():
        loop(start_rows, unpack_rows)

    @pl.when(k == n_blocks)
    def _():
        loop(unpack_rows)


def _combine(ys_rt, dest3, n):
    n_blocks = n // ROWS_PER_STEP
    return pl.pallas_call(
        _combine_kernel,
        grid=(n_blocks + 1,),
        in_specs=[
            pl.BlockSpec((1, 1, ROWS_PER_STEP), lambda k: (jnp.minimum(k, n_blocks - 1), 0, 0),
                         memory_space=pltpu.SMEM),
            pl.BlockSpec(memory_space=pl.ANY),
        ],
        out_specs=pl.BlockSpec((ROWS_PER_STEP, D_MODEL), lambda k: (jnp.maximum(k - 1, 0), 0)),
        out_shape=jax.ShapeDtypeStruct((n, D_MODEL), F32),
        scratch_shapes=[pltpu.VMEM((2, ROWS_PER_STEP * ROW_TILE, 128), F32), pltpu.SemaphoreType.DMA((2,))],
        compiler_params=pltpu.CompilerParams(
            dimension_semantics=("arbitrary",), vmem_limit_bytes=V7X_VMEM_LIMIT),
        name="combine",
    )(dest3, ys_rt)


def _moe_kernel(tg_ref, tlo_ref, thi_ref, nused_ref, hs_ref, wg_ref, wu_ref, wd_ref, wr_ref, br_ref,
                gm_ref, gfin_ref, ys_ref):
    step = pl.program_id(0)

    def tiles(t, k, n_tiles):
        rows = n_tiles * TM_MOE
        base = k * TM_MOE * ROW_TILE
        lane = lax.broadcasted_iota(I32, (rows, BUCKET_LANES), 1)
        hrows = jnp.concatenate(
            [hs_ref[pl.ds(base + cb, rows, stride=ROW_TILE), :] for cb in range(ROW_TILE)], axis=1)
        hn = _rms(hrows, gm_ref[...]).astype(BF16)
        logits = jnp.dot(hn, wr_ref[...], preferred_element_type=F32) + br_ref[...]
        g, lo, hi = tg_ref[t], tlo_ref[t], thi_ref[t]
        coarse = jnp.where(lane < N_GROUPS, logits, -jnp.inf)
        ec = jnp.exp(coarse - jnp.max(coarse, axis=-1, keepdims=True))
        pick = lambda col, v: jnp.sum(jnp.where(lane == col, v, 0.0), axis=-1, keepdims=True)
        g_w = pick(g, ec) / jnp.sum(ec, axis=-1, keepdims=True)
        f_lo = pick(8 + g * EPG + lo, logits)
        f_hi = pick(8 + g * EPG + hi, logits)
        f_max = jnp.maximum(f_lo, f_hi)
        e_lo = jnp.exp(f_lo - f_max)
        e_hi = jnp.exp(f_hi - f_max)
        den = e_lo + e_hi

        def expert(e, w):
            gate = jnp.dot(hn, wg_ref[e], preferred_element_type=F32)
            up = jnp.dot(hn, wu_ref[e], preferred_element_type=F32)
            act = (gate * jax.nn.sigmoid(gate) * up).astype(BF16)
            return w * jnp.dot(act, wd_ref[e], preferred_element_type=F32)

        y = expert(lo, (e_lo / den) * g_w) + expert(hi, (e_hi / den) * g_w)
        res = _rms(hrows + y, gfin_ref[...])
        for cb in range(ROW_TILE):
            ys_ref[pl.ds(base + cb, rows, stride=ROW_TILE), :] = res[:, cb * 128:(cb + 1) * 128]

    @pl.when(step * TILES_PER_STEP < nused_ref[0])
    def _():
        for p in range(TILES_PER_STEP // 2):
            t = step * TILES_PER_STEP + 2 * p
            same = jnp.logical_and(tlo_ref[t] == tlo_ref[t + 1], thi_ref[t] == thi_ref[t + 1])

            @pl.when(same)
            def _():
                tiles(t, 2 * p, 2)

            @pl.when(jnp.logical_not(same))
            def _():
                tiles(t, 2 * p, 1)
                tiles(t + 1, 2 * p + 1, 1)

    @pl.when(step * TILES_PER_STEP >= nused_ref[0])
    def _():
        ys_ref[...] = jnp.zeros_like(ys_ref)


def _moe(hs_rt, tile_g, tile_lo, tile_hi, n_used, wg, wu, wd, wr, br, gm, gfin):
    rows_step = TM_MOE * TILES_PER_STEP * ROW_TILE
    n_steps = hs_rt.shape[0] // rows_step
    by_group = lambda s, tg, *_: (tg[s * TILES_PER_STEP], 0, 0, 0)
    full2 = lambda a, c: pl.BlockSpec((a, c), lambda s, *_: (0, 0))

    def hs_index(s, tg, tlo, thi, nu):
        last_step = jnp.maximum(nu[0] - 1, 0) // TILES_PER_STEP
        return (jnp.minimum(s, last_step), 0)

    grid_spec = pltpu.PrefetchScalarGridSpec(
        num_scalar_prefetch=4,
        grid=(n_steps,),
        in_specs=[
            pl.BlockSpec((rows_step, 128), hs_index),
            pl.BlockSpec((None, EPG, D_MODEL, D_EXPERT), by_group),
            pl.BlockSpec((None, EPG, D_MODEL, D_EXPERT), by_group),
            pl.BlockSpec((None, EPG, D_EXPERT, D_MODEL), by_group),
            full2(D_MODEL, BUCKET_LANES), full2(1, BUCKET_LANES), full2(1, D_MODEL), full2(1, D_MODEL),
        ],
        out_specs=pl.BlockSpec((rows_step, 128), lambda s, *_: (s, 0)),
    )
    return pl.pallas_call(
        _moe_kernel,
        grid_spec=grid_spec,
        out_shape=jax.ShapeDtypeStruct(hs_rt.shape, F32),
        compiler_params=pltpu.CompilerParams(
            dimension_semantics=("arbitrary",), vmem_limit_bytes=V7X_VMEM_LIMIT),
        name="moe",
    )(tile_g, tile_lo, tile_hi, n_used, hs_rt, wg, wu, wd, wr, br, gm, gfin)


def _bucket_plan(bucket, rank, counts, n):
    nt = n // TM_MOE + N_BUCKETS + N_GROUPS * (TILES_PER_STEP - 1)
    nt = -(-nt // TILES_PER_STEP) * TILES_PER_STEP
    tiles_b = (counts + (TM_MOE - 1)) // TM_MOE
    tiles_g = jnp.sum(tiles_b.reshape(N_GROUPS, N_PAIRS), axis=1)
    extra_g = (-tiles_g) % TILES_PER_STEP
    is_last = (np.arange(N_PAIRS) == N_PAIRS - 1)[None, :]
    tiles_b = (tiles_b.reshape(N_GROUPS, N_PAIRS) + jnp.where(is_last, extra_g[:, None], 0)).reshape(N_BUCKETS)
    tile_end = jnp.cumsum(tiles_b)
    tile_start = tile_end - tiles_b
    n_used = tile_end[-1]
    b_ids = jnp.arange(N_BUCKETS, dtype=I32)
    dest = rank + TM_MOE * jnp.sum(jnp.where(bucket[:, None] == b_ids[None, :], tile_start[None, :], 0), axis=1)
    t_idx = jnp.arange(nt, dtype=I32)
    tb = jnp.sum((tile_end[None, :] <= t_idx[:, None]).astype(I32), axis=1)
    tb_last = jnp.sum((tile_end <= n_used - 1).astype(I32))
    tb = jnp.minimum(jnp.where(t_idx < n_used, tb, tb_last), N_BUCKETS - 1)
    sel = tb[:, None] == b_ids[None, :]
    pick = lambda table: jnp.sum(jnp.where(sel, table[None, :], 0), axis=1).astype(I32)
    tile_g = tb // N_PAIRS
    tile_lo = pick(jnp.asarray(np.tile(_PAIR_LO, N_GROUPS)))
    tile_hi = pick(jnp.asarray(np.tile(_PAIR_HI, N_GROUPS)))
    nv = jnp.clip(pick(counts) - (t_idx - pick(tile_start)) * TM_MOE, 0, TM_MOE)
    tile_nv = jnp.where(t_idx < n_used, nv, 0).astype(I32)
    return dest.astype(I32), tile_g.astype(I32), tile_lo, tile_hi, tile_nv, n_used.reshape(1).astype(I32)


def kernel(x, norm_mix, w_in, rpb, w_four, b_four, g_attn_out, g_four_out, w_out, norm_moe,
           w_router_coarse, b_router_coarse, w_router_fine, b_router_fine, w_gate, w_up, w_down, norm_final):
    b, seq, d = x.shape
    assert (seq, d) == (SEQ, D_MODEL) and norm_mix.shape[0] == 1
    n = b * seq
    x2 = x.reshape(n, d)

    qkv, u = _inproj(x2, norm_mix[0][None], w_in[0].astype(BF16))

    oa, wg_bf, wu_bf, wd_bf = _attention(qkv.reshape(b, seq, 3 * D_ATTN), _bias_columns(rpb[0]),
                                         w_gate[0], w_up[0], w_down[0])

    eye4 = jnp.eye(4, dtype=F32)
    wf = w_four[0].reshape(2, 4, FOUR_GROUP_DIM, FOUR_GROUP_DIM)
    wbd = (eye4[None, :, None, :, None] * wf[:, :, :, None, :]).reshape(2, 256, 256).astype(BF16)
    yf = _fourier(u, wbd, b_four[0][None]).reshape(N_CBLK, b * ROWS * GRID_PITCH, 128)

    wrt = jnp.zeros((BUCKET_LANES, d), F32)
    wrt = wrt.at[0:N_GROUPS].set(w_router_coarse[0].T).at[8:8 + N_EXPERTS].set(w_router_fine[0].T)
    br = jnp.zeros((BUCKET_LANES, 1), F32)
    br = br.at[0:N_GROUPS, 0].set(b_router_coarse[0]).at[8:8 + N_EXPERTS, 0].set(b_router_fine[0])
    tri = (np.arange(TM_OUT)[:, None] < np.arange(TM_OUT)[None, :]).astype(np.float32)
    wrt_bf = wrt.astype(BF16)
    h_rt, bucket, rank, cnt = _mixout(
        oa.reshape(n, D_ATTN), yf, x2, g_attn_out[0][None], g_four_out[0][None],
        w_out[0].astype(BF16), norm_moe[0][None], wrt_bf, br, jnp.asarray(tri, BF16))

    counts = cnt[:N_BUCKETS, 0].astype(I32)
    dest, tile_g, tile_lo, tile_hi, tile_nv, n_used = _bucket_plan(bucket.reshape(n), rank.reshape(n), counts, n)
    dest3 = dest.reshape(n // ROWS_PER_STEP, 1, ROWS_PER_STEP)
    hs_rt = _dispatch(h_rt, dest, tile_nv, tile_nv.shape[0] * TM_MOE)
    shape_e = (N_GROUPS, EPG)
    ys_rt = _moe(hs_rt, tile_g, tile_lo, tile_hi, n_used,
                 wg_bf.reshape(shape_e + (d, D_EXPERT)),
                 wu_bf.reshape(shape_e + (d, D_EXPERT)),
                 wd_bf.reshape(shape_e + (D_EXPERT, d)),
                 wrt_bf.T, br.T, norm_moe[0][None], norm_final[None])
    return _combine(ys_rt, dest3, n).reshape(b, seq, d)
```

```python
import numpy as np
import jax
import jax.numpy as jnp
from jax import lax
from jax.experimental import pallas as pl
from jax.experimental.pallas import tpu as pltpu

F32 = jnp.float32
BF16 = jnp.bfloat16
I32 = jnp.int32

D_MODEL = 1024
SEQ = 4096
GRID_W = 64
ROWS = SEQ // GRID_W
D_ATTN = 512
D_FOUR = 512
N_HEADS = 8
HEAD_DIM = 64
WIN_H = 8
WIN_W = 16
N_FOUR_GROUPS = 8
FOUR_GROUP_DIM = 64
D_PROJ = 3 * D_ATTN + D_FOUR
N_GROUPS = 4
EPG = 8
N_EXPERTS = N_GROUPS * EPG
D_EXPERT = 256
EPS = 1e-6
NEG = -1e30

V7X_VMEM_LIMIT = 56 * 1024 * 1024

TM_IN = 1024
TM_OUT = 1024
OUT_SUBTILES = 1
TM_MOE = 128
ROW_TILE = D_MODEL // 128
ROWS_PER_STEP = 2048
DISPATCH_ROWS = 4096
TILES_PER_STEP = 8
N_PAIRS = EPG * (EPG - 1) // 2
N_BUCKETS = N_GROUPS * N_PAIRS
BUCKET_LANES = 128

QB_ROWS = 8
QB_COLS = 16
KB_ROWS = 16
KB_COLS = 32
GRID_PITCH = 72


def _rms(x, g):
    ms = jnp.mean(x * x, axis=-1, keepdims=True)
    return x * lax.rsqrt(ms + EPS) * g


X_RING = 3


def _inproj_kernel(x_hbm, g_ref, w_ref, qkv_ref, u_ref, xbuf, xsem):
    i = pl.program_id(0)
    n_steps = pl.num_programs(0)

    def fetch(step):
        slot = step % X_RING
        return pltpu.make_async_copy(x_hbm.at[pl.ds(step * TM_IN, TM_IN)], xbuf.at[slot], xsem.at[slot])

    @pl.when(i == 0)
    def _():
        fetch(0).start()
        fetch(1).start()

    @pl.when(i + 2 < n_steps)
    def _():
        fetch(i + 2).start()

    fetch(i).wait()
    xn = _rms(xbuf[i % X_RING], g_ref[...]).astype(BF16)
    p = jnp.dot(xn, w_ref[...], preferred_element_type=F32)
    qkv_ref[:, :D_ATTN] = (p[:, :D_ATTN] * (HEAD_DIM ** -0.5)).astype(BF16)
    qkv_ref[:, D_ATTN:] = p[:, D_ATTN:3 * D_ATTN].astype(BF16)
    for cb in range(D_FOUR // 128):
        lanes = slice(3 * D_ATTN + cb * 128, 3 * D_ATTN + (cb + 1) * 128)
        for r in range(TM_IN // GRID_W):
            u_ref[cb, r * GRID_PITCH:r * GRID_PITCH + GRID_W, :] = p[r * GRID_W:(r + 1) * GRID_W, lanes]
            u_ref[cb, r * GRID_PITCH + GRID_W:(r + 1) * GRID_PITCH, :] = jnp.zeros((GRID_PITCH - GRID_W, 128), F32)


def _inproj(x2, g, w_bf):
    n = x2.shape[0]
    rows_step = TM_IN // GRID_W
    steps_b = ROWS // rows_step
    assert n // TM_IN >= X_RING - 1
    return pl.pallas_call(
        _inproj_kernel,
        grid=(n // TM_IN,),
        in_specs=[
            pl.BlockSpec(memory_space=pl.ANY),
            pl.BlockSpec((1, D_MODEL), lambda i: (0, 0)),
            pl.BlockSpec((D_MODEL, D_PROJ), lambda i: (0, 0)),
        ],
        out_specs=[
            pl.BlockSpec((TM_IN, 3 * D_ATTN), lambda i: (i, 0)),
            pl.BlockSpec((D_FOUR // 128, None, rows_step * GRID_PITCH, 128),
                         lambda i: (0, i // steps_b, i % steps_b, 0)),
        ],
        out_shape=[
            jax.ShapeDtypeStruct((n, 3 * D_ATTN), BF16),
            jax.ShapeDtypeStruct((D_FOUR // 128, n // SEQ, ROWS * GRID_PITCH, 128), F32),
        ],
        scratch_shapes=[pltpu.VMEM((X_RING, TM_IN, D_MODEL), F32), pltpu.SemaphoreType.DMA((X_RING,))],
        compiler_params=pltpu.CompilerParams(
            dimension_semantics=("arbitrary",), vmem_limit_bytes=V7X_VMEM_LIMIT),
        name="inproj",
    )(x2, g, w_bf)


_KCOL_SHIFTED = (False, True, True, False)
_KCOL_OFF = (0, 0, 16, 32)
_COL_TYPE = (0, 1, 1, 2)


def _bias_index_tables():
    dr = np.zeros((9, 128, 512), np.int32)
    dc = np.zeros((9, 128, 512), np.int32)
    ok = np.zeros((9, 128, 512), bool)
    qi, qc = np.divmod(np.arange(128), QB_COLS)
    ki, kc = np.divmod(np.arange(512), KB_COLS)
    for rt, (q0, k0) in enumerate(((0, 0), (8, 4), (56, 48))):
        qrow = q0 + qi
        krow = k0 + ki
        rs = np.clip(qrow - WIN_H // 2, 0, ROWS - WIN_H)
        rok = (krow[None, :] >= rs[:, None]) & (krow[None, :] < rs[:, None] + WIN_H)
        drr = krow[None, :] - qrow[:, None] + (WIN_H - 1)
        for ct, (c0, kc0) in enumerate(((0, 0), (16, 8), (48, 32))):
            qcol = c0 + qc
            kcol = kc0 + kc
            cs = np.clip(qcol - WIN_W // 2, 0, GRID_W - WIN_W)
            cok = (kcol[None, :] >= cs[:, None]) & (kcol[None, :] < cs[:, None] + WIN_W)
            dcc = kcol[None, :] - qcol[:, None] + (WIN_W - 1)
            t = rt * 3 + ct
            ok[t] = rok & cok
            dr[t] = np.where(ok[t], drr, 0)
            dc[t] = np.where(ok[t], dcc, 0)
    return dr, dc, ok


_BIAS_DR, _BIAS_DC, _BIAS_OK = _bias_index_tables()


def _bias_selectors():
    ok = _BIAS_OK.reshape(3, 3, QB_ROWS, QB_COLS, KB_ROWS, KB_COLS)
    dr = _BIAS_DR.reshape(ok.shape)
    dc = _BIAS_DC.reshape(ok.shape)
    row_ok = ok.any(axis=(1, 3, 5))
    col_ok = ok.any(axis=(0, 2, 4))
    dr_r = dr.max(axis=(1, 3, 5))
    dc_c = dc.max(axis=(0, 2, 4))
    sc = (np.arange(2 * WIN_W - 1)[:, None, None, None] == dc_c[None]) & col_ok[None]
    return row_ok, dr_r, col_ok, sc.astype(np.float32)


_BIAS_ROW_OK, _BIAS_ROW_DR, _BIAS_COL_OK, _BIAS_SC = _bias_selectors()


def _bias_columns(rpb):
    sc = jnp.asarray(np.tile(_BIAS_SC, 128 // KB_COLS))
    ok = np.tile(_BIAS_COL_OK, 128 // KB_COLS)
    t1 = jnp.sum(rpb[:, :, :, None, None, None] * sc[None, None], axis=2)
    return jnp.where(ok[None, None], t1, NEG)


def _attn_kernel(q_ref, k_ref, v_ref, bcol_ref, wg_ref, wu_ref, wd_ref, o_ref, wg_out, wu_out, wd_out,
                 ksh_ref, vsh_ref, bias_ref, s_ref, p_ref, l_ref):
    wg_out[...] = wg_ref[...].astype(BF16)
    wu_out[...] = wu_ref[...].astype(BF16)
    wd_out[...] = wd_ref[...].astype(BF16)

    @pl.when(pl.program_id(1) == 0)
    def _():
        key_row = lax.broadcasted_iota(I32, (QB_COLS, KB_ROWS * KB_COLS), 1) // KB_COLS
        for hh in range(2):
            for rt in range(3):
                for ct in range(3):
                    for i in range(QB_ROWS):
                        acc = jnp.full((QB_COLS, KB_ROWS * KB_COLS), NEG, F32)
                        for y in range(KB_ROWS):
                            if _BIAS_ROW_OK[rt, i, y]:
                                cols = bcol_ref[hh, int(_BIAS_ROW_DR[rt, i, y]), ct]
                                cols = jnp.concatenate([cols] * (KB_ROWS * KB_COLS // 128), axis=1)
                                acc = jnp.where(key_row == y, cols, acc)
                        bias_ref[rt * 3 + ct, pl.ds(hh * 128 + i * QB_COLS, QB_COLS), :] = acc

    zpad = jnp.zeros((8, 128), F32)
    ksh_ref[...] = jnp.concatenate([k_ref[...].astype(F32)[8:], zpad], axis=0).astype(BF16)
    vsh_ref[...] = jnp.concatenate([v_ref[...].astype(F32)[8:], zpad], axis=0).astype(BF16)
    lane = lax.broadcasted_iota(I32, (1, 128), 1)
    head_masks = (lane < HEAD_DIM, lane >= HEAD_DIM)

    n_rb = ROWS // QB_ROWS
    n_q = GRID_W // QB_COLS

    def window(ref_plain, ref_shift, rb, j):
        ks = jnp.clip(QB_ROWS * rb - WIN_H // 2, 0, ROWS - KB_ROWS)
        src = ref_shift if _KCOL_SHIFTED[j] else ref_plain
        return jnp.concatenate(
            [src[pl.ds(pl.multiple_of((ks + i) * GRID_W + _KCOL_OFF[j], 16), KB_COLS), :] for i in range(KB_ROWS)],
            axis=0)

    def scores(rb, j):
        q = jnp.concatenate(
            [q_ref[pl.ds(pl.multiple_of((QB_ROWS * rb + i) * GRID_W + QB_COLS * j, 16), QB_COLS), :]
             for i in range(QB_ROWS)], axis=0)
        qm = jnp.concatenate([jnp.where(hm, q, jnp.zeros_like(q)) for hm in head_masks], axis=0)
        s_ref[j] = lax.dot_general(qm, window(k_ref, ksh_ref, rb, j), (((1,), (1,)), ((), ())),
                                   preferred_element_type=F32)

    def softmax(rb, j):
        rt = jnp.where(rb == 0, 0, jnp.where(rb == n_rb - 1, 2, 1))
        s = s_ref[j] + bias_ref[rt * 3 + _COL_TYPE[j]]
        e = jnp.exp(s - jnp.max(s, axis=-1, keepdims=True))
        p_ref[j] = e.astype(BF16)
        l_ref[j] = jnp.broadcast_to(jnp.sum(e, axis=-1, keepdims=True), (2 * QB_ROWS * QB_COLS, 128))

    def values(rb, j):
        o = jnp.dot(p_ref[j], window(v_ref, vsh_ref, rb, j), preferred_element_type=F32) / l_ref[j]
        out = jnp.where(head_masks[0], o[:128], o[128:]).astype(BF16)
        for i in range(QB_ROWS):
            o_ref[pl.ds(pl.multiple_of((QB_ROWS * rb + i) * GRID_W + QB_COLS * j, 16), QB_COLS), :] = (
                out[QB_COLS * i:QB_COLS * (i + 1)])

    def stage(fn, rb):
        for j in range(n_q):
            fn(jnp.asarray(rb, I32), j)

    stage(scores, 0)
    stage(softmax, 0)
    stage(scores, 1)

    def pipeline_step(i, carry):
        stage(values, i - 2)
        stage(softmax, i - 1)
        stage(scores, i)
        return carry

    lax.fori_loop(2, n_rb, pipeline_step, 0)
    stage(values, n_rb - 2)
    stage(softmax, n_rb - 1)
    stage(values, n_rb - 1)


def _attention(qkv3, bias_cols, w_gate, w_up, w_down):
    b = qkv3.shape[0]
    n_hp = N_HEADS // 2
    assert N_EXPERTS % (n_hp * b) == 0
    eps = N_EXPERTS // (n_hp * b)
    wspec = lambda shape: pl.BlockSpec((eps,) + shape, lambda hp, bi: (hp * b + bi, 0, 0))
    wspecs = [wspec((D_MODEL, D_EXPERT)), wspec((D_MODEL, D_EXPERT)), wspec((D_EXPERT, D_MODEL))]
    blk = lambda off: pl.BlockSpec((None, SEQ, 128), lambda hp, bi: (bi, 0, off + hp))
    return pl.pallas_call(
        _attn_kernel,
        grid=(n_hp, b),
        in_specs=[
            blk(0), blk(n_hp), blk(2 * n_hp),
            pl.BlockSpec((2,) + bias_cols.shape[1:], lambda hp, bi: (hp, 0, 0, 0, 0)),
        ] + wspecs,
        out_specs=[pl.BlockSpec((None, SEQ, 128), lambda hp, bi: (bi, 0, hp))] + wspecs,
        out_shape=[jax.ShapeDtypeStruct((b, SEQ, D_ATTN), BF16), jax.ShapeDtypeStruct(w_gate.shape, BF16),
                   jax.ShapeDtypeStruct(w_up.shape, BF16), jax.ShapeDtypeStruct(w_down.shape, BF16)],
        scratch_shapes=[pltpu.VMEM((SEQ, 128), BF16), pltpu.VMEM((SEQ, 128), BF16),
                        pltpu.VMEM((9, 2 * QB_ROWS * QB_COLS, KB_ROWS * KB_COLS), F32),
                        pltpu.VMEM((GRID_W // QB_COLS, 2 * QB_ROWS * QB_COLS, KB_ROWS * KB_COLS), F32),
                        pltpu.VMEM((GRID_W // QB_COLS, 2 * QB_ROWS * QB_COLS, KB_ROWS * KB_COLS), BF16),
                        pltpu.VMEM((GRID_W // QB_COLS, 2 * QB_ROWS * QB_COLS, 128), F32)],
        compiler_params=pltpu.CompilerParams(
            dimension_semantics=("arbitrary", "arbitrary"), vmem_limit_bytes=V7X_VMEM_LIMIT),
        name="nattn",
    )(qkv3, qkv3, qkv3, bias_cols, w_gate, w_up, w_down)


Z_PITCH = 72
N_CBLK = D_FOUR // 128


def _fourier_tables():
    n = 64
    k = np.arange(n)
    ang = 2.0 * np.pi * np.outer(k, k) / n
    c64, s64 = np.cos(ang), np.sin(ang)
    w1 = np.concatenate([c64, -s64], axis=0)
    t1p = np.arange(n)[:, None, None]
    t2p = np.arange(n)[None, :, None]
    t2 = np.arange(n)[None, None, :]
    th = 2.0 * np.pi * ((t2 * (t1p + n * t2p)) % SEQ) / SEQ
    cc, ss = np.cos(th), np.sin(th)
    m2 = np.concatenate([np.concatenate([cc, ss], axis=2), np.concatenate([-ss, cc], axis=2)], axis=1)
    cbd = np.kron(np.eye(4), c64)
    sbd = np.kron(np.eye(4), s64)
    cs = np.concatenate([cbd, sbd], axis=0)
    return w1.astype(np.float32), m2.astype(np.float32), cs.astype(np.float32)


_W1_NP, _M2_NP, _CS_NP = _fourier_tables()


HALF_CBLK = N_CBLK // 2
T2_UNROLL = 16


def _fourier_kernel(u_ref, w1_ref, m2_ref, cs_ref, wbd_ref, bf_ref, y_ref, zs_ref):
    for cb in range(HALF_CBLK):
        for k in range(GRID_PITCH - GRID_W):
            y_ref[cb, pl.ds(GRID_W + k, ROWS, stride=GRID_PITCH), :] = jnp.zeros((ROWS, 128), F32)

    def dft_cols(i, carry):
        for k in range(T2_UNROLL):
            t2 = i * T2_UNROLL + k
            x = jnp.concatenate([u_ref[cb, pl.ds(t2, ROWS, stride=GRID_PITCH), :] for cb in range(HALF_CBLK)],
                                axis=1).astype(BF16)
            z = jnp.dot(w1_ref[...], x, preferred_element_type=F32)
            for cb in range(HALF_CBLK):
                zs_ref[cb, pl.ds(t2, 128, stride=Z_PITCH), :] = z[:, cb * 128:(cb + 1) * 128]
        return carry

    lax.fori_loop(0, GRID_W // T2_UNROLL, dft_cols, 0)

    def dft_rows(a, carry):
        xs = []
        for jo in range(8):
            t1p = a * 8 + jo
            rhs = jnp.concatenate(
                [jnp.concatenate([zs_ref[cb, pl.ds(pl.multiple_of((part * 64 + t1p) * Z_PITCH, 8), 64), :]
                                  for cb in range(HALF_CBLK)], axis=1) for part in range(2)], axis=0)
            xs.append(jnp.dot(m2_ref[t1p], rhs.astype(BF16), preferred_element_type=F32))
        xr = jnp.concatenate([x[:64] for x in xs], axis=0).astype(BF16)
        xi = jnp.concatenate([x[64:] for x in xs], axis=0).astype(BF16)
        lhs = jnp.concatenate([xr, xi], axis=1)
        f = jnp.dot(lhs, cs_ref[...], preferred_element_type=F32) * (1.0 / 512.0)
        y = jnp.dot(f.astype(BF16), wbd_ref[...], preferred_element_type=F32) + bf_ref[...]
        for jo in range(8):
            t1p = a * 8 + jo
            for cb in range(HALF_CBLK):
                y_ref[cb, pl.ds(t1p, ROWS, stride=GRID_PITCH), :] = y[jo * 64:(jo + 1) * 64, cb * 128:(cb + 1) * 128]
        return carry

    lax.fori_loop(0, GRID_W // 8, dft_rows, 0)


def _fourier(u_p, wbd, bf):
    b = u_p.shape[1]
    w1 = jnp.asarray(_W1_NP).astype(BF16)
    m2 = jnp.asarray(_M2_NP).astype(BF16)
    cs = jnp.asarray(_CS_NP).astype(BF16)
    half_blk = pl.BlockSpec((HALF_CBLK, None, ROWS * GRID_PITCH, 128), lambda bi, hf: (hf, bi, 0, 0))
    return pl.pallas_call(
        _fourier_kernel,
        grid=(b, 2),
        in_specs=[
            half_blk,
            pl.BlockSpec((128, 64), lambda bi, hf: (0, 0)),
            pl.BlockSpec((64, 128, 128), lambda bi, hf: (0, 0, 0)),
            pl.BlockSpec((512, 256), lambda bi, hf: (0, 0)),
            pl.BlockSpec((None, 256, 256), lambda bi, hf: (hf, 0, 0)),
            pl.BlockSpec((1, 256), lambda bi, hf: (0, hf)),
        ],
        out_specs=half_blk,
        out_shape=jax.ShapeDtypeStruct(u_p.shape, F32),
        scratch_shapes=[pltpu.VMEM((HALF_CBLK, 128 * Z_PITCH, 128), F32)],
        compiler_params=pltpu.CompilerParams(
            dimension_semantics=("parallel", "parallel"), vmem_limit_bytes=V7X_VMEM_LIMIT),
        name="fourier",
    )(u_p, w1, m2, cs, wbd, bf)


def _mixout_kernel(oa_ref, yf_ref, x_ref, ga_ref, gf_ref, wout_ref, gm_ref, wrt_ref, br_ref, tri_ref,
                   h_ref, bucket_ref, rank_ref, cnt_ref, carry_ref):
    i = pl.program_id(0)

    @pl.when(i == 0)
    def _():
        carry_ref[...] = jnp.zeros_like(carry_ref)

    carry = carry_ref[...]
    for k in range(OUT_SUBTILES):
        carry = _mixout_subtile(k, carry, oa_ref, yf_ref, x_ref, ga_ref, gf_ref, wout_ref, gm_ref, wrt_ref,
                                br_ref, tri_ref, h_ref, bucket_ref, rank_ref)
    carry_ref[...] = carry
    cnt_ref[...] = carry


def _mixout_subtile(k, carry, oa_ref, yf_ref, x_ref, ga_ref, gf_ref, wout_ref, gm_ref, wrt_ref, br_ref, tri_ref,
                    h_ref, bucket_ref, rank_ref):
    rows_k = pl.ds(k * TM_OUT, TM_OUT)
    na = _rms(oa_ref[rows_k, :].astype(F32), ga_ref[...]).astype(BF16)
    grid_rows = [k * (TM_OUT // GRID_W) + r for r in range(TM_OUT // GRID_W)]
    yf = jnp.concatenate(
        [jnp.concatenate([yf_ref[cb, gr * GRID_PITCH:gr * GRID_PITCH + GRID_W, :] for gr in grid_rows], axis=0)
         for cb in range(N_CBLK)], axis=1)
    nf = _rms(yf, gf_ref[...]).astype(BF16)
    merged = jnp.concatenate([na, nf], axis=1)
    h = x_ref[rows_k, :] + jnp.dot(merged, wout_ref[...], preferred_element_type=F32)
    for cb in range(ROW_TILE):
        h_ref[pl.ds(k * TM_OUT * ROW_TILE + cb, TM_OUT, stride=ROW_TILE), :] = h[:, cb * 128:(cb + 1) * 128]
    hn = _rms(h, gm_ref[...]).astype(BF16)
    lt = lax.dot_general(wrt_ref[...], hn, (((1,), (1,)), ((), ())), preferred_element_type=F32)
    lt = lt + br_ref[...]
    c = [lt[k:k + 1] for k in range(N_GROUPS)]
    cmax = jnp.maximum(jnp.maximum(c[0], c[1]), jnp.maximum(c[2], c[3]))
    e = [jnp.exp(ck - cmax) for ck in c]
    esum = (e[0] + e[1]) + (e[2] + e[3])
    p = [ek / esum for ek in e]
    pmax = jnp.maximum(jnp.maximum(p[0], p[1]), jnp.maximum(p[2], p[3]))
    g = jnp.where(p[0] == pmax, 0, jnp.where(p[1] == pmax, 1, jnp.where(p[2] == pmax, 2, 3))).astype(I32)
    fine = jnp.where(g == 0, lt[8:16], jnp.where(g == 1, lt[16:24], jnp.where(g == 2, lt[24:32], lt[32:40])))
    rows = lax.broadcasted_iota(I32, fine.shape, 0)
    v1 = jnp.max(fine, axis=0, keepdims=True)
    i1 = jnp.min(jnp.where(fine == v1, rows, EPG), axis=0, keepdims=True)
    rest = jnp.where(rows == i1, -jnp.inf, fine)
    v2 = jnp.max(rest, axis=0, keepdims=True)
    i2 = jnp.min(jnp.where(rest == v2, rows, EPG), axis=0, keepdims=True)
    lo = jnp.minimum(i1, i2)
    hi = jnp.maximum(i1, i2)
    pair = lax.shift_right_logical(lo * (2 * EPG - 1 - lo), 1) + (hi - lo - 1)
    bucket = g * N_PAIRS + pair
    bucket_ref[k] = bucket
    brow = lax.broadcasted_iota(I32, (BUCKET_LANES, TM_OUT), 0)
    onehot = (brow == bucket).astype(F32)
    prefix = jnp.dot(onehot.astype(BF16), tri_ref[...], preferred_element_type=F32)
    rank = jnp.sum(onehot * (prefix + carry), axis=0, keepdims=True)
    rank_ref[k] = rank.astype(I32)
    return carry + jnp.sum(onehot, axis=1, keepdims=True)


def _mixout(oa, yf, x2, ga, gf, wout_bf, gm, wrt, br, tri):
    n = x2.shape[0]
    nt = n // TM_OUT
    rows_step = TM_OUT * OUT_SUBTILES
    full = lambda *shape: pl.BlockSpec(shape, lambda i: (0,) * len(shape))
    row3 = pl.BlockSpec((OUT_SUBTILES, 1, TM_OUT), lambda i: (i, 0, 0))
    return pl.pallas_call(
        _mixout_kernel,
        grid=(n // rows_step,),
        in_specs=[
            pl.BlockSpec((rows_step, D_ATTN), lambda i: (i, 0)),
            pl.BlockSpec((N_CBLK, rows_step // GRID_W * GRID_PITCH, 128), lambda i: (0, i, 0)),
            pl.BlockSpec((rows_step, D_MODEL), lambda i: (i, 0)),
            full(1, D_ATTN), full(1, D_FOUR), full(D_MODEL, D_MODEL), full(1, D_MODEL),
            full(BUCKET_LANES, D_MODEL), full(BUCKET_LANES, 1), full(TM_OUT, TM_OUT),
        ],
        out_specs=[
            pl.BlockSpec((rows_step * ROW_TILE, 128), lambda i: (i, 0)),
            row3, row3,
            full(BUCKET_LANES, 1),
        ],
        out_shape=[
            jax.ShapeDtypeStruct((n * ROW_TILE, 128), F32),
            jax.ShapeDtypeStruct((nt, 1, TM_OUT), I32),
            jax.ShapeDtypeStruct((nt, 1, TM_OUT), I32),
            jax.ShapeDtypeStruct((BUCKET_LANES, 1), F32),
        ],
        scratch_shapes=[pltpu.VMEM((BUCKET_LANES, 1), F32)],
        compiler_params=pltpu.CompilerParams(
            dimension_semantics=("arbitrary",), vmem_limit_bytes=V7X_VMEM_LIMIT),
        name="mixout",
    )(oa, yf, x2, ga, gf, wout_bf, gm, wrt, br, tri)


def _pair_tables():
    lo, hi = [], []
    for a in range(EPG):
        for b in range(a + 1, EPG):
            lo.append(a)
            hi.append(b)
    return np.asarray(lo, np.int32), np.asarray(hi, np.int32)


_PAIR_LO, _PAIR_HI = _pair_tables()


def _dispatch_kernel(tnv_ref, dest_ref, h_ref, hs_hbm, zbuf, zsem, sem):
    k = pl.program_id(0)
    tile_rows = TM_MOE * ROW_TILE
    n_tiles = hs_hbm.shape[0] // tile_rows

    def for_tiles(pred, which, action):
        def body(t, c):
            @pl.when(pred(tnv_ref[t]))
            def _():
                dst = hs_hbm.at[pl.ds(pl.multiple_of(t * tile_rows, tile_rows), tile_rows)]
                action(pltpu.make_async_copy(zbuf, dst, zsem.at[which]))
            return c
        lax.fori_loop(0, n_tiles, body, 0)

    partly_owned = lambda nv: jnp.logical_and(nv > 0, nv < TM_MOE)
    unowned = lambda nv: nv == 0

    @pl.when(k == 0)
    def _():
        zbuf[...] = jnp.zeros_like(zbuf)
        for_tiles(partly_owned, 0, lambda c: c.start())
        for_tiles(unowned, 1, lambda c: c.start())
        for_tiles(partly_owned, 0, lambda c: c.wait())

    @pl.when(k == pl.num_programs(0) - 1)
    def _():
        for_tiles(unowned, 1, lambda c: c.wait())

    def rows(r8, c):
        for u in range(8):
            r = r8 * 8 + u
            dst = pl.multiple_of(dest_ref[0, 0, r] * ROW_TILE, ROW_TILE)
            pltpu.make_async_copy(h_ref.at[pl.ds(pl.multiple_of(r * ROW_TILE, ROW_TILE), ROW_TILE)],
                                  hs_hbm.at[pl.ds(dst, ROW_TILE)], sem).start(priority=u % 2)
        return c

    lax.fori_loop(0, DISPATCH_ROWS // 8, rows, 0)
    pltpu.make_async_copy(h_ref, hs_hbm.at[pl.ds(0, DISPATCH_ROWS * ROW_TILE)], sem).wait()


def _dispatch(h_rt, dest, tile_nv, n_slots):
    n = h_rt.shape[0] // ROW_TILE
    dest3 = dest.reshape(n // DISPATCH_ROWS, 1, DISPATCH_ROWS)
    grid_spec = pltpu.PrefetchScalarGridSpec(
        num_scalar_prefetch=1,
        grid=(n // DISPATCH_ROWS,),
        in_specs=[
            pl.BlockSpec((1, 1, DISPATCH_ROWS), lambda k, *_: (k, 0, 0), memory_space=pltpu.SMEM),
            pl.BlockSpec((DISPATCH_ROWS * ROW_TILE, 128), lambda k, *_: (k, 0)),
        ],
        out_specs=pl.BlockSpec(memory_space=pl.ANY),
        scratch_shapes=[pltpu.VMEM((TM_MOE * ROW_TILE, 128), F32),
                        pltpu.SemaphoreType.DMA((2,)), pltpu.SemaphoreType.DMA(())],
    )
    return pl.pallas_call(
        _dispatch_kernel,
        grid_spec=grid_spec,
        out_shape=jax.ShapeDtypeStruct((n_slots * ROW_TILE, 128), F32),
        compiler_params=pltpu.CompilerParams(
            dimension_semantics=("arbitrary",), vmem_limit_bytes=V7X_VMEM_LIMIT),
        name="dispatch",
    )(tile_nv, dest3, h_rt)


def _combine_kernel(dest_ref, ys_hbm, o_ref, buf, sem):
    k = pl.program_id(0)
    n_blocks = pl.num_programs(0) - 1
    slot = k % 2

    def start_rows(r8):
        for u in range(8):
            r = r8 * 8 + u
            src = pl.multiple_of(dest_ref[0, 0, r] * ROW_TILE, ROW_TILE)
            pltpu.make_async_copy(ys_hbm.at[pl.ds(src, ROW_TILE)],
                                  buf.at[slot, pl.ds(pl.multiple_of(r * ROW_TILE, ROW_TILE), ROW_TILE)],
                                  sem.at[slot]).start(priority=u % 2)

    def unpack_rows(r8):
        base = pl.multiple_of(r8 * (8 * ROW_TILE), 8 * ROW_TILE)
        for cb in range(ROW_TILE):
            o_ref[pl.ds(pl.multiple_of(r8 * 8, 8), 8), cb * 128:(cb + 1) * 128] = (
                buf[1 - slot, pl.ds(base + cb, 8, stride=ROW_TILE), :])

    def loop(*parts):
        def body(r8, c):
            for part in parts:
                part(r8)
            return c
        lax.fori_loop(0, ROWS_PER_STEP // 8, body, 0)

    @pl.when(k > 0)
    def _():
        pltpu.make_async_copy(ys_hbm.at[pl.ds(0, ROWS_PER_STEP * ROW_TILE)], buf.at[1 - slot],
                              sem.at[1 - slot]).wait()

    @pl.when(k == 0)
    def _():
        loop(start_rows)

    @pl.when(jnp.logical_and(k > 0, k < n_blocks))
    def _():
        loop(start_rows, unpack_rows)

    @pl.when(k == n_blocks)
    def _():
        loop(unpack_rows)


def _combine(ys_rt, dest3, n):
    n_blocks = n // ROWS_PER_STEP
    return pl.pallas_call(
        _combine_kernel,
        grid=(n_blocks + 1,),
        in_specs=[
            pl.BlockSpec((1, 1, ROWS_PER_STEP), lambda k: (jnp.minimum(k, n_blocks - 1), 0, 0),
                         memory_space=pltpu.SMEM),
            pl.BlockSpec(memory_space=pl.ANY),
        ],
        out_specs=pl.BlockSpec((ROWS_PER_STEP, D_MODEL), lambda k: (jnp.maximum(k - 1, 0), 0)),
        out_shape=jax.ShapeDtypeStruct((n, D_MODEL), F32),
        scratch_shapes=[pltpu.VMEM((2, ROWS_PER_STEP * ROW_TILE, 128), F32), pltpu.SemaphoreType.DMA((2,))],
        compiler_params=pltpu.CompilerParams(
            dimension_semantics=("arbitrary",), vmem_limit_bytes=V7X_VMEM_LIMIT),
        name="combine",
    )(dest3, ys_rt)


def _moe_kernel(tg_ref, tlo_ref, thi_ref, nused_ref, hs_ref, wg_ref, wu_ref, wd_ref, wr_ref, br_ref,
                gm_ref, gfin_ref, ys_ref):
    step = pl.program_id(0)

    def tiles(t, k, n_tiles):
        rows = n_tiles * TM_MOE
        base = k * TM_MOE * ROW_TILE
        lane = lax.broadcasted_iota(I32, (rows, BUCKET_LANES), 1)
        hrows = jnp.concatenate(
            [hs_ref[pl.ds(base + cb, rows, stride=ROW_TILE), :] for cb in range(ROW_TILE)], axis=1)
        hn = _rms(hrows, gm_ref[...]).astype(BF16)
        logits = jnp.dot(hn, wr_ref[...], preferred_element_type=F32) + br_ref[...]
        g, lo, hi = tg_ref[t], tlo_ref[t], thi_ref[t]
        coarse = jnp.where(lane < N_GROUPS, logits, -jnp.inf)
        ec = jnp.exp(coarse - jnp.max(coarse, axis=-1, keepdims=True))
        pick = lambda col, v: jnp.sum(jnp.where(lane == col, v, 0.0), axis=-1, keepdims=True)
        g_w = pick(g, ec) / jnp.sum(ec, axis=-1, keepdims=True)
        f_lo = pick(8 + g * EPG + lo, logits)
        f_hi = pick(8 + g * EPG + hi, logits)
        f_max = jnp.maximum(f_lo, f_hi)
        e_lo = jnp.exp(f_lo - f_max)
        e_hi = jnp.exp(f_hi - f_max)
        den = e_lo + e_hi

        def expert(e, w):
            gate = jnp.dot(hn, wg_ref[e], preferred_element_type=F32)
            up = jnp.dot(hn, wu_ref[e], preferred_element_type=F32)
            act = (gate * jax.nn.sigmoid(gate) * up).astype(BF16)
            return w * jnp.dot(act, wd_ref[e], preferred_element_type=F32)

        y = expert(lo, (e_lo / den) * g_w) + expert(hi, (e_hi / den) * g_w)
        res = _rms(hrows + y, gfin_ref[...])
        for cb in range(ROW_TILE):
            ys_ref[pl.ds(base + cb, rows, stride=ROW_TILE), :] = res[:, cb * 128:(cb + 1) * 128]

    @pl.when(step * TILES_PER_STEP < nused_ref[0])
    def _():
        for p in range(TILES_PER_STEP // 2):
            t = step * TILES_PER_STEP + 2 * p
            same = jnp.logical_and(tlo_ref[t] == tlo_ref[t + 1], thi_ref[t] == thi_ref[t + 1])

            @pl.when(same)
            def _():
                tiles(t, 2 * p, 2)

            @pl.when(jnp.logical_not(same))
            def _():
                tiles(t, 2 * p, 1)
                tiles(t + 1, 2 * p + 1, 1)

    @pl.when(step * TILES_PER_STEP >= nused_ref[0])
    def _():
        ys_ref[...] = jnp.zeros_like(ys_ref)


def _moe(hs_rt, tile_g, tile_lo, tile_hi, n_used, wg, wu, wd, wr, br, gm, gfin):
    rows_step = TM_MOE * TILES_PER_STEP * ROW_TILE
    n_steps = hs_rt.shape[0] // rows_step
    by_group = lambda s, tg, *_: (tg[s * TILES_PER_STEP], 0, 0, 0)
    full2 = lambda a, c: pl.BlockSpec((a, c), lambda s, *_: (0, 0))

    def hs_index(s, tg, tlo, thi, nu):
        last_step = jnp.maximum(nu[0] - 1, 0) // TILES_PER_STEP
        return (jnp.minimum(s, last_step), 0)

    grid_spec = pltpu.PrefetchScalarGridSpec(
        num_scalar_prefetch=4,
        grid=(n_steps,),
        in_specs=[
            pl.BlockSpec((rows_step, 128), hs_index),
            pl.BlockSpec((None, EPG, D_MODEL, D_EXPERT), by_group),
            pl.BlockSpec((None, EPG, D_MODEL, D_EXPERT), by_group),
            pl.BlockSpec((None, EPG, D_EXPERT, D_MODEL), by_group),
            full2(D_MODEL, BUCKET_LANES), full2(1, BUCKET_LANES), full2(1, D_MODEL), full2(1, D_MODEL),
        ],
        out_specs=pl.BlockSpec((rows_step, 128), lambda s, *_: (s, 0)),
    )
    return pl.pallas_call(
        _moe_kernel,
        grid_spec=grid_spec,
        out_shape=jax.ShapeDtypeStruct(hs_rt.shape, F32),
        compiler_params=pltpu.CompilerParams(
            dimension_semantics=("arbitrary",), vmem_limit_bytes=V7X_VMEM_LIMIT),
        name="moe",
    )(tile_g, tile_lo, tile_hi, n_used, hs_rt, wg, wu, wd, wr, br, gm, gfin)


def _bucket_plan(bucket, rank, counts, n):
    nt = n // TM_MOE + N_BUCKETS + N_GROUPS * (TILES_PER_STEP - 1)
    nt = -(-nt // TILES_PER_STEP) * TILES_PER_STEP
    tiles_b = (counts + (TM_MOE - 1)) // TM_MOE
    tiles_g = jnp.sum(tiles_b.reshape(N_GROUPS, N_PAIRS), axis=1)
    extra_g = (-tiles_g) % TILES_PER_STEP
    is_last = (np.arange(N_PAIRS) == N_PAIRS - 1)[None, :]
    tiles_b = (tiles_b.reshape(N_GROUPS, N_PAIRS) + jnp.where(is_last, extra_g[:, None], 0)).reshape(N_BUCKETS)
    tile_end = jnp.cumsum(tiles_b)
    tile_start = tile_end - tiles_b
    n_used = tile_end[-1]
    b_ids = jnp.arange(N_BUCKETS, dtype=I32)
    dest = rank + TM_MOE * jnp.sum(jnp.where(bucket[:, None] == b_ids[None, :], tile_start[None, :], 0), axis=1)
    t_idx = jnp.arange(nt, dtype=I32)
    tb = jnp.sum((tile_end[None, :] <= t_idx[:, None]).astype(I32), axis=1)
    tb_last = jnp.sum((tile_end <= n_used - 1).astype(I32))
    tb = jnp.minimum(jnp.where(t_idx < n_used, tb, tb_last), N_BUCKETS - 1)
    sel = tb[:, None] == b_ids[None, :]
    pick = lambda table: jnp.sum(jnp.where(sel, table[None, :], 0), axis=1).astype(I32)
    tile_g = tb // N_PAIRS
    tile_lo = pick(jnp.asarray(np.tile(_PAIR_LO, N_GROUPS)))
    tile_hi = pick(jnp.asarray(np.tile(_PAIR_HI, N_GROUPS)))
    nv = jnp.clip(pick(counts) - (t_idx - pick(tile_start)) * TM_MOE, 0, TM_MOE)
    tile_nv = jnp.where(t_idx < n_used, nv, 0).astype(I32)
    return dest.astype(I32), tile_g.astype(I32), tile_lo, tile_hi, tile_nv, n_used.reshape(1).astype(I32)


def kernel(x, norm_mix, w_in, rpb, w_four, b_four, g_attn_out, g_four_out, w_out, norm_moe,
           w_router_coarse, b_router_coarse, w_router_fine, b_router_fine, w_gate, w_up, w_down, norm_final):
    b, seq, d = x.shape
    assert (seq, d) == (SEQ, D_MODEL) and norm_mix.shape[0] == 1
    n = b * seq
    x2 = x.reshape(n, d)

    qkv, u = _inproj(x2, norm_mix[0][None], w_in[0].astype(BF16))

    oa, wg_bf, wu_bf, wd_bf = _attention(qkv.reshape(b, seq, 3 * D_ATTN), _bias_columns(rpb[0]),
                                         w_gate[0], w_up[0], w_down[0])

    eye4 = jnp.eye(4, dtype=F32)
    wf = w_four[0].reshape(2, 4, FOUR_GROUP_DIM, FOUR_GROUP_DIM)
    wbd = (eye4[None, :, None, :, None] * wf[:, :, :, None, :]).reshape(2, 256, 256).astype(BF16)
    yf = _fourier(u, wbd, b_four[0][None]).reshape(N_CBLK, b * ROWS * GRID_PITCH, 128)

    wrt = jnp.zeros((BUCKET_LANES, d), F32)
    wrt = wrt.at[0:N_GROUPS].set(w_router_coarse[0].T).at[8:8 + N_EXPERTS].set(w_router_fine[0].T)
    br = jnp.zeros((BUCKET_LANES, 1), F32)
    br = br.at[0:N_GROUPS, 0].set(b_router_coarse[0]).at[8:8 + N_EXPERTS, 0].set(b_router_fine[0])
    tri = (np.arange(TM_OUT)[:, None] < np.arange(TM_OUT)[None, :]).astype(np.float32)
    wrt_bf = wrt.astype(BF16)
    h_rt, bucket, rank, cnt = _mixout(
        oa.reshape(n, D_ATTN), yf, x2, g_attn_out[0][None], g_four_out[0][None],
        w_out[0].astype(BF16), norm_moe[0][None], wrt_bf, br, jnp.asarray(tri, BF16))

    counts = cnt[:N_BUCKETS, 0].astype(I32)
    dest, tile_g, tile_lo, tile_hi, tile_nv, n_used = _bucket_plan(bucket.reshape(n), rank.reshape(n), counts, n)
    dest3 = dest.reshape(n // ROWS_PER_STEP, 1, ROWS_PER_STEP)
    hs_rt = _dispatch(h_rt, dest, tile_nv, tile_nv.shape[0] * TM_MOE)
    shape_e = (N_GROUPS, EPG)
    ys_rt = _moe(hs_rt, tile_g, tile_lo, tile_hi, n_used,
                 wg_bf.reshape(shape_e + (d, D_EXPERT)),
                 wu_bf.reshape(shape_e + (d, D_EXPERT)),
                 wd_bf.reshape(shape_e + (D_EXPERT, d)),
                 wrt_bf.T, br.T, norm_moe[0][None], norm_final[None])
    return _combine(ys_rt, dest3, n).reshape(b, seq, d)
```
